```python
import jax, jax.numpy as jnp
from jax import lax
import numpy as np

D_MODEL = 1024
BATCH = 8
SEQ = 2048
DEPTH = 2
DEC_BATCH = 32
DEC_SEQ = 64
PAST_LEN = 2048

CHUNK = 64
N_MIXERS = 2
N_GLA_LAYERS = (DEPTH + 1) // 2
N_FOX_LAYERS = DEPTH // 2

GLA_HEADS = 4
GLA_DK = D_MODEL // 2 // GLA_HEADS
GLA_DV = D_MODEL // GLA_HEADS
GLA_HK = GLA_HEADS * GLA_DK
GLA_HV = GLA_HEADS * GLA_DV
GLA_GATE_RANK = 16
GLA_GATE_TAU = 16.0
GLA_IN = 2 * GLA_HK + 2 * GLA_HV + GLA_GATE_RANK

FOX_HEADS = 16
FOX_HD = D_MODEL // FOX_HEADS
FOX_HW = FOX_HEADS * FOX_HD
FOX_IN = 3 * FOX_HW + FOX_HEADS
FOX_QBLOCK = 128
FOX_FORGET_BIAS_INIT = 3.0

D_FF = -(-8 * D_MODEL // (3 * 256)) * 256
EPS = 1e-6
MASK_VALUE = -1e30

kernel_name = "gla_fox_hybrid_stream_step"


def rmsnorm(x, g):
    xf = x.astype(jnp.float32)
    y = xf * lax.rsqrt(jnp.mean(xf * xf, axis=-1, keepdims=True) + EPS)
    return (y * g.astype(jnp.float32)).astype(x.dtype)


def swiglu(h, w_in, w_down):
    gu = h @ w_in
    gate, up = jnp.split(gu, [D_FF], axis=-1)
    return (jax.nn.silu(gate) * up) @ w_down


def gla_recurrence(q, k, v, g, s0):
    B, T = q.shape[:2]
    n = -(-T // CHUNK)
    pad = n * CHUNK - T

    def pad_t(a):
        return jnp.pad(a, ((0, 0), (0, pad), (0, 0), (0, 0)))

    def to_blocks(a):
        return a.reshape(B, n, CHUNK, a.shape[2], a.shape[3]).transpose(1, 0, 3, 2, 4)

    qc = to_blocks(pad_t(q)).astype(jnp.float32)
    kc = to_blocks(pad_t(k)).astype(jnp.float32)
    vc = to_blocks(pad_t(v)).astype(jnp.float32)
    gc = to_blocks(pad_t(g)).astype(jnp.float32)
    b = jnp.cumsum(gc, axis=3)
    b_last = b[:, :, :, -1:, :]
    qe = qc * jnp.exp(b) * (GLA_DK ** -0.5)
    ke = kc * jnp.exp(-b)
    kd = kc * jnp.exp(b_last - b)
    causal = jnp.tril(jnp.ones((CHUNK, CHUNK), dtype=bool))
    a = jnp.where(causal, jnp.einsum('nbhtd,nbhsd->nbhts', qe, ke), 0.0)
    o_intra = jnp.einsum('nbhts,nbhsv->nbhtv', a, vc)

    def step(s, xs):
        qe_c, kd_c, v_c, dec_c = xs
        o = jnp.einsum('bhtd,bhdv->bhtv', qe_c, s)
        s = s * dec_c[..., None] + jnp.einsum('bhsd,bhsv->bhdv', kd_c, v_c)
        return s, o

    s_fin, o_inter = lax.scan(step, s0.astype(jnp.float32),
                              (qe, kd, vc, jnp.exp(b_last[:, :, :, 0, :])))
    o = (o_intra + o_inter).transpose(1, 0, 3, 2, 4).reshape(B, n * CHUNK, GLA_HEADS, GLA_DV)[:, :T]
    return o.astype(q.dtype), s_fin.astype(s0.dtype)


def gla_mixer(h, s0, w_in, w_g2, b_g, norm_g, w_out):
    B, T, _ = h.shape
    proj = h @ w_in
    q, k, v, r, gl = jnp.split(proj, [GLA_HK, 2 * GLA_HK, 2 * GLA_HK + GLA_HV, 2 * GLA_HK + 2 * GLA_HV], axis=-1)
    q = q.reshape(B, T, GLA_HEADS, GLA_DK)
    k = k.reshape(B, T, GLA_HEADS, GLA_DK)
    v = v.reshape(B, T, GLA_HEADS, GLA_DV)
    g = (jax.nn.log_sigmoid((gl @ w_g2 + b_g).astype(jnp.float32)) / GLA_GATE_TAU)
    g = g.reshape(B, T, GLA_HEADS, GLA_DK)
    o, s = gla_recurrence(q, k, v, g, s0)
    o = rmsnorm(o, norm_g.reshape(GLA_HEADS, GLA_DV)).reshape(B, T, GLA_HV)
    return (o * jax.nn.silu(r)) @ w_out, s


def fox_attend(q, k, v, c_q, c_k, q_pos, k_pos):
    s = jnp.einsum('bqhd,bkhd->bhqk', q, k).astype(jnp.float32) * (FOX_HD ** -0.5)
    bias = c_q.transpose(0, 2, 1)[:, :, :, None] - c_k.transpose(0, 2, 1)[:, :, None, :]
    mask = k_pos[None, :] <= q_pos[:, None]
    p = jax.nn.softmax(jnp.where(mask, s + bias, MASK_VALUE), axis=-1)
    return jnp.einsum('bhqk,bkhd->bqhd', p.astype(v.dtype), v)


def fox_project(h, w_in, b_f):
    B, T, _ = h.shape
    proj = h @ w_in
    q, k, v, fl = jnp.split(proj, [FOX_HW, 2 * FOX_HW, 3 * FOX_HW], axis=-1)
    q = q.reshape(B, T, FOX_HEADS, FOX_HD)
    k = k.reshape(B, T, FOX_HEADS, FOX_HD)
    v = v.reshape(B, T, FOX_HEADS, FOX_HD)
    logf = jax.nn.log_sigmoid((fl + b_f).astype(jnp.float32))
    return q, k, v, logf


def fox_mixer_prompt(h, w_in, b_f, w_out):
    B, T, _ = h.shape
    q, k, v, logf = fox_project(h, w_in, b_f)
    c = jnp.cumsum(logf, axis=1)
    nb = T // FOX_QBLOCK
    qb = q.reshape(B, nb, FOX_QBLOCK, FOX_HEADS, FOX_HD).transpose(1, 0, 2, 3, 4)
    cb = c.reshape(B, nb, FOX_QBLOCK, FOX_HEADS).transpose(1, 0, 2, 3)
    pos = jnp.arange(T, dtype=jnp.int32)
    pb = pos.reshape(nb, FOX_QBLOCK)
    o = lax.map(lambda xs: fox_attend(xs[0], k, v, xs[1], c, xs[2], pos), (qb, cb, pb))
    o = o.transpose(1, 0, 2, 3, 4).reshape(B, T, FOX_HW)
    return o @ w_out, k, v, logf.astype(h.dtype)


def fox_mixer_sample(h, k_cache, v_cache, logf_cache, w_in, b_f, w_out):
    B, S, _ = h.shape
    P = k_cache.shape[1]
    q, k, v, logf = fox_project(h, w_in, b_f)
    k_all = jnp.concatenate([k_cache.astype(k.dtype), k], axis=1)
    v_all = jnp.concatenate([v_cache.astype(v.dtype), v], axis=1)
    c = jnp.cumsum(jnp.concatenate([logf_cache.astype(jnp.float32), logf], axis=1), axis=1)
    k_pos = jnp.arange(P + S, dtype=jnp.int32)
    q_pos = P + jnp.arange(S, dtype=jnp.int32)
    o = fox_attend(q, k_all, v_all, c[:, P:], c, q_pos, k_pos).reshape(B, S, FOX_HW)
    return o @ w_out, k, v, logf.astype(h.dtype)


def setup_inputs(seed: int = 0) -> dict:
    key = jax.random.key(seed)
    ks = jax.random.split(key, 24)
    nrm = lambda k, shape, scale: jax.random.normal(k, shape, jnp.float32) * scale
    return {
        "x_prompt": nrm(ks[0], (BATCH, SEQ, D_MODEL), 1.0),
        "x_sample": nrm(ks[1], (DEC_BATCH, DEC_SEQ, D_MODEL), 1.0),
        "state_gla": nrm(ks[2], (N_GLA_LAYERS, DEC_BATCH, GLA_HEADS, GLA_DK, GLA_DV), 0.5),
        "cache_fox_k": nrm(ks[3], (N_FOX_LAYERS, DEC_BATCH, PAST_LEN, FOX_HEADS, FOX_HD), 1.0),
        "cache_fox_v": nrm(ks[4], (N_FOX_LAYERS, DEC_BATCH, PAST_LEN, FOX_HEADS, FOX_HD), 1.0),
        "cache_fox_logf": jax.nn.log_sigmoid(FOX_FORGET_BIAS_INIT + nrm(ks[5], (N_FOX_LAYERS, DEC_BATCH, PAST_LEN, FOX_HEADS), 1.0)),
        "norm_mix": 1.0 + nrm(ks[6], (DEPTH, D_MODEL), 0.02),
        "gla_w_in": nrm(ks[7], (N_GLA_LAYERS, D_MODEL, GLA_IN), D_MODEL ** -0.5),
        "gla_w_g2": nrm(ks[8], (N_GLA_LAYERS, GLA_GATE_RANK, GLA_HK), GLA_GATE_RANK ** -0.5),
        "gla_b_g": nrm(ks[9], (N_GLA_LAYERS, GLA_HK), 0.1),
        "gla_norm": 1.0 + nrm(ks[10], (N_GLA_LAYERS, GLA_HV), 0.02),
        "gla_w_out": nrm(ks[11], (N_GLA_LAYERS, GLA_HV, D_MODEL), GLA_HV ** -0.5),
        "fox_w_in": nrm(ks[12], (N_FOX_LAYERS, D_MODEL, FOX_IN), D_MODEL ** -0.5),
        "fox_b_f": FOX_FORGET_BIAS_INIT + nrm(ks[13], (N_FOX_LAYERS, FOX_HEADS), 0.1),
        "fox_w_out": nrm(ks[14], (N_FOX_LAYERS, FOX_HW, D_MODEL), FOX_HW ** -0.5),
        "norm_ffn": 1.0 + nrm(ks[15], (DEPTH, D_MODEL), 0.02),
        "ffn_w_in": nrm(ks[16], (DEPTH, D_MODEL, 2 * D_FF), D_MODEL ** -0.5),
        "ffn_w_down": nrm(ks[17], (DEPTH, D_FF, D_MODEL), D_FF ** -0.5),
        "norm_final": 1.0 + nrm(ks[18], (D_MODEL,), 0.02),
    }


def reference(x_prompt, x_sample, state_gla, cache_fox_k, cache_fox_v, cache_fox_logf,
              norm_mix, gla_w_in, gla_w_g2, gla_b_g, gla_norm, gla_w_out,
              fox_w_in, fox_b_f, fox_w_out, norm_ffn, ffn_w_in, ffn_w_down, norm_final):
    xp, xs = x_prompt, x_sample
    gla_sp, gla_ss = [], []
    fox_kp, fox_vp, fox_fp, fox_ks, fox_vs, fox_fs = [], [], [], [], [], []
    for i in range(DEPTH):
        j = i // N_MIXERS
        hp = rmsnorm(xp, norm_mix[i])
        hs = rmsnorm(xs, norm_mix[i])
        if i % N_MIXERS == 0:
            s0 = jnp.zeros((xp.shape[0], GLA_HEADS, GLA_DK, GLA_DV), xp.dtype)
            op, sp = gla_mixer(hp, s0, gla_w_in[j], gla_w_g2[j], gla_b_g[j], gla_norm[j], gla_w_out[j])
            os_, ss = gla_mixer(hs, state_gla[j], gla_w_in[j], gla_w_g2[j], gla_b_g[j], gla_norm[j], gla_w_out[j])
            gla_sp.append(sp)
            gla_ss.append(ss)
        else:
            op, kp, vp, fp = fox_mixer_prompt(hp, fox_w_in[j], fox_b_f[j], fox_w_out[j])
            os_, ks_, vs_, fs_ = fox_mixer_sample(hs, cache_fox_k[j], cache_fox_v[j], cache_fox_logf[j],
                                                 fox_w_in[j], fox_b_f[j], fox_w_out[j])
            fox_kp.append(kp); fox_vp.append(vp); fox_fp.append(fp)
            fox_ks.append(ks_); fox_vs.append(vs_); fox_fs.append(fs_)
        xp = xp + op
        xs = xs + os_
        xp = xp + swiglu(rmsnorm(xp, norm_ffn[i]), ffn_w_in[i], ffn_w_down[i])
        xs = xs + swiglu(rmsnorm(xs, norm_ffn[i]), ffn_w_in[i], ffn_w_down[i])
    y_prompt = rmsnorm(xp, norm_final)
    y_sample = rmsnorm(xs, norm_final)
    gla_state_p = jnp.stack(gla_sp, axis=0)
    gla_state_s = jnp.stack(gla_ss, axis=0)
    fox_k_p = jnp.stack(fox_kp, axis=0)
    fox_v_p = jnp.stack(fox_vp, axis=0)
    fox_logf_p = jnp.stack(fox_fp, axis=0)
    fox_k_s = jnp.stack(fox_ks, axis=0)
    fox_v_s = jnp.stack(fox_vs, axis=0)
    fox_logf_s = jnp.stack(fox_fs, axis=0)
    return (y_prompt, y_sample, gla_state_p, fox_k_p, fox_v_p, fox_logf_p,
            gla_state_s, fox_k_s, fox_v_s, fox_logf_s)
```

```python
import functools

import jax
import jax.numpy as jnp
from jax import lax
from jax.experimental import pallas as pl
from jax.experimental.pallas import tpu as pltpu

F32 = jnp.float32
BF16 = jnp.bfloat16

EPS = 1e-6
MASK_VALUE = -1e30

GLA_HEADS = 4
GLA_CHUNK = 64
GLA_GATE_TAU = 16.0
FOX_HEADS = 16
FOX_GROUP = 4

LANES = 128
VMEM_LIMIT_BYTES = 56 * 1024 * 1024


def _params(n_grid):
    return pltpu.CompilerParams(
        dimension_semantics=("arbitrary",) * n_grid,
        vmem_limit_bytes=VMEM_LIMIT_BYTES,
    )


def _resident(shape):
    nd = len(shape)
    return pl.BlockSpec(shape, lambda *_: (0,) * nd, pipeline_mode=pl.Buffered(1))


def _dot(a, b):
    return jnp.dot(a, b, preferred_element_type=F32)


def _dot_nt(a, b):
    return lax.dot_general(a, b, (((1,), (1,)), ((), ())), preferred_element_type=F32)


def _dot_tn(a, b):
    return lax.dot_general(a, b, (((0,), (0,)), ((), ())), preferred_element_type=F32)


def _split3(x):
    hi = x.astype(BF16)
    r1 = x - hi.astype(F32)
    mid = r1.astype(BF16)
    lo = (r1 - mid.astype(F32)).astype(BF16)
    return hi, mid, lo


def _sum01(dot_fn, x, ones_first, mat01):
    acc = None
    for part in _split3(x):
        term = dot_fn(mat01, part) if ones_first else dot_fn(part, mat01)
        acc = term if acc is None else acc + term
    return acc


def _rmsnorm(x, g):
    var = jnp.mean(x * x, axis=-1, keepdims=True)
    return x * lax.rsqrt(var + EPS) * g


def _log_sigmoid(z):
    return jnp.minimum(z, 0.0) - jnp.log1p(jnp.exp(-jnp.abs(z)))


def _silu(z):
    return z * jax.nn.sigmoid(z)


def _gla_proj_kernel(x_ref, g_ref, w_ref, wgl_ref, wg2_ref, bg_ref, proj_ref, glog_ref):
    h = _rmsnorm(x_ref[...], g_ref[...]).astype(BF16)
    proj_ref[...] = _dot(h, w_ref[...])
    gl = _dot(h, wgl_ref[...]).astype(BF16)
    z = _dot(gl, wg2_ref[...]) + bg_ref[...]
    glog_ref[...] = _log_sigmoid(z) / GLA_GATE_TAU


def _gla_proj(x2d, g, w_main, w_gl, w_g2, b_g, tm):
    n, d = x2d.shape
    n_main, hk = w_main.shape[1], w_g2.shape[1]
    return pl.pallas_call(
        _gla_proj_kernel,
        grid=(n // tm,),
        in_specs=[
            pl.BlockSpec((tm, d), lambda i: (i, 0)),
            _resident((1, d)),
            _resident(w_main.shape),
            _resident(w_gl.shape),
            _resident(w_g2.shape),
            _resident((1, hk)),
        ],
        out_specs=[
            pl.BlockSpec((tm, n_main), lambda i: (i, 0)),
            pl.BlockSpec((tm, hk), lambda i: (i, 0)),
        ],
        out_shape=[
            jax.ShapeDtypeStruct((n, n_main), F32),
            jax.ShapeDtypeStruct((n, hk), F32),
        ],
        compiler_params=_params(1),
        name="gla_proj",
    )(x2d, g, w_main, w_gl, w_g2, b_g)


def _gla_mix_kernel(q_ref, k_ref, v_ref, r_ref, glog_ref, s0_ref, ng_ref,
                    og_ref, sfin_ref, s_ref, *, tb, dk, dv):
    t = pl.program_id(1)
    n_chunk = tb // GLA_CHUNK

    @pl.when(t == 0)
    def _():
        s_ref[...] = s0_ref[...]

    row = lax.broadcasted_iota(jnp.int32, (tb, tb), 0)
    col = lax.broadcasted_iota(jnp.int32, (tb, tb), 1)
    same_chunk = (row // GLA_CHUNK) == (col // GLA_CHUNK)
    causal = same_chunk & (col <= row)
    cum_mat = causal.astype(BF16)
    tot_mat = same_chunk.astype(BF16)
    row_c = lax.broadcasted_iota(jnp.int32, (tb, LANES), 0)
    col_c = lax.broadcasted_iota(jnp.int32, (tb, LANES), 1)
    chunk_sel = ((row_c // GLA_CHUNK) == col_c).astype(BF16)

    g = glog_ref[...]
    b = _sum01(_dot, g, True, cum_mat)
    b_last = _sum01(_dot, g, True, tot_mat)
    dec_t = jnp.exp(_sum01(_dot_tn, g, False, chunk_sel))
    q = q_ref[...]
    k = k_ref[...]
    qe = (q * jnp.exp(b) * (dk ** -0.5)).astype(BF16)
    ke = (k * jnp.exp(-b)).astype(BF16)
    kd = (k * jnp.exp(b_last - b)).astype(BF16)

    for h in range(GLA_HEADS):
        ks = slice(h * dk, (h + 1) * dk)
        vs = slice(h * dv, (h + 1) * dv)
        qe_h, ke_h, kd_h = qe[:, ks], ke[:, ks], kd[:, ks]
        v_h = v_ref[:, vs].astype(BF16)
        a = jnp.where(causal, _dot_nt(qe_h, ke_h), 0.0).astype(BF16)
        o_intra = _dot(a, v_h)
        s = s_ref[h]
        o_parts = []
        for c in range(n_chunk):
            rs = slice(c * GLA_CHUNK, (c + 1) * GLA_CHUNK)
            o_parts.append(o_intra[rs] + _dot(qe_h[rs], s.astype(BF16)))
            s = s * dec_t[ks, c:c + 1] + _dot_tn(kd_h[rs], v_h[rs])
        s_ref[h] = s
        o = o_parts[0] if n_chunk == 1 else jnp.concatenate(o_parts, axis=0)
        on = _rmsnorm(o, ng_ref[:, vs])
        og_ref[:, vs] = (on * _silu(r_ref[:, vs])).astype(BF16)

    @pl.when(t == pl.num_programs(1) - 1)
    def _():
        sfin_ref[...] = s_ref[...]


def _gla_mix(proj3, glog3, s0, norm_g, tb):
    bsz, t_len, _ = proj3.shape
    _, n_head, dk, dv = s0.shape
    hk, hv = n_head * dk, n_head * dv
    assert t_len % tb == 0 and tb % GLA_CHUNK == 0 and hv == 2 * hk
    kern = functools.partial(_gla_mix_kernel, tb=tb, dk=dk, dv=dv)
    state_spec = pl.BlockSpec((None, n_head, dk, dv), lambda b, t: (b, 0, 0, 0))
    return pl.pallas_call(
        kern,
        grid=(bsz, t_len // tb),
        in_specs=[
            pl.BlockSpec((None, tb, hk), lambda b, t: (b, t, 0)),
            pl.BlockSpec((None, tb, hk), lambda b, t: (b, t, 1)),
            pl.BlockSpec((None, tb, hv), lambda b, t: (b, t, 1)),
            pl.BlockSpec((None, tb, hv), lambda b, t: (b, t, 2)),
            pl.BlockSpec((None, tb, hk), lambda b, t: (b, t, 0)),
            state_spec,
            _resident((1, hv)),
        ],
        out_specs=[
            pl.BlockSpec((None, tb, hv), lambda b, t: (b, t, 0)),
            state_spec,
        ],
        out_shape=[
            jax.ShapeDtypeStruct((bsz, t_len, hv), BF16),
            jax.ShapeDtypeStruct(s0.shape, F32),
        ],
        scratch_shapes=[pltpu.VMEM((n_head, dk, dv), F32)],
        compiler_params=_params(2),
        name="gla_mix",
    )(proj3, proj3, proj3, proj3, glog3, s0, norm_g)


def _post_kernel(x_ref, o_ref, wo_ref, g_ref, win_ref, wdown_ref, gfin_ref, y_ref, act_ref,
                 *, d_ff, ff_tile, final_norm):
    x1 = x_ref[...] + _dot(o_ref[...], wo_ref[...])
    h = _rmsnorm(x1, g_ref[...]).astype(BF16)
    for j in range(d_ff // ff_tile):
        gate = _dot(h, win_ref[:, j * ff_tile:(j + 1) * ff_tile])
        up = _dot(h, win_ref[:, d_ff + j * ff_tile:d_ff + (j + 1) * ff_tile])
        act_ref[:, j * ff_tile:(j + 1) * ff_tile] = (_silu(gate) * up).astype(BF16)
    y = x1 + _dot(act_ref[...], wdown_ref[...])
    if final_norm:
        y = _rmsnorm(y, gfin_ref[...])
    y_ref[...] = y


def _post(x2d, o2d, w_out, g_ffn, w_in, w_down, g_final, tm, final_norm):
    n, d = x2d.shape
    d_ff = w_down.shape[0]
    ff_tile = 256
    assert d_ff % ff_tile == 0 and n % tm == 0
    kern = functools.partial(_post_kernel, d_ff=d_ff, ff_tile=ff_tile, final_norm=final_norm)
    return pl.pallas_call(
        kern,
        grid=(n // tm,),
        in_specs=[
            pl.BlockSpec((tm, d), lambda i: (i, 0)),
            pl.BlockSpec((tm, o2d.shape[1]), lambda i: (i, 0)),
            _resident(w_out.shape),
            _resident((1, d)),
            _resident(w_in.shape),
            _resident(w_down.shape),
            _resident((1, d)),
        ],
        out_specs=pl.BlockSpec((tm, d), lambda i: (i, 0)),
        out_shape=jax.ShapeDtypeStruct((n, d), F32),
        scratch_shapes=[pltpu.VMEM((tm, d_ff), BF16)],
        compiler_params=_params(1),
        name="post_final" if final_norm else "post",
    )(x2d, o2d, w_out, g_ffn, w_in, w_down, g_final)


def _lane_cumsum_mat(n):
    r = lax.broadcasted_iota(jnp.int32, (n, n), 0)
    c = lax.broadcasted_iota(jnp.int32, (n, n), 1)
    return (r <= c).astype(BF16)


def _fox_proj_kernel(*refs, tm, hw, n_head, scale, cache_len, cache_tile):
    has_cache = cache_len > 0
    if has_cache:
        (x_ref, g_ref, w_ref, wf_ref, wft_ref, bfr_ref, bfc_ref, lc_ref, lct_ref,
         q_ref, k_ref, v_ref, logf_ref, c_ref, ct_ref, ctc_ref, crow_ref, ccol_ref) = refs
    else:
        (x_ref, g_ref, w_ref, wf_ref, wft_ref, bfr_ref, bfc_ref,
         q_ref, k_ref, v_ref, logf_ref, c_ref, ct_ref, crow_ref, ccol_ref) = refs
    t = pl.program_id(1)

    @pl.when(t == 0)
    def _():
        if has_cache:
            mat = _lane_cumsum_mat(cache_tile)
            carry = jnp.zeros((n_head, 1), F32)
            for j in range(cache_len // cache_tile):
                cs = slice(j * cache_tile, (j + 1) * cache_tile)
                blk = carry + _sum01(_dot, lct_ref[:, cs], False, mat)
                ctc_ref[:, cs] = blk
                carry = blk[:, cache_tile - 1:cache_tile]
            ccol_ref[...] = jnp.broadcast_to(carry, ccol_ref.shape)
            crow_ref[...] = jnp.sum(lc_ref[...], axis=0, keepdims=True)
        else:
            ccol_ref[...] = jnp.zeros_like(ccol_ref)
            crow_ref[...] = jnp.zeros_like(crow_ref)

    h = _rmsnorm(x_ref[...], g_ref[...]).astype(BF16)
    q_ref[...] = _dot(h, w_ref[:, 0:hw]) * scale
    k_ref[...] = _dot(h, w_ref[:, hw:2 * hw])
    v_ref[...] = _dot(h, w_ref[:, 2 * hw:3 * hw])

    logf = _log_sigmoid(_dot(h, wf_ref[...]) + bfr_ref[...])
    logf_ref[...] = logf[:, :n_head]
    row = lax.broadcasted_iota(jnp.int32, (tm, tm), 0)
    col = lax.broadcasted_iota(jnp.int32, (tm, tm), 1)
    c_blk = _sum01(_dot, logf, True, (col <= row).astype(BF16))[:, :n_head] + crow_ref[...]
    c_ref[...] = c_blk
    crow_ref[...] = c_blk[tm - 1:tm, :]

    logf_t = _log_sigmoid(_dot_nt(wft_ref[...], h) + bfc_ref[...])
    ct_blk = _sum01(_dot, logf_t, False, _lane_cumsum_mat(tm)) + ccol_ref[:, 0:1]
    ct_ref[...] = ct_blk
    ccol_ref[...] = jnp.broadcast_to(ct_blk[:, tm - 1:tm], ccol_ref.shape)


def _fox_proj(x3, g, w_main, w_f, w_ft, bf_row, bf_col, logf_cache, logf_cache_t, tm, n_head):
    bsz, t_len, d = x3.shape
    hw = w_main.shape[1] // 3
    cache_len = 0 if logf_cache is None else logf_cache.shape[1]
    assert t_len % tm == 0
    kern = functools.partial(_fox_proj_kernel, tm=tm, hw=hw, n_head=n_head,
                             scale=(hw // n_head) ** -0.5, cache_len=cache_len,
                             cache_tile=min(cache_len, 256))
    in_specs = [
        pl.BlockSpec((None, tm, d), lambda b, t: (b, t, 0)),
        _resident((1, d)),
        _resident(w_main.shape),
        _resident(w_f.shape),
        _resident(w_ft.shape),
        _resident(bf_row.shape),
        _resident(bf_col.shape),
    ]
    args = [x3, g, w_main, w_f, w_ft, bf_row, bf_col]
    tok = pl.BlockSpec((None, tm, hw), lambda b, t: (b, t, 0))
    out_specs = [
        tok, tok, tok,
        pl.BlockSpec((None, tm, n_head), lambda b, t: (b, t, 0)),
        pl.BlockSpec((None, tm, n_head), lambda b, t: (b, t, 0)),
        pl.BlockSpec((None, n_head, tm), lambda b, t: (b, 0, t)),
    ]
    out_shape = [
        jax.ShapeDtypeStruct((bsz, t_len, hw), F32),
        jax.ShapeDtypeStruct((bsz, t_len, hw), F32),
        jax.ShapeDtypeStruct((bsz, t_len, hw), F32),
        jax.ShapeDtypeStruct((bsz, t_len, n_head), F32),
        jax.ShapeDtypeStruct((bsz, t_len, n_head), F32),
        jax.ShapeDtypeStruct((bsz, n_head, t_len), F32),
    ]
    if cache_len:
        in_specs += [
            pl.BlockSpec((None, cache_len, n_head), lambda b, t: (b, 0, 0)),
            pl.BlockSpec((None, n_head, cache_len), lambda b, t: (b, 0, 0)),
        ]
        args += [logf_cache, logf_cache_t]
        out_specs.append(pl.BlockSpec((None, n_head, cache_len), lambda b, t: (b, 0, 0)))
        out_shape.append(jax.ShapeDtypeStruct((bsz, n_head, cache_len), F32))
    return pl.pallas_call(
        kern,
        grid=(bsz, t_len // tm),
        in_specs=in_specs,
        out_specs=out_specs,
        out_shape=out_shape,
        scratch_shapes=[pltpu.VMEM((1, n_head), F32), pltpu.VMEM((n_head, LANES), F32)],
        compiler_params=_params(2),
        name="fox_proj_cache" if cache_len else "fox_proj",
    )(*args)


def _attn_init(q_ref, c_ref, qm_ref, cq_ref, m_ref, l_ref, acc_ref, hg, tq, hd):
    q = q_ref[...]
    lane = lax.broadcasted_iota(jnp.int32, q.shape, 1) // hd
    c = c_ref[...]
    head = lax.broadcasted_iota(jnp.int32, c.shape, 1)
    for h in range(FOX_GROUP):
        rows = slice(h * tq, (h + 1) * tq)
        qm_ref[rows, :] = jnp.where(lane == h, q, 0.0).astype(BF16)
        cq_ref[rows, :] = jnp.sum(jnp.where(head == hg * FOX_GROUP + h, c, 0.0), axis=1, keepdims=True)
    m_ref[...] = jnp.full_like(m_ref, MASK_VALUE)
    l_ref[...] = jnp.zeros_like(l_ref)
    acc_ref[...] = jnp.zeros_like(acc_ref)


def _attn_block(kb, vb, ck_rows, mask, qm_ref, cq_ref, m_ref, l_ref, alpha_ref, acc_ref, p_ref, tq):
    s = _dot_nt(qm_ref[...], kb.astype(BF16))
    for h in range(FOX_GROUP):
        rows = slice(h * tq, (h + 1) * tq)
        sp = s[rows] - ck_rows[h]
        if mask is not None:
            sp = jnp.where(mask, sp, MASK_VALUE)
        cq = cq_ref[rows, :]
        m_old = m_ref[rows, :]
        m_new = jnp.maximum(m_old, jnp.max(sp, axis=1, keepdims=True) + cq)
        p = jnp.exp(sp + (cq - m_new))
        alpha = jnp.exp(m_old - m_new)
        l_ref[rows, :] = alpha * l_ref[rows, :] + jnp.sum(p, axis=1, keepdims=True)
        m_ref[rows, :] = m_new
        alpha_ref[rows, :] = alpha
        p_ref[rows, :] = p.astype(BF16)
    acc_ref[...] = acc_ref[...] * alpha_ref[...] + _dot(p_ref[...], vb.astype(BF16))


def _attn_finish(o_ref, l_ref, acc_ref, tq, hd):
    lane = lax.broadcasted_iota(jnp.int32, (tq, FOX_GROUP * hd), 1) // hd
    out = jnp.zeros((tq, FOX_GROUP * hd), F32)
    for h in range(FOX_GROUP):
        rows = slice(h * tq, (h + 1) * tq)
        out = jnp.where(lane == h, acc_ref[rows, :] / l_ref[rows, :], out)
    o_ref[...] = out.astype(o_ref.dtype)


def _fox_attn_prompt_kernel(q_ref, k_ref, v_ref, c_ref, ct_ref, o_ref,
                            qm_ref, cq_ref, m_ref, l_ref, alpha_ref, acc_ref, p_ref, *, tq, hd):
    hg = pl.program_id(1)
    i = pl.program_id(2)
    _attn_init(q_ref, c_ref, qm_ref, cq_ref, m_ref, l_ref, acc_ref, hg, tq, hd)
    state = (qm_ref, cq_ref, m_ref, l_ref, alpha_ref, acc_ref, p_ref, tq)

    def kv_block(j, mask):
        ks = pl.ds(pl.multiple_of(j * tq, tq), tq)
        ck_rows = [ct_ref[h:h + 1, ks] for h in range(FOX_GROUP)]
        _attn_block(k_ref[ks, :], v_ref[ks, :], ck_rows, mask, *state)

    def body(j, carry):
        kv_block(j, None)
        return carry

    lax.fori_loop(0, i, body, 0)
    row = lax.broadcasted_iota(jnp.int32, (tq, tq), 0)
    col = lax.broadcasted_iota(jnp.int32, (tq, tq), 1)
    kv_block(i, col <= row)
    _attn_finish(o_ref, l_ref, acc_ref, tq, hd)


def _attn_scratch(tq, tk, width):
    rows = FOX_GROUP * tq
    return [
        pltpu.VMEM((rows, width), BF16),
        pltpu.VMEM((rows, 1), F32),
        pltpu.VMEM((rows, 1), F32),
        pltpu.VMEM((rows, 1), F32),
        pltpu.VMEM((rows, 1), F32),
        pltpu.VMEM((rows, width), F32),
        pltpu.VMEM((rows, tk), BF16),
    ]


def _fox_attn_prompt(q, k, v, c, ct, tq, n_head):
    bsz, t_len, hw = q.shape
    hd = hw // n_head
    width = FOX_GROUP * hd
    n_group = n_head // FOX_GROUP
    ct4 = ct.reshape(bsz, n_group, FOX_GROUP, t_len)
    kern = functools.partial(_fox_attn_prompt_kernel, tq=tq, hd=hd)
    return pl.pallas_call(
        kern,
        grid=(bsz, n_group, t_len // tq),
        in_specs=[
            pl.BlockSpec((None, tq, width), lambda b, g, i: (b, i, g)),
            pl.BlockSpec((None, t_len, width), lambda b, g, i: (b, 0, g)),
            pl.BlockSpec((None, t_len, width), lambda b, g, i: (b, 0, g)),
            pl.BlockSpec((None, tq, n_head), lambda b, g, i: (b, i, 0)),
            pl.BlockSpec((None, None, FOX_GROUP, t_len), lambda b, g, i: (b, g, 0, 0)),
        ],
        out_specs=pl.BlockSpec((None, tq, width), lambda b, g, i: (b, i, g)),
        out_shape=jax.ShapeDtypeStruct((bsz, t_len, hw), BF16),
        scratch_shapes=_attn_scratch(tq, tq, width),
        compiler_params=_params(3),
        name="fox_attn_prompt",
    )(q, k, v, c, ct4)


def _fox_attn_sample_kernel(q_ref, kc_ref, vc_ref, kn_ref, vn_ref, c_ref, ctc_ref, ctn_ref, o_ref,
                            qm_ref, cq_ref, m_ref, l_ref, alpha_ref, acc_ref, p_ref, pn_ref,
                            *, tq, tk, hd, cache_len):
    hg = pl.program_id(1)
    _attn_init(q_ref, c_ref, qm_ref, cq_ref, m_ref, l_ref, acc_ref, hg, tq, hd)
    for j in range(cache_len // tk):
        ks = slice(j * tk, (j + 1) * tk)
        ck_rows = [ctc_ref[h:h + 1, ks] for h in range(FOX_GROUP)]
        _attn_block(kc_ref[ks, :], vc_ref[ks, :], ck_rows, None,
                    qm_ref, cq_ref, m_ref, l_ref, alpha_ref, acc_ref, p_ref, tq)
    row = lax.broadcasted_iota(jnp.int32, (tq, tq), 0)
    col = lax.broadcasted_iota(jnp.int32, (tq, tq), 1)
    ck_rows = [ctn_ref[h:h + 1, :] for h in range(FOX_GROUP)]
    _attn_block(kn_ref[...], vn_ref[...], ck_rows, col <= row,
                qm_ref, cq_ref, m_ref, l_ref, alpha_ref, acc_ref, pn_ref, tq)
    _attn_finish(o_ref, l_ref, acc_ref, tq, hd)


def _fox_attn_sample(q, k_new, v_new, k_cache, v_cache, c_new, ct_cache, ct_new, n_head):
    bsz, tq, hw = q.shape
    cache_len = k_cache.shape[1]
    hd = hw // n_head
    width = FOX_GROUP * hd
    n_group = n_head // FOX_GROUP
    tk = 256
    assert cache_len % tk == 0
    ctc4 = ct_cache.reshape(bsz, n_group, FOX_GROUP, cache_len)
    ctn4 = ct_new.reshape(bsz, n_group, FOX_GROUP, tq)
    kern = functools.partial(_fox_attn_sample_kernel, tq=tq, tk=tk, hd=hd, cache_len=cache_len)
    new_spec = pl.BlockSpec((None, tq, width), lambda b, g: (b, 0, g))
    cache_spec = pl.BlockSpec((None, cache_len, width), lambda b, g: (b, 0, g))
    return pl.pallas_call(
        kern,
        grid=(bsz, n_group),
        in_specs=[
            new_spec, cache_spec, cache_spec, new_spec, new_spec,
            pl.BlockSpec((None, tq, n_head), lambda b, g: (b, 0, 0)),
            pl.BlockSpec((None, None, FOX_GROUP, cache_len), lambda b, g: (b, g, 0, 0)),
            pl.BlockSpec((None, None, FOX_GROUP, tq), lambda b, g: (b, g, 0, 0)),
        ],
        out_specs=new_spec,
        out_shape=jax.ShapeDtypeStruct((bsz, tq, hw), BF16),
        scratch_shapes=_attn_scratch(tq, tk, width) + [pltpu.VMEM((FOX_GROUP * tq, tq), BF16)],
        compiler_params=_params(2),
        name="fox_attn_sample",
    )(q, k_cache, v_cache, k_new, v_new, c_new, ctc4, ctn4)


def _pad_to(a, shape):
    return jnp.pad(a, [(0, s - d) for s, d in zip(shape, a.shape)])


def kernel(x_prompt, x_sample, state_gla, cache_fox_k, cache_fox_v, cache_fox_logf,
           norm_mix, gla_w_in, gla_w_g2, gla_b_g, gla_norm, gla_w_out,
           fox_w_in, fox_b_f, fox_w_out, norm_ffn, ffn_w_in, ffn_w_down, norm_final):
    d = x_prompt.shape[-1]
    depth = norm_mix.shape[0]
    groups = [x_prompt, x_sample]
    shapes = [x.shape for x in groups]
    xs = [x.reshape(-1, d) for x in groups]
    row_tiles = [min(512, x.shape[0]) for x in xs]

    gla_states = [[], []]
    fox_k, fox_v, fox_f = [[], []], [[], []], [[], []]
    for i in range(depth):
        j = i // 2
        g_mix = norm_mix[i].reshape(1, d)
        g_ffn = norm_ffn[i].reshape(1, d)
        w_ffn_in = ffn_w_in[i].astype(BF16)
        w_ffn_down = ffn_w_down[i].astype(BF16)
        last = i == depth - 1
        if i % 2 == 0:
            _, n_head, dk, dv = state_gla.shape[1:]
            hk, hv = n_head * dk, n_head * dv
            n_main = 2 * hk + 2 * hv
            w_in = gla_w_in[j]
            w_main = w_in[:, :n_main].astype(BF16)
            w_gl = _pad_to(w_in[:, n_main:], (d, LANES)).astype(BF16)
            w_g2 = _pad_to(gla_w_g2[j], (LANES, hk)).astype(BF16)
            b_g = gla_b_g[j].reshape(1, hk)
            w_out = gla_w_out[j].astype(BF16)
            norm_g = gla_norm[j].reshape(1, hv)
            s0s = [jnp.zeros((shapes[0][0], n_head, dk, dv), F32), state_gla[j]]
            for gi in range(2):
                bsz, t_len, _ = shapes[gi]
                proj, glog = _gla_proj(xs[gi], g_mix, w_main, w_gl, w_g2, b_g, row_tiles[gi])
                og, s_fin = _gla_mix(proj.reshape(bsz, t_len, n_main), glog.reshape(bsz, t_len, hk),
                                     s0s[gi], norm_g, min(t_len, 256))
                gla_states[gi].append(s_fin)
                xs[gi] = _post(xs[gi], og.reshape(-1, hv), w_out, g_ffn, w_ffn_in, w_ffn_down,
                               norm_final.reshape(1, d), row_tiles[gi], last)
        else:
            n_head = fox_b_f.shape[1]
            hw = fox_w_out.shape[1]
            w_in = fox_w_in[j]
            w_main = w_in[:, :3 * hw].astype(BF16)
            w_f = _pad_to(w_in[:, 3 * hw:], (d, LANES)).astype(BF16)
            w_ft = w_in[:, 3 * hw:].T.astype(BF16)
            bf_row = _pad_to(fox_b_f[j].reshape(1, n_head), (1, LANES))
            bf_col = fox_b_f[j].reshape(n_head, 1)
            w_out = fox_w_out[j].astype(BF16)
            for gi in range(2):
                bsz, t_len, _ = shapes[gi]
                x3 = xs[gi].reshape(bsz, t_len, d)
                if gi == 0:
                    q, k, v, logf, c, ct = _fox_proj(x3, g_mix, w_main, w_f, w_ft, bf_row, bf_col,
                                                     None, None, min(512, t_len), n_head)
                    o = _fox_attn_prompt(q, k, v, c, ct, 256, n_head)
                else:
                    lc = cache_fox_logf[j]
                    q, k, v, logf, c, ct, ct_cache = _fox_proj(
                        x3, g_mix, w_main, w_f, w_ft, bf_row, bf_col,
                        lc, jnp.transpose(lc, (0, 2, 1)), t_len, n_head)
                    cache_len = lc.shape[1]
                    o = _fox_attn_sample(q, k, v, cache_fox_k[j].reshape(bsz, cache_len, hw),
                                         cache_fox_v[j].reshape(bsz, cache_len, hw),
                                         c, ct_cache, ct, n_head)
                fox_k[gi].append(k.reshape(bsz, t_len, n_head, hw // n_head))
                fox_v[gi].append(v.reshape(bsz, t_len, n_head, hw // n_head))
                fox_f[gi].append(logf)
                xs[gi] = _post(xs[gi], o.reshape(-1, hw), w_out, g_ffn, w_ffn_in, w_ffn_down,
                               norm_final.reshape(1, d), row_tiles[gi], last)

    y_prompt = xs[0].reshape(shapes[0])
    y_sample = xs[1].reshape(shapes[1])
    st = lambda parts: jnp.stack(parts, axis=0)
    return (y_prompt, y_sample, st(gla_states[0]), st(fox_k[0]), st(fox_v[0]), st(fox_f[0]),
            st(gla_states[1]), st(fox_k[1]), st(fox_v[1]), st(fox_f[1]))
```

```python
import functools

import jax
import jax.numpy as jnp
from jax import lax
from jax.experimental import pallas as pl
from jax.experimental.pallas import tpu as pltpu

F32 = jnp.float32
BF16 = jnp.bfloat16

EPS = 1e-6
MASK_VALUE = -1e30

GLA_HEADS = 4
GLA_CHUNK = 64
GLA_GATE_TAU = 16.0

LANES = 128
VMEM_LIMIT_BYTES = 56 * 1024 * 1024


def _params(n_grid):
    return pltpu.CompilerParams(
        dimension_semantics=("arbitrary",) * n_grid,
        vmem_limit_bytes=VMEM_LIMIT_BYTES,
    )


def _resident(shape):
    nd = len(shape)
    return pl.BlockSpec(shape, lambda *_: (0,) * nd, pipeline_mode=pl.Buffered(1))


def _dot(a, b):
    return jnp.dot(a, b, preferred_element_type=F32)


def _dot_nt(a, b):
    return lax.dot_general(a, b, (((1,), (1,)), ((), ())), preferred_element_type=F32)


def _dot_tn(a, b):
    return lax.dot_general(a, b, (((0,), (0,)), ((), ())), preferred_element_type=F32)


def _split3(x):
    hi = x.astype(BF16)
    r1 = x - hi.astype(F32)
    mid = r1.astype(BF16)
    lo = (r1 - mid.astype(F32)).astype(BF16)
    return hi, mid, lo


def _sum01(dot_fn, x, ones_first, mat01):
    acc = None
    for part in _split3(x):
        term = dot_fn(mat01, part) if ones_first else dot_fn(part, mat01)
        acc = term if acc is None else acc + term
    return acc


def _rmsnorm(x, g):
    var = jnp.mean(x * x, axis=-1, keepdims=True)
    return x * lax.rsqrt(var + EPS) * g


def _log_sigmoid(z):
    return jnp.minimum(z, 0.0) - jnp.log1p(jnp.exp(-jnp.abs(z)))


def _silu(z):
    return z * jax.nn.sigmoid(z)


def _gla_proj_kernel(x_ref, g_ref, w_ref, wgl_ref, wg2_ref, bg_ref, proj_ref, glog_ref):
    h = _rmsnorm(x_ref[...], g_ref[...]).astype(BF16)
    proj_ref[...] = _dot(h, w_ref[...])
    gl = _dot(h, wgl_ref[...]).astype(BF16)
    z = _dot(gl, wg2_ref[...]) + bg_ref[...]
    glog_ref[...] = _log_sigmoid(z) / GLA_GATE_TAU


def _gla_proj(x2d, g, w_main, w_gl, w_g2, b_g, tm):
    n, d = x2d.shape
    n_main, hk = w_main.shape[1], w_g2.shape[1]
    return pl.pallas_call(
        _gla_proj_kernel,
        grid=(n // tm,),
        in_specs=[
            pl.BlockSpec((tm, d), lambda i: (i, 0)),
            _resident((1, d)),
            _resident(w_main.shape),
            _resident(w_gl.shape),
            _resident(w_g2.shape),
            _resident((1, hk)),
        ],
        out_specs=[
            pl.BlockSpec((tm, n_main), lambda i: (i, 0)),
            pl.BlockSpec((tm, hk), lambda i: (i, 0)),
        ],
        out_shape=[
            jax.ShapeDtypeStruct((n, n_main), F32),
            jax.ShapeDtypeStruct((n, hk), F32),
        ],
        compiler_params=_params(1),
        name="gla_proj",
    )(x2d, g, w_main, w_gl, w_g2, b_g)


def _gla_mix_kernel(q_ref, k_ref, v_ref, r_ref, glog_ref, s0_ref, ng_ref,
                    og_ref, sfin_ref, s_ref, *, tb, dk, dv):
    t = pl.program_id(1)
    n_chunk = tb // GLA_CHUNK

    @pl.when(t == 0)
    def _():
        s_ref[...] = s0_ref[...]

    row = lax.broadcasted_iota(jnp.int32, (tb, tb), 0)
    col = lax.broadcasted_iota(jnp.int32, (tb, tb), 1)
    same_chunk = (row // GLA_CHUNK) == (col // GLA_CHUNK)
    causal = same_chunk & (col <= row)
    cum_mat = causal.astype(BF16)
    tot_mat = same_chunk.astype(BF16)
    row_c = lax.broadcasted_iota(jnp.int32, (tb, LANES), 0)
    col_c = lax.broadcasted_iota(jnp.int32, (tb, LANES), 1)
    chunk_sel = ((row_c // GLA_CHUNK) == col_c).astype(BF16)

    g = glog_ref[...]
    b = _sum01(_dot, g, True, cum_mat)
    b_last = _sum01(_dot, g, True, tot_mat)
    dec_t = jnp.exp(_sum01(_dot_tn, g, False, chunk_sel))
    q = q_ref[...]
    k = k_ref[...]
    qe = (q * jnp.exp(b) * (dk ** -0.5)).astype(BF16)
    ke = (k * jnp.exp(-b)).astype(BF16)
    kd = (k * jnp.exp(b_last - b)).astype(BF16)

    for h in range(GLA_HEADS):
        ks = slice(h * dk, (h + 1) * dk)
        vs = slice(h * dv, (h + 1) * dv)
        qe_h, ke_h, kd_h = qe[:, ks], ke[:, ks], kd[:, ks]
        v_h = v_ref[:, vs].astype(BF16)
        a = jnp.where(causal, _dot_nt(qe_h, ke_h), 0.0).astype(BF16)
        o_intra = _dot(a, v_h)
        s = s_ref[h]
        o_parts = []
        for c in range(n_chunk):
            rs = slice(c * GLA_CHUNK, (c + 1) * GLA_CHUNK)
            o_parts.append(o_intra[rs] + _dot(qe_h[rs], s.astype(BF16)))
            s = s * dec_t[ks, c:c + 1] + _dot_tn(kd_h[rs], v_h[rs])
        s_ref[h] = s
        o = o_parts[0] if n_chunk == 1 else jnp.concatenate(o_parts, axis=0)
        on = _rmsnorm(o, ng_ref[:, vs])
        og_ref[:, vs] = (on * _silu(r_ref[:, vs])).astype(BF16)

    @pl.when(t == pl.num_programs(1) - 1)
    def _():
        sfin_ref[...] = s_ref[...]


def _gla_mix(proj3, glog3, s0, norm_g, tb):
    bsz, t_len, _ = proj3.shape
    _, n_head, dk, dv = s0.shape
    hk, hv = n_head * dk, n_head * dv
    assert t_len % tb == 0 and tb % GLA_CHUNK == 0 and hv == 2 * hk
    kern = functools.partial(_gla_mix_kernel, tb=tb, dk=dk, dv=dv)
    state_spec = pl.BlockSpec((None, n_head, dk, dv), lambda b, t: (b, 0, 0, 0))
    return pl.pallas_call(
        kern,
        grid=(bsz, t_len // tb),
        in_specs=[
            pl.BlockSpec((None, tb, hk), lambda b, t: (b, t, 0)),
            pl.BlockSpec((None, tb, hk), lambda b, t: (b, t, 1)),
            pl.BlockSpec((None, tb, hv), lambda b, t: (b, t, 1)),
            pl.BlockSpec((None, tb, hv), lambda b, t: (b, t, 2)),
            pl.BlockSpec((None, tb, hk), lambda b, t: (b, t, 0)),
            state_spec,
            _resident((1, hv)),
        ],
        out_specs=[
            pl.BlockSpec((None, tb, hv), lambda b, t: (b, t, 0)),
            state_spec,
        ],
        out_shape=[
            jax.ShapeDtypeStruct((bsz, t_len, hv), BF16),
            jax.ShapeDtypeStruct(s0.shape, F32),
        ],
        scratch_shapes=[pltpu.VMEM((n_head, dk, dv), F32)],
        compiler_params=_params(2),
        name="gla_mix",
    )(proj3, proj3, proj3, proj3, glog3, s0, norm_g)


def _post_kernel(x_ref, o_ref, wo_ref, g_ref, win_ref, wdown_ref, gfin_ref, y_ref, act_ref,
                 *, d_ff, ff_tile, final_norm):
    x1 = x_ref[...] + _dot(o_ref[...], wo_ref[...])
    h = _rmsnorm(x1, g_ref[...]).astype(BF16)
    for j in range(d_ff // ff_tile):
        gate = _dot(h, win_ref[:, j * ff_tile:(j + 1) * ff_tile])
        up = _dot(h, win_ref[:, d_ff + j * ff_tile:d_ff + (j + 1) * ff_tile])
        act_ref[:, j * ff_tile:(j + 1) * ff_tile] = (_silu(gate) * up).astype(BF16)
    y = x1 + _dot(act_ref[...], wdown_ref[...])
    if final_norm:
        y = _rmsnorm(y, gfin_ref[...])
    y_ref[...] = y


def _post(x2d, o2d, w_out, g_ffn, w_in, w_down, g_final, tm, final_norm):
    n, d = x2d.shape
    d_ff = w_down.shape[0]
    ff_tile = 256
    assert d_ff % ff_tile == 0 and n % tm == 0
    kern = functools.partial(_post_kernel, d_ff=d_ff, ff_tile=ff_tile, final_norm=final_norm)
    return pl.pallas_call(
        kern,
        grid=(n // tm,),
        in_specs=[
            pl.BlockSpec((tm, d), lambda i: (i, 0)),
            pl.BlockSpec((tm, o2d.shape[1]), lambda i: (i, 0)),
            _resident(w_out.shape),
            _resident((1, d)),
            _resident(w_in.shape),
            _resident(w_down.shape),
            _resident((1, d)),
        ],
        out_specs=pl.BlockSpec((tm, d), lambda i: (i, 0)),
        out_shape=jax.ShapeDtypeStruct((n, d), F32),
        scratch_shapes=[pltpu.VMEM((tm, d_ff), BF16)],
        compiler_params=_params(1),
        name="post_final" if final_norm else "post",
    )(x2d, o2d, w_out, g_ffn, w_in, w_down, g_final)


def _lane_cumsum(x, tile):
    r = lax.broadcasted_iota(jnp.int32, (tile, tile), 0)
    c = lax.broadcasted_iota(jnp.int32, (tile, tile), 1)
    upper = (r <= c).astype(BF16)
    rows = x.shape[0]
    carry = jnp.zeros((rows, 1), F32)
    out = []
    for j in range(x.shape[1] // tile):
        parts = jnp.concatenate(_split3(x[:, j * tile:(j + 1) * tile]), axis=0)
        s = _dot(parts, upper)
        blk = carry + s[0:rows] + s[rows:2 * rows] + s[2 * rows:3 * rows]
        out.append(blk)
        carry = blk[:, tile - 1:tile]
    return out[0] if len(out) == 1 else jnp.concatenate(out, axis=1)


def _to_columns(x_t):
    rows = x_t.shape[0]
    parts = jnp.concatenate(_split3(x_t), axis=0)
    r = lax.broadcasted_iota(jnp.int32, (3 * rows, LANES), 0)
    c = lax.broadcasted_iota(jnp.int32, (3 * rows, LANES), 1)
    place = ((r % rows) == c).astype(BF16)
    return _dot_tn(parts, place)


def _key_side_ext(c_cols, n_head):
    hi, mid, lo = (p.astype(F32) for p in _split3(c_cols))
    lane = lax.broadcasted_iota(jnp.int32, c_cols.shape, 1)
    ones = ((lane >= 3 * n_head) & (lane < 3 * n_head + 3)).astype(F32)
    ext = hi + pltpu.roll(mid, n_head, 1) + pltpu.roll(lo, 2 * n_head, 1) + ones
    return ext.astype(BF16)


def _fox_proj_kernel(*refs, hw, n_head, scale, cache_len):
    if cache_len:
        (x_ref, g_ref, w_ref, wft_ref, bfc_ref, lct_ref,
         q_ref, k_ref, v_ref, logf_ref, ct_ref, cext_ref, cextc_ref, carry_ref) = refs
    else:
        (x_ref, g_ref, w_ref, wft_ref, bfc_ref,
         q_ref, k_ref, v_ref, logf_ref, ct_ref, cext_ref, carry_ref) = refs
    t = pl.program_id(1)
    tm = x_ref.shape[0]

    @pl.when(t == 0)
    def _():
        if cache_len:
            ct_cache = _lane_cumsum(lct_ref[...], 256)
            cextc_ref[...] = _key_side_ext(_to_columns(ct_cache), n_head)
            carry_ref[...] = jnp.broadcast_to(ct_cache[:, cache_len - 1:cache_len], carry_ref.shape)
        else:
            carry_ref[...] = jnp.zeros_like(carry_ref)

    h = _rmsnorm(x_ref[...], g_ref[...]).astype(BF16)
    q_ref[...] = (_dot(h, w_ref[:, 0:hw]) * scale).astype(BF16)
    k_ref[...] = _dot(h, w_ref[:, hw:2 * hw])
    v_ref[...] = _dot(h, w_ref[:, 2 * hw:3 * hw])

    logf_t = _log_sigmoid(_dot_nt(wft_ref[...], h) + bfc_ref[...])
    logf_ref[...] = _to_columns(logf_t)[:, :n_head]
    ct_blk = _lane_cumsum(logf_t, min(tm, 256)) + carry_ref[:, 0:1]
    ct_ref[...] = ct_blk
    carry_ref[...] = jnp.broadcast_to(ct_blk[:, tm - 1:tm], carry_ref.shape)
    cext_ref[...] = _key_side_ext(_to_columns(ct_blk), n_head)


def _fox_proj(x3, g, w_main, w_ft, bf_col, logf_cache_t, tm, n_head):
    bsz, t_len, d = x3.shape
    hw = w_main.shape[1] // 3
    cache_len = 0 if logf_cache_t is None else logf_cache_t.shape[2]
    assert t_len % tm == 0 and 3 * n_head + 3 <= LANES
    kern = functools.partial(_fox_proj_kernel, hw=hw, n_head=n_head,
                             scale=(hw // n_head) ** -0.5, cache_len=cache_len)
    in_specs = [
        pl.BlockSpec((None, tm, d), lambda b, t: (b, t, 0)),
        _resident((1, d)),
        _resident(w_main.shape),
        _resident(w_ft.shape),
        _resident(bf_col.shape),
    ]
    args = [x3, g, w_main, w_ft, bf_col]
    tok = pl.BlockSpec((None, tm, hw), lambda b, t: (b, t, 0))
    out_specs = [
        tok, tok, tok,
        pl.BlockSpec((None, tm, n_head), lambda b, t: (b, t, 0)),
        pl.BlockSpec((None, n_head, tm), lambda b, t: (b, 0, t)),
        pl.BlockSpec((None, tm, LANES), lambda b, t: (b, t, 0)),
    ]
    out_shape = [
        jax.ShapeDtypeStruct((bsz, t_len, hw), BF16),
        jax.ShapeDtypeStruct((bsz, t_len, hw), F32),
        jax.ShapeDtypeStruct((bsz, t_len, hw), F32),
        jax.ShapeDtypeStruct((bsz, t_len, n_head), F32),
        jax.ShapeDtypeStruct((bsz, n_head, t_len), F32),
        jax.ShapeDtypeStruct((bsz, t_len, LANES), BF16),
    ]
    if cache_len:
        in_specs.append(pl.BlockSpec((None, n_head, cache_len), lambda b, t: (b, 0, 0)))
        args.append(logf_cache_t)
        out_specs.append(pl.BlockSpec((None, cache_len, LANES), lambda b, t: (b, 0, 0)))
        out_shape.append(jax.ShapeDtypeStruct((bsz, cache_len, LANES), BF16))
    return pl.pallas_call(
        kern,
        grid=(bsz, t_len // tm),
        in_specs=in_specs,
        out_specs=out_specs,
        out_shape=out_shape,
        scratch_shapes=[pltpu.VMEM((n_head, LANES), F32)],
        compiler_params=_params(2),
        name="fox_proj_cache" if cache_len else "fox_proj",
    )(*args)


def _attn_setup(q_ref, cq_rows, qaug_ref, m_ref, l_ref, acc_ref, *, head0, n_pair, tq, hd, n_head):
    r = lax.broadcasted_iota(jnp.int32, (LANES, LANES), 0)
    c = lax.broadcasted_iota(jnp.int32, (LANES, LANES), 1)
    eye = (r == c).astype(BF16)
    rr = lax.broadcasted_iota(jnp.int32, (LANES, tq), 0)
    for p in range(n_pair):
        q_t = _dot_nt(eye, q_ref[:, p * LANES:(p + 1) * LANES])
        for h in range(2):
            hh = head0 + 2 * p + h
            cq_hi, cq_mid, cq_lo = (x.astype(F32) for x in _split3(cq_rows[2 * p + h]))
            ext = jnp.where((rr == hh) | (rr == n_head + hh) | (rr == 2 * n_head + hh), -1.0, 0.0)
            ext = jnp.where(rr == 3 * n_head, cq_hi, ext)
            ext = jnp.where(rr == 3 * n_head + 1, cq_mid, ext)
            ext = jnp.where(rr == 3 * n_head + 2, cq_lo, ext)
            cols = slice(h * tq, (h + 1) * tq)
            qaug_ref[p, 0:LANES, cols] = jnp.where(rr // hd == h, q_t, 0.0).astype(BF16)
            qaug_ref[p, LANES:2 * LANES, cols] = ext.astype(BF16)
    m_ref[...] = jnp.full_like(m_ref, MASK_VALUE)
    l_ref[...] = jnp.zeros_like(l_ref)
    acc_ref[...] = jnp.zeros_like(acc_ref)


def _attn_update(p, k_blk, v_blk, cext_blk, mask, qaug_ref, m_ref, l_ref, acc_ref):
    kaug = jnp.concatenate([k_blk.astype(BF16), cext_blk], axis=1)
    s = _dot(kaug, qaug_ref[p])
    if mask is not None:
        s = jnp.where(mask, s, MASK_VALUE)
    m_old = m_ref[p]
    m_new = jnp.maximum(m_old, jnp.max(s, axis=0, keepdims=True))
    alpha = jnp.exp(m_old - m_new)
    pr = jnp.exp(s - m_new)
    l_ref[p] = alpha * l_ref[p] + jnp.sum(pr, axis=0, keepdims=True)
    m_ref[p] = m_new
    acc_ref[p] = acc_ref[p] * alpha + _dot_tn(v_blk.astype(BF16), pr.astype(BF16))


def _attn_finish(o_ref, l_ref, acc_ref, *, n_pair, tq, hd):
    for p in range(n_pair):
        full = acc_ref[p] / l_ref[p]
        if tq % LANES == 0:
            z = jnp.concatenate([full[0:hd, 0:tq], full[hd:2 * hd, tq:2 * tq]], axis=0)
        else:
            row = lax.broadcasted_iota(jnp.int32, full.shape, 0)
            z = jnp.where(row < hd, full, pltpu.roll(full, tq, 1))
        o_ref[:, p * LANES:(p + 1) * LANES] = z.T[0:tq].astype(o_ref.dtype)


def _causal_mask(tk, tq):
    row = lax.broadcasted_iota(jnp.int32, (tk, 2 * tq), 0)
    col = lax.broadcasted_iota(jnp.int32, (tk, 2 * tq), 1) % tq
    return row <= col


def _fox_attn_prompt_kernel(q_ref, k_ref, v_ref, cext_ref, ct_ref, o_ref,
                            qaug_ref, m_ref, l_ref, acc_ref, *, tq, n_pair, hd, n_head):
    g = pl.program_id(1)
    i = pl.program_id(2)
    qs = pl.ds(pl.multiple_of(i * tq, tq), tq)
    cq_rows = [ct_ref[h:h + 1, qs] for h in range(2 * n_pair)]
    _attn_setup(q_ref, cq_rows, qaug_ref, m_ref, l_ref, acc_ref,
                head0=g * (2 * n_pair), n_pair=n_pair, tq=tq, hd=hd, n_head=n_head)

    def kv_block(j, mask):
        ks = pl.ds(pl.multiple_of(j * tq, tq), tq)
        ce = cext_ref[ks, :]
        for p in range(n_pair):
            cols = slice(p * LANES, (p + 1) * LANES)
            _attn_update(p, k_ref[ks, cols], v_ref[ks, cols], ce, mask, qaug_ref, m_ref, l_ref, acc_ref)

    def body(j, carry):
        kv_block(j, None)
        return carry

    lax.fori_loop(0, i, body, 0)
    kv_block(i, _causal_mask(tq, tq))
    _attn_finish(o_ref, l_ref, acc_ref, n_pair=n_pair, tq=tq, hd=hd)


def _attn_scratch(n_pair, tq):
    return [
        pltpu.VMEM((n_pair, 2 * LANES, 2 * tq), BF16),
        pltpu.VMEM((n_pair, 1, 2 * tq), F32),
        pltpu.VMEM((n_pair, 1, 2 * tq), F32),
        pltpu.VMEM((n_pair, LANES, 2 * tq), F32),
    ]


def _fox_attn_prompt(q, k, v, cext, ct, tq, n_pair, n_head):
    bsz, t_len, hw = q.shape
    hd = hw // n_head
    assert 2 * hd == LANES and t_len % tq == 0 and tq % LANES == 0
    width = n_pair * LANES
    n_group = hw // width
    ct4 = ct.reshape(bsz, n_group, 2 * n_pair, t_len)
    kern = functools.partial(_fox_attn_prompt_kernel, tq=tq, n_pair=n_pair, hd=hd, n_head=n_head)
    return pl.pallas_call(
        kern,
        grid=(bsz, n_group, t_len // tq),
        in_specs=[
            pl.BlockSpec((None, tq, width), lambda b, g, i: (b, i, g)),
            pl.BlockSpec((None, t_len, width), lambda b, g, i: (b, 0, g)),
            pl.BlockSpec((None, t_len, width), lambda b, g, i: (b, 0, g)),
            pl.BlockSpec((None, t_len, LANES), lambda b, g, i: (b, 0, 0)),
            pl.BlockSpec((None, None, 2 * n_pair, t_len), lambda b, g, i: (b, g, 0, 0)),
        ],
        out_specs=pl.BlockSpec((None, tq, width), lambda b, g, i: (b, i, g)),
        out_shape=jax.ShapeDtypeStruct((bsz, t_len, hw), BF16),
        scratch_shapes=_attn_scratch(n_pair, tq),
        compiler_params=_params(3),
        name="fox_attn_prompt",
    )(q, k, v, cext, ct4)


def _fox_attn_sample_kernel(q_ref, kc_ref, vc_ref, kn_ref, vn_ref, cextc_ref, cextn_ref, ctn_ref, o_ref,
                            qaug_ref, m_ref, l_ref, acc_ref, *, tq, tk, n_pair, hd, n_head, cache_len):
    g = pl.program_id(1)
    cq_rows = [ctn_ref[h:h + 1, :] for h in range(2 * n_pair)]
    _attn_setup(q_ref, cq_rows, qaug_ref, m_ref, l_ref, acc_ref,
                head0=g * (2 * n_pair), n_pair=n_pair, tq=tq, hd=hd, n_head=n_head)
    for j in range(cache_len // tk):
        ks = slice(j * tk, (j + 1) * tk)
        for p in range(n_pair):
            cols = slice(p * LANES, (p + 1) * LANES)
            _attn_update(p, kc_ref[ks, cols], vc_ref[ks, cols], cextc_ref[ks, :], None,
                         qaug_ref, m_ref, l_ref, acc_ref)
    mask = _causal_mask(tq, tq)
    for p in range(n_pair):
        cols = slice(p * LANES, (p + 1) * LANES)
        _attn_update(p, kn_ref[:, cols], vn_ref[:, cols], cextn_ref[...], mask,
                     qaug_ref, m_ref, l_ref, acc_ref)
    _attn_finish(o_ref, l_ref, acc_ref, n_pair=n_pair, tq=tq, hd=hd)


def _fox_attn_sample(q, k_new, v_new, k_cache, v_cache, cext_cache, cext_new, ct_new, n_pair, n_head):
    bsz, tq, hw = q.shape
    cache_len = k_cache.shape[1]
    hd = hw // n_head
    tk = 512
    assert 2 * hd == LANES and cache_len % tk == 0 and 2 * tq == LANES
    width = n_pair * LANES
    n_group = hw // width
    ctn4 = ct_new.reshape(bsz, n_group, 2 * n_pair, tq)
    kern = functools.partial(_fox_attn_sample_kernel, tq=tq, tk=tk, n_pair=n_pair, hd=hd,
                             n_head=n_head, cache_len=cache_len)
    new_spec = pl.BlockSpec((None, tq, width), lambda b, g: (b, 0, g))
    cache_spec = pl.BlockSpec((None, cache_len, width), lambda b, g: (b, 0, g))
    return pl.pallas_call(
        kern,
        grid=(bsz, n_group),
        in_specs=[
            new_spec, cache_spec, cache_spec, new_spec, new_spec,
            pl.BlockSpec((None, cache_len, LANES), lambda b, g: (b, 0, 0)),
            pl.BlockSpec((None, tq, LANES), lambda b, g: (b, 0, 0)),
            pl.BlockSpec((None, None, 2 * n_pair, tq), lambda b, g: (b, g, 0, 0)),
        ],
        out_specs=new_spec,
        out_shape=jax.ShapeDtypeStruct((bsz, tq, hw), BF16),
        scratch_shapes=_attn_scratch(n_pair, tq),
        compiler_params=_params(2),
        name="fox_attn_sample",
    )(q, k_cache, v_cache, k_new, v_new, cext_cache, cext_new, ctn4)


def _pad_to(a, shape):
    return jnp.pad(a, [(0, s - d) for s, d in zip(shape, a.shape)])


def kernel(x_prompt, x_sample, state_gla, cache_fox_k, cache_fox_v, cache_fox_logf,
           norm_mix, gla_w_in, gla_w_g2, gla_b_g, gla_norm, gla_w_out,
           fox_w_in, fox_b_f, fox_w_out, norm_ffn, ffn_w_in, ffn_w_down, norm_final):
    d = x_prompt.shape[-1]
    depth = norm_mix.shape[0]
    groups = [x_prompt, x_sample]
    shapes = [x.shape for x in groups]
    xs = [x.reshape(-1, d) for x in groups]
    row_tiles = [min(512, x.shape[0]) for x in xs]

    gla_states = [[], []]
    fox_k, fox_v, fox_f = [[], []], [[], []], [[], []]
    for i in range(depth):
        j = i // 2
        g_mix = norm_mix[i].reshape(1, d)
        g_ffn = norm_ffn[i].reshape(1, d)
        w_ffn_in = ffn_w_in[i].astype(BF16)
        w_ffn_down = ffn_w_down[i].astype(BF16)
        last = i == depth - 1
        if i % 2 == 0:
            _, n_head, dk, dv = state_gla.shape[1:]
            hk, hv = n_head * dk, n_head * dv
            n_main = 2 * hk + 2 * hv
            w_in = gla_w_in[j]
            w_main = w_in[:, :n_main].astype(BF16)
            w_gl = _pad_to(w_in[:, n_main:], (d, LANES)).astype(BF16)
            w_g2 = _pad_to(gla_w_g2[j], (LANES, hk)).astype(BF16)
            b_g = gla_b_g[j].reshape(1, hk)
            w_out = gla_w_out[j].astype(BF16)
            norm_g = gla_norm[j].reshape(1, hv)
            s0s = [jnp.zeros((shapes[0][0], n_head, dk, dv), F32), state_gla[j]]
            for gi in range(2):
                bsz, t_len, _ = shapes[gi]
                proj, glog = _gla_proj(xs[gi], g_mix, w_main, w_gl, w_g2, b_g, row_tiles[gi])
                og, s_fin = _gla_mix(proj.reshape(bsz, t_len, n_main), glog.reshape(bsz, t_len, hk),
                                     s0s[gi], norm_g, min(t_len, 256))
                gla_states[gi].append(s_fin)
                xs[gi] = _post(xs[gi], og.reshape(-1, hv), w_out, g_ffn, w_ffn_in, w_ffn_down,
                               norm_final.reshape(1, d), row_tiles[gi], last)
        else:
            n_head = fox_b_f.shape[1]
            hw = fox_w_out.shape[1]
            w_in = fox_w_in[j]
            w_main = w_in[:, :3 * hw].astype(BF16)
            w_ft = w_in[:, 3 * hw:].T.astype(BF16)
            bf_col = fox_b_f[j].reshape(n_head, 1)
            w_out = fox_w_out[j].astype(BF16)
            for gi in range(2):
                bsz, t_len, _ = shapes[gi]
                x3 = xs[gi].reshape(bsz, t_len, d)
                if gi == 0:
                    q, k, v, logf, ct, cext = _fox_proj(x3, g_mix, w_main, w_ft, bf_col, None,
                                                        min(512, t_len), n_head)
                    o = _fox_attn_prompt(q, k, v, cext, ct, 256, 2, n_head)
                else:
                    cache_len = cache_fox_logf.shape[2]
                    q, k, v, logf, ct, cext, cext_cache = _fox_proj(
                        x3, g_mix, w_main, w_ft, bf_col,
                        jnp.transpose(cache_fox_logf[j], (0, 2, 1)), t_len, n_head)
                    o = _fox_attn_sample(q, k, v, cache_fox_k[j].reshape(bsz, cache_len, hw),
                                         cache_fox_v[j].reshape(bsz, cache_len, hw),
                                         cext_cache, cext, ct, 2, n_head)
                fox_k[gi].append(k.reshape(bsz, t_len, n_head, hw // n_head))
                fox_v[gi].append(v.reshape(bsz, t_len, n_head, hw // n_head))
                fox_f[gi].append(logf)
                xs[gi] = _post(xs[gi], o.reshape(-1, hw), w_out, g_ffn, w_ffn_in, w_ffn_down,
                               norm_final.reshape(1, d), row_tiles[gi], last)

    y_prompt = xs[0].reshape(shapes[0])
    y_sample = xs[1].reshape(shapes[1])
    st = lambda parts: jnp.stack(parts, axis=0)
    return (y_prompt, y_sample, st(gla_states[0]), st(fox_k[0]), st(fox_v[0]), st(fox_f[0]),
            st(gla_states[1]), st(fox_k[1]), st(fox_v[1]), st(fox_f[1]))
```

```python
import functools

import jax
import jax.numpy as jnp
from jax import lax
from jax.experimental import pallas as pl
from jax.experimental.pallas import tpu as pltpu

F32 = jnp.float32
BF16 = jnp.bfloat16

EPS = 1e-6
MASK_VALUE = -1e30

GLA_HEADS = 4
GLA_CHUNK = 64
GLA_GATE_TAU = 16.0

LANES = 128
VMEM_LIMIT_BYTES = 56 * 1024 * 1024


def _params(n_grid, flags=None):
    return pltpu.CompilerParams(
        dimension_semantics=("arbitrary",) * n_grid,
        vmem_limit_bytes=VMEM_LIMIT_BYTES,
        flags=flags,
    )


def _resident(shape):
    nd = len(shape)
    return pl.BlockSpec(shape, lambda *_: (0,) * nd, pipeline_mode=pl.Buffered(1))


def _dot(a, b):
    return jnp.dot(a, b, preferred_element_type=F32)


def _dot_nt(a, b):
    return lax.dot_general(a, b, (((1,), (1,)), ((), ())), preferred_element_type=F32)


def _dot_tn(a, b):
    return lax.dot_general(a, b, (((0,), (0,)), ((), ())), preferred_element_type=F32)


def _split3(x):
    hi = x.astype(BF16)
    r1 = x - hi.astype(F32)
    mid = r1.astype(BF16)
    lo = (r1 - mid.astype(F32)).astype(BF16)
    return hi, mid, lo


def _sum01(dot_fn, x, ones_first, mat01):
    acc = None
    for part in _split3(x):
        term = dot_fn(mat01, part) if ones_first else dot_fn(part, mat01)
        acc = term if acc is None else acc + term
    return acc


def _rmsnorm(x, g):
    var = jnp.mean(x * x, axis=-1, keepdims=True)
    return x * lax.rsqrt(var + EPS) * g


def _log_sigmoid(z):
    return jnp.minimum(z, 0.0) - jnp.log1p(jnp.exp(-jnp.abs(z)))


def _silu(z):
    return z * jax.nn.sigmoid(z)


def _gla_proj_kernel(x_ref, g_ref, w_ref, wgl_ref, wg2_ref, bg_ref, proj_ref, glog_ref):
    h = _rmsnorm(x_ref[...], g_ref[...]).astype(BF16)
    proj_ref[...] = _dot(h, w_ref[...])
    gl = _dot(h, wgl_ref[...]).astype(BF16)
    z = _dot(gl, wg2_ref[...]) + bg_ref[...]
    glog_ref[...] = _log_sigmoid(z) / GLA_GATE_TAU


def _gla_proj(x2d, g, w_main, w_gl, w_g2, b_g, tm):
    n, d = x2d.shape
    n_main, hk = w_main.shape[1], w_g2.shape[1]
    return pl.pallas_call(
        _gla_proj_kernel,
        grid=(n // tm,),
        in_specs=[
            pl.BlockSpec((tm, d), lambda i: (i, 0)),
            _resident((1, d)),
            _resident(w_main.shape),
            _resident(w_gl.shape),
            _resident(w_g2.shape),
            _resident((1, hk)),
        ],
        out_specs=[
            pl.BlockSpec((tm, n_main), lambda i: (i, 0)),
            pl.BlockSpec((tm, hk), lambda i: (i, 0)),
        ],
        out_shape=[
            jax.ShapeDtypeStruct((n, n_main), F32),
            jax.ShapeDtypeStruct((n, hk), F32),
        ],
        compiler_params=_params(1),
        name="gla_proj",
    )(x2d, g, w_main, w_gl, w_g2, b_g)


def _gla_mix_kernel(q_ref, k_ref, v_ref, r_ref, glog_ref, s0_ref, ng_ref,
                    og_ref, sfin_ref, s_ref, *, tb, dk, dv):
    t = pl.program_id(1)
    n_chunk = tb // GLA_CHUNK

    @pl.when(t == 0)
    def _():
        s_ref[...] = s0_ref[...]

    row = lax.broadcasted_iota(jnp.int32, (tb, tb), 0)
    col = lax.broadcasted_iota(jnp.int32, (tb, tb), 1)
    same_chunk = (row // GLA_CHUNK) == (col // GLA_CHUNK)
    causal = same_chunk & (col <= row)
    cum_mat = causal.astype(BF16)
    tot_mat = same_chunk.astype(BF16)
    row_c = lax.broadcasted_iota(jnp.int32, (tb, LANES), 0)
    col_c = lax.broadcasted_iota(jnp.int32, (tb, LANES), 1)
    chunk_sel = ((row_c // GLA_CHUNK) == col_c).astype(BF16)

    g = glog_ref[...]
    b = _sum01(_dot, g, True, cum_mat)
    b_last = _sum01(_dot, g, True, tot_mat)
    dec_t = jnp.exp(_sum01(_dot_tn, g, False, chunk_sel))
    q = q_ref[...]
    k = k_ref[...]
    qe = (q * jnp.exp(b) * (dk ** -0.5)).astype(BF16)
    ke = (k * jnp.exp(-b)).astype(BF16)
    kd = (k * jnp.exp(b_last - b)).astype(BF16)

    for h in range(GLA_HEADS):
        ks = slice(h * dk, (h + 1) * dk)
        vs = slice(h * dv, (h + 1) * dv)
        qe_h, ke_h, kd_h = qe[:, ks], ke[:, ks], kd[:, ks]
        v_h = v_ref[:, vs].astype(BF16)
        a = jnp.where(causal, _dot_nt(qe_h, ke_h), 0.0).astype(BF16)
        o_intra = _dot(a, v_h)
        s = s_ref[h]
        o_parts = []
        for c in range(n_chunk):
            rs = slice(c * GLA_CHUNK, (c + 1) * GLA_CHUNK)
            o_parts.append(o_intra[rs] + _dot(qe_h[rs], s.astype(BF16)))
            s = s * dec_t[ks, c:c + 1] + _dot_tn(kd_h[rs], v_h[rs])
        s_ref[h] = s
        o = o_parts[0] if n_chunk == 1 else jnp.concatenate(o_parts, axis=0)
        on = _rmsnorm(o, ng_ref[:, vs])
        og_ref[:, vs] = (on * _silu(r_ref[:, vs])).astype(BF16)

    @pl.when(t == pl.num_programs(1) - 1)
    def _():
        sfin_ref[...] = s_ref[...]


def _gla_mix(proj3, glog3, s0, norm_g, tb):
    bsz, t_len, _ = proj3.shape
    _, n_head, dk, dv = s0.shape
    hk, hv = n_head * dk, n_head * dv
    assert t_len % tb == 0 and tb % GLA_CHUNK == 0 and hv == 2 * hk
    kern = functools.partial(_gla_mix_kernel, tb=tb, dk=dk, dv=dv)
    state_spec = pl.BlockSpec((None, n_head, dk, dv), lambda b, t: (b, 0, 0, 0))
    return pl.pallas_call(
        kern,
        grid=(bsz, t_len // tb),
        in_specs=[
            pl.BlockSpec((None, tb, hk), lambda b, t: (b, t, 0)),
            pl.BlockSpec((None, tb, hk), lambda b, t: (b, t, 1)),
            pl.BlockSpec((None, tb, hv), lambda b, t: (b, t, 1)),
            pl.BlockSpec((None, tb, hv), lambda b, t: (b, t, 2)),
            pl.BlockSpec((None, tb, hk), lambda b, t: (b, t, 0)),
            state_spec,
            _resident((1, hv)),
        ],
        out_specs=[
            pl.BlockSpec((None, tb, hv), lambda b, t: (b, t, 0)),
            state_spec,
        ],
        out_shape=[
            jax.ShapeDtypeStruct((bsz, t_len, hv), BF16),
            jax.ShapeDtypeStruct(s0.shape, F32),
        ],
        scratch_shapes=[pltpu.VMEM((n_head, dk, dv), F32)],
        compiler_params=_params(2),
        name="gla_mix",
    )(proj3, proj3, proj3, proj3, glog3, s0, norm_g)


def _post_kernel(x_ref, o_ref, wo_ref, g_ref, win_ref, wdown_ref, gfin_ref, y_ref, act_ref,
                 *, d_ff, ff_tile, final_norm):
    x1 = x_ref[...] + _dot(o_ref[...], wo_ref[...])
    h = _rmsnorm(x1, g_ref[...]).astype(BF16)
    for j in range(d_ff // ff_tile):
        gate = _dot(h, win_ref[:, j * ff_tile:(j + 1) * ff_tile])
        up = _dot(h, win_ref[:, d_ff + j * ff_tile:d_ff + (j + 1) * ff_tile])
        act_ref[:, j * ff_tile:(j + 1) * ff_tile] = (_silu(gate) * up).astype(BF16)
    y = x1 + _dot(act_ref[...], wdown_ref[...])
    if final_norm:
        y = _rmsnorm(y, gfin_ref[...])
    y_ref[...] = y


def _post(x2d, o2d, w_out, g_ffn, w_in, w_down, g_final, tm, final_norm):
    n, d = x2d.shape
    d_ff = w_down.shape[0]
    ff_tile = 256
    assert d_ff % ff_tile == 0 and n % tm == 0
    kern = functools.partial(_post_kernel, d_ff=d_ff, ff_tile=ff_tile, final_norm=final_norm)
    return pl.pallas_call(
        kern,
        grid=(n // tm,),
        in_specs=[
            pl.BlockSpec((tm, d), lambda i: (i, 0)),
            pl.BlockSpec((tm, o2d.shape[1]), lambda i: (i, 0)),
            _resident(w_out.shape),
            _resident((1, d)),
            _resident(w_in.shape),
            _resident(w_down.shape),
            _resident((1, d)),
        ],
        out_specs=pl.BlockSpec((tm, d), lambda i: (i, 0)),
        out_shape=jax.ShapeDtypeStruct((n, d), F32),
        scratch_shapes=[pltpu.VMEM((tm, d_ff), BF16)],
        compiler_params=_params(1),
        name="post_final" if final_norm else "post",
    )(x2d, o2d, w_out, g_ffn, w_in, w_down, g_final)


def _lane_cumsum(x, tile):
    r = lax.broadcasted_iota(jnp.int32, (tile, tile), 0)
    c = lax.broadcasted_iota(jnp.int32, (tile, tile), 1)
    upper = (r <= c).astype(BF16)
    rows = x.shape[0]
    carry = jnp.zeros((rows, 1), F32)
    out = []
    for j in range(x.shape[1] // tile):
        parts = jnp.concatenate(_split3(x[:, j * tile:(j + 1) * tile]), axis=0)
        s = _dot(parts, upper)
        blk = carry + s[0:rows] + s[rows:2 * rows] + s[2 * rows:3 * rows]
        out.append(blk)
        carry = blk[:, tile - 1:tile]
    return out[0] if len(out) == 1 else jnp.concatenate(out, axis=1)


def _to_columns(x_t):
    rows = x_t.shape[0]
    parts = jnp.concatenate(_split3(x_t), axis=0)
    r = lax.broadcasted_iota(jnp.int32, (3 * rows, LANES), 0)
    c = lax.broadcasted_iota(jnp.int32, (3 * rows, LANES), 1)
    place = ((r % rows) == c).astype(BF16)
    return _dot_tn(parts, place)


def _key_side_ext(c_cols, n_head):
    hi, mid, lo = (p.astype(F32) for p in _split3(c_cols))
    lane = lax.broadcasted_iota(jnp.int32, c_cols.shape, 1)
    ones = ((lane >= 3 * n_head) & (lane < 3 * n_head + 3)).astype(F32)
    ext = hi + pltpu.roll(mid, n_head, 1) + pltpu.roll(lo, 2 * n_head, 1) + ones
    return ext.astype(BF16)


def _identity(n):
    r = lax.broadcasted_iota(jnp.int32, (n, n), 0)
    c = lax.broadcasted_iota(jnp.int32, (n, n), 1)
    return (r == c).astype(BF16)


def _fox_proj_kernel(*refs, hw, n_head, scale, cache_len):
    if cache_len:
        (x_ref, g_ref, w_ref, wft_ref, bfc_ref, lct_ref,
         q_ref, k_ref, v_ref, logf_ref, ct_ref, cext_ref, cextc_ref, carry_ref) = refs
    else:
        (x_ref, g_ref, w_ref, wft_ref, bfc_ref,
         q_ref, k_ref, v_ref, logf_ref, ct_ref, kaug_ref, vt_ref, carry_ref) = refs
    t = pl.program_id(1)
    tm = x_ref.shape[0]

    @pl.when(t == 0)
    def _():
        if cache_len:
            ct_cache = _lane_cumsum(lct_ref[...], 256)
            cextc_ref[...] = _key_side_ext(_to_columns(ct_cache), n_head)
            carry_ref[...] = jnp.broadcast_to(ct_cache[:, cache_len - 1:cache_len], carry_ref.shape)
        else:
            carry_ref[...] = jnp.zeros_like(carry_ref)

    h = _rmsnorm(x_ref[...], g_ref[...]).astype(BF16)
    q_ref[...] = (_dot(h, w_ref[:, 0:hw]) * scale).astype(BF16)
    k = _dot(h, w_ref[:, hw:2 * hw])
    v = _dot(h, w_ref[:, 2 * hw:3 * hw])
    k_ref[...] = k
    v_ref[...] = v

    logf_t = _log_sigmoid(_dot_nt(wft_ref[...], h) + bfc_ref[...])
    logf_ref[...] = _to_columns(logf_t)[:, :n_head]
    ct_blk = _lane_cumsum(logf_t, min(tm, 256)) + carry_ref[:, 0:1]
    ct_ref[...] = ct_blk
    carry_ref[...] = jnp.broadcast_to(ct_blk[:, tm - 1:tm], carry_ref.shape)
    cext = _key_side_ext(_to_columns(ct_blk), n_head)
    if cache_len:
        cext_ref[...] = cext
    else:
        eye = _identity(LANES)
        for p in range(hw // LANES):
            cols = slice(p * LANES, (p + 1) * LANES)
            kaug_ref[p, :, 0:LANES] = k[:, cols].astype(BF16)
            kaug_ref[p, :, LANES:2 * LANES] = cext
            vt_ref[cols, :] = _dot_nt(eye, v[:, cols].astype(BF16)).astype(BF16)


def _fox_proj(x3, g, w_main, w_ft, bf_col, logf_cache_t, tm, n_head):
    bsz, t_len, d = x3.shape
    hw = w_main.shape[1] // 3
    cache_len = 0 if logf_cache_t is None else logf_cache_t.shape[2]
    assert t_len % tm == 0 and 3 * n_head + 3 <= LANES
    kern = functools.partial(_fox_proj_kernel, hw=hw, n_head=n_head,
                             scale=(hw // n_head) ** -0.5, cache_len=cache_len)
    in_specs = [
        pl.BlockSpec((None, tm, d), lambda b, t: (b, t, 0)),
        _resident((1, d)),
        _resident(w_main.shape),
        _resident(w_ft.shape),
        _resident(bf_col.shape),
    ]
    args = [x3, g, w_main, w_ft, bf_col]
    tok = pl.BlockSpec((None, tm, hw), lambda b, t: (b, t, 0))
    out_specs = [
        tok, tok, tok,
        pl.BlockSpec((None, tm, n_head), lambda b, t: (b, t, 0)),
        pl.BlockSpec((None, n_head, tm), lambda b, t: (b, 0, t)),
    ]
    out_shape = [
        jax.ShapeDtypeStruct((bsz, t_len, hw), BF16),
        jax.ShapeDtypeStruct((bsz, t_len, hw), F32),
        jax.ShapeDtypeStruct((bsz, t_len, hw), F32),
        jax.ShapeDtypeStruct((bsz, t_len, n_head), F32),
        jax.ShapeDtypeStruct((bsz, n_head, t_len), F32),
    ]
    if cache_len:
        in_specs.append(pl.BlockSpec((None, n_head, cache_len), lambda b, t: (b, 0, 0)))
        args.append(logf_cache_t)
        out_specs += [pl.BlockSpec((None, tm, LANES), lambda b, t: (b, t, 0)),
                      pl.BlockSpec((None, cache_len, LANES), lambda b, t: (b, 0, 0))]
        out_shape += [jax.ShapeDtypeStruct((bsz, t_len, LANES), BF16),
                      jax.ShapeDtypeStruct((bsz, cache_len, LANES), BF16)]
    else:
        n_pair_all = hw // LANES
        out_specs += [pl.BlockSpec((None, n_pair_all, tm, 2 * LANES), lambda b, t: (b, 0, t, 0)),
                      pl.BlockSpec((None, hw, tm), lambda b, t: (b, 0, t))]
        out_shape += [jax.ShapeDtypeStruct((bsz, n_pair_all, t_len, 2 * LANES), BF16),
                      jax.ShapeDtypeStruct((bsz, hw, t_len), BF16)]
    return pl.pallas_call(
        kern,
        grid=(bsz, t_len // tm),
        in_specs=in_specs,
        out_specs=out_specs,
        out_shape=out_shape,
        scratch_shapes=[pltpu.VMEM((n_head, LANES), F32)],
        compiler_params=_params(2),
        name="fox_proj_cache" if cache_len else "fox_proj",
    )(*args)


def _attn_setup(q_ref, cq_rows, qaug_ref, m_ref, l_ref, acc_ref, *, head0, n_pair, tq, hd, n_head):
    r = lax.broadcasted_iota(jnp.int32, (LANES, LANES), 0)
    c = lax.broadcasted_iota(jnp.int32, (LANES, LANES), 1)
    eye = (r == c).astype(BF16)
    rr = lax.broadcasted_iota(jnp.int32, (LANES, tq), 0)
    for p in range(n_pair):
        q_t = _dot_nt(eye, q_ref[:, p * LANES:(p + 1) * LANES])
        for h in range(2):
            hh = head0 + 2 * p + h
            cq_hi, cq_mid, cq_lo = (x.astype(F32) for x in _split3(cq_rows[2 * p + h]))
            ext = jnp.where((rr == hh) | (rr == n_head + hh) | (rr == 2 * n_head + hh), -1.0, 0.0)
            ext = jnp.where(rr == 3 * n_head, cq_hi, ext)
            ext = jnp.where(rr == 3 * n_head + 1, cq_mid, ext)
            ext = jnp.where(rr == 3 * n_head + 2, cq_lo, ext)
            cols = slice(h * tq, (h + 1) * tq)
            qaug_ref[p, 0:LANES, cols] = jnp.where(rr // hd == h, q_t, 0.0).astype(BF16)
            qaug_ref[p, LANES:2 * LANES, cols] = ext.astype(BF16)
    m_ref[...] = jnp.full_like(m_ref, MASK_VALUE)
    l_ref[...] = jnp.zeros_like(l_ref)
    acc_ref[...] = jnp.zeros_like(acc_ref)


def _attn_update(n_pair, kaug_of, vt_of, mask, qaug_ref, m_ref, l_ref, acc_ref, s0_ref=None, next_kaug0=None):
    if s0_ref is None:
        s_next = _dot(kaug_of(0), qaug_ref[0])
    else:
        s_next = s0_ref[...]
    for p in range(n_pair):
        s = s_next
        if p + 1 < n_pair:
            s_next = _dot(kaug_of(p + 1), qaug_ref[p + 1])
        elif next_kaug0 is not None:
            s0_ref[...] = _dot(next_kaug0(), qaug_ref[0])
        if mask is not None:
            s = jnp.where(mask, s, MASK_VALUE)
        m_old = m_ref[p]
        m_new = jnp.maximum(m_old, jnp.max(s, axis=0, keepdims=True))
        alpha = jnp.exp(m_old - m_new)
        pr = jnp.exp(s - m_new)
        l_ref[p] = alpha * l_ref[p] + jnp.sum(pr, axis=0, keepdims=True)
        m_ref[p] = m_new
        acc_ref[p] = acc_ref[p] * alpha + _dot(vt_of(p), pr.astype(BF16))


def _attn_finish(o_ref, l_ref, acc_ref, *, n_pair, tq, hd):
    for p in range(n_pair):
        full = acc_ref[p] / l_ref[p]
        if tq % LANES == 0:
            z = jnp.concatenate([full[0:hd, 0:tq], full[hd:2 * hd, tq:2 * tq]], axis=0)
        else:
            row = lax.broadcasted_iota(jnp.int32, full.shape, 0)
            z = jnp.where(row < hd, full, pltpu.roll(full, tq, 1))
        o_ref[:, p * LANES:(p + 1) * LANES] = z.T[0:tq].astype(o_ref.dtype)


def _causal_mask(tk, tq):
    row = lax.broadcasted_iota(jnp.int32, (tk, 2 * tq), 0)
    col = lax.broadcasted_iota(jnp.int32, (tk, 2 * tq), 1) % tq
    return row <= col


def _fox_attn_prompt_kernel(q_ref, kaug_ref, vt_ref, ct_ref, o_ref,
                            qaug_ref, m_ref, l_ref, acc_ref, s0_ref, *, tq, n_pair, hd, n_head):
    g = pl.program_id(1)
    i = pl.program_id(2)
    qs = pl.ds(pl.multiple_of(i * tq, tq), tq)
    cq_rows = [ct_ref[h:h + 1, qs] for h in range(2 * n_pair)]
    _attn_setup(q_ref, cq_rows, qaug_ref, m_ref, l_ref, acc_ref,
                head0=g * (2 * n_pair), n_pair=n_pair, tq=tq, hd=hd, n_head=n_head)

    def keys(j):
        return pl.ds(pl.multiple_of(j * tq, tq), tq)

    def kv_block(j, mask, has_next):
        ks = keys(j)
        _attn_update(n_pair, lambda p: kaug_ref[p, ks, :], lambda p: vt_ref[p * LANES:(p + 1) * LANES, ks],
                     mask, qaug_ref, m_ref, l_ref, acc_ref, s0_ref,
                     (lambda: kaug_ref[0, keys(j + 1), :]) if has_next else None)

    def body(j, carry):
        kv_block(j, None, True)
        return carry

    s0_ref[...] = _dot(kaug_ref[0, 0:tq, :], qaug_ref[0])
    lax.fori_loop(0, i, body, 0)
    kv_block(i, _causal_mask(tq, tq), False)
    _attn_finish(o_ref, l_ref, acc_ref, n_pair=n_pair, tq=tq, hd=hd)


def _attn_scratch(n_pair, tq):
    return [
        pltpu.VMEM((n_pair, 2 * LANES, 2 * tq), BF16),
        pltpu.VMEM((n_pair, 1, 2 * tq), F32),
        pltpu.VMEM((n_pair, 1, 2 * tq), F32),
        pltpu.VMEM((n_pair, LANES, 2 * tq), F32),
    ]


def _fox_attn_prompt(q, kaug, vt, ct, tq, n_pair, n_head):
    bsz, t_len, hw = q.shape
    hd = hw // n_head
    assert 2 * hd == LANES and t_len % tq == 0 and tq % LANES == 0
    width = n_pair * LANES
    n_group = hw // width
    ct4 = ct.reshape(bsz, n_group, 2 * n_pair, t_len)
    kern = functools.partial(_fox_attn_prompt_kernel, tq=tq, n_pair=n_pair, hd=hd, n_head=n_head)
    return pl.pallas_call(
        kern,
        grid=(bsz, n_group, t_len // tq),
        in_specs=[
            pl.BlockSpec((None, tq, width), lambda b, g, i: (b, i, g)),
            pl.BlockSpec((None, n_pair, t_len, 2 * LANES), lambda b, g, i: (b, g, 0, 0)),
            pl.BlockSpec((None, width, t_len), lambda b, g, i: (b, g, 0)),
            pl.BlockSpec((None, None, 2 * n_pair, t_len), lambda b, g, i: (b, g, 0, 0)),
        ],
        out_specs=pl.BlockSpec((None, tq, width), lambda b, g, i: (b, i, g)),
        out_shape=jax.ShapeDtypeStruct((bsz, t_len, hw), BF16),
        scratch_shapes=_attn_scratch(n_pair, tq) + [pltpu.VMEM((tq, 2 * tq), F32)],
        compiler_params=_params(3),
        name="fox_attn_prompt",
    )(q, kaug, vt, ct4)


def _fox_attn_sample_kernel(q_ref, kc_ref, vc_ref, kn_ref, vn_ref, cextc_ref, cextn_ref, ctn_ref, o_ref,
                            qaug_ref, m_ref, l_ref, acc_ref, *, tq, n_pair, hd, n_head):
    kc = pl.program_id(1)
    state = (qaug_ref, m_ref, l_ref, acc_ref)
    eye = _identity(LANES)
    head0 = 0

    @pl.when(kc == 0)
    def _():
        cq_rows = [ctn_ref[h:h + 1, :] for h in range(2 * n_pair)]
        _attn_setup(q_ref, cq_rows, *state, head0=head0, n_pair=n_pair, tq=tq, hd=hd, n_head=n_head)

    tk = kc_ref.shape[0] // n_head

    def pair_of(ref, p):
        h0 = head0 + 2 * p
        return jnp.concatenate([ref[pl.ds(h0, tk, stride=n_head), :],
                                ref[pl.ds(h0 + 1, tk, stride=n_head), :]], axis=1).astype(BF16)

    _attn_update(n_pair,
                 lambda p: jnp.concatenate([pair_of(kc_ref, p), cextc_ref[...]], axis=1),
                 lambda p: _dot_nt(eye, pair_of(vc_ref, p)).astype(BF16),
                 None, *state)

    @pl.when(kc == pl.num_programs(1) - 1)
    def _():
        cols = lambda p: slice(p * LANES, (p + 1) * LANES)
        _attn_update(n_pair,
                     lambda p: jnp.concatenate([kn_ref[:, cols(p)].astype(BF16), cextn_ref[...]], axis=1),
                     lambda p: _dot_nt(eye, vn_ref[:, cols(p)].astype(BF16)).astype(BF16),
                     _causal_mask(tq, tq), *state)
        _attn_finish(o_ref, l_ref, acc_ref, n_pair=n_pair, tq=tq, hd=hd)


def _fox_attn_sample(q, k_new, v_new, k_cache, v_cache, cext_cache, cext_new, ct_new, tk):
    bsz, tq, hw = q.shape
    _, cache_len, n_head, hd = k_cache.shape
    assert 2 * hd == LANES and cache_len % tk == 0 and 2 * tq == LANES and hw == n_head * hd
    n_pair = n_head // 2
    k_rows = k_cache.reshape(bsz, cache_len * n_head, hd)
    v_rows = v_cache.reshape(bsz, cache_len * n_head, hd)
    kern = functools.partial(_fox_attn_sample_kernel, tq=tq, n_pair=n_pair, hd=hd, n_head=n_head)
    new_spec = pl.BlockSpec((None, tq, hw), lambda b, c: (b, 0, 0))
    cache_spec = pl.BlockSpec((None, tk * n_head, hd), lambda b, c: (b, c, 0))
    return pl.pallas_call(
        kern,
        grid=(bsz, cache_len // tk),
        in_specs=[
            new_spec, cache_spec, cache_spec, new_spec, new_spec,
            pl.BlockSpec((None, tk, LANES), lambda b, c: (b, c, 0)),
            pl.BlockSpec((None, tq, LANES), lambda b, c: (b, 0, 0)),
            pl.BlockSpec((None, n_head, tq), lambda b, c: (b, 0, 0)),
        ],
        out_specs=new_spec,
        out_shape=jax.ShapeDtypeStruct((bsz, tq, hw), BF16),
        scratch_shapes=_attn_scratch(n_pair, tq),
        compiler_params=_params(2),
        name="fox_attn_sample",
    )(q, k_rows, v_rows, k_new, v_new, cext_cache, cext_new, ct_new)


def _pad_to(a, shape):
    return jnp.pad(a, [(0, s - d) for s, d in zip(shape, a.shape)])


def kernel(x_prompt, x_sample, state_gla, cache_fox_k, cache_fox_v, cache_fox_logf,
           norm_mix, gla_w_in, gla_w_g2, gla_b_g, gla_norm, gla_w_out,
           fox_w_in, fox_b_f, fox_w_out, norm_ffn, ffn_w_in, ffn_w_down, norm_final):
    d = x_prompt.shape[-1]
    depth = norm_mix.shape[0]
    groups = [x_prompt, x_sample]
    shapes = [x.shape for x in groups]
    xs = [x.reshape(-1, d) for x in groups]
    row_tiles = [min(512, x.shape[0]) for x in xs]

    gla_states = [[], []]
    fox_k, fox_v, fox_f = [[], []], [[], []], [[], []]
    for i in range(depth):
        j = i // 2
        g_mix = norm_mix[i].reshape(1, d)
        g_ffn = norm_ffn[i].reshape(1, d)
        w_ffn_in = ffn_w_in[i].astype(BF16)
        w_ffn_down = ffn_w_down[i].astype(BF16)
        last = i == depth - 1
        if i % 2 == 0:
            _, n_head, dk, dv = state_gla.shape[1:]
            hk, hv = n_head * dk, n_head * dv
            n_main = 2 * hk + 2 * hv
            w_in = gla_w_in[j]
            w_main = w_in[:, :n_main].astype(BF16)
            w_gl = _pad_to(w_in[:, n_main:], (d, LANES)).astype(BF16)
            w_g2 = _pad_to(gla_w_g2[j], (LANES, hk)).astype(BF16)
            b_g = gla_b_g[j].reshape(1, hk)
            w_out = gla_w_out[j].astype(BF16)
            norm_g = gla_norm[j].reshape(1, hv)
            s0s = [jnp.zeros((shapes[0][0], n_head, dk, dv), F32), state_gla[j]]
            for gi in range(2):
                bsz, t_len, _ = shapes[gi]
                proj, glog = _gla_proj(xs[gi], g_mix, w_main, w_gl, w_g2, b_g, row_tiles[gi])
                og, s_fin = _gla_mix(proj.reshape(bsz, t_len, n_main), glog.reshape(bsz, t_len, hk),
                                     s0s[gi], norm_g, min(t_len, 256))
                gla_states[gi].append(s_fin)
                xs[gi] = _post(xs[gi], og.reshape(-1, hv), w_out, g_ffn, w_ffn_in, w_ffn_down,
                               norm_final.reshape(1, d), row_tiles[gi], last)
        else:
            n_head = fox_b_f.shape[1]
            hw = fox_w_out.shape[1]
            w_in = fox_w_in[j]
            w_main = w_in[:, :3 * hw].astype(BF16)
            w_ft = w_in[:, 3 * hw:].T.astype(BF16)
            bf_col = fox_b_f[j].reshape(n_head, 1)
            w_out = fox_w_out[j].astype(BF16)
            for gi in range(2):
                bsz, t_len, _ = shapes[gi]
                x3 = xs[gi].reshape(bsz, t_len, d)
                if gi == 0:
                    q, k, v, logf, ct, kaug, vt = _fox_proj(x3, g_mix, w_main, w_ft, bf_col, None,
                                                            min(512, t_len), n_head)
                    o = _fox_attn_prompt(q, kaug, vt, ct, 256, 4, n_head)
                else:
                    cache_len = cache_fox_logf.shape[2]
                    q, k, v, logf, ct, cext, cext_cache = _fox_proj(
                        x3, g_mix, w_main, w_ft, bf_col,
                        jnp.transpose(cache_fox_logf[j], (0, 2, 1)), t_len, n_head)
                    o = _fox_attn_sample(q, k, v, cache_fox_k[j], cache_fox_v[j],
                                         cext_cache, cext, ct, min(512, cache_len))
                fox_k[gi].append(k.reshape(bsz, t_len, n_head, hw // n_head))
                fox_v[gi].append(v.reshape(bsz, t_len, n_head, hw // n_head))
                fox_f[gi].append(logf)
                xs[gi] = _post(xs[gi], o.reshape(-1, hw), w_out, g_ffn, w_ffn_in, w_ffn_down,
                               norm_final.reshape(1, d), row_tiles[gi], last)

    y_prompt = xs[0].reshape(shapes[0])
    y_sample = xs[1].reshape(shapes[1])
    st = lambda parts: jnp.stack(parts, axis=0)
    return (y_prompt, y_sample, st(gla_states[0]), st(fox_k[0]), st(fox_v[0]), st(fox_f[0]),
            st(gla_states[1]), st(fox_k[1]), st(fox_v[1]), st(fox_f[1]))
```

```python
import functools

import jax
import jax.numpy as jnp
from jax import lax
from jax.experimental import pallas as pl
from jax.experimental.pallas import tpu as pltpu

F32 = jnp.float32
BF16 = jnp.bfloat16

EPS = 1e-6
MASK_VALUE = -1e30

GLA_HEADS = 4
GLA_CHUNK = 64
GLA_GATE_TAU = 16.0

LANES = 128
VMEM_LIMIT_BYTES = 56 * 1024 * 1024


def _params(n_grid):
    return pltpu.CompilerParams(
        dimension_semantics=("arbitrary",) * n_grid,
        vmem_limit_bytes=VMEM_LIMIT_BYTES,
    )


def _resident(shape):
    nd = len(shape)
    return pl.BlockSpec(shape, lambda *_: (0,) * nd, pipeline_mode=pl.Buffered(1))


def _dot(a, b):
    return jnp.dot(a, b, preferred_element_type=F32)


def _dot_nt(a, b):
    return lax.dot_general(a, b, (((1,), (1,)), ((), ())), preferred_element_type=F32)


def _dot_tn(a, b):
    return lax.dot_general(a, b, (((0,), (0,)), ((), ())), preferred_element_type=F32)


def _split3(x):
    hi = x.astype(BF16)
    r1 = x - hi.astype(F32)
    mid = r1.astype(BF16)
    lo = (r1 - mid.astype(F32)).astype(BF16)
    return hi, mid, lo


def _sum01(dot_fn, x, ones_first, mat01):
    acc = None
    for part in _split3(x):
        term = dot_fn(mat01, part) if ones_first else dot_fn(part, mat01)
        acc = term if acc is None else acc + term
    return acc


def _rmsnorm(x, g):
    var = jnp.mean(x * x, axis=-1, keepdims=True)
    return x * lax.rsqrt(var + EPS) * g


def _log_sigmoid(z):
    return jnp.minimum(z, 0.0) - jnp.log1p(jnp.exp(-jnp.abs(z)))


def _silu(z):
    return z * jax.nn.sigmoid(z)


def _identity(n):
    r = lax.broadcasted_iota(jnp.int32, (n, n), 0)
    c = lax.broadcasted_iota(jnp.int32, (n, n), 1)
    return (r == c).astype(BF16)


def _gla_proj_kernel(x_ref, g_ref, w_ref, wgl_ref, wg2_ref, bg_ref, proj_ref, glog_ref):
    h = _rmsnorm(x_ref[...], g_ref[...]).astype(BF16)
    proj_ref[...] = _dot(h, w_ref[...])
    gl = _dot(h, wgl_ref[...]).astype(BF16)
    z = _dot(gl, wg2_ref[...]) + bg_ref[...]
    glog_ref[...] = _log_sigmoid(z) / GLA_GATE_TAU


def _gla_proj(x2d, g, w_main, w_gl, w_g2, b_g, tm):
    n, d = x2d.shape
    n_main, hk = w_main.shape[1], w_g2.shape[1]
    return pl.pallas_call(
        _gla_proj_kernel,
        grid=(n // tm,),
        in_specs=[
            pl.BlockSpec((tm, d), lambda i: (i, 0)),
            _resident((1, d)),
            _resident(w_main.shape),
            _resident(w_gl.shape),
            _resident(w_g2.shape),
            _resident((1, hk)),
        ],
        out_specs=[
            pl.BlockSpec((tm, n_main), lambda i: (i, 0)),
            pl.BlockSpec((tm, hk), lambda i: (i, 0)),
        ],
        out_shape=[
            jax.ShapeDtypeStruct((n, n_main), F32),
            jax.ShapeDtypeStruct((n, hk), F32),
        ],
        compiler_params=_params(1),
        name="gla_proj",
    )(x2d, g, w_main, w_gl, w_g2, b_g)


def _gla_mix_kernel(q_ref, k_ref, v_ref, r_ref, glog_ref, s0_ref, ng_ref,
                    og_ref, sfin_ref, s_ref, *, tb, dk, dv):
    t = pl.program_id(1)
    n_chunk = tb // GLA_CHUNK

    @pl.when(t == 0)
    def _():
        s_ref[...] = s0_ref[...]

    row = lax.broadcasted_iota(jnp.int32, (tb, tb), 0)
    col = lax.broadcasted_iota(jnp.int32, (tb, tb), 1)
    same_chunk = (row // GLA_CHUNK) == (col // GLA_CHUNK)
    causal = same_chunk & (col <= row)
    cum_mat = causal.astype(BF16)
    tot_mat = same_chunk.astype(BF16)
    row_c = lax.broadcasted_iota(jnp.int32, (tb, LANES), 0)
    col_c = lax.broadcasted_iota(jnp.int32, (tb, LANES), 1)
    chunk_sel = ((row_c // GLA_CHUNK) == col_c).astype(BF16)

    g = glog_ref[...]
    b = _sum01(_dot, g, True, cum_mat)
    b_last = _sum01(_dot, g, True, tot_mat)
    dec_t = jnp.exp(_sum01(_dot_tn, g, False, chunk_sel))
    q = q_ref[...]
    k = k_ref[...]
    qe = (q * jnp.exp(b) * (dk ** -0.5)).astype(BF16)
    ke = (k * jnp.exp(-b)).astype(BF16)
    kd = (k * jnp.exp(b_last - b)).astype(BF16)

    for h in range(GLA_HEADS):
        ks = slice(h * dk, (h + 1) * dk)
        vs = slice(h * dv, (h + 1) * dv)
        qe_h, ke_h, kd_h = qe[:, ks], ke[:, ks], kd[:, ks]
        v_h = v_ref[:, vs].astype(BF16)
        a = jnp.where(causal, _dot_nt(qe_h, ke_h), 0.0).astype(BF16)
        o_intra = _dot(a, v_h)
        s = s_ref[h]
        o_parts = []
        for c in range(n_chunk):
            rs = slice(c * GLA_CHUNK, (c + 1) * GLA_CHUNK)
            o_parts.append(o_intra[rs] + _dot(qe_h[rs], s.astype(BF16)))
            s = s * dec_t[ks, c:c + 1] + _dot_tn(kd_h[rs], v_h[rs])
        s_ref[h] = s
        o = o_parts[0] if n_chunk == 1 else jnp.concatenate(o_parts, axis=0)
        on = _rmsnorm(o, ng_ref[:, vs])
        og_ref[:, vs] = (on * _silu(r_ref[:, vs])).astype(BF16)

    @pl.when(t == pl.num_programs(1) - 1)
    def _():
        sfin_ref[...] = s_ref[...]


def _gla_mix(proj3, glog3, s0, norm_g, tb):
    bsz, t_len, _ = proj3.shape
    _, n_head, dk, dv = s0.shape
    hk, hv = n_head * dk, n_head * dv
    assert t_len % tb == 0 and tb % GLA_CHUNK == 0 and hv == 2 * hk
    kern = functools.partial(_gla_mix_kernel, tb=tb, dk=dk, dv=dv)
    state_spec = pl.BlockSpec((None, n_head, dk, dv), lambda b, t: (b, 0, 0, 0))
    return pl.pallas_call(
        kern,
        grid=(bsz, t_len // tb),
        in_specs=[
            pl.BlockSpec((None, tb, hk), lambda b, t: (b, t, 0)),
            pl.BlockSpec((None, tb, hk), lambda b, t: (b, t, 1)),
            pl.BlockSpec((None, tb, hv), lambda b, t: (b, t, 1)),
            pl.BlockSpec((None, tb, hv), lambda b, t: (b, t, 2)),
            pl.BlockSpec((None, tb, hk), lambda b, t: (b, t, 0)),
            state_spec,
            _resident((1, hv)),
        ],
        out_specs=[
            pl.BlockSpec((None, tb, hv), lambda b, t: (b, t, 0)),
            state_spec,
        ],
        out_shape=[
            jax.ShapeDtypeStruct((bsz, t_len, hv), BF16),
            jax.ShapeDtypeStruct(s0.shape, F32),
        ],
        scratch_shapes=[pltpu.VMEM((n_head, dk, dv), F32)],
        compiler_params=_params(2),
        name="gla_mix",
    )(proj3, proj3, proj3, proj3, glog3, s0, norm_g)


def _post_kernel(x_ref, o_ref, wo_ref, g_ref, win_ref, wdown_ref, gfin_ref, y_ref, act_ref,
                 *, d_ff, ff_tile, final_norm):
    x1 = x_ref[...] + _dot(o_ref[...], wo_ref[...])
    h = _rmsnorm(x1, g_ref[...]).astype(BF16)
    for j in range(d_ff // ff_tile):
        gate = _dot(h, win_ref[:, j * ff_tile:(j + 1) * ff_tile])
        up = _dot(h, win_ref[:, d_ff + j * ff_tile:d_ff + (j + 1) * ff_tile])
        act_ref[:, j * ff_tile:(j + 1) * ff_tile] = (_silu(gate) * up).astype(BF16)
    y = x1 + _dot(act_ref[...], wdown_ref[...])
    if final_norm:
        y = _rmsnorm(y, gfin_ref[...])
    y_ref[...] = y


def _post(x2d, o2d, w_out, g_ffn, w_in, w_down, g_final, tm, final_norm):
    n, d = x2d.shape
    d_ff = w_down.shape[0]
    ff_tile = 256
    assert d_ff % ff_tile == 0 and n % tm == 0
    kern = functools.partial(_post_kernel, d_ff=d_ff, ff_tile=ff_tile, final_norm=final_norm)
    return pl.pallas_call(
        kern,
        grid=(n // tm,),
        in_specs=[
            pl.BlockSpec((tm, d), lambda i: (i, 0)),
            pl.BlockSpec((tm, o2d.shape[1]), lambda i: (i, 0)),
            _resident(w_out.shape),
            _resident((1, d)),
            _resident(w_in.shape),
            _resident(w_down.shape),
            _resident((1, d)),
        ],
        out_specs=pl.BlockSpec((tm, d), lambda i: (i, 0)),
        out_shape=jax.ShapeDtypeStruct((n, d), F32),
        scratch_shapes=[pltpu.VMEM((tm, d_ff), BF16)],
        compiler_params=_params(1),
        name="post_final" if final_norm else "post",
    )(x2d, o2d, w_out, g_ffn, w_in, w_down, g_final)


def _lane_cumsum(x, tile):
    r = lax.broadcasted_iota(jnp.int32, (tile, tile), 0)
    c = lax.broadcasted_iota(jnp.int32, (tile, tile), 1)
    upper = (r <= c).astype(BF16)
    rows = x.shape[0]
    carry = jnp.zeros((rows, 1), F32)
    out = []
    for j in range(x.shape[1] // tile):
        parts = jnp.concatenate(_split3(x[:, j * tile:(j + 1) * tile]), axis=0)
        s = _dot(parts, upper)
        blk = carry + s[0:rows] + s[rows:2 * rows] + s[2 * rows:3 * rows]
        out.append(blk)
        carry = blk[:, tile - 1:tile]
    return out[0] if len(out) == 1 else jnp.concatenate(out, axis=1)


def _to_columns(x_t):
    rows = x_t.shape[0]
    parts = jnp.concatenate(_split3(x_t), axis=0)
    r = lax.broadcasted_iota(jnp.int32, (3 * rows, LANES), 0)
    c = lax.broadcasted_iota(jnp.int32, (3 * rows, LANES), 1)
    place = ((r % rows) == c).astype(BF16)
    return _dot_tn(parts, place)


def _key_ext_cols(c_cols, n_head):
    hi, mid, lo = (p.astype(F32) for p in _split3(c_cols))
    lane = lax.broadcasted_iota(jnp.int32, c_cols.shape, 1)
    ones = ((lane >= 3 * n_head) & (lane < 3 * n_head + 3)).astype(F32)
    ext = hi + pltpu.roll(mid, n_head, 1) + pltpu.roll(lo, 2 * n_head, 1) + ones
    return ext.astype(BF16)


def _key_ext_rows(c_rows):
    n_head, n = c_rows.shape
    hi, mid, lo = _split3(c_rows)
    r = lax.broadcasted_iota(jnp.int32, (n_head, n), 0)
    ones = (r < 3).astype(BF16)
    zeros = jnp.zeros((LANES - 4 * n_head, n), BF16)
    return jnp.concatenate([hi, mid, lo, ones, zeros], axis=0)


def _query_ext(idx, hh, n_head, cq):
    cq_hi, cq_mid, cq_lo = (p.astype(F32) for p in _split3(cq))
    ext = jnp.where((idx == hh) | (idx == n_head + hh) | (idx == 2 * n_head + hh), -1.0, 0.0)
    ext = jnp.where(idx == 3 * n_head, cq_hi, ext)
    ext = jnp.where(idx == 3 * n_head + 1, cq_mid, ext)
    ext = jnp.where(idx == 3 * n_head + 2, cq_lo, ext)
    return ext.astype(BF16)


def _fox_proj_prompt_kernel(x_ref, g_ref, wt_ref, bfc_ref,
                            qt_ref, kaug_ref, kt_ref, vt_ref, vtb_ref, logft_ref, ct_ref, carry_ref,
                            *, hw, n_head, scale):
    t = pl.program_id(1)
    tm = x_ref.shape[0]

    @pl.when(t == 0)
    def _():
        carry_ref[...] = jnp.zeros_like(carry_ref)

    h = _rmsnorm(x_ref[...], g_ref[...]).astype(BF16)
    qt_ref[...] = (_dot_nt(wt_ref[0:hw, :], h) * scale).astype(BF16)
    kt_ref[...] = _dot_nt(wt_ref[hw:2 * hw, :], h)
    vt = _dot_nt(wt_ref[2 * hw:3 * hw, :], h)
    vt_ref[...] = vt
    vtb_ref[...] = vt.astype(BF16)
    k = _dot_nt(h, wt_ref[hw:2 * hw, :])

    logf_t = _log_sigmoid(_dot_nt(wt_ref[3 * hw:3 * hw + n_head, :], h) + bfc_ref[...])
    logft_ref[...] = logf_t
    ct_blk = _lane_cumsum(logf_t, min(tm, 256)) + carry_ref[:, 0:1]
    ct_ref[...] = ct_blk
    carry_ref[...] = jnp.broadcast_to(ct_blk[:, tm - 1:tm], carry_ref.shape)
    cext = _key_ext_cols(_to_columns(ct_blk), n_head)
    for p in range(hw // LANES):
        kaug_ref[p, :, 0:LANES] = k[:, p * LANES:(p + 1) * LANES].astype(BF16)
        kaug_ref[p, :, LANES:2 * LANES] = cext


def _fox_proj_prompt(x3, g, wt, bf_col, tm, n_head):
    bsz, t_len, d = x3.shape
    hw = (wt.shape[0] - n_head) // 3
    n_pair = hw // LANES
    assert t_len % tm == 0 and 3 * n_head + 3 <= LANES
    kern = functools.partial(_fox_proj_prompt_kernel, hw=hw, n_head=n_head, scale=(hw // n_head) ** -0.5)
    feat = pl.BlockSpec((None, hw, tm), lambda b, t: (b, 0, t))
    head = pl.BlockSpec((None, n_head, tm), lambda b, t: (b, 0, t))
    return pl.pallas_call(
        kern,
        grid=(bsz, t_len // tm),
        in_specs=[
            pl.BlockSpec((None, tm, d), lambda b, t: (b, t, 0)),
            _resident((1, d)),
            _resident(wt.shape),
            _resident(bf_col.shape),
        ],
        out_specs=[
            feat,
            pl.BlockSpec((None, n_pair, tm, 2 * LANES), lambda b, t: (b, 0, t, 0)),
            feat, feat, feat, head, head,
        ],
        out_shape=[
            jax.ShapeDtypeStruct((bsz, hw, t_len), BF16),
            jax.ShapeDtypeStruct((bsz, n_pair, t_len, 2 * LANES), BF16),
            jax.ShapeDtypeStruct((bsz, hw, t_len), F32),
            jax.ShapeDtypeStruct((bsz, hw, t_len), F32),
            jax.ShapeDtypeStruct((bsz, hw, t_len), BF16),
            jax.ShapeDtypeStruct((bsz, n_head, t_len), F32),
            jax.ShapeDtypeStruct((bsz, n_head, t_len), F32),
        ],
        scratch_shapes=[pltpu.VMEM((n_head, LANES), F32)],
        compiler_params=_params(2),
        name="fox_proj_prompt",
    )(x3, g, wt, bf_col)


def _fox_proj_sample_kernel(x_ref, g_ref, wt_ref, bfc_ref, lct_ref,
                            q_ref, k_ref, v_ref, logft_ref, ccol_ref, cextn_ref, cextc_ref,
                            *, hw, n_head, scale):
    cache_len = lct_ref.shape[1]
    tm = x_ref.shape[0]
    h = _rmsnorm(x_ref[...], g_ref[...]).astype(BF16)
    q_ref[...] = (_dot_nt(h, wt_ref[0:hw, :]) * scale).astype(BF16)
    k_ref[...] = _dot_nt(h, wt_ref[hw:2 * hw, :])
    v_ref[...] = _dot_nt(h, wt_ref[2 * hw:3 * hw, :])

    ct_cache = _lane_cumsum(lct_ref[...], min(cache_len, 256))
    cextc_ref[...] = _key_ext_rows(ct_cache)
    logf_t = _log_sigmoid(_dot_nt(wt_ref[3 * hw:3 * hw + n_head, :], h) + bfc_ref[...])
    logft_ref[...] = logf_t
    ct_new = _lane_cumsum(logf_t, tm) + ct_cache[:, cache_len - 1:cache_len]
    cextn_ref[...] = _key_ext_rows(ct_new)
    ccol_ref[...] = _to_columns(ct_new)


def _fox_proj_sample(x3, g, wt, bf_col, logf_cache_t, n_head):
    bsz, tm, d = x3.shape
    hw = (wt.shape[0] - n_head) // 3
    cache_len = logf_cache_t.shape[2]
    assert 4 * n_head <= LANES
    kern = functools.partial(_fox_proj_sample_kernel, hw=hw, n_head=n_head, scale=(hw // n_head) ** -0.5)
    tok = pl.BlockSpec((None, tm, hw), lambda b: (b, 0, 0))
    return pl.pallas_call(
        kern,
        grid=(bsz,),
        in_specs=[
            pl.BlockSpec((None, tm, d), lambda b: (b, 0, 0)),
            _resident((1, d)),
            _resident(wt.shape),
            _resident(bf_col.shape),
            pl.BlockSpec((None, n_head, cache_len), lambda b: (b, 0, 0)),
        ],
        out_specs=[
            tok, tok, tok,
            pl.BlockSpec((None, n_head, tm), lambda b: (b, 0, 0)),
            pl.BlockSpec((None, tm, LANES), lambda b: (b, 0, 0)),
            pl.BlockSpec((None, LANES, tm), lambda b: (b, 0, 0)),
            pl.BlockSpec((None, LANES, cache_len), lambda b: (b, 0, 0)),
        ],
        out_shape=[
            jax.ShapeDtypeStruct((bsz, tm, hw), BF16),
            jax.ShapeDtypeStruct((bsz, tm, hw), F32),
            jax.ShapeDtypeStruct((bsz, tm, hw), F32),
            jax.ShapeDtypeStruct((bsz, n_head, tm), F32),
            jax.ShapeDtypeStruct((bsz, tm, LANES), F32),
            jax.ShapeDtypeStruct((bsz, LANES, tm), BF16),
            jax.ShapeDtypeStruct((bsz, LANES, cache_len), BF16),
        ],
        compiler_params=_params(1),
        name="fox_proj_sample",
    )(x3, g, wt, bf_col, logf_cache_t)


def _attn_setup(qt_ref, cq_rows, qaug_ref, m_ref, l_ref, acc_ref, *, head0, n_pair, tq, hd, n_head):
    rr = lax.broadcasted_iota(jnp.int32, (LANES, tq), 0)
    for p in range(n_pair):
        q_t = qt_ref[p * LANES:(p + 1) * LANES, :]
        for h in range(2):
            cols = slice(h * tq, (h + 1) * tq)
            qaug_ref[p, 0:LANES, cols] = jnp.where(rr // hd == h, q_t, jnp.zeros_like(q_t))
            qaug_ref[p, LANES:2 * LANES, cols] = _query_ext(rr, head0 + 2 * p + h, n_head, cq_rows[2 * p + h])
    m_ref[...] = jnp.full_like(m_ref, MASK_VALUE)
    l_ref[...] = jnp.zeros_like(l_ref)
    acc_ref[...] = jnp.zeros_like(acc_ref)


def _attn_update(n_pair, kaug_of, vt_of, mask, qaug_ref, m_ref, l_ref, acc_ref, s0_ref, next_kaug0):
    s_next = s0_ref[...]
    for p in range(n_pair):
        s = s_next
        if p + 1 < n_pair:
            s_next = _dot(kaug_of(p + 1), qaug_ref[p + 1])
        elif next_kaug0 is not None:
            s0_ref[...] = _dot(next_kaug0(), qaug_ref[0])
        if mask is not None:
            s = jnp.where(mask, s, MASK_VALUE)
        m_old = m_ref[p]
        m_new = jnp.maximum(m_old, jnp.max(s, axis=0, keepdims=True))
        alpha = jnp.exp(m_old - m_new)
        pr = jnp.exp(s - m_new)
        l_ref[p] = alpha * l_ref[p] + jnp.sum(pr, axis=0, keepdims=True)
        m_ref[p] = m_new
        acc_ref[p] = acc_ref[p] * alpha + _dot(vt_of(p), pr.astype(BF16))


def _attn_finish(o_ref, l_ref, acc_ref, *, n_pair, tq, hd):
    for p in range(n_pair):
        full = acc_ref[p] / l_ref[p]
        z = jnp.concatenate([full[0:hd, 0:tq], full[hd:2 * hd, tq:2 * tq]], axis=0)
        o_ref[:, p * LANES:(p + 1) * LANES] = z.T.astype(o_ref.dtype)


def _fox_attn_prompt_kernel(qt_ref, kaug_ref, vt_ref, ct_ref, o_ref,
                            qaug_ref, m_ref, l_ref, acc_ref, s0_ref, *, tq, n_pair, hd, n_head):
    g = pl.program_id(1)
    i = pl.program_id(2)
    qs = pl.ds(pl.multiple_of(i * tq, tq), tq)
    cq_rows = [ct_ref[h:h + 1, qs] for h in range(2 * n_pair)]
    _attn_setup(qt_ref, cq_rows, qaug_ref, m_ref, l_ref, acc_ref,
                head0=g * (2 * n_pair), n_pair=n_pair, tq=tq, hd=hd, n_head=n_head)

    def keys(j):
        return pl.ds(pl.multiple_of(j * tq, tq), tq)

    def kv_block(j, mask, has_next):
        ks = keys(j)
        _attn_update(n_pair, lambda p: kaug_ref[p, ks, :], lambda p: vt_ref[p * LANES:(p + 1) * LANES, ks],
                     mask, qaug_ref, m_ref, l_ref, acc_ref, s0_ref,
                     (lambda: kaug_ref[0, keys(j + 1), :]) if has_next else None)

    def body(j, carry):
        kv_block(j, None, True)
        return carry

    s0_ref[...] = _dot(kaug_ref[0, 0:tq, :], qaug_ref[0])
    lax.fori_loop(0, i, body, 0)
    row = lax.broadcasted_iota(jnp.int32, (tq, 2 * tq), 0)
    col = lax.broadcasted_iota(jnp.int32, (tq, 2 * tq), 1) % tq
    kv_block(i, row <= col, False)
    _attn_finish(o_ref, l_ref, acc_ref, n_pair=n_pair, tq=tq, hd=hd)


def _fox_attn_prompt(qt, kaug, vtb, ct, tq, n_pair, n_head):
    bsz, hw, t_len = qt.shape
    hd = hw // n_head
    assert 2 * hd == LANES and t_len % tq == 0 and tq % LANES == 0
    width = n_pair * LANES
    n_group = hw // width
    ct4 = ct.reshape(bsz, n_group, 2 * n_pair, t_len)
    kern = functools.partial(_fox_attn_prompt_kernel, tq=tq, n_pair=n_pair, hd=hd, n_head=n_head)
    return pl.pallas_call(
        kern,
        grid=(bsz, n_group, t_len // tq),
        in_specs=[
            pl.BlockSpec((None, width, tq), lambda b, g, i: (b, g, i)),
            pl.BlockSpec((None, n_pair, t_len, 2 * LANES), lambda b, g, i: (b, g, 0, 0)),
            pl.BlockSpec((None, width, t_len), lambda b, g, i: (b, g, 0)),
            pl.BlockSpec((None, None, 2 * n_pair, t_len), lambda b, g, i: (b, g, 0, 0)),
        ],
        out_specs=pl.BlockSpec((None, tq, width), lambda b, g, i: (b, i, g)),
        out_shape=jax.ShapeDtypeStruct((bsz, t_len, hw), BF16),
        scratch_shapes=[
            pltpu.VMEM((n_pair, 2 * LANES, 2 * tq), BF16),
            pltpu.VMEM((n_pair, 1, 2 * tq), F32),
            pltpu.VMEM((n_pair, 1, 2 * tq), F32),
            pltpu.VMEM((n_pair, LANES, 2 * tq), F32),
            pltpu.VMEM((tq, 2 * tq), F32),
        ],
        compiler_params=_params(3),
        name="fox_attn_prompt",
    )(qt, kaug, vtb, ct4)


def _fox_attn_sample_kernel(q_ref, ccol_ref, kt_ref, vt_ref, cextc_ref, kn_ref, vn_ref, cextn_ref, o_ref,
                            *, n_pair, hd, n_head):
    g = pl.program_id(1)
    tq = q_ref.shape[0]
    eye = _identity(LANES)
    lane = lax.broadcasted_iota(jnp.int32, (tq, LANES), 1)
    row = lax.broadcasted_iota(jnp.int32, (2 * tq, tq), 0) % tq
    col = lax.broadcasted_iota(jnp.int32, (2 * tq, tq), 1)
    causal_new = col <= row
    ccol = ccol_ref[...]

    def scores(p):
        feat = slice(p * LANES, (p + 1) * LANES)
        q_pair = q_ref[:, feat]
        blocks = []
        for h in range(2):
            hh = g * (2 * n_pair) + 2 * p + h
            cq = jnp.sum(jnp.where(lane == hh, ccol, 0.0), axis=1, keepdims=True)
            top = jnp.where(lane // hd == h, q_pair, jnp.zeros_like(q_pair))
            blocks.append(jnp.concatenate([top, _query_ext(lane, hh, n_head, cq)], axis=1))
        qaug = jnp.concatenate(blocks, axis=0)
        s_c = _dot(qaug, jnp.concatenate([kt_ref[feat, :].astype(BF16), cextc_ref[...]], axis=0))
        kn_t = _dot_nt(eye, kn_ref[:, feat].astype(BF16)).astype(BF16)
        s_n = _dot(qaug, jnp.concatenate([kn_t, cextn_ref[...]], axis=0))
        return s_c, jnp.where(causal_new, s_n, MASK_VALUE)

    nxt = scores(0)
    for p in range(n_pair):
        s_c, s_n = nxt
        if p + 1 < n_pair:
            nxt = scores(p + 1)
        feat = slice(p * LANES, (p + 1) * LANES)
        m = jnp.maximum(jnp.max(s_c, axis=1, keepdims=True), jnp.max(s_n, axis=1, keepdims=True))
        p_c = jnp.exp(s_c - m)
        p_n = jnp.exp(s_n - m)
        l = jnp.sum(p_c, axis=1, keepdims=True) + jnp.sum(p_n, axis=1, keepdims=True)
        o = _dot_nt(p_c.astype(BF16), vt_ref[feat, :].astype(BF16)) + _dot(p_n.astype(BF16), vn_ref[:, feat].astype(BF16))
        o = o / l
        o_ref[:, feat] = jnp.where(lane < hd, o[0:tq], o[tq:2 * tq]).astype(o_ref.dtype)


def _fox_attn_sample(q, c_col, kt_cache, vt_cache, cext_cache, k_new, v_new, cext_new, n_pair, n_head):
    bsz, tq, hw = q.shape
    cache_len = kt_cache.shape[2]
    hd = hw // n_head
    assert 2 * hd == LANES and 4 * n_head <= LANES
    width = n_pair * LANES
    new_spec = pl.BlockSpec((None, tq, width), lambda b, g: (b, 0, g))
    cache_spec = pl.BlockSpec((None, width, cache_len), lambda b, g: (b, g, 0))
    kern = functools.partial(_fox_attn_sample_kernel, n_pair=n_pair, hd=hd, n_head=n_head)
    return pl.pallas_call(
        kern,
        grid=(bsz, hw // width),
        in_specs=[
            new_spec,
            pl.BlockSpec((None, tq, LANES), lambda b, g: (b, 0, 0)),
            cache_spec, cache_spec,
            pl.BlockSpec((None, LANES, cache_len), lambda b, g: (b, 0, 0)),
            new_spec, new_spec,
            pl.BlockSpec((None, LANES, tq), lambda b, g: (b, 0, 0)),
        ],
        out_specs=new_spec,
        out_shape=jax.ShapeDtypeStruct((bsz, tq, hw), BF16),
        compiler_params=_params(2),
        name="fox_attn_sample",
    )(q, c_col, kt_cache, vt_cache, cext_cache, k_new, v_new, cext_new)


def _pad_to(a, shape):
    return jnp.pad(a, [(0, s - d) for s, d in zip(shape, a.shape)])


def kernel(x_prompt, x_sample, state_gla, cache_fox_k, cache_fox_v, cache_fox_logf,
           norm_mix, gla_w_in, gla_w_g2, gla_b_g, gla_norm, gla_w_out,
           fox_w_in, fox_b_f, fox_w_out, norm_ffn, ffn_w_in, ffn_w_down, norm_final):
    d = x_prompt.shape[-1]
    depth = norm_mix.shape[0]
    groups = [x_prompt, x_sample]
    shapes = [x.shape for x in groups]
    xs = [x.reshape(-1, d) for x in groups]
    row_tiles = [min(512, x.shape[0]) for x in xs]

    gla_states = [[], []]
    fox_k, fox_v, fox_f = [[], []], [[], []], [[], []]
    for i in range(depth):
        j = i // 2
        g_mix = norm_mix[i].reshape(1, d)
        g_ffn = norm_ffn[i].reshape(1, d)
        w_ffn_in = ffn_w_in[i].astype(BF16)
        w_ffn_down = ffn_w_down[i].astype(BF16)
        last = i == depth - 1
        if i % 2 == 0:
            _, n_head, dk, dv = state_gla.shape[1:]
            hk, hv = n_head * dk, n_head * dv
            n_main = 2 * hk + 2 * hv
            w_in = gla_w_in[j]
            w_main = w_in[:, :n_main].astype(BF16)
            w_gl = _pad_to(w_in[:, n_main:], (d, LANES)).astype(BF16)
            w_g2 = _pad_to(gla_w_g2[j], (LANES, hk)).astype(BF16)
            b_g = gla_b_g[j].reshape(1, hk)
            w_out = gla_w_out[j].astype(BF16)
            norm_g = gla_norm[j].reshape(1, hv)
            s0s = [jnp.zeros((shapes[0][0], n_head, dk, dv), F32), state_gla[j]]
            for gi in range(2):
                bsz, t_len, _ = shapes[gi]
                proj, glog = _gla_proj(xs[gi], g_mix, w_main, w_gl, w_g2, b_g, row_tiles[gi])
                og, s_fin = _gla_mix(proj.reshape(bsz, t_len, n_main), glog.reshape(bsz, t_len, hk),
                                     s0s[gi], norm_g, min(t_len, 256))
                gla_states[gi].append(s_fin)
                xs[gi] = _post(xs[gi], og.reshape(-1, hv), w_out, g_ffn, w_ffn_in, w_ffn_down,
                               norm_final.reshape(1, d), row_tiles[gi], last)
        else:
            n_head = fox_b_f.shape[1]
            hw = fox_w_out.shape[1]
            hd = hw // n_head
            wt = fox_w_in[j].T.astype(BF16)
            bf_col = fox_b_f[j].reshape(n_head, 1)
            w_out = fox_w_out[j].astype(BF16)
            for gi in range(2):
                bsz, t_len, _ = shapes[gi]
                x3 = xs[gi].reshape(bsz, t_len, d)
                if gi == 0:
                    qt, kaug, kt, vt, vtb, logf_t, ct = _fox_proj_prompt(x3, g_mix, wt, bf_col,
                                                                         min(512, t_len), n_head)
                    o = _fox_attn_prompt(qt, kaug, vtb, ct, 256, 4, n_head)
                    k_out = kt.reshape(bsz, n_head, hd, t_len).transpose(0, 3, 1, 2)
                    v_out = vt.reshape(bsz, n_head, hd, t_len).transpose(0, 3, 1, 2)
                else:
                    cache_len = cache_fox_logf.shape[2]
                    q, k, v, logf_t, c_col, cext_new, cext_cache = _fox_proj_sample(
                        x3, g_mix, wt, bf_col, jnp.transpose(cache_fox_logf[j], (0, 2, 1)), n_head)
                    kt_cache = jnp.transpose(cache_fox_k[j], (0, 2, 3, 1)).reshape(bsz, hw, cache_len)
                    vt_cache = jnp.transpose(cache_fox_v[j], (0, 2, 3, 1)).reshape(bsz, hw, cache_len)
                    o = _fox_attn_sample(q, c_col, kt_cache, vt_cache, cext_cache, k, v, cext_new, 4, n_head)
                    k_out = k.reshape(bsz, t_len, n_head, hd)
                    v_out = v.reshape(bsz, t_len, n_head, hd)
                fox_k[gi].append(k_out)
                fox_v[gi].append(v_out)
                fox_f[gi].append(jnp.transpose(logf_t, (0, 2, 1)))
                xs[gi] = _post(xs[gi], o.reshape(-1, hw), w_out, g_ffn, w_ffn_in, w_ffn_down,
                               norm_final.reshape(1, d), row_tiles[gi], last)

    y_prompt = xs[0].reshape(shapes[0])
    y_sample = xs[1].reshape(shapes[1])
    st = lambda parts: jnp.stack(parts, axis=0)
    return (y_prompt, y_sample, st(gla_states[0]), st(fox_k[0]), st(fox_v[0]), st(fox_f[0]),
            st(gla_states[1]), st(fox_k[1]), st(fox_v[1]), st(fox_f[1]))
```

```python
import functools

import jax
import jax.numpy as jnp
from jax import lax
from jax.experimental import pallas as pl
from jax.experimental.pallas import tpu as pltpu

F32 = jnp.float32
BF16 = jnp.bfloat16

EPS = 1e-6
MASK_VALUE = -1e30
LOG2E = 1.4426950408889634

GLA_HEADS = 4
GLA_CHUNK = 64
GLA_GATE_TAU = 16.0

LANES = 128
VT_ROWS = LANES + 16
VMEM_LIMIT_BYTES = 56 * 1024 * 1024


def _params(n_grid):
    return pltpu.CompilerParams(
        dimension_semantics=("arbitrary",) * n_grid,
        vmem_limit_bytes=VMEM_LIMIT_BYTES,
    )


def _resident(shape):
    nd = len(shape)
    return pl.BlockSpec(shape, lambda *_: (0,) * nd, pipeline_mode=pl.Buffered(1))


def _dot(a, b):
    return jnp.dot(a, b, preferred_element_type=F32)


def _dot_nt(a, b):
    return lax.dot_general(a, b, (((1,), (1,)), ((), ())), preferred_element_type=F32)


def _dot_tn(a, b):
    return lax.dot_general(a, b, (((0,), (0,)), ((), ())), preferred_element_type=F32)


def _split3(x):
    hi = x.astype(BF16)
    r1 = x - hi.astype(F32)
    mid = r1.astype(BF16)
    lo = (r1 - mid.astype(F32)).astype(BF16)
    return hi, mid, lo


def _sum01(dot_fn, x, ones_first, mat01):
    acc = None
    for part in _split3(x):
        term = dot_fn(mat01, part) if ones_first else dot_fn(part, mat01)
        acc = term if acc is None else acc + term
    return acc


def _rmsnorm(x, g):
    var = jnp.mean(x * x, axis=-1, keepdims=True)
    return x * lax.rsqrt(var + EPS) * g


def _log_sigmoid(z):
    return jnp.minimum(z, 0.0) - jnp.log1p(jnp.exp(-jnp.abs(z)))


def _silu(z):
    return z * jax.nn.sigmoid(z)


def _identity(n):
    r = lax.broadcasted_iota(jnp.int32, (n, n), 0)
    c = lax.broadcasted_iota(jnp.int32, (n, n), 1)
    return (r == c).astype(BF16)


def _gla_proj_kernel(x_ref, g_ref, w_ref, wgl_ref, wg2_ref, bg_ref, proj_ref, glog_ref):
    h = _rmsnorm(x_ref[...], g_ref[...]).astype(BF16)
    proj_ref[...] = _dot(h, w_ref[...])
    gl = _dot(h, wgl_ref[...]).astype(BF16)
    z = _dot(gl, wg2_ref[...]) + bg_ref[...]
    glog_ref[...] = _log_sigmoid(z) / GLA_GATE_TAU


def _gla_proj(x2d, g, w_main, w_gl, w_g2, b_g, tm):
    n, d = x2d.shape
    n_main, hk = w_main.shape[1], w_g2.shape[1]
    return pl.pallas_call(
        _gla_proj_kernel,
        grid=(n // tm,),
        in_specs=[
            pl.BlockSpec((tm, d), lambda i: (i, 0)),
            _resident((1, d)),
            _resident(w_main.shape),
            _resident(w_gl.shape),
            _resident(w_g2.shape),
            _resident((1, hk)),
        ],
        out_specs=[
            pl.BlockSpec((tm, n_main), lambda i: (i, 0)),
            pl.BlockSpec((tm, hk), lambda i: (i, 0)),
        ],
        out_shape=[
            jax.ShapeDtypeStruct((n, n_main), F32),
            jax.ShapeDtypeStruct((n, hk), F32),
        ],
        compiler_params=_params(1),
        name="gla_proj",
    )(x2d, g, w_main, w_gl, w_g2, b_g)


def _gla_mix_kernel(q_ref, k_ref, v_ref, r_ref, glog_ref, s0_ref, ng_ref,
                    og_ref, sfin_ref, s_ref, *, tb, dk, dv):
    t = pl.program_id(1)
    n_chunk = tb // GLA_CHUNK

    @pl.when(t == 0)
    def _():
        s_ref[...] = s0_ref[...]

    row = lax.broadcasted_iota(jnp.int32, (tb, tb), 0)
    col = lax.broadcasted_iota(jnp.int32, (tb, tb), 1)
    same_chunk = (row // GLA_CHUNK) == (col // GLA_CHUNK)
    causal = same_chunk & (col <= row)
    cum_mat = causal.astype(BF16)
    tot_mat = same_chunk.astype(BF16)
    row_c = lax.broadcasted_iota(jnp.int32, (tb, LANES), 0)
    col_c = lax.broadcasted_iota(jnp.int32, (tb, LANES), 1)
    chunk_sel = ((row_c // GLA_CHUNK) == col_c).astype(BF16)

    g = glog_ref[...]
    b = _sum01(_dot, g, True, cum_mat)
    b_last = _sum01(_dot, g, True, tot_mat)
    dec_t = jnp.exp(_sum01(_dot_tn, g, False, chunk_sel))
    q = q_ref[...]
    k = k_ref[...]
    qe = (q * jnp.exp(b) * (dk ** -0.5)).astype(BF16)
    ke = (k * jnp.exp(-b)).astype(BF16)
    kd = (k * jnp.exp(b_last - b)).astype(BF16)

    heads = range(GLA_HEADS)
    chunks = [slice(c * GLA_CHUNK, (c + 1) * GLA_CHUNK) for c in range(n_chunk)]
    ksl = [slice(h * dk, (h + 1) * dk) for h in heads]
    vsl = [slice(h * dv, (h + 1) * dv) for h in heads]
    o_intra, upd = [], []
    for h in heads:
        v_h = v_ref[:, vsl[h]].astype(BF16)
        a = jnp.where(causal, _dot_nt(qe[:, ksl[h]], ke[:, ksl[h]]), 0.0).astype(BF16)
        o_intra.append(_dot(a, v_h))
        upd.append([_dot_tn(kd[rs, ksl[h]], v_h[rs]) for rs in chunks])
    s = [s_ref[h] for h in heads]
    o_parts = [[] for _ in heads]
    for c, rs in enumerate(chunks):
        for h in heads:
            o_parts[h].append(o_intra[h][rs] + _dot(qe[rs, ksl[h]], s[h].astype(BF16)))
            s[h] = s[h] * dec_t[ksl[h], c:c + 1] + upd[h][c]
    for h in heads:
        s_ref[h] = s[h]
        o = o_parts[h][0] if n_chunk == 1 else jnp.concatenate(o_parts[h], axis=0)
        on = _rmsnorm(o, ng_ref[:, vsl[h]])
        og_ref[:, vsl[h]] = (on * _silu(r_ref[:, vsl[h]])).astype(BF16)

    @pl.when(t == pl.num_programs(1) - 1)
    def _():
        sfin_ref[...] = s_ref[...]


def _gla_mix(proj3, glog3, s0, norm_g, tb):
    bsz, t_len, _ = proj3.shape
    _, n_head, dk, dv = s0.shape
    hk, hv = n_head * dk, n_head * dv
    assert t_len % tb == 0 and tb % GLA_CHUNK == 0 and hv == 2 * hk
    kern = functools.partial(_gla_mix_kernel, tb=tb, dk=dk, dv=dv)
    state_spec = pl.BlockSpec((None, n_head, dk, dv), lambda b, t: (b, 0, 0, 0))
    return pl.pallas_call(
        kern,
        grid=(bsz, t_len // tb),
        in_specs=[
            pl.BlockSpec((None, tb, hk), lambda b, t: (b, t, 0)),
            pl.BlockSpec((None, tb, hk), lambda b, t: (b, t, 1)),
            pl.BlockSpec((None, tb, hv), lambda b, t: (b, t, 1)),
            pl.BlockSpec((None, tb, hv), lambda b, t: (b, t, 2)),
            pl.BlockSpec((None, tb, hk), lambda b, t: (b, t, 0)),
            state_spec,
            _resident((1, hv)),
        ],
        out_specs=[
            pl.BlockSpec((None, tb, hv), lambda b, t: (b, t, 0)),
            state_spec,
        ],
        out_shape=[
            jax.ShapeDtypeStruct((bsz, t_len, hv), BF16),
            jax.ShapeDtypeStruct(s0.shape, F32),
        ],
        scratch_shapes=[pltpu.VMEM((n_head, dk, dv), F32)],
        compiler_params=_params(2),
        name="gla_mix",
    )(proj3, proj3, proj3, proj3, glog3, s0, norm_g)


def _post_kernel(x_ref, o_ref, wo_ref, g_ref, win_ref, wdown_ref, gfin_ref, y_ref, act_ref,
                 *, d_ff, ff_tile, final_norm):
    x1 = x_ref[...] + _dot(o_ref[...], wo_ref[...])
    h = _rmsnorm(x1, g_ref[...]).astype(BF16)
    for j in range(d_ff // ff_tile):
        gate = _dot(h, win_ref[:, j * ff_tile:(j + 1) * ff_tile])
        up = _dot(h, win_ref[:, d_ff + j * ff_tile:d_ff + (j + 1) * ff_tile])
        act_ref[:, j * ff_tile:(j + 1) * ff_tile] = (_silu(gate) * up).astype(BF16)
    y = x1 + _dot(act_ref[...], wdown_ref[...])
    if final_norm:
        y = _rmsnorm(y, gfin_ref[...])
    y_ref[...] = y


def _post(x2d, o2d, w_out, g_ffn, w_in, w_down, g_final, tm, final_norm):
    n, d = x2d.shape
    d_ff = w_down.shape[0]
    ff_tile = 256
    assert d_ff % ff_tile == 0 and n % tm == 0
    kern = functools.partial(_post_kernel, d_ff=d_ff, ff_tile=ff_tile, final_norm=final_norm)
    return pl.pallas_call(
        kern,
        grid=(n // tm,),
        in_specs=[
            pl.BlockSpec((tm, d), lambda i: (i, 0)),
            pl.BlockSpec((tm, o2d.shape[1]), lambda i: (i, 0)),
            _resident(w_out.shape),
            _resident((1, d)),
            _resident(w_in.shape),
            _resident(w_down.shape),
            _resident((1, d)),
        ],
        out_specs=pl.BlockSpec((tm, d), lambda i: (i, 0)),
        out_shape=jax.ShapeDtypeStruct((n, d), F32),
        scratch_shapes=[pltpu.VMEM((tm, d_ff), BF16)],
        compiler_params=_params(1),
        name="post_final" if final_norm else "post",
    )(x2d, o2d, w_out, g_ffn, w_in, w_down, g_final)


def _lane_cumsum(x, tile):
    r = lax.broadcasted_iota(jnp.int32, (tile, tile), 0)
    c = lax.broadcasted_iota(jnp.int32, (tile, tile), 1)
    upper = (r <= c).astype(BF16)
    rows = x.shape[0]
    carry = jnp.zeros((rows, 1), F32)
    out = []
    for j in range(x.shape[1] // tile):
        parts = jnp.concatenate(_split3(x[:, j * tile:(j + 1) * tile]), axis=0)
        s = _dot(parts, upper)
        blk = carry + s[0:rows] + s[rows:2 * rows] + s[2 * rows:3 * rows]
        out.append(blk)
        carry = blk[:, tile - 1:tile]
    return out[0] if len(out) == 1 else jnp.concatenate(out, axis=1)


def _to_columns(x_t):
    rows = x_t.shape[0]
    parts = jnp.concatenate(_split3(x_t), axis=0)
    r = lax.broadcasted_iota(jnp.int32, (3 * rows, LANES), 0)
    c = lax.broadcasted_iota(jnp.int32, (3 * rows, LANES), 1)
    place = ((r % rows) == c).astype(BF16)
    return _dot_tn(parts, place)


def _key_ext_cols(c_cols, n_head):
    hi, mid, lo = (p.astype(F32) for p in _split3(c_cols * LOG2E))
    lane = lax.broadcasted_iota(jnp.int32, c_cols.shape, 1)
    ones = ((lane >= 3 * n_head) & (lane < 3 * n_head + 3)).astype(F32)
    ext = hi + pltpu.roll(mid, n_head, 1) + pltpu.roll(lo, 2 * n_head, 1) + ones
    return ext.astype(BF16)


def _key_ext_rows(c_rows):
    n_head, n = c_rows.shape
    hi, mid, lo = _split3(c_rows * LOG2E)
    r = lax.broadcasted_iota(jnp.int32, (n_head, n), 0)
    ones = (r < 3).astype(BF16)
    zeros = jnp.zeros((LANES - 4 * n_head, n), BF16)
    return jnp.concatenate([hi, mid, lo, ones, zeros], axis=0)


def _query_ext(idx, hh, n_head, cq):
    cq_hi, cq_mid, cq_lo = (p.astype(F32) for p in _split3(cq * LOG2E))
    ext = jnp.where((idx == hh) | (idx == n_head + hh) | (idx == 2 * n_head + hh), -1.0, 0.0)
    ext = jnp.where(idx == 3 * n_head, cq_hi, ext)
    ext = jnp.where(idx == 3 * n_head + 1, cq_mid, ext)
    ext = jnp.where(idx == 3 * n_head + 2, cq_lo, ext)
    return ext.astype(BF16)


def _fox_proj_prompt_kernel(x_ref, g_ref, wt_ref, bfc_ref,
                            qt_ref, kaug_ref, kt_ref, vt_ref, vtb_ref, logft_ref, ct_ref, carry_ref,
                            *, hw, n_head, scale):
    t = pl.program_id(1)
    tm = x_ref.shape[0]

    @pl.when(t == 0)
    def _():
        carry_ref[...] = jnp.zeros_like(carry_ref)

    h = _rmsnorm(x_ref[...], g_ref[...]).astype(BF16)
    qt_ref[...] = (_dot_nt(wt_ref[0:hw, :], h) * scale).astype(BF16)
    kt_ref[...] = _dot_nt(wt_ref[hw:2 * hw, :], h)
    vt = _dot_nt(wt_ref[2 * hw:3 * hw, :], h)
    vt_ref[...] = vt
    ones = jnp.ones((VT_ROWS - LANES, tm), BF16)
    for p in range(hw // LANES):
        vtb_ref[p, 0:LANES, :] = vt[p * LANES:(p + 1) * LANES, :].astype(BF16)
        vtb_ref[p, LANES:VT_ROWS, :] = ones
    k = _dot_nt(h, wt_ref[hw:2 * hw, :])

    logf_t = _log_sigmoid(_dot_nt(wt_ref[3 * hw:3 * hw + n_head, :], h) + bfc_ref[...])
    logft_ref[...] = logf_t
    ct_blk = _lane_cumsum(logf_t, min(tm, 256)) + carry_ref[:, 0:1]
    ct_ref[...] = ct_blk
    carry_ref[...] = jnp.broadcast_to(ct_blk[:, tm - 1:tm], carry_ref.shape)
    cext = _key_ext_cols(_to_columns(ct_blk), n_head)
    for p in range(hw // LANES):
        kaug_ref[p, :, 0:LANES] = k[:, p * LANES:(p + 1) * LANES].astype(BF16)
        kaug_ref[p, :, LANES:2 * LANES] = cext


def _fox_proj_prompt(x3, g, wt, bf_col, tm, n_head):
    bsz, t_len, d = x3.shape
    hw = (wt.shape[0] - n_head) // 3
    n_pair = hw // LANES
    assert t_len % tm == 0 and 3 * n_head + 3 <= LANES
    kern = functools.partial(_fox_proj_prompt_kernel, hw=hw, n_head=n_head,
                             scale=(hw // n_head) ** -0.5 * LOG2E)
    feat = pl.BlockSpec((None, hw, tm), lambda b, t: (b, 0, t))
    head = pl.BlockSpec((None, n_head, tm), lambda b, t: (b, 0, t))
    return pl.pallas_call(
        kern,
        grid=(bsz, t_len // tm),
        in_specs=[
            pl.BlockSpec((None, tm, d), lambda b, t: (b, t, 0)),
            _resident((1, d)),
            _resident(wt.shape),
            _resident(bf_col.shape),
        ],
        out_specs=[
            feat,
            pl.BlockSpec((None, n_pair, tm, 2 * LANES), lambda b, t: (b, 0, t, 0)),
            feat, feat,
            pl.BlockSpec((None, n_pair, VT_ROWS, tm), lambda b, t: (b, 0, 0, t)),
            head, head,
        ],
        out_shape=[
            jax.ShapeDtypeStruct((bsz, hw, t_len), BF16),
            jax.ShapeDtypeStruct((bsz, n_pair, t_len, 2 * LANES), BF16),
            jax.ShapeDtypeStruct((bsz, hw, t_len), F32),
            jax.ShapeDtypeStruct((bsz, hw, t_len), F32),
            jax.ShapeDtypeStruct((bsz, n_pair, VT_ROWS, t_len), BF16),
            jax.ShapeDtypeStruct((bsz, n_head, t_len), F32),
            jax.ShapeDtypeStruct((bsz, n_head, t_len), F32),
        ],
        scratch_shapes=[pltpu.VMEM((n_head, LANES), F32)],
        compiler_params=_params(2),
        name="fox_proj_prompt",
    )(x3, g, wt, bf_col)


def _fox_proj_sample_kernel(x_ref, g_ref, wt_ref, bfc_ref, lct_ref,
                            q_ref, k_ref, v_ref, logft_ref, ccol_ref, cextn_ref, cextc_ref,
                            *, hw, n_head, scale):
    cache_len = lct_ref.shape[1]
    tm = x_ref.shape[0]
    h = _rmsnorm(x_ref[...], g_ref[...]).astype(BF16)
    q_ref[...] = (_dot_nt(h, wt_ref[0:hw, :]) * scale).astype(BF16)
    k_ref[...] = _dot_nt(h, wt_ref[hw:2 * hw, :])
    v_ref[...] = _dot_nt(h, wt_ref[2 * hw:3 * hw, :])

    ct_cache = _lane_cumsum(lct_ref[...], min(cache_len, 256))
    cextc_ref[...] = _key_ext_rows(ct_cache)
    logf_t = _log_sigmoid(_dot_nt(wt_ref[3 * hw:3 * hw + n_head, :], h) + bfc_ref[...])
    logft_ref[...] = logf_t
    ct_new = _lane_cumsum(logf_t, tm) + ct_cache[:, cache_len - 1:cache_len]
    cextn_ref[...] = _key_ext_rows(ct_new)
    ccol_ref[...] = _to_columns(ct_new)


def _fox_proj_sample(x3, g, wt, bf_col, logf_cache_t, n_head):
    bsz, tm, d = x3.shape
    hw = (wt.shape[0] - n_head) // 3
    cache_len = logf_cache_t.shape[2]
    assert 4 * n_head <= LANES
    kern = functools.partial(_fox_proj_sample_kernel, hw=hw, n_head=n_head,
                             scale=(hw // n_head) ** -0.5 * LOG2E)
    tok = pl.BlockSpec((None, tm, hw), lambda b: (b, 0, 0))
    return pl.pallas_call(
        kern,
        grid=(bsz,),
        in_specs=[
            pl.BlockSpec((None, tm, d), lambda b: (b, 0, 0)),
            _resident((1, d)),
            _resident(wt.shape),
            _resident(bf_col.shape),
            pl.BlockSpec((None, n_head, cache_len), lambda b: (b, 0, 0)),
        ],
        out_specs=[
            tok, tok, tok,
            pl.BlockSpec((None, n_head, tm), lambda b: (b, 0, 0)),
            pl.BlockSpec((None, tm, LANES), lambda b: (b, 0, 0)),
            pl.BlockSpec((None, LANES, tm), lambda b: (b, 0, 0)),
            pl.BlockSpec((None, LANES, cache_len), lambda b: (b, 0, 0)),
        ],
        out_shape=[
            jax.ShapeDtypeStruct((bsz, tm, hw), BF16),
            jax.ShapeDtypeStruct((bsz, tm, hw), F32),
            jax.ShapeDtypeStruct((bsz, tm, hw), F32),
            jax.ShapeDtypeStruct((bsz, n_head, tm), F32),
            jax.ShapeDtypeStruct((bsz, tm, LANES), F32),
            jax.ShapeDtypeStruct((bsz, LANES, tm), BF16),
            jax.ShapeDtypeStruct((bsz, LANES, cache_len), BF16),
        ],
        compiler_params=_params(1),
        name="fox_proj_sample",
    )(x3, g, wt, bf_col, logf_cache_t)


def _attn_setup(qt_ref, cq_rows, qaug_ref, m_ref, acc_ref, *, head0, n_pair, tq, hd, n_head):
    rr = lax.broadcasted_iota(jnp.int32, (LANES, tq), 0)
    for p in range(n_pair):
        q_t = qt_ref[p * LANES:(p + 1) * LANES, :]
        for h in range(2):
            cols = slice(h * tq, (h + 1) * tq)
            qaug_ref[p, 0:LANES, cols] = jnp.where(rr // hd == h, q_t, jnp.zeros_like(q_t))
            qaug_ref[p, LANES:2 * LANES, cols] = _query_ext(rr, head0 + 2 * p + h, n_head, cq_rows[2 * p + h])
    m_ref[...] = jnp.full_like(m_ref, MASK_VALUE)
    acc_ref[...] = jnp.zeros_like(acc_ref)


def _attn_update(n_pair, kaug_of, vt_of, vt_prev_last, mask, next_kaug0,
                 qaug_ref, m_ref, acc_ref, s0_ref, plast_ref, alast_ref):
    last = n_pair - 1
    s_next = s0_ref[...]
    acc_ref[last] = acc_ref[last] * alast_ref[...] + _dot(vt_prev_last(), plast_ref[...])
    pending = None
    for p in range(n_pair):
        s = s_next
        if p + 1 < n_pair:
            s_next = _dot(kaug_of(p + 1), qaug_ref[p + 1])
        elif next_kaug0 is not None:
            s0_ref[...] = _dot(next_kaug0(), qaug_ref[0])
        if mask is not None:
            s = jnp.where(mask, s, MASK_VALUE)
        m_old = m_ref[p]
        m_new = jnp.maximum(m_old, jnp.max(s, axis=0, keepdims=True))
        alpha = jnp.exp2(m_old - m_new)
        pr = jnp.exp2(s - m_new).astype(BF16)
        m_ref[p] = m_new
        if pending is not None:
            q, pr_q, alpha_q = pending
            acc_ref[q] = acc_ref[q] * alpha_q + _dot(vt_of(q), pr_q)
        pending = (p, pr, alpha)
    plast_ref[...] = pending[1]
    alast_ref[...] = pending[2]


def _attn_finish(o_ref, acc_ref, *, n_pair, tq, hd):
    for p in range(n_pair):
        full = acc_ref[p, 0:LANES, :] / acc_ref[p, LANES:LANES + 1, :]
        z = jnp.concatenate([full[0:hd, 0:tq], full[hd:2 * hd, tq:2 * tq]], axis=0)
        o_ref[:, p * LANES:(p + 1) * LANES] = z.T.astype(o_ref.dtype)


def _fox_attn_prompt_kernel(qt_ref, kaug_ref, vt_ref, ct_ref, o_ref,
                            qaug_ref, m_ref, acc_ref, s0_ref, plast_ref, alast_ref,
                            *, tq, n_pair, hd, n_head):
    g = pl.program_id(1)
    i = pl.program_id(2)
    last = n_pair - 1
    qs = pl.ds(pl.multiple_of(i * tq, tq), tq)
    cq_rows = [ct_ref[h:h + 1, qs] for h in range(2 * n_pair)]
    _attn_setup(qt_ref, cq_rows, qaug_ref, m_ref, acc_ref,
                head0=g * (2 * n_pair), n_pair=n_pair, tq=tq, hd=hd, n_head=n_head)
    state = (qaug_ref, m_ref, acc_ref, s0_ref, plast_ref, alast_ref)

    def keys(j):
        return pl.ds(pl.multiple_of(j * tq, tq), tq)

    def vt_of(p, ks):
        return vt_ref[p, :, ks]

    def kv_block(j, mask, has_next):
        ks = keys(j)
        _attn_update(n_pair, lambda p: kaug_ref[p, ks, :], lambda p: vt_of(p, ks),
                     lambda: vt_of(last, keys(jnp.maximum(j - 1, 0))), mask,
                     (lambda: kaug_ref[0, keys(j + 1), :]) if has_next else None, *state)

    def body(j, carry):
        kv_block(j, None, True)
        return carry

    s0_ref[...] = _dot(kaug_ref[0, 0:tq, :], qaug_ref[0])
    plast_ref[...] = jnp.zeros_like(plast_ref)
    alast_ref[...] = jnp.ones_like(alast_ref)
    lax.fori_loop(0, i, body, 0)
    row = lax.broadcasted_iota(jnp.int32, (tq, 2 * tq), 0)
    col = lax.broadcasted_iota(jnp.int32, (tq, 2 * tq), 1) % tq
    kv_block(i, row <= col, False)
    acc_ref[last] = acc_ref[last] * alast_ref[...] + _dot(vt_of(last, keys(i)), plast_ref[...])
    _attn_finish(o_ref, acc_ref, n_pair=n_pair, tq=tq, hd=hd)


def _fox_attn_prompt(qt, kaug, vtb, ct, tq, n_pair, n_head):
    bsz, hw, t_len = qt.shape
    hd = hw // n_head
    assert 2 * hd == LANES and t_len % tq == 0 and tq % LANES == 0
    width = n_pair * LANES
    n_group = hw // width
    ct4 = ct.reshape(bsz, n_group, 2 * n_pair, t_len)
    kern = functools.partial(_fox_attn_prompt_kernel, tq=tq, n_pair=n_pair, hd=hd, n_head=n_head)
    return pl.pallas_call(
        kern,
        grid=(bsz, n_group, t_len // tq),
        in_specs=[
            pl.BlockSpec((None, width, tq), lambda b, g, i: (b, g, i)),
            pl.BlockSpec((None, n_pair, t_len, 2 * LANES), lambda b, g, i: (b, g, 0, 0)),
            pl.BlockSpec((None, n_pair, VT_ROWS, t_len), lambda b, g, i: (b, g, 0, 0)),
            pl.BlockSpec((None, None, 2 * n_pair, t_len), lambda b, g, i: (b, g, 0, 0)),
        ],
        out_specs=pl.BlockSpec((None, tq, width), lambda b, g, i: (b, i, g)),
        out_shape=jax.ShapeDtypeStruct((bsz, t_len, hw), BF16),
        scratch_shapes=[
            pltpu.VMEM((n_pair, 2 * LANES, 2 * tq), BF16),
            pltpu.VMEM((n_pair, 1, 2 * tq), F32),
            pltpu.VMEM((n_pair, VT_ROWS, 2 * tq), F32),
            pltpu.VMEM((tq, 2 * tq), F32),
            pltpu.VMEM((tq, 2 * tq), BF16),
            pltpu.VMEM((1, 2 * tq), F32),
        ],
        compiler_params=_params(3),
        name="fox_attn_prompt",
    )(qt, kaug, vtb, ct4)


def _fox_attn_sample_kernel(q_ref, ccol_ref, kt_ref, vt_ref, cextc_ref, kn_ref, vn_ref, cextn_ref, o_ref,
                            *, n_pair, hd, n_head):
    g = pl.program_id(1)
    tq = q_ref.shape[0]
    eye = _identity(LANES)
    lane = lax.broadcasted_iota(jnp.int32, (tq, LANES), 1)
    row = lax.broadcasted_iota(jnp.int32, (2 * tq, tq), 0) % tq
    col = lax.broadcasted_iota(jnp.int32, (2 * tq, tq), 1)
    causal_new = col <= row
    ccol = ccol_ref[...]

    def scores(p):
        feat = slice(p * LANES, (p + 1) * LANES)
        q_pair = q_ref[:, feat]
        blocks = []
        for h in range(2):
            hh = g * (2 * n_pair) + 2 * p + h
            cq = jnp.sum(jnp.where(lane == hh, ccol, 0.0), axis=1, keepdims=True)
            top = jnp.where(lane // hd == h, q_pair, jnp.zeros_like(q_pair))
            blocks.append(jnp.concatenate([top, _query_ext(lane, hh, n_head, cq)], axis=1))
        qaug = jnp.concatenate(blocks, axis=0)
        s_c = _dot(qaug, jnp.concatenate([kt_ref[feat, :].astype(BF16), cextc_ref[...]], axis=0))
        kn_t = _dot_nt(eye, kn_ref[:, feat].astype(BF16)).astype(BF16)
        s_n = _dot(qaug, jnp.concatenate([kn_t, cextn_ref[...]], axis=0))
        return s_c, jnp.where(causal_new, s_n, MASK_VALUE)

    nxt = scores(0)
    for p in range(n_pair):
        s_c, s_n = nxt
        if p + 1 < n_pair:
            nxt = scores(p + 1)
        feat = slice(p * LANES, (p + 1) * LANES)
        m = jnp.maximum(jnp.max(s_c, axis=1, keepdims=True), jnp.max(s_n, axis=1, keepdims=True))
        p_c = jnp.exp2(s_c - m)
        p_n = jnp.exp2(s_n - m)
        l = jnp.sum(p_c, axis=1, keepdims=True) + jnp.sum(p_n, axis=1, keepdims=True)
        o = _dot_nt(p_c.astype(BF16), vt_ref[feat, :].astype(BF16)) + _dot(p_n.astype(BF16), vn_ref[:, feat].astype(BF16))
        o = o / l
        o_ref[:, feat] = jnp.where(lane < hd, o[0:tq], o[tq:2 * tq]).astype(o_ref.dtype)


def _fox_attn_sample(q, c_col, kt_cache, vt_cache, cext_cache, k_new, v_new, cext_new, n_pair, n_head):
    bsz, tq, hw = q.shape
    cache_len = kt_cache.shape[2]
    hd = hw // n_head
    assert 2 * hd == LANES and 4 * n_head <= LANES
    width = n_pair * LANES
    new_spec = pl.BlockSpec((None, tq, width), lambda b, g: (b, 0, g))
    cache_spec = pl.BlockSpec((None, width, cache_len), lambda b, g: (b, g, 0))
    kern = functools.partial(_fox_attn_sample_kernel, n_pair=n_pair, hd=hd, n_head=n_head)
    return pl.pallas_call(
        kern,
        grid=(bsz, hw // width),
        in_specs=[
            new_spec,
            pl.BlockSpec((None, tq, LANES), lambda b, g: (b, 0, 0)),
            cache_spec, cache_spec,
            pl.BlockSpec((None, LANES, cache_len), lambda b, g: (b, 0, 0)),
            new_spec, new_spec,
            pl.BlockSpec((None, LANES, tq), lambda b, g: (b, 0, 0)),
        ],
        out_specs=new_spec,
        out_shape=jax.ShapeDtypeStruct((bsz, tq, hw), BF16),
        compiler_params=_params(2),
        name="fox_attn_sample",
    )(q, c_col, kt_cache, vt_cache, cext_cache, k_new, v_new, cext_new)


def _pad_to(a, shape):
    return jnp.pad(a, [(0, s - d) for s, d in zip(shape, a.shape)])


def kernel(x_prompt, x_sample, state_gla, cache_fox_k, cache_fox_v, cache_fox_logf,
           norm_mix, gla_w_in, gla_w_g2, gla_b_g, gla_norm, gla_w_out,
           fox_w_in, fox_b_f, fox_w_out, norm_ffn, ffn_w_in, ffn_w_down, norm_final):
    d = x_prompt.shape[-1]
    depth = norm_mix.shape[0]
    groups = [x_prompt, x_sample]
    shapes = [x.shape for x in groups]
    xs = [x.reshape(-1, d) for x in groups]
    row_tiles = [min(512, x.shape[0]) for x in xs]

    gla_states = [[], []]
    fox_k, fox_v, fox_f = [[], []], [[], []], [[], []]
    for i in range(depth):
        j = i // 2
        g_mix = norm_mix[i].reshape(1, d)
        g_ffn = norm_ffn[i].reshape(1, d)
        w_ffn_in = ffn_w_in[i].astype(BF16)
        w_ffn_down = ffn_w_down[i].astype(BF16)
        last = i == depth - 1
        if i % 2 == 0:
            _, n_head, dk, dv = state_gla.shape[1:]
            hk, hv = n_head * dk, n_head * dv
            n_main = 2 * hk + 2 * hv
            w_in = gla_w_in[j]
            w_main = w_in[:, :n_main].astype(BF16)
            w_gl = _pad_to(w_in[:, n_main:], (d, LANES)).astype(BF16)
            w_g2 = _pad_to(gla_w_g2[j], (LANES, hk)).astype(BF16)
            b_g = gla_b_g[j].reshape(1, hk)
            w_out = gla_w_out[j].astype(BF16)
            norm_g = gla_norm[j].reshape(1, hv)
            s0s = [jnp.zeros((shapes[0][0], n_head, dk, dv), F32), state_gla[j]]
            for gi in range(2):
                bsz, t_len, _ = shapes[gi]
                proj, glog = _gla_proj(xs[gi], g_mix, w_main, w_gl, w_g2, b_g, row_tiles[gi])
                og, s_fin = _gla_mix(proj.reshape(bsz, t_len, n_main), glog.reshape(bsz, t_len, hk),
                                     s0s[gi], norm_g, min(t_len, 256))
                gla_states[gi].append(s_fin)
                xs[gi] = _post(xs[gi], og.reshape(-1, hv), w_out, g_ffn, w_ffn_in, w_ffn_down,
                               norm_final.reshape(1, d), row_tiles[gi], last)
        else:
            n_head = fox_b_f.shape[1]
            hw = fox_w_out.shape[1]
            hd = hw // n_head
            wt = fox_w_in[j].T.astype(BF16)
            bf_col = fox_b_f[j].reshape(n_head, 1)
            w_out = fox_w_out[j].astype(BF16)
            for gi in range(2):
                bsz, t_len, _ = shapes[gi]
                x3 = xs[gi].reshape(bsz, t_len, d)
                if gi == 0:
                    qt, kaug, kt, vt, vtb, logf_t, ct = _fox_proj_prompt(x3, g_mix, wt, bf_col,
                                                                         min(512, t_len), n_head)
                    o = _fox_attn_prompt(qt, kaug, vtb, ct, 256, 4, n_head)
                    k_out = kt.reshape(bsz, n_head, hd, t_len).transpose(0, 3, 1, 2)
                    v_out = vt.reshape(bsz, n_head, hd, t_len).transpose(0, 3, 1, 2)
                else:
                    cache_len = cache_fox_logf.shape[2]
                    q, k, v, logf_t, c_col, cext_new, cext_cache = _fox_proj_sample(
                        x3, g_mix, wt, bf_col, jnp.transpose(cache_fox_logf[j], (0, 2, 1)), n_head)
                    kt_cache = jnp.transpose(cache_fox_k[j], (0, 2, 3, 1)).reshape(bsz, hw, cache_len)
                    vt_cache = jnp.transpose(cache_fox_v[j], (0, 2, 3, 1)).reshape(bsz, hw, cache_len)
                    o = _fox_attn_sample(q, c_col, kt_cache, vt_cache, cext_cache, k, v, cext_new, 4, n_head)
                    k_out = k.reshape(bsz, t_len, n_head, hd)
                    v_out = v.reshape(bsz, t_len, n_head, hd)
                fox_k[gi].append(k_out)
                fox_v[gi].append(v_out)
                fox_f[gi].append(jnp.transpose(logf_t, (0, 2, 1)))
                xs[gi] = _post(xs[gi], o.reshape(-1, hw), w_out, g_ffn, w_ffn_in, w_ffn_down,
                               norm_final.reshape(1, d), row_tiles[gi], last)

    y_prompt = xs[0].reshape(shapes[0])
    y_sample = xs[1].reshape(shapes[1])
    st = lambda parts: jnp.stack(parts, axis=0)
    return (y_prompt, y_sample, st(gla_states[0]), st(fox_k[0]), st(fox_v[0]), st(fox_f[0]),
            st(gla_states[1]), st(fox_k[1]), st(fox_v[1]), st(fox_f[1]))
```

```python
import functools

import jax
import jax.numpy as jnp
from jax import lax
from jax.experimental import pallas as pl
from jax.experimental.pallas import tpu as pltpu

F32 = jnp.float32
BF16 = jnp.bfloat16

EPS = 1e-6
MASK_VALUE = -1e30
LOG2E = 1.4426950408889634

GLA_HEADS = 4
GLA_CHUNK = 64
GLA_GATE_TAU = 16.0

LANES = 128
VT_ROWS = LANES + 16
VMEM_LIMIT_BYTES = 56 * 1024 * 1024


def _params(n_grid):
    return pltpu.CompilerParams(
        dimension_semantics=("arbitrary",) * n_grid,
        vmem_limit_bytes=VMEM_LIMIT_BYTES,
    )


def _resident(shape):
    nd = len(shape)
    return pl.BlockSpec(shape, lambda *_: (0,) * nd, pipeline_mode=pl.Buffered(1))


def _dot(a, b):
    return jnp.dot(a, b, preferred_element_type=F32)


def _dot_nt(a, b):
    return lax.dot_general(a, b, (((1,), (1,)), ((), ())), preferred_element_type=F32)


def _dot_tn(a, b):
    return lax.dot_general(a, b, (((0,), (0,)), ((), ())), preferred_element_type=F32)


def _split3(x):
    hi = x.astype(BF16)
    r1 = x - hi.astype(F32)
    mid = r1.astype(BF16)
    lo = (r1 - mid.astype(F32)).astype(BF16)
    return hi, mid, lo


def _sum01(dot_fn, x, ones_first, mat01):
    acc = None
    for part in _split3(x):
        term = dot_fn(mat01, part) if ones_first else dot_fn(part, mat01)
        acc = term if acc is None else acc + term
    return acc


def _rmsnorm(x, g):
    var = jnp.mean(x * x, axis=-1, keepdims=True)
    return x * lax.rsqrt(var + EPS) * g


def _log_sigmoid(z):
    return jnp.minimum(z, 0.0) - jnp.log1p(jnp.exp(-jnp.abs(z)))


def _silu(z):
    return z * jax.nn.sigmoid(z)


def _identity(n):
    r = lax.broadcasted_iota(jnp.int32, (n, n), 0)
    c = lax.broadcasted_iota(jnp.int32, (n, n), 1)
    return (r == c).astype(BF16)


def _gla_proj_kernel(x_ref, g_ref, w_ref, wgl_ref, wg2_ref, bg_ref, proj_ref, glog_ref):
    h = _rmsnorm(x_ref[...], g_ref[...]).astype(BF16)
    proj_ref[...] = _dot(h, w_ref[...])
    gl = _dot(h, wgl_ref[...]).astype(BF16)
    z = _dot(gl, wg2_ref[...]) + bg_ref[...]
    glog_ref[...] = _log_sigmoid(z) / GLA_GATE_TAU


def _gla_proj(x2d, g, w_main, w_gl, w_g2, b_g, tm):
    n, d = x2d.shape
    n_main, hk = w_main.shape[1], w_g2.shape[1]
    return pl.pallas_call(
        _gla_proj_kernel,
        grid=(n // tm,),
        in_specs=[
            pl.BlockSpec((tm, d), lambda i: (i, 0)),
            _resident((1, d)),
            _resident(w_main.shape),
            _resident(w_gl.shape),
            _resident(w_g2.shape),
            _resident((1, hk)),
        ],
        out_specs=[
            pl.BlockSpec((tm, n_main), lambda i: (i, 0)),
            pl.BlockSpec((tm, hk), lambda i: (i, 0)),
        ],
        out_shape=[
            jax.ShapeDtypeStruct((n, n_main), F32),
            jax.ShapeDtypeStruct((n, hk), F32),
        ],
        compiler_params=_params(1),
        name="gla_proj",
    )(x2d, g, w_main, w_gl, w_g2, b_g)


def _gla_mix_kernel(q_ref, k_ref, v_ref, r_ref, glog_ref, s0_ref, ng_ref,
                    og_ref, sfin_ref, s_ref, *, tb, dk, dv):
    t = pl.program_id(1)
    n_chunk = tb // GLA_CHUNK

    @pl.when(t == 0)
    def _():
        s_ref[...] = s0_ref[...]

    row = lax.broadcasted_iota(jnp.int32, (tb, tb), 0)
    col = lax.broadcasted_iota(jnp.int32, (tb, tb), 1)
    same_chunk = (row // GLA_CHUNK) == (col // GLA_CHUNK)
    causal = same_chunk & (col <= row)
    cum_mat = causal.astype(BF16)

    g = glog_ref[...]
    b = _sum01(_dot, g, True, cum_mat)
    tot_rows = [b[(c + 1) * GLA_CHUNK - 1:(c + 1) * GLA_CHUNK, :] for c in range(n_chunk)]
    b_last = jnp.concatenate([jnp.broadcast_to(r, (GLA_CHUNK, r.shape[1])) for r in tot_rows], axis=0)
    pad_rows = [jnp.zeros_like(tot_rows[0])] * (8 - n_chunk % 8 if n_chunk % 8 else 0)
    dec_t = jnp.exp(_to_columns(jnp.concatenate(tot_rows + pad_rows, axis=0)))
    q = q_ref[...]
    k = k_ref[...]
    qe = (q * jnp.exp(b) * (dk ** -0.5)).astype(BF16)
    ke = (k * jnp.exp(-b)).astype(BF16)
    kd = (k * jnp.exp(b_last - b)).astype(BF16)

    heads = range(GLA_HEADS)
    chunks = [slice(c * GLA_CHUNK, (c + 1) * GLA_CHUNK) for c in range(n_chunk)]
    ksl = [slice(h * dk, (h + 1) * dk) for h in heads]
    vsl = [slice(h * dv, (h + 1) * dv) for h in heads]
    o_intra, upd = [], []
    for h in heads:
        v_h = v_ref[:, vsl[h]].astype(BF16)
        a = jnp.where(causal, _dot_nt(qe[:, ksl[h]], ke[:, ksl[h]]), 0.0).astype(BF16)
        o_intra.append(_dot(a, v_h))
        upd.append([_dot_tn(kd[rs, ksl[h]], v_h[rs]) for rs in chunks])
    s = [s_ref[h] for h in heads]
    o_parts = [[] for _ in heads]
    for c, rs in enumerate(chunks):
        for h in heads:
            o_parts[h].append(o_intra[h][rs] + _dot(qe[rs, ksl[h]], s[h].astype(BF16)))
            s[h] = s[h] * dec_t[ksl[h], c:c + 1] + upd[h][c]
    for h in heads:
        s_ref[h] = s[h]
        o = o_parts[h][0] if n_chunk == 1 else jnp.concatenate(o_parts[h], axis=0)
        on = _rmsnorm(o, ng_ref[:, vsl[h]])
        og_ref[:, vsl[h]] = (on * _silu(r_ref[:, vsl[h]])).astype(BF16)

    @pl.when(t == pl.num_programs(1) - 1)
    def _():
        sfin_ref[...] = s_ref[...]


def _gla_mix(proj3, glog3, s0, norm_g, tb):
    bsz, t_len, _ = proj3.shape
    _, n_head, dk, dv = s0.shape
    hk, hv = n_head * dk, n_head * dv
    assert t_len % tb == 0 and tb % GLA_CHUNK == 0 and hv == 2 * hk
    kern = functools.partial(_gla_mix_kernel, tb=tb, dk=dk, dv=dv)
    state_spec = pl.BlockSpec((None, n_head, dk, dv), lambda b, t: (b, 0, 0, 0))
    return pl.pallas_call(
        kern,
        grid=(bsz, t_len // tb),
        in_specs=[
            pl.BlockSpec((None, tb, hk), lambda b, t: (b, t, 0)),
            pl.BlockSpec((None, tb, hk), lambda b, t: (b, t, 1)),
            pl.BlockSpec((None, tb, hv), lambda b, t: (b, t, 1)),
            pl.BlockSpec((None, tb, hv), lambda b, t: (b, t, 2)),
            pl.BlockSpec((None, tb, hk), lambda b, t: (b, t, 0)),
            state_spec,
            _resident((1, hv)),
        ],
        out_specs=[
            pl.BlockSpec((None, tb, hv), lambda b, t: (b, t, 0)),
            state_spec,
        ],
        out_shape=[
            jax.ShapeDtypeStruct((bsz, t_len, hv), BF16),
            jax.ShapeDtypeStruct(s0.shape, F32),
        ],
        scratch_shapes=[pltpu.VMEM((n_head, dk, dv), F32)],
        compiler_params=_params(2),
        name="gla_mix",
    )(proj3, proj3, proj3, proj3, glog3, s0, norm_g)


def _post_kernel(x_ref, o_ref, wo_ref, g_ref, win_ref, wdown_ref, gfin_ref, y_ref, act_ref,
                 *, d_ff, ff_tile, final_norm):
    x1 = x_ref[...] + _dot(o_ref[...], wo_ref[...])
    h = _rmsnorm(x1, g_ref[...]).astype(BF16)
    for j in range(d_ff // ff_tile):
        gate = _dot(h, win_ref[:, j * ff_tile:(j + 1) * ff_tile])
        up = _dot(h, win_ref[:, d_ff + j * ff_tile:d_ff + (j + 1) * ff_tile])
        act_ref[:, j * ff_tile:(j + 1) * ff_tile] = (_silu(gate) * up).astype(BF16)
    y = x1 + _dot(act_ref[...], wdown_ref[...])
    if final_norm:
        y = _rmsnorm(y, gfin_ref[...])
    y_ref[...] = y


def _post(x2d, o2d, w_out, g_ffn, w_in, w_down, g_final, tm, final_norm):
    n, d = x2d.shape
    d_ff = w_down.shape[0]
    ff_tile = 256
    assert d_ff % ff_tile == 0 and n % tm == 0
    kern = functools.partial(_post_kernel, d_ff=d_ff, ff_tile=ff_tile, final_norm=final_norm)
    return pl.pallas_call(
        kern,
        grid=(n // tm,),
        in_specs=[
            pl.BlockSpec((tm, d), lambda i: (i, 0)),
            pl.BlockSpec((tm, o2d.shape[1]), lambda i: (i, 0)),
            _resident(w_out.shape),
            _resident((1, d)),
            _resident(w_in.shape),
            _resident(w_down.shape),
            _resident((1, d)),
        ],
        out_specs=pl.BlockSpec((tm, d), lambda i: (i, 0)),
        out_shape=jax.ShapeDtypeStruct((n, d), F32),
        scratch_shapes=[pltpu.VMEM((tm, d_ff), BF16)],
        compiler_params=_params(1),
        name="post_final" if final_norm else "post",
    )(x2d, o2d, w_out, g_ffn, w_in, w_down, g_final)


def _lane_cumsum(x, tile):
    r = lax.broadcasted_iota(jnp.int32, (tile, tile), 0)
    c = lax.broadcasted_iota(jnp.int32, (tile, tile), 1)
    upper = (r <= c).astype(BF16)
    rows = x.shape[0]
    carry = jnp.zeros((rows, 1), F32)
    out = []
    for j in range(x.shape[1] // tile):
        parts = jnp.concatenate(_split3(x[:, j * tile:(j + 1) * tile]), axis=0)
        s = _dot(parts, upper)
        blk = carry + s[0:rows] + s[rows:2 * rows] + s[2 * rows:3 * rows]
        out.append(blk)
        carry = blk[:, tile - 1:tile]
    return out[0] if len(out) == 1 else jnp.concatenate(out, axis=1)


def _to_columns(x_t):
    rows = x_t.shape[0]
    parts = jnp.concatenate(_split3(x_t), axis=0)
    r = lax.broadcasted_iota(jnp.int32, (3 * rows, LANES), 0)
    c = lax.broadcasted_iota(jnp.int32, (3 * rows, LANES), 1)
    place = ((r % rows) == c).astype(BF16)
    return _dot_tn(parts, place)


def _key_ext_cols(c_cols, n_head):
    hi, mid, lo = (p.astype(F32) for p in _split3(c_cols * LOG2E))
    lane = lax.broadcasted_iota(jnp.int32, c_cols.shape, 1)
    ones = ((lane >= 3 * n_head) & (lane < 3 * n_head + 3)).astype(F32)
    ext = hi + pltpu.roll(mid, n_head, 1) + pltpu.roll(lo, 2 * n_head, 1) + ones
    return ext.astype(BF16)


def _key_ext_rows(c_rows):
    n_head, n = c_rows.shape
    hi, mid, lo = _split3(c_rows * LOG2E)
    r = lax.broadcasted_iota(jnp.int32, (n_head, n), 0)
    ones = (r < 3).astype(BF16)
    zeros = jnp.zeros((LANES - 4 * n_head, n), BF16)
    return jnp.concatenate([hi, mid, lo, ones, zeros], axis=0)


def _query_ext(idx, hh, n_head, cq):
    cq_hi, cq_mid, cq_lo = (p.astype(F32) for p in _split3(cq * LOG2E))
    ext = jnp.where((idx == hh) | (idx == n_head + hh) | (idx == 2 * n_head + hh), -1.0, 0.0)
    ext = jnp.where(idx == 3 * n_head, cq_hi, ext)
    ext = jnp.where(idx == 3 * n_head + 1, cq_mid, ext)
    ext = jnp.where(idx == 3 * n_head + 2, cq_lo, ext)
    return ext.astype(BF16)


def _fox_proj_prompt_kernel(x_ref, g_ref, wt_ref, bfc_ref,
                            qt_ref, kaug_ref, kt_ref, vt_ref, vtb_ref, logft_ref, ct_ref, carry_ref,
                            *, hw, n_head, scale):
    t = pl.program_id(1)
    tm = x_ref.shape[0]

    @pl.when(t == 0)
    def _():
        carry_ref[...] = jnp.zeros_like(carry_ref)

    h = _rmsnorm(x_ref[...], g_ref[...]).astype(BF16)
    qt_ref[...] = (_dot_nt(wt_ref[0:hw, :], h) * scale).astype(BF16)
    kt = _dot_nt(wt_ref[hw:2 * hw, :], h)
    kt_ref[...] = kt
    vt = _dot_nt(wt_ref[2 * hw:3 * hw, :], h)
    vt_ref[...] = vt
    ones = jnp.ones((VT_ROWS - LANES, tm), BF16)
    for p in range(hw // LANES):
        vtb_ref[p, 0:LANES, :] = vt[p * LANES:(p + 1) * LANES, :].astype(BF16)
        vtb_ref[p, LANES:VT_ROWS, :] = ones
    k = kt.T

    logf_t = _log_sigmoid(_dot_nt(wt_ref[3 * hw:3 * hw + n_head, :], h) + bfc_ref[...])
    logft_ref[...] = logf_t
    ct_blk = _lane_cumsum(logf_t, min(tm, 256)) + carry_ref[:, 0:1]
    ct_ref[...] = ct_blk
    carry_ref[...] = jnp.broadcast_to(ct_blk[:, tm - 1:tm], carry_ref.shape)
    cext = _key_ext_cols(_to_columns(ct_blk), n_head)
    for p in range(hw // LANES):
        kaug_ref[p, :, 0:LANES] = k[:, p * LANES:(p + 1) * LANES].astype(BF16)
        kaug_ref[p, :, LANES:2 * LANES] = cext


def _fox_proj_prompt(x3, g, wt, bf_col, tm, n_head):
    bsz, t_len, d = x3.shape
    hw = (wt.shape[0] - n_head) // 3
    n_pair = hw // LANES
    assert t_len % tm == 0 and 3 * n_head + 3 <= LANES
    kern = functools.partial(_fox_proj_prompt_kernel, hw=hw, n_head=n_head,
                             scale=(hw // n_head) ** -0.5 * LOG2E)
    feat = pl.BlockSpec((None, hw, tm), lambda b, t: (b, 0, t))
    head = pl.BlockSpec((None, n_head, tm), lambda b, t: (b, 0, t))
    return pl.pallas_call(
        kern,
        grid=(bsz, t_len // tm),
        in_specs=[
            pl.BlockSpec((None, tm, d), lambda b, t: (b, t, 0)),
            _resident((1, d)),
            _resident(wt.shape),
            _resident(bf_col.shape),
        ],
        out_specs=[
            feat,
            pl.BlockSpec((None, n_pair, tm, 2 * LANES), lambda b, t: (b, 0, t, 0)),
            feat, feat,
            pl.BlockSpec((None, n_pair, VT_ROWS, tm), lambda b, t: (b, 0, 0, t)),
            head, head,
        ],
        out_shape=[
            jax.ShapeDtypeStruct((bsz, hw, t_len), BF16),
            jax.ShapeDtypeStruct((bsz, n_pair, t_len, 2 * LANES), BF16),
            jax.ShapeDtypeStruct((bsz, hw, t_len), F32),
            jax.ShapeDtypeStruct((bsz, hw, t_len), F32),
            jax.ShapeDtypeStruct((bsz, n_pair, VT_ROWS, t_len), BF16),
            jax.ShapeDtypeStruct((bsz, n_head, t_len), F32),
            jax.ShapeDtypeStruct((bsz, n_head, t_len), F32),
        ],
        scratch_shapes=[pltpu.VMEM((n_head, LANES), F32)],
        compiler_params=_params(2),
        name="fox_proj_prompt",
    )(x3, g, wt, bf_col)


def _fox_proj_sample_kernel(x_ref, g_ref, wt_ref, bfc_ref, lct_ref,
                            q_ref, k_ref, v_ref, logft_ref, ccol_ref, cextn_ref, cextc_ref,
                            *, hw, n_head, scale):
    nb, tm, d = x_ref.shape
    cache_len = lct_ref.shape[2]
    h = _rmsnorm(x_ref[...].reshape(nb * tm, d), g_ref[...]).astype(BF16)
    q_ref[...] = (_dot_nt(h, wt_ref[0:hw, :]) * scale).astype(BF16).reshape(nb, tm, hw)
    k_ref[...] = _dot_nt(h, wt_ref[hw:2 * hw, :]).reshape(nb, tm, hw)
    v_ref[...] = _dot_nt(h, wt_ref[2 * hw:3 * hw, :]).reshape(nb, tm, hw)

    ct_cache = _lane_cumsum(lct_ref[...].reshape(nb * n_head, cache_len), min(cache_len, 256))
    logf_all = _log_sigmoid(_dot_nt(wt_ref[3 * hw:3 * hw + n_head, :], h) + bfc_ref[...])
    for b in range(nb):
        ct_b = ct_cache[b * n_head:(b + 1) * n_head, :]
        cextc_ref[b] = _key_ext_rows(ct_b)
        logf_t = logf_all[:, b * tm:(b + 1) * tm]
        logft_ref[b] = logf_t
        ct_new = _lane_cumsum(logf_t, tm) + ct_b[:, cache_len - 1:cache_len]
        cextn_ref[b] = _key_ext_rows(ct_new)
        ccol_ref[b] = _to_columns(ct_new)


def _fox_proj_sample(x3, g, wt, bf_col, logf_cache_t, n_head, nb):
    bsz, tm, d = x3.shape
    hw = (wt.shape[0] - n_head) // 3
    cache_len = logf_cache_t.shape[2]
    assert 4 * n_head <= LANES and bsz % nb == 0
    kern = functools.partial(_fox_proj_sample_kernel, hw=hw, n_head=n_head,
                             scale=(hw // n_head) ** -0.5 * LOG2E)
    tok = pl.BlockSpec((nb, tm, hw), lambda b: (b, 0, 0))
    return pl.pallas_call(
        kern,
        grid=(bsz // nb,),
        in_specs=[
            pl.BlockSpec((nb, tm, d), lambda b: (b, 0, 0)),
            _resident((1, d)),
            _resident(wt.shape),
            _resident(bf_col.shape),
            pl.BlockSpec((nb, n_head, cache_len), lambda b: (b, 0, 0)),
        ],
        out_specs=[
            tok, tok, tok,
            pl.BlockSpec((nb, n_head, tm), lambda b: (b, 0, 0)),
            pl.BlockSpec((nb, tm, LANES), lambda b: (b, 0, 0)),
            pl.BlockSpec((nb, LANES, tm), lambda b: (b, 0, 0)),
            pl.BlockSpec((nb, LANES, cache_len), lambda b: (b, 0, 0)),
        ],
        out_shape=[
            jax.ShapeDtypeStruct((bsz, tm, hw), BF16),
            jax.ShapeDtypeStruct((bsz, tm, hw), F32),
            jax.ShapeDtypeStruct((bsz, tm, hw), F32),
            jax.ShapeDtypeStruct((bsz, n_head, tm), F32),
            jax.ShapeDtypeStruct((bsz, tm, LANES), F32),
            jax.ShapeDtypeStruct((bsz, LANES, tm), BF16),
            jax.ShapeDtypeStruct((bsz, LANES, cache_len), BF16),
        ],
        compiler_params=_params(1),
        name="fox_proj_sample",
    )(x3, g, wt, bf_col, logf_cache_t)


def _attn_setup(qt_ref, cq_rows, qaug_ref, m_ref, acc_ref, *, head0, n_pair, tq, hd, n_head):
    rr = lax.broadcasted_iota(jnp.int32, (LANES, tq), 0)
    for p in range(n_pair):
        q_t = qt_ref[p * LANES:(p + 1) * LANES, :]
        for h in range(2):
            cols = slice(h * tq, (h + 1) * tq)
            qaug_ref[p, 0:LANES, cols] = jnp.where(rr // hd == h, q_t, jnp.zeros_like(q_t))
            qaug_ref[p, LANES:2 * LANES, cols] = _query_ext(rr, head0 + 2 * p + h, n_head, cq_rows[2 * p + h])
    m_ref[...] = jnp.full_like(m_ref, MASK_VALUE)
    acc_ref[...] = jnp.zeros_like(acc_ref)


def _attn_update(n_pair, kaug_of, vt_of, vt_prev_last, mask, next_kaug0,
                 qaug_ref, m_ref, acc_ref, s0_ref, plast_ref, alast_ref):
    last = n_pair - 1
    s_next = s0_ref[...]
    acc_ref[last] = acc_ref[last] * alast_ref[...] + _dot(vt_prev_last(), plast_ref[...])
    pending = None
    for p in range(n_pair):
        s = s_next
        if p + 1 < n_pair:
            s_next = _dot(kaug_of(p + 1), qaug_ref[p + 1])
        elif next_kaug0 is not None:
            s0_ref[...] = _dot(next_kaug0(), qaug_ref[0])
        if mask is not None:
            s = jnp.where(mask, s, MASK_VALUE)
        m_old = m_ref[p]
        m_new = jnp.maximum(m_old, jnp.max(s, axis=0, keepdims=True))
        alpha = jnp.exp2(m_old - m_new)
        pr = jnp.exp2(s - m_new).astype(BF16)
        m_ref[p] = m_new
        if pending is not None:
            q, pr_q, alpha_q = pending
            acc_ref[q] = acc_ref[q] * alpha_q + _dot(vt_of(q), pr_q)
        pending = (p, pr, alpha)
    plast_ref[...] = pending[1]
    alast_ref[...] = pending[2]


def _attn_finish(o_ref, acc_ref, *, n_pair, tq, hd):
    for p in range(n_pair):
        full = acc_ref[p, 0:LANES, :] / acc_ref[p, LANES:LANES + 1, :]
        z = jnp.concatenate([full[0:hd, 0:tq], full[hd:2 * hd, tq:2 * tq]], axis=0)
        o_ref[:, p * LANES:(p + 1) * LANES] = z.T.astype(o_ref.dtype)


def _fox_attn_prompt_kernel(qt_ref, kaug_ref, vt_ref, ct_ref, o_ref,
                            qaug_ref, m_ref, acc_ref, s0_ref, plast_ref, alast_ref,
                            *, tq, n_pair, hd, n_head):
    g = pl.program_id(1)
    i = pl.program_id(2)
    last = n_pair - 1
    qs = pl.ds(pl.multiple_of(i * tq, tq), tq)
    cq_rows = [ct_ref[h:h + 1, qs] for h in range(2 * n_pair)]
    _attn_setup(qt_ref, cq_rows, qaug_ref, m_ref, acc_ref,
                head0=g * (2 * n_pair), n_pair=n_pair, tq=tq, hd=hd, n_head=n_head)
    state = (qaug_ref, m_ref, acc_ref, s0_ref, plast_ref, alast_ref)

    def keys(j):
        return pl.ds(pl.multiple_of(j * tq, tq), tq)

    def vt_of(p, ks):
        return vt_ref[p, :, ks]

    def kv_block(j, mask, has_next):
        ks = keys(j)
        _attn_update(n_pair, lambda p: kaug_ref[p, ks, :], lambda p: vt_of(p, ks),
                     lambda: vt_of(last, keys(jnp.maximum(j - 1, 0))), mask,
                     (lambda: kaug_ref[0, keys(j + 1), :]) if has_next else None, *state)

    def body(j, carry):
        kv_block(j, None, True)
        return carry

    s0_ref[...] = _dot(kaug_ref[0, 0:tq, :], qaug_ref[0])
    plast_ref[...] = jnp.zeros_like(plast_ref)
    alast_ref[...] = jnp.ones_like(alast_ref)
    lax.fori_loop(0, i, body, 0)
    row = lax.broadcasted_iota(jnp.int32, (tq, 2 * tq), 0)
    col = lax.broadcasted_iota(jnp.int32, (tq, 2 * tq), 1) % tq
    kv_block(i, row <= col, False)
    acc_ref[last] = acc_ref[last] * alast_ref[...] + _dot(vt_of(last, keys(i)), plast_ref[...])
    _attn_finish(o_ref, acc_ref, n_pair=n_pair, tq=tq, hd=hd)


def _fox_attn_prompt(qt, kaug, vtb, ct, tq, n_pair, n_head):
    bsz, hw, t_len = qt.shape
    hd = hw // n_head
    assert 2 * hd == LANES and t_len % tq == 0 and tq % LANES == 0
    width = n_pair * LANES
    n_group = hw // width
    ct4 = ct.reshape(bsz, n_group, 2 * n_pair, t_len)
    kern = functools.partial(_fox_attn_prompt_kernel, tq=tq, n_pair=n_pair, hd=hd, n_head=n_head)
    return pl.pallas_call(
        kern,
        grid=(bsz, n_group, t_len // tq),
        in_specs=[
            pl.BlockSpec((None, width, tq), lambda b, g, i: (b, g, i)),
            pl.BlockSpec((None, n_pair, t_len, 2 * LANES), lambda b, g, i: (b, g, 0, 0)),
            pl.BlockSpec((None, n_pair, VT_ROWS, t_len), lambda b, g, i: (b, g, 0, 0)),
            pl.BlockSpec((None, None, 2 * n_pair, t_len), lambda b, g, i: (b, g, 0, 0)),
        ],
        out_specs=pl.BlockSpec((None, tq, width), lambda b, g, i: (b, i, g)),
        out_shape=jax.ShapeDtypeStruct((bsz, t_len, hw), BF16),
        scratch_shapes=[
            pltpu.VMEM((n_pair, 2 * LANES, 2 * tq), BF16),
            pltpu.VMEM((n_pair, 1, 2 * tq), F32),
            pltpu.VMEM((n_pair, VT_ROWS, 2 * tq), F32),
            pltpu.VMEM((tq, 2 * tq), F32),
            pltpu.VMEM((tq, 2 * tq), BF16),
            pltpu.VMEM((1, 2 * tq), F32),
        ],
        compiler_params=_params(3),
        name="fox_attn_prompt",
    )(qt, kaug, vtb, ct4)


def _fox_attn_sample_kernel(q_ref, ccol_ref, kt_ref, vt_ref, cextc_ref, kn_ref, vn_ref, cextn_ref, o_ref,
                            *, n_pair, hd, n_head):
    g = pl.program_id(1)
    tq = q_ref.shape[0]
    eye = _identity(LANES)
    lane = lax.broadcasted_iota(jnp.int32, (tq, LANES), 1)
    row = lax.broadcasted_iota(jnp.int32, (2 * tq, tq), 0) % tq
    col = lax.broadcasted_iota(jnp.int32, (2 * tq, tq), 1)
    causal_new = col <= row
    ccol = ccol_ref[...]

    def scores(p):
        feat = slice(p * LANES, (p + 1) * LANES)
        q_pair = q_ref[:, feat]
        blocks = []
        for h in range(2):
            hh = g * (2 * n_pair) + 2 * p + h
            cq = jnp.sum(jnp.where(lane == hh, ccol, 0.0), axis=1, keepdims=True)
            top = jnp.where(lane // hd == h, q_pair, jnp.zeros_like(q_pair))
            blocks.append(jnp.concatenate([top, _query_ext(lane, hh, n_head, cq)], axis=1))
        qaug = jnp.concatenate(blocks, axis=0)
        s_c = _dot(qaug, jnp.concatenate([kt_ref[feat, :].astype(BF16), cextc_ref[...]], axis=0))
        kn_t = _dot_nt(eye, kn_ref[:, feat].astype(BF16)).astype(BF16)
        s_n = _dot(qaug, jnp.concatenate([kn_t, cextn_ref[...]], axis=0))
        return s_c, jnp.where(causal_new, s_n, MASK_VALUE)

    nxt = scores(0)
    for p in range(n_pair):
        s_c, s_n = nxt
        if p + 1 < n_pair:
            nxt = scores(p + 1)
        feat = slice(p * LANES, (p + 1) * LANES)
        m = jnp.maximum(jnp.max(s_c, axis=1, keepdims=True), jnp.max(s_n, axis=1, keepdims=True))
        p_c = jnp.exp2(s_c - m)
        p_n = jnp.exp2(s_n - m)
        l = jnp.sum(p_c, axis=1, keepdims=True) + jnp.sum(p_n, axis=1, keepdims=True)
        o = _dot_nt(p_c.astype(BF16), vt_ref[feat, :].astype(BF16)) + _dot(p_n.astype(BF16), vn_ref[:, feat].astype(BF16))
        o = o / l
        o_ref[:, feat] = jnp.where(lane < hd, o[0:tq], o[tq:2 * tq]).astype(o_ref.dtype)


def _fox_attn_sample(q, c_col, kt_cache, vt_cache, cext_cache, k_new, v_new, cext_new, n_pair, n_head):
    bsz, tq, hw = q.shape
    cache_len = kt_cache.shape[2]
    hd = hw // n_head
    assert 2 * hd == LANES and 4 * n_head <= LANES
    width = n_pair * LANES
    new_spec = pl.BlockSpec((None, tq, width), lambda b, g: (b, 0, g))
    cache_spec = pl.BlockSpec((None, width, cache_len), lambda b, g: (b, g, 0))
    kern = functools.partial(_fox_attn_sample_kernel, n_pair=n_pair, hd=hd, n_head=n_head)
    return pl.pallas_call(
        kern,
        grid=(bsz, hw // width),
        in_specs=[
            new_spec,
            pl.BlockSpec((None, tq, LANES), lambda b, g: (b, 0, 0)),
            cache_spec, cache_spec,
            pl.BlockSpec((None, LANES, cache_len), lambda b, g: (b, 0, 0)),
            new_spec, new_spec,
            pl.BlockSpec((None, LANES, tq), lambda b, g: (b, 0, 0)),
        ],
        out_specs=new_spec,
        out_shape=jax.ShapeDtypeStruct((bsz, tq, hw), BF16),
        compiler_params=_params(2),
        name="fox_attn_sample",
    )(q, c_col, kt_cache, vt_cache, cext_cache, k_new, v_new, cext_new)


def _pad_to(a, shape):
    return jnp.pad(a, [(0, s - d) for s, d in zip(shape, a.shape)])


def kernel(x_prompt, x_sample, state_gla, cache_fox_k, cache_fox_v, cache_fox_logf,
           norm_mix, gla_w_in, gla_w_g2, gla_b_g, gla_norm, gla_w_out,
           fox_w_in, fox_b_f, fox_w_out, norm_ffn, ffn_w_in, ffn_w_down, norm_final):
    d = x_prompt.shape[-1]
    depth = norm_mix.shape[0]
    groups = [x_prompt, x_sample]
    shapes = [x.shape for x in groups]
    xs = [x.reshape(-1, d) for x in groups]
    row_tiles = [min(512, x.shape[0]) for x in xs]

    gla_states = [[], []]
    fox_k, fox_v, fox_f = [[], []], [[], []], [[], []]
    for i in range(depth):
        j = i // 2
        g_mix = norm_mix[i].reshape(1, d)
        g_ffn = norm_ffn[i].reshape(1, d)
        w_ffn_in = ffn_w_in[i].astype(BF16)
        w_ffn_down = ffn_w_down[i].astype(BF16)
        last = i == depth - 1
        if i % 2 == 0:
            _, n_head, dk, dv = state_gla.shape[1:]
            hk, hv = n_head * dk, n_head * dv
            n_main = 2 * hk + 2 * hv
            w_in = gla_w_in[j]
            w_main = w_in[:, :n_main].astype(BF16)
            w_gl = _pad_to(w_in[:, n_main:], (d, LANES)).astype(BF16)
            w_g2 = _pad_to(gla_w_g2[j], (LANES, hk)).astype(BF16)
            b_g = gla_b_g[j].reshape(1, hk)
            w_out = gla_w_out[j].astype(BF16)
            norm_g = gla_norm[j].reshape(1, hv)
            s0s = [jnp.zeros((shapes[0][0], n_head, dk, dv), F32), state_gla[j]]
            for gi in range(2):
                bsz, t_len, _ = shapes[gi]
                proj, glog = _gla_proj(xs[gi], g_mix, w_main, w_gl, w_g2, b_g, row_tiles[gi])
                og, s_fin = _gla_mix(proj.reshape(bsz, t_len, n_main), glog.reshape(bsz, t_len, hk),
                                     s0s[gi], norm_g, min(t_len, 256))
                gla_states[gi].append(s_fin)
                xs[gi] = _post(xs[gi], og.reshape(-1, hv), w_out, g_ffn, w_ffn_in, w_ffn_down,
                               norm_final.reshape(1, d), row_tiles[gi], last)
        else:
            n_head = fox_b_f.shape[1]
            hw = fox_w_out.shape[1]
            hd = hw // n_head
            wt = fox_w_in[j].T.astype(BF16)
            bf_col = fox_b_f[j].reshape(n_head, 1)
            w_out = fox_w_out[j].astype(BF16)
            for gi in range(2):
                bsz, t_len, _ = shapes[gi]
                x3 = xs[gi].reshape(bsz, t_len, d)
                if gi == 0:
                    qt, kaug, kt, vt, vtb, logf_t, ct = _fox_proj_prompt(x3, g_mix, wt, bf_col,
                                                                         min(512, t_len), n_head)
                    o = _fox_attn_prompt(qt, kaug, vtb, ct, 256, 4, n_head)
                    k_out = kt.reshape(bsz, n_head, hd, t_len).transpose(0, 3, 1, 2)
                    v_out = vt.reshape(bsz, n_head, hd, t_len).transpose(0, 3, 1, 2)
                else:
                    cache_len = cache_fox_logf.shape[2]
                    q, k, v, logf_t, c_col, cext_new, cext_cache = _fox_proj_sample(
                        x3, g_mix, wt, bf_col, jnp.transpose(cache_fox_logf[j], (0, 2, 1)), n_head,
                        min(8, bsz))
                    kt_cache = jnp.transpose(cache_fox_k[j], (0, 2, 3, 1)).reshape(bsz, hw, cache_len)
                    vt_cache = jnp.transpose(cache_fox_v[j], (0, 2, 3, 1)).reshape(bsz, hw, cache_len)
                    o = _fox_attn_sample(q, c_col, kt_cache, vt_cache, cext_cache, k, v, cext_new, 4, n_head)
                    k_out = k.reshape(bsz, t_len, n_head, hd)
                    v_out = v.reshape(bsz, t_len, n_head, hd)
                fox_k[gi].append(k_out)
                fox_v[gi].append(v_out)
                fox_f[gi].append(jnp.transpose(logf_t, (0, 2, 1)))
                xs[gi] = _post(xs[gi], o.reshape(-1, hw), w_out, g_ffn, w_ffn_in, w_ffn_down,
                               norm_final.reshape(1, d), row_tiles[gi], last)

    y_prompt = xs[0].reshape(shapes[0])
    y_sample = xs[1].reshape(shapes[1])
    st = lambda parts: jnp.stack(parts, axis=0)
    return (y_prompt, y_sample, st(gla_states[0]), st(fox_k[0]), st(fox_v[0]), st(fox_f[0]),
            st(gla_states[1]), st(fox_k[1]), st(fox_v[1]), st(fox_f[1]))
```

```python
import functools

import jax
import jax.numpy as jnp
from jax import lax
from jax.experimental import pallas as pl
from jax.experimental.pallas import tpu as pltpu

F32 = jnp.float32
BF16 = jnp.bfloat16

EPS = 1e-6
MASK_VALUE = -1e30
LOG2E = 1.4426950408889634

GLA_HEADS = 4
GLA_CHUNK = 64
GLA_GATE_TAU = 16.0

LANES = 128
VT_ROWS = LANES + 16
VMEM_LIMIT_BYTES = 56 * 1024 * 1024


def _params(n_grid):
    return pltpu.CompilerParams(
        dimension_semantics=("arbitrary",) * n_grid,
        vmem_limit_bytes=VMEM_LIMIT_BYTES,
    )


def _resident(shape):
    nd = len(shape)
    return pl.BlockSpec(shape, lambda *_: (0,) * nd, pipeline_mode=pl.Buffered(1))


def _dot(a, b):
    return jnp.dot(a, b, preferred_element_type=F32)


def _dot_nt(a, b):
    return lax.dot_general(a, b, (((1,), (1,)), ((), ())), preferred_element_type=F32)


def _dot_tn(a, b):
    return lax.dot_general(a, b, (((0,), (0,)), ((), ())), preferred_element_type=F32)


def _split3(x):
    hi = x.astype(BF16)
    r1 = x - hi.astype(F32)
    mid = r1.astype(BF16)
    lo = (r1 - mid.astype(F32)).astype(BF16)
    return hi, mid, lo


def _sum01(dot_fn, x, ones_first, mat01):
    acc = None
    for part in _split3(x):
        term = dot_fn(mat01, part) if ones_first else dot_fn(part, mat01)
        acc = term if acc is None else acc + term
    return acc


def _rmsnorm(x, g):
    var = jnp.mean(x * x, axis=-1, keepdims=True)
    return x * lax.rsqrt(var + EPS) * g


def _log_sigmoid(z):
    return jnp.minimum(z, 0.0) - jnp.log1p(jnp.exp(-jnp.abs(z)))


def _silu(z):
    return z * jax.nn.sigmoid(z)


def _identity(n):
    r = lax.broadcasted_iota(jnp.int32, (n, n), 0)
    c = lax.broadcasted_iota(jnp.int32, (n, n), 1)
    return (r == c).astype(BF16)


def _gla_proj_kernel(x_ref, g_ref, w_ref, wgl_ref, wg2_ref, bg_ref, proj_ref, glog_ref):
    h = _rmsnorm(x_ref[...], g_ref[...]).astype(BF16)
    proj_ref[...] = _dot(h, w_ref[...])
    gl = _dot(h, wgl_ref[...]).astype(BF16)
    z = _dot(gl, wg2_ref[...]) + bg_ref[...]
    glog_ref[...] = _log_sigmoid(z) / GLA_GATE_TAU


def _gla_proj(x2d, g, w_main, w_gl, w_g2, b_g, tm):
    n, d = x2d.shape
    n_main, hk = w_main.shape[1], w_g2.shape[1]
    return pl.pallas_call(
        _gla_proj_kernel,
        grid=(n // tm,),
        in_specs=[
            pl.BlockSpec((tm, d), lambda i: (i, 0)),
            _resident((1, d)),
            _resident(w_main.shape),
            _resident(w_gl.shape),
            _resident(w_g2.shape),
            _resident((1, hk)),
        ],
        out_specs=[
            pl.BlockSpec((tm, n_main), lambda i: (i, 0)),
            pl.BlockSpec((tm, hk), lambda i: (i, 0)),
        ],
        out_shape=[
            jax.ShapeDtypeStruct((n, n_main), F32),
            jax.ShapeDtypeStruct((n, hk), F32),
        ],
        compiler_params=_params(1),
        name="gla_proj",
    )(x2d, g, w_main, w_gl, w_g2, b_g)


def _gla_mix_kernel(q_ref, k_ref, v_ref, r_ref, glog_ref, s0_ref, ng_ref,
                    og_ref, sfin_ref, s_ref, *, tb, dk, dv):
    t = pl.program_id(1)
    n_chunk = tb // GLA_CHUNK

    @pl.when(t == 0)
    def _():
        s_ref[...] = s0_ref[...]

    row = lax.broadcasted_iota(jnp.int32, (tb, tb), 0)
    col = lax.broadcasted_iota(jnp.int32, (tb, tb), 1)
    same_chunk = (row // GLA_CHUNK) == (col // GLA_CHUNK)
    causal = same_chunk & (col <= row)
    cum_mat = causal.astype(BF16)

    g = glog_ref[...]
    b = _sum01(_dot, g, True, cum_mat)
    tot_rows = [b[(c + 1) * GLA_CHUNK - 1:(c + 1) * GLA_CHUNK, :] for c in range(n_chunk)]
    b_last = jnp.concatenate([jnp.broadcast_to(r, (GLA_CHUNK, r.shape[1])) for r in tot_rows], axis=0)
    pad_rows = [jnp.zeros_like(tot_rows[0])] * (8 - n_chunk % 8 if n_chunk % 8 else 0)
    dec_t = jnp.exp(_to_columns(jnp.concatenate(tot_rows + pad_rows, axis=0)))
    q = q_ref[...]
    k = k_ref[...]
    qe = (q * jnp.exp(b) * (dk ** -0.5)).astype(BF16)
    ke = (k * jnp.exp(-b)).astype(BF16)
    kd = (k * jnp.exp(b_last - b)).astype(BF16)

    heads = range(GLA_HEADS)
    chunks = [slice(c * GLA_CHUNK, (c + 1) * GLA_CHUNK) for c in range(n_chunk)]
    ksl = [slice(h * dk, (h + 1) * dk) for h in heads]
    vsl = [slice(h * dv, (h + 1) * dv) for h in heads]
    o_intra, upd = [], []
    for h in heads:
        v_h = v_ref[:, vsl[h]].astype(BF16)
        a = jnp.where(causal, _dot_nt(qe[:, ksl[h]], ke[:, ksl[h]]), 0.0).astype(BF16)
        o_intra.append(_dot(a, v_h))
        upd.append([_dot_tn(kd[rs, ksl[h]], v_h[rs]) for rs in chunks])
    s = [s_ref[h] for h in heads]
    o_parts = [[] for _ in heads]
    for c, rs in enumerate(chunks):
        for h in heads:
            o_parts[h].append(o_intra[h][rs] + _dot(qe[rs, ksl[h]], s[h].astype(BF16)))
            s[h] = s[h] * dec_t[ksl[h], c:c + 1] + upd[h][c]
    for h in heads:
        s_ref[h] = s[h]
        o = o_parts[h][0] if n_chunk == 1 else jnp.concatenate(o_parts[h], axis=0)
        on = _rmsnorm(o, ng_ref[:, vsl[h]])
        og_ref[:, vsl[h]] = (on * _silu(r_ref[:, vsl[h]])).astype(BF16)

    @pl.when(t == pl.num_programs(1) - 1)
    def _():
        sfin_ref[...] = s_ref[...]


def _gla_mix(proj3, glog3, s0, norm_g, tb):
    bsz, t_len, _ = proj3.shape
    _, n_head, dk, dv = s0.shape
    hk, hv = n_head * dk, n_head * dv
    assert t_len % tb == 0 and tb % GLA_CHUNK == 0 and hv == 2 * hk
    kern = functools.partial(_gla_mix_kernel, tb=tb, dk=dk, dv=dv)
    state_spec = pl.BlockSpec((None, n_head, dk, dv), lambda b, t: (b, 0, 0, 0))
    return pl.pallas_call(
        kern,
        grid=(bsz, t_len // tb),
        in_specs=[
            pl.BlockSpec((None, tb, hk), lambda b, t: (b, t, 0)),
            pl.BlockSpec((None, tb, hk), lambda b, t: (b, t, 1)),
            pl.BlockSpec((None, tb, hv), lambda b, t: (b, t, 1)),
            pl.BlockSpec((None, tb, hv), lambda b, t: (b, t, 2)),
            pl.BlockSpec((None, tb, hk), lambda b, t: (b, t, 0)),
            state_spec,
            _resident((1, hv)),
        ],
        out_specs=[
            pl.BlockSpec((None, tb, hv), lambda b, t: (b, t, 0)),
            state_spec,
        ],
        out_shape=[
            jax.ShapeDtypeStruct((bsz, t_len, hv), BF16),
            jax.ShapeDtypeStruct(s0.shape, F32),
        ],
        scratch_shapes=[pltpu.VMEM((n_head, dk, dv), F32)],
        compiler_params=_params(2),
        name="gla_mix",
    )(proj3, proj3, proj3, proj3, glog3, s0, norm_g)


def _post_kernel(x_ref, o_ref, wo_ref, g_ref, win_ref, wdown_ref, gfin_ref, y_ref, act_ref,
                 *, d_ff, ff_tile, final_norm):
    x1 = x_ref[...] + _dot(o_ref[...], wo_ref[...])
    h = _rmsnorm(x1, g_ref[...]).astype(BF16)
    for j in range(d_ff // ff_tile):
        gate = _dot(h, win_ref[:, j * ff_tile:(j + 1) * ff_tile])
        up = _dot(h, win_ref[:, d_ff + j * ff_tile:d_ff + (j + 1) * ff_tile])
        act_ref[:, j * ff_tile:(j + 1) * ff_tile] = (_silu(gate) * up).astype(BF16)
    y = x1 + _dot(act_ref[...], wdown_ref[...])
    if final_norm:
        y = _rmsnorm(y, gfin_ref[...])
    y_ref[...] = y


def _post(x2d, o2d, w_out, g_ffn, w_in, w_down, g_final, tm, final_norm):
    n, d = x2d.shape
    d_ff = w_down.shape[0]
    ff_tile = 256
    assert d_ff % ff_tile == 0 and n % tm == 0
    kern = functools.partial(_post_kernel, d_ff=d_ff, ff_tile=ff_tile, final_norm=final_norm)
    return pl.pallas_call(
        kern,
        grid=(n // tm,),
        in_specs=[
            pl.BlockSpec((tm, d), lambda i: (i, 0)),
            pl.BlockSpec((tm, o2d.shape[1]), lambda i: (i, 0)),
            _resident(w_out.shape),
            _resident((1, d)),
            _resident(w_in.shape),
            _resident(w_down.shape),
            _resident((1, d)),
        ],
        out_specs=pl.BlockSpec((tm, d), lambda i: (i, 0)),
        out_shape=jax.ShapeDtypeStruct((n, d), F32),
        scratch_shapes=[pltpu.VMEM((tm, d_ff), BF16)],
        compiler_params=_params(1),
        name="post_final" if final_norm else "post",
    )(x2d, o2d, w_out, g_ffn, w_in, w_down, g_final)


def _lane_cumsum(x, tile):
    r = lax.broadcasted_iota(jnp.int32, (tile, tile), 0)
    c = lax.broadcasted_iota(jnp.int32, (tile, tile), 1)
    upper = (r <= c).astype(BF16)
    rows = x.shape[0]
    carry = jnp.zeros((rows, 1), F32)
    out = []
    for j in range(x.shape[1] // tile):
        parts = jnp.concatenate(_split3(x[:, j * tile:(j + 1) * tile]), axis=0)
        s = _dot(parts, upper)
        blk = carry + s[0:rows] + s[rows:2 * rows] + s[2 * rows:3 * rows]
        out.append(blk)
        carry = blk[:, tile - 1:tile]
    return out[0] if len(out) == 1 else jnp.concatenate(out, axis=1)


def _to_columns(x_t):
    rows = x_t.shape[0]
    parts = jnp.concatenate(_split3(x_t), axis=0)
    r = lax.broadcasted_iota(jnp.int32, (3 * rows, LANES), 0)
    c = lax.broadcasted_iota(jnp.int32, (3 * rows, LANES), 1)
    place = ((r % rows) == c).astype(BF16)
    return _dot_tn(parts, place)


def _key_ext_cols(c_cols, n_head):
    hi, mid, lo = (p.astype(F32) for p in _split3(c_cols * LOG2E))
    lane = lax.broadcasted_iota(jnp.int32, c_cols.shape, 1)
    ones = ((lane >= 3 * n_head) & (lane < 3 * n_head + 3)).astype(F32)
    ext = hi + pltpu.roll(mid, n_head, 1) + pltpu.roll(lo, 2 * n_head, 1) + ones
    return ext.astype(BF16)


def _key_ext_rows(c_rows):
    n_head, n = c_rows.shape
    hi, mid, lo = _split3(c_rows * LOG2E)
    r = lax.broadcasted_iota(jnp.int32, (n_head, n), 0)
    ones = (r < 3).astype(BF16)
    zeros = jnp.zeros((LANES - 4 * n_head, n), BF16)
    return jnp.concatenate([hi, mid, lo, ones, zeros], axis=0)


def _query_ext(idx, hh, n_head, cq):
    cq_hi, cq_mid, cq_lo = (p.astype(F32) for p in _split3(cq * LOG2E))
    ext = jnp.where((idx == hh) | (idx == n_head + hh) | (idx == 2 * n_head + hh), -1.0, 0.0)
    ext = jnp.where(idx == 3 * n_head, cq_hi, ext)
    ext = jnp.where(idx == 3 * n_head + 1, cq_mid, ext)
    ext = jnp.where(idx == 3 * n_head + 2, cq_lo, ext)
    return ext.astype(BF16)


def _fox_proj_prompt_kernel(x_ref, g_ref, wt_ref, bfc_ref,
                            qt_ref, kaug_ref, kt_ref, vt_ref, vtb_ref, logft_ref, ct_ref, carry_ref,
                            *, hw, n_head, scale):
    t = pl.program_id(1)
    tm = x_ref.shape[0]

    @pl.when(t == 0)
    def _():
        carry_ref[...] = jnp.zeros_like(carry_ref)

    h = _rmsnorm(x_ref[...], g_ref[...]).astype(BF16)
    qt_ref[...] = (_dot_nt(wt_ref[0:hw, :], h) * scale).astype(BF16)
    kt = _dot_nt(wt_ref[hw:2 * hw, :], h)
    kt_ref[...] = kt
    vt = _dot_nt(wt_ref[2 * hw:3 * hw, :], h)
    vt_ref[...] = vt
    ones = jnp.ones((VT_ROWS - LANES, tm), BF16)
    for p in range(hw // LANES):
        vtb_ref[p, 0:LANES, :] = vt[p * LANES:(p + 1) * LANES, :].astype(BF16)
        vtb_ref[p, LANES:VT_ROWS, :] = ones
    k = kt.T

    logf_t = _log_sigmoid(_dot_nt(wt_ref[3 * hw:3 * hw + n_head, :], h) + bfc_ref[...])
    logft_ref[...] = logf_t
    ct_blk = _lane_cumsum(logf_t, min(tm, 256)) + carry_ref[:, 0:1]
    ct_ref[...] = ct_blk
    carry_ref[...] = jnp.broadcast_to(ct_blk[:, tm - 1:tm], carry_ref.shape)
    cext = _key_ext_cols(_to_columns(ct_blk), n_head)
    for p in range(hw // LANES):
        kaug_ref[p, :, 0:LANES] = k[:, p * LANES:(p + 1) * LANES].astype(BF16)
        kaug_ref[p, :, LANES:2 * LANES] = cext


def _fox_proj_prompt(x3, g, wt, bf_col, tm, n_head):
    bsz, t_len, d = x3.shape
    hw = (wt.shape[0] - n_head) // 3
    n_pair = hw // LANES
    assert t_len % tm == 0 and 3 * n_head + 3 <= LANES
    kern = functools.partial(_fox_proj_prompt_kernel, hw=hw, n_head=n_head,
                             scale=(hw // n_head) ** -0.5 * LOG2E)
    feat = pl.BlockSpec((None, hw, tm), lambda b, t: (b, 0, t))
    head = pl.BlockSpec((None, n_head, tm), lambda b, t: (b, 0, t))
    return pl.pallas_call(
        kern,
        grid=(bsz, t_len // tm),
        in_specs=[
            pl.BlockSpec((None, tm, d), lambda b, t: (b, t, 0)),
            _resident((1, d)),
            _resident(wt.shape),
            _resident(bf_col.shape),
        ],
        out_specs=[
            feat,
            pl.BlockSpec((None, n_pair, tm, 2 * LANES), lambda b, t: (b, 0, t, 0)),
            feat, feat,
            pl.BlockSpec((None, n_pair, VT_ROWS, tm), lambda b, t: (b, 0, 0, t)),
            head, head,
        ],
        out_shape=[
            jax.ShapeDtypeStruct((bsz, hw, t_len), BF16),
            jax.ShapeDtypeStruct((bsz, n_pair, t_len, 2 * LANES), BF16),
            jax.ShapeDtypeStruct((bsz, hw, t_len), F32),
            jax.ShapeDtypeStruct((bsz, hw, t_len), F32),
            jax.ShapeDtypeStruct((bsz, n_pair, VT_ROWS, t_len), BF16),
            jax.ShapeDtypeStruct((bsz, n_head, t_len), F32),
            jax.ShapeDtypeStruct((bsz, n_head, t_len), F32),
        ],
        scratch_shapes=[pltpu.VMEM((n_head, LANES), F32)],
        compiler_params=_params(2),
        name="fox_proj_prompt",
    )(x3, g, wt, bf_col)


def _fox_proj_sample_kernel(x_ref, g_ref, wt_ref, bfc_ref, lct_ref,
                            q_ref, k_ref, v_ref, logft_ref, ccol_ref, cextn_ref, cextc_ref,
                            *, hw, n_head, scale):
    nb, tm, d = x_ref.shape
    cache_len = lct_ref.shape[2]
    h = _rmsnorm(x_ref[...].reshape(nb * tm, d), g_ref[...]).astype(BF16)
    q_ref[...] = (_dot_nt(h, wt_ref[0:hw, :]) * scale).astype(BF16).reshape(nb, tm, hw)
    k_ref[...] = _dot_nt(h, wt_ref[hw:2 * hw, :]).reshape(nb, tm, hw)
    v_ref[...] = _dot_nt(h, wt_ref[2 * hw:3 * hw, :]).reshape(nb, tm, hw)

    ct_cache = _lane_cumsum(lct_ref[...].reshape(nb * n_head, cache_len), min(cache_len, 256))
    logf_all = _log_sigmoid(_dot_nt(wt_ref[3 * hw:3 * hw + n_head, :], h) + bfc_ref[...])
    for b in range(nb):
        ct_b = ct_cache[b * n_head:(b + 1) * n_head, :]
        cextc_ref[b] = _key_ext_rows(ct_b)
        logf_t = logf_all[:, b * tm:(b + 1) * tm]
        logft_ref[b] = logf_t
        ct_new = _lane_cumsum(logf_t, tm) + ct_b[:, cache_len - 1:cache_len]
        cextn_ref[b] = _key_ext_rows(ct_new)
        ccol_ref[b] = _to_columns(ct_new)


def _fox_proj_sample(x3, g, wt, bf_col, logf_cache_t, n_head, nb):
    bsz, tm, d = x3.shape
    hw = (wt.shape[0] - n_head) // 3
    cache_len = logf_cache_t.shape[2]
    assert 4 * n_head <= LANES and bsz % nb == 0
    kern = functools.partial(_fox_proj_sample_kernel, hw=hw, n_head=n_head,
                             scale=(hw // n_head) ** -0.5 * LOG2E)
    tok = pl.BlockSpec((nb, tm, hw), lambda b: (b, 0, 0))
    return pl.pallas_call(
        kern,
        grid=(bsz // nb,),
        in_specs=[
            pl.BlockSpec((nb, tm, d), lambda b: (b, 0, 0)),
            _resident((1, d)),
            _resident(wt.shape),
            _resident(bf_col.shape),
            pl.BlockSpec((nb, n_head, cache_len), lambda b: (b, 0, 0)),
        ],
        out_specs=[
            tok, tok, tok,
            pl.BlockSpec((nb, n_head, tm), lambda b: (b, 0, 0)),
            pl.BlockSpec((nb, tm, LANES), lambda b: (b, 0, 0)),
            pl.BlockSpec((nb, LANES, tm), lambda b: (b, 0, 0)),
            pl.BlockSpec((nb, LANES, cache_len), lambda b: (b, 0, 0)),
        ],
        out_shape=[
            jax.ShapeDtypeStruct((bsz, tm, hw), BF16),
            jax.ShapeDtypeStruct((bsz, tm, hw), F32),
            jax.ShapeDtypeStruct((bsz, tm, hw), F32),
            jax.ShapeDtypeStruct((bsz, n_head, tm), F32),
            jax.ShapeDtypeStruct((bsz, tm, LANES), F32),
            jax.ShapeDtypeStruct((bsz, LANES, tm), BF16),
            jax.ShapeDtypeStruct((bsz, LANES, cache_len), BF16),
        ],
        compiler_params=_params(1),
        name="fox_proj_sample",
    )(x3, g, wt, bf_col, logf_cache_t)


def _attn_setup_pair(p, qt_ref, cq_rows, qaug_ref, *, head0, tq, hd, n_head):
    ext_rows = 4 * n_head
    rr = lax.broadcasted_iota(jnp.int32, (ext_rows, tq), 0)
    for h in range(2):
        cols = slice(h * tq, (h + 1) * tq)
        rows = slice(p * LANES + h * hd, p * LANES + (h + 1) * hd)
        qaug_ref[p, h * hd:(h + 1) * hd, cols] = qt_ref[rows, :]
        qaug_ref[p, LANES:LANES + ext_rows, cols] = _query_ext(rr, head0 + 2 * p + h, n_head, cq_rows[2 * p + h])


def _attn_update(n_pair, kaug_of, vt_of, vt_prev_last, mask, next_kaug0,
                 qaug_ref, m_ref, acc_ref, s0_ref, plast_ref, alast_ref):
    last = n_pair - 1
    s_next = s0_ref[...]
    acc_ref[last] = acc_ref[last] * alast_ref[...] + _dot(vt_prev_last(), plast_ref[...])
    pending = None
    for p in range(n_pair):
        s = s_next
        if p + 1 < n_pair:
            s_next = _dot(kaug_of(p + 1), qaug_ref[p + 1])
        elif next_kaug0 is not None:
            s0_ref[...] = _dot(next_kaug0(), qaug_ref[0])
        if mask is not None:
            s = jnp.where(mask, s, MASK_VALUE)
        m_old = m_ref[p]
        m_new = jnp.maximum(m_old, jnp.max(s, axis=0, keepdims=True))
        alpha = jnp.exp2(m_old - m_new)
        pr = jnp.exp2(s - m_new).astype(BF16)
        m_ref[p] = m_new
        if pending is not None:
            q, pr_q, alpha_q = pending
            acc_ref[q] = acc_ref[q] * alpha_q + _dot(vt_of(q), pr_q)
        pending = (p, pr, alpha)
    plast_ref[...] = pending[1]
    alast_ref[...] = pending[2]


def _attn_finish(o_ref, acc_ref, *, n_pair, tq, hd):
    for p in range(n_pair):
        full = acc_ref[p, 0:LANES, :] * (1.0 / acc_ref[p, LANES:LANES + 1, :])
        z = jnp.concatenate([full[0:hd, 0:tq], full[hd:2 * hd, tq:2 * tq]], axis=0)
        o_ref[:, p * LANES:(p + 1) * LANES] = z.T.astype(o_ref.dtype)


def _fox_attn_prompt_kernel(qt_ref, kaug_ref, vt_ref, ct_ref, o_ref,
                            qaug_ref, m_ref, acc_ref, s0_ref, plast_ref, alast_ref,
                            *, tq, n_pair, hd, n_head):
    g = pl.program_id(1)
    i = pl.program_id(2)
    last = n_pair - 1
    qs = pl.ds(pl.multiple_of(i * tq, tq), tq)
    cq_rows = [ct_ref[h:h + 1, qs] for h in range(2 * n_pair)]
    setup = functools.partial(_attn_setup_pair, qt_ref=qt_ref, cq_rows=cq_rows, qaug_ref=qaug_ref,
                              head0=g * (2 * n_pair), tq=tq, hd=hd, n_head=n_head)
    state = (qaug_ref, m_ref, acc_ref, s0_ref, plast_ref, alast_ref)

    @pl.when((pl.program_id(0) == 0) & (g == 0) & (i == 0))
    def _():
        qaug_ref[...] = jnp.zeros_like(qaug_ref)

    setup(0)
    s0_ref[...] = _dot(kaug_ref[0, 0:tq, :], qaug_ref[0])
    for p in range(1, n_pair):
        setup(p)
    m_ref[...] = jnp.full_like(m_ref, MASK_VALUE)
    acc_ref[...] = jnp.zeros_like(acc_ref)
    plast_ref[...] = jnp.zeros_like(plast_ref)
    alast_ref[...] = jnp.ones_like(alast_ref)

    def keys(j):
        return pl.ds(pl.multiple_of(j * tq, tq), tq)

    def vt_of(p, ks):
        return vt_ref[p, :, ks]

    def kv_block(j, mask, has_next):
        ks = keys(j)
        _attn_update(n_pair, lambda p: kaug_ref[p, ks, :], lambda p: vt_of(p, ks),
                     lambda: vt_of(last, keys(jnp.maximum(j - 1, 0))), mask,
                     (lambda: kaug_ref[0, keys(j + 1), :]) if has_next else None, *state)

    def body(j, carry):
        kv_block(j, None, True)
        return carry

    lax.fori_loop(0, i, body, 0)
    row = lax.broadcasted_iota(jnp.int32, (tq, 2 * tq), 0)
    col = lax.broadcasted_iota(jnp.int32, (tq, 2 * tq), 1) % tq
    kv_block(i, row <= col, False)
    acc_ref[last] = acc_ref[last] * alast_ref[...] + _dot(vt_of(last, keys(i)), plast_ref[...])
    _attn_finish(o_ref, acc_ref, n_pair=n_pair, tq=tq, hd=hd)


def _fox_attn_prompt(qt, kaug, vtb, ct, tq, n_pair, n_head):
    bsz, hw, t_len = qt.shape
    hd = hw // n_head
    assert 2 * hd == LANES and t_len % tq == 0 and tq % LANES == 0
    width = n_pair * LANES
    n_group = hw // width
    ct4 = ct.reshape(bsz, n_group, 2 * n_pair, t_len)
    kern = functools.partial(_fox_attn_prompt_kernel, tq=tq, n_pair=n_pair, hd=hd, n_head=n_head)
    return pl.pallas_call(
        kern,
        grid=(bsz, n_group, t_len // tq),
        in_specs=[
            pl.BlockSpec((None, width, tq), lambda b, g, i: (b, g, i)),
            pl.BlockSpec((None, n_pair, t_len, 2 * LANES), lambda b, g, i: (b, g, 0, 0)),
            pl.BlockSpec((None, n_pair, VT_ROWS, t_len), lambda b, g, i: (b, g, 0, 0)),
            pl.BlockSpec((None, None, 2 * n_pair, t_len), lambda b, g, i: (b, g, 0, 0)),
        ],
        out_specs=pl.BlockSpec((None, tq, width), lambda b, g, i: (b, i, g)),
        out_shape=jax.ShapeDtypeStruct((bsz, t_len, hw), BF16),
        scratch_shapes=[
            pltpu.VMEM((n_pair, 2 * LANES, 2 * tq), BF16),
            pltpu.VMEM((n_pair, 1, 2 * tq), F32),
            pltpu.VMEM((n_pair, VT_ROWS, 2 * tq), F32),
            pltpu.VMEM((tq, 2 * tq), F32),
            pltpu.VMEM((tq, 2 * tq), BF16),
            pltpu.VMEM((1, 2 * tq), F32),
        ],
        compiler_params=_params(3),
        name="fox_attn_prompt",
    )(qt, kaug, vtb, ct4)


def _fox_attn_sample_kernel(q_ref, ccol_ref, kt_ref, vt_ref, cextc_ref, kn_ref, vn_ref, cextn_ref, o_ref,
                            *, n_pair, hd, n_head):
    g = pl.program_id(1)
    tq = q_ref.shape[0]
    eye = _identity(LANES)
    lane = lax.broadcasted_iota(jnp.int32, (tq, LANES), 1)
    row = lax.broadcasted_iota(jnp.int32, (2 * tq, tq), 0) % tq
    col = lax.broadcasted_iota(jnp.int32, (2 * tq, tq), 1)
    causal_new = col <= row
    ccol = ccol_ref[...]

    def scores(p):
        feat = slice(p * LANES, (p + 1) * LANES)
        q_pair = q_ref[:, feat]
        blocks = []
        for h in range(2):
            hh = g * (2 * n_pair) + 2 * p + h
            cq = jnp.sum(jnp.where(lane == hh, ccol, 0.0), axis=1, keepdims=True)
            top = jnp.where(lane // hd == h, q_pair, jnp.zeros_like(q_pair))
            blocks.append(jnp.concatenate([top, _query_ext(lane, hh, n_head, cq)], axis=1))
        qaug = jnp.concatenate(blocks, axis=0)
        s_c = _dot(qaug, jnp.concatenate([kt_ref[feat, :].astype(BF16), cextc_ref[...]], axis=0))
        kn_t = _dot_nt(eye, kn_ref[:, feat].astype(BF16)).astype(BF16)
        s_n = _dot(qaug, jnp.concatenate([kn_t, cextn_ref[...]], axis=0))
        return s_c, jnp.where(causal_new, s_n, MASK_VALUE)

    nxt = scores(0)
    for p in range(n_pair):
        s_c, s_n = nxt
        if p + 1 < n_pair:
            nxt = scores(p + 1)
        feat = slice(p * LANES, (p + 1) * LANES)
        m = jnp.maximum(jnp.max(s_c, axis=1, keepdims=True), jnp.max(s_n, axis=1, keepdims=True))
        p_c = jnp.exp2(s_c - m)
        p_n = jnp.exp2(s_n - m)
        l = jnp.sum(p_c, axis=1, keepdims=True) + jnp.sum(p_n, axis=1, keepdims=True)
        o = _dot_nt(p_c.astype(BF16), vt_ref[feat, :].astype(BF16)) + _dot(p_n.astype(BF16), vn_ref[:, feat].astype(BF16))
        o = o / l
        o_ref[:, feat] = jnp.where(lane < hd, o[0:tq], o[tq:2 * tq]).astype(o_ref.dtype)


def _fox_attn_sample(q, c_col, kt_cache, vt_cache, cext_cache, k_new, v_new, cext_new, n_pair, n_head):
    bsz, tq, hw = q.shape
    cache_len = kt_cache.shape[2]
    hd = hw // n_head
    assert 2 * hd == LANES and 4 * n_head <= LANES
    width = n_pair * LANES
    new_spec = pl.BlockSpec((None, tq, width), lambda b, g: (b, 0, g))
    cache_spec = pl.BlockSpec((None, width, cache_len), lambda b, g: (b, g, 0))
    kern = functools.partial(_fox_attn_sample_kernel, n_pair=n_pair, hd=hd, n_head=n_head)
    return pl.pallas_call(
        kern,
        grid=(bsz, hw // width),
        in_specs=[
            new_spec,
            pl.BlockSpec((None, tq, LANES), lambda b, g: (b, 0, 0)),
            cache_spec, cache_spec,
            pl.BlockSpec((None, LANES, cache_len), lambda b, g: (b, 0, 0)),
            new_spec, new_spec,
            pl.BlockSpec((None, LANES, tq), lambda b, g: (b, 0, 0)),
        ],
        out_specs=new_spec,
        out_shape=jax.ShapeDtypeStruct((bsz, tq, hw), BF16),
        compiler_params=_params(2),
        name="fox_attn_sample",
    )(q, c_col, kt_cache, vt_cache, cext_cache, k_new, v_new, cext_new)


def _pad_to(a, shape):
    return jnp.pad(a, [(0, s - d) for s, d in zip(shape, a.shape)])


def kernel(x_prompt, x_sample, state_gla, cache_fox_k, cache_fox_v, cache_fox_logf,
           norm_mix, gla_w_in, gla_w_g2, gla_b_g, gla_norm, gla_w_out,
           fox_w_in, fox_b_f, fox_w_out, norm_ffn, ffn_w_in, ffn_w_down, norm_final):
    d = x_prompt.shape[-1]
    depth = norm_mix.shape[0]
    groups = [x_prompt, x_sample]
    shapes = [x.shape for x in groups]
    xs = [x.reshape(-1, d) for x in groups]
    row_tiles = [min(512, x.shape[0]) for x in xs]

    gla_states = [[], []]
    fox_k, fox_v, fox_f = [[], []], [[], []], [[], []]
    for i in range(depth):
        j = i // 2
        g_mix = norm_mix[i].reshape(1, d)
        g_ffn = norm_ffn[i].reshape(1, d)
        w_ffn_in = ffn_w_in[i].astype(BF16)
        w_ffn_down = ffn_w_down[i].astype(BF16)
        last = i == depth - 1
        if i % 2 == 0:
            _, n_head, dk, dv = state_gla.shape[1:]
            hk, hv = n_head * dk, n_head * dv
            n_main = 2 * hk + 2 * hv
            w_in = gla_w_in[j]
            w_main = w_in[:, :n_main].astype(BF16)
            w_gl = _pad_to(w_in[:, n_main:], (d, LANES)).astype(BF16)
            w_g2 = _pad_to(gla_w_g2[j], (LANES, hk)).astype(BF16)
            b_g = gla_b_g[j].reshape(1, hk)
            w_out = gla_w_out[j].astype(BF16)
            norm_g = gla_norm[j].reshape(1, hv)
            s0s = [jnp.zeros((shapes[0][0], n_head, dk, dv), F32), state_gla[j]]
            for gi in range(2):
                bsz, t_len, _ = shapes[gi]
                proj, glog = _gla_proj(xs[gi], g_mix, w_main, w_gl, w_g2, b_g, row_tiles[gi])
                og, s_fin = _gla_mix(proj.reshape(bsz, t_len, n_main), glog.reshape(bsz, t_len, hk),
                                     s0s[gi], norm_g, min(t_len, 256))
                gla_states[gi].append(s_fin)
                xs[gi] = _post(xs[gi], og.reshape(-1, hv), w_out, g_ffn, w_ffn_in, w_ffn_down,
                               norm_final.reshape(1, d), row_tiles[gi], last)
        else:
            n_head = fox_b_f.shape[1]
            hw = fox_w_out.shape[1]
            hd = hw // n_head
            wt = fox_w_in[j].T.astype(BF16)
            bf_col = fox_b_f[j].reshape(n_head, 1)
            w_out = fox_w_out[j].astype(BF16)
            for gi in range(2):
                bsz, t_len, _ = shapes[gi]
                x3 = xs[gi].reshape(bsz, t_len, d)
                if gi == 0:
                    qt, kaug, kt, vt, vtb, logf_t, ct = _fox_proj_prompt(x3, g_mix, wt, bf_col,
                                                                         min(512, t_len), n_head)
                    o = _fox_attn_prompt(qt, kaug, vtb, ct, 256, 8, n_head)
                    k_out = kt.reshape(bsz, n_head, hd, t_len).transpose(0, 3, 1, 2)
                    v_out = vt.reshape(bsz, n_head, hd, t_len).transpose(0, 3, 1, 2)
                else:
                    cache_len = cache_fox_logf.shape[2]
                    q, k, v, logf_t, c_col, cext_new, cext_cache = _fox_proj_sample(
                        x3, g_mix, wt, bf_col, jnp.transpose(cache_fox_logf[j], (0, 2, 1)), n_head,
                        min(8, bsz))
                    kt_cache = jnp.transpose(cache_fox_k[j], (0, 2, 3, 1)).reshape(bsz, hw, cache_len)
                    vt_cache = jnp.transpose(cache_fox_v[j], (0, 2, 3, 1)).reshape(bsz, hw, cache_len)
                    o = _fox_attn_sample(q, c_col, kt_cache, vt_cache, cext_cache, k, v, cext_new, 4, n_head)
                    k_out = k.reshape(bsz, t_len, n_head, hd)
                    v_out = v.reshape(bsz, t_len, n_head, hd)
                fox_k[gi].append(k_out)
                fox_v[gi].append(v_out)
                fox_f[gi].append(jnp.transpose(logf_t, (0, 2, 1)))
                xs[gi] = _post(xs[gi], o.reshape(-1, hw), w_out, g_ffn, w_ffn_in, w_ffn_down,
                               norm_final.reshape(1, d), row_tiles[gi], last)

    y_prompt = xs[0].reshape(shapes[0])
    y_sample = xs[1].reshape(shapes[1])
    st = lambda parts: jnp.stack(parts, axis=0)
    return (y_prompt, y_sample, st(gla_states[0]), st(fox_k[0]), st(fox_v[0]), st(fox_f[0]),
            st(gla_states[1]), st(fox_k[1]), st(fox_v[1]), st(fox_f[1]))
```

```python
import functools

import jax
import jax.numpy as jnp
from jax import lax
from jax.experimental import pallas as pl
from jax.experimental.pallas import tpu as pltpu

F32 = jnp.float32
BF16 = jnp.bfloat16

EPS = 1e-6
MASK_VALUE = -1e30
LOG2E = 1.4426950408889634

GLA_HEADS = 4
GLA_CHUNK = 64
GLA_GATE_TAU = 16.0

LANES = 128
VT_ROWS = LANES + 16
VMEM_LIMIT_BYTES = 56 * 1024 * 1024


def _params(n_grid):
    return pltpu.CompilerParams(
        dimension_semantics=("arbitrary",) * n_grid,
        vmem_limit_bytes=VMEM_LIMIT_BYTES,
    )


def _resident(shape):
    nd = len(shape)
    return pl.BlockSpec(shape, lambda *_: (0,) * nd, pipeline_mode=pl.Buffered(1))


def _dot(a, b):
    return jnp.dot(a, b, preferred_element_type=F32)


def _dot_nt(a, b):
    return lax.dot_general(a, b, (((1,), (1,)), ((), ())), preferred_element_type=F32)


def _dot_tn(a, b):
    return lax.dot_general(a, b, (((0,), (0,)), ((), ())), preferred_element_type=F32)


def _split3(x):
    hi = x.astype(BF16)
    r1 = x - hi.astype(F32)
    mid = r1.astype(BF16)
    lo = (r1 - mid.astype(F32)).astype(BF16)
    return hi, mid, lo


def _sum01(dot_fn, x, ones_first, mat01):
    acc = None
    for part in _split3(x):
        term = dot_fn(mat01, part) if ones_first else dot_fn(part, mat01)
        acc = term if acc is None else acc + term
    return acc


def _rmsnorm(x, g):
    var = jnp.mean(x * x, axis=-1, keepdims=True)
    return x * lax.rsqrt(var + EPS) * g


def _log_sigmoid(z):
    return jnp.minimum(z, 0.0) - jnp.log1p(jnp.exp(-jnp.abs(z)))


def _silu(z):
    return z * jax.nn.sigmoid(z)


def _identity(n):
    r = lax.broadcasted_iota(jnp.int32, (n, n), 0)
    c = lax.broadcasted_iota(jnp.int32, (n, n), 1)
    return (r == c).astype(BF16)


def _cast_weights_once(first_step, wt_ref, wtb_ref):
    @pl.when(first_step)
    def _():
        wtb_ref[...] = wt_ref[...].astype(BF16)


def _gla_proj_kernel(x_ref, g_ref, wt_ref, wg2_ref, bg_ref, proj_ref, glog_ref, wtb_ref, *, n_main):
    _cast_weights_once(pl.program_id(0) == 0, wt_ref, wtb_ref)
    rank = wg2_ref.shape[0]
    h = _rmsnorm(x_ref[...], g_ref[...]).astype(BF16)
    proj_ref[...] = _dot_nt(h, wtb_ref[0:n_main, :])
    gl_t = _dot_nt(wtb_ref[n_main:n_main + rank, :], h).astype(BF16)
    z = _dot_tn(gl_t, wg2_ref[...]) + bg_ref[...]
    glog_ref[...] = _log_sigmoid(z) / GLA_GATE_TAU


def _gla_proj(x2d, g, wt, w_g2, b_g, tm):
    n, d = x2d.shape
    rank, hk = w_g2.shape
    n_main = wt.shape[0] - rank
    return pl.pallas_call(
        functools.partial(_gla_proj_kernel, n_main=n_main),
        grid=(n // tm,),
        in_specs=[
            pl.BlockSpec((tm, d), lambda i: (i, 0)),
            _resident((1, d)),
            _resident(wt.shape),
            _resident(w_g2.shape),
            _resident((1, hk)),
        ],
        out_specs=[
            pl.BlockSpec((tm, n_main), lambda i: (i, 0)),
            pl.BlockSpec((tm, hk), lambda i: (i, 0)),
        ],
        out_shape=[
            jax.ShapeDtypeStruct((n, n_main), F32),
            jax.ShapeDtypeStruct((n, hk), F32),
        ],
        scratch_shapes=[pltpu.VMEM(wt.shape, BF16)],
        compiler_params=_params(1),
        name="gla_proj",
    )(x2d, g, wt, w_g2, b_g)


def _gla_mix_kernel(q_ref, k_ref, v_ref, r_ref, glog_ref, s0_ref, ng_ref,
                    og_ref, sfin_ref, s_ref, *, tb, dk, dv):
    t = pl.program_id(1)
    n_chunk = tb // GLA_CHUNK

    @pl.when(t == 0)
    def _():
        s_ref[...] = s0_ref[...]

    row = lax.broadcasted_iota(jnp.int32, (tb, tb), 0)
    col = lax.broadcasted_iota(jnp.int32, (tb, tb), 1)
    same_chunk = (row // GLA_CHUNK) == (col // GLA_CHUNK)
    causal = same_chunk & (col <= row)
    cum_mat = causal.astype(BF16)

    g = glog_ref[...]
    b = _sum01(_dot, g, True, cum_mat)
    tot_rows = [b[(c + 1) * GLA_CHUNK - 1:(c + 1) * GLA_CHUNK, :] for c in range(n_chunk)]
    b_last = jnp.concatenate([jnp.broadcast_to(r, (GLA_CHUNK, r.shape[1])) for r in tot_rows], axis=0)
    pad_rows = [jnp.zeros_like(tot_rows[0])] * (8 - n_chunk % 8 if n_chunk % 8 else 0)
    dec_t = jnp.exp(_to_columns(jnp.concatenate(tot_rows + pad_rows, axis=0)))
    q = q_ref[...]
    k = k_ref[...]
    qe = (q * jnp.exp(b) * (dk ** -0.5)).astype(BF16)
    ke = (k * jnp.exp(-b)).astype(BF16)
    kd = (k * jnp.exp(b_last - b)).astype(BF16)

    heads = range(GLA_HEADS)
    chunks = [slice(c * GLA_CHUNK, (c + 1) * GLA_CHUNK) for c in range(n_chunk)]
    ksl = [slice(h * dk, (h + 1) * dk) for h in heads]
    vsl = [slice(h * dv, (h + 1) * dv) for h in heads]
    o_intra, upd = [], []
    for h in heads:
        v_h = v_ref[:, vsl[h]].astype(BF16)
        a = jnp.where(causal, _dot_nt(qe[:, ksl[h]], ke[:, ksl[h]]), 0.0).astype(BF16)
        o_intra.append(_dot(a, v_h))
        upd.append([_dot_tn(kd[rs, ksl[h]], v_h[rs]) for rs in chunks])
    s = [s_ref[h] for h in heads]
    o_parts = [[] for _ in heads]
    for c, rs in enumerate(chunks):
        for h in heads:
            o_parts[h].append(o_intra[h][rs] + _dot(qe[rs, ksl[h]], s[h].astype(BF16)))
            s[h] = s[h] * dec_t[ksl[h], c:c + 1] + upd[h][c]
    for h in heads:
        s_ref[h] = s[h]
        o = o_parts[h][0] if n_chunk == 1 else jnp.concatenate(o_parts[h], axis=0)
        on = _rmsnorm(o, ng_ref[:, vsl[h]])
        og_ref[:, vsl[h]] = (on * _silu(r_ref[:, vsl[h]])).astype(BF16)

    @pl.when(t == pl.num_programs(1) - 1)
    def _():
        sfin_ref[...] = s_ref[...]


def _gla_mix(proj3, glog3, s0, norm_g, tb):
    bsz, t_len, _ = proj3.shape
    _, n_head, dk, dv = s0.shape
    hk, hv = n_head * dk, n_head * dv
    assert t_len % tb == 0 and tb % GLA_CHUNK == 0 and hv == 2 * hk
    kern = functools.partial(_gla_mix_kernel, tb=tb, dk=dk, dv=dv)
    state_spec = pl.BlockSpec((None, n_head, dk, dv), lambda b, t: (b, 0, 0, 0))
    return pl.pallas_call(
        kern,
        grid=(bsz, t_len // tb),
        in_specs=[
            pl.BlockSpec((None, tb, hk), lambda b, t: (b, t, 0)),
            pl.BlockSpec((None, tb, hk), lambda b, t: (b, t, 1)),
            pl.BlockSpec((None, tb, hv), lambda b, t: (b, t, 1)),
            pl.BlockSpec((None, tb, hv), lambda b, t: (b, t, 2)),
            pl.BlockSpec((None, tb, hk), lambda b, t: (b, t, 0)),
            state_spec,
            _resident((1, hv)),
        ],
        out_specs=[
            pl.BlockSpec((None, tb, hv), lambda b, t: (b, t, 0)),
            state_spec,
        ],
        out_shape=[
            jax.ShapeDtypeStruct((bsz, t_len, hv), BF16),
            jax.ShapeDtypeStruct(s0.shape, F32),
        ],
        scratch_shapes=[pltpu.VMEM((n_head, dk, dv), F32)],
        compiler_params=_params(2),
        name="gla_mix",
    )(proj3, proj3, proj3, proj3, glog3, s0, norm_g)


def _post_kernel(x_ref, o_ref, wo_ref, g_ref, win_ref, wdown_ref, gfin_ref, y_ref, act_ref,
                 *, d_ff, ff_tile, final_norm):
    x1 = x_ref[...] + _dot(o_ref[...], wo_ref[...])
    h = _rmsnorm(x1, g_ref[...]).astype(BF16)
    for j in range(d_ff // ff_tile):
        gate = _dot(h, win_ref[:, j * ff_tile:(j + 1) * ff_tile])
        up = _dot(h, win_ref[:, d_ff + j * ff_tile:d_ff + (j + 1) * ff_tile])
        act_ref[:, j * ff_tile:(j + 1) * ff_tile] = (_silu(gate) * up).astype(BF16)
    y = x1 + _dot(act_ref[...], wdown_ref[...])
    if final_norm:
        y = _rmsnorm(y, gfin_ref[...])
    y_ref[...] = y


def _post(x2d, o2d, w_out, g_ffn, w_in_all, w_down_all, layer, g_final, tm, final_norm):
    n, d = x2d.shape
    d_ff = w_down_all.shape[1]
    ff_tile = 256
    assert d_ff % ff_tile == 0 and n % tm == 0
    kern = functools.partial(_post_kernel, d_ff=d_ff, ff_tile=ff_tile, final_norm=final_norm)
    return pl.pallas_call(
        kern,
        grid=(n // tm,),
        in_specs=[
            pl.BlockSpec((tm, d), lambda i: (i, 0)),
            pl.BlockSpec((tm, o2d.shape[1]), lambda i: (i, 0)),
            _resident(w_out.shape),
            _resident((1, d)),
            pl.BlockSpec((None,) + w_in_all.shape[1:], lambda i: (layer, 0, 0), pipeline_mode=pl.Buffered(1)),
            pl.BlockSpec((None,) + w_down_all.shape[1:], lambda i: (layer, 0, 0), pipeline_mode=pl.Buffered(1)),
            _resident((1, d)),
        ],
        out_specs=pl.BlockSpec((tm, d), lambda i: (i, 0)),
        out_shape=jax.ShapeDtypeStruct((n, d), F32),
        scratch_shapes=[pltpu.VMEM((tm, d_ff), BF16)],
        compiler_params=_params(1),
        name="post_final" if final_norm else "post",
    )(x2d, o2d, w_out, g_ffn, w_in_all, w_down_all, g_final)


def _lane_cumsum(x, tile):
    r = lax.broadcasted_iota(jnp.int32, (tile, tile), 0)
    c = lax.broadcasted_iota(jnp.int32, (tile, tile), 1)
    upper = (r <= c).astype(BF16)
    rows = x.shape[0]
    carry = jnp.zeros((rows, 1), F32)
    out = []
    for j in range(x.shape[1] // tile):
        parts = jnp.concatenate(_split3(x[:, j * tile:(j + 1) * tile]), axis=0)
        s = _dot(parts, upper)
        blk = carry + s[0:rows] + s[rows:2 * rows] + s[2 * rows:3 * rows]
        out.append(blk)
        carry = blk[:, tile - 1:tile]
    return out[0] if len(out) == 1 else jnp.concatenate(out, axis=1)


def _to_columns(x_t):
    rows = x_t.shape[0]
    parts = jnp.concatenate(_split3(x_t), axis=0)
    r = lax.broadcasted_iota(jnp.int32, (3 * rows, LANES), 0)
    c = lax.broadcasted_iota(jnp.int32, (3 * rows, LANES), 1)
    place = ((r % rows) == c).astype(BF16)
    return _dot_tn(parts, place)


def _key_ext_cols(c_cols, n_head):
    hi, mid, lo = (p.astype(F32) for p in _split3(c_cols * LOG2E))
    lane = lax.broadcasted_iota(jnp.int32, c_cols.shape, 1)
    ones = ((lane >= 3 * n_head) & (lane < 3 * n_head + 3)).astype(F32)
    ext = hi + pltpu.roll(mid, n_head, 1) + pltpu.roll(lo, 2 * n_head, 1) + ones
    return ext.astype(BF16)


def _key_ext_rows(c_rows):
    n_head, n = c_rows.shape
    hi, mid, lo = _split3(c_rows * LOG2E)
    r = lax.broadcasted_iota(jnp.int32, (n_head, n), 0)
    ones = (r < 3).astype(BF16)
    zeros = jnp.zeros((LANES - 4 * n_head, n), BF16)
    return jnp.concatenate([hi, mid, lo, ones, zeros], axis=0)


def _query_ext(idx, hh, n_head, cq):
    cq_hi, cq_mid, cq_lo = (p.astype(F32) for p in _split3(cq * LOG2E))
    ext = jnp.where((idx == hh) | (idx == n_head + hh) | (idx == 2 * n_head + hh), -1.0, 0.0)
    ext = jnp.where(idx == 3 * n_head, cq_hi, ext)
    ext = jnp.where(idx == 3 * n_head + 1, cq_mid, ext)
    ext = jnp.where(idx == 3 * n_head + 2, cq_lo, ext)
    return ext.astype(BF16)


def _fox_proj_prompt_kernel(x_ref, g_ref, wtf_ref, bfc_ref,
                            qt_ref, kaug_ref, kt_ref, vt_ref, vtb_ref, logft_ref, ct_ref, carry_ref, wt_ref,
                            *, hw, n_head, scale):
    t = pl.program_id(1)
    tm = x_ref.shape[0]
    _cast_weights_once((pl.program_id(0) == 0) & (t == 0), wtf_ref, wt_ref)

    @pl.when(t == 0)
    def _():
        carry_ref[...] = jnp.zeros_like(carry_ref)

    h = _rmsnorm(x_ref[...], g_ref[...]).astype(BF16)
    qt_ref[...] = (_dot_nt(wt_ref[0:hw, :], h) * scale).astype(BF16)
    kt = _dot_nt(wt_ref[hw:2 * hw, :], h)
    kt_ref[...] = kt
    vt = _dot_nt(wt_ref[2 * hw:3 * hw, :], h)
    vt_ref[...] = vt
    ones = jnp.ones((VT_ROWS - LANES, tm), BF16)
    for p in range(hw // LANES):
        vtb_ref[p, 0:LANES, :] = vt[p * LANES:(p + 1) * LANES, :].astype(BF16)
        vtb_ref[p, LANES:VT_ROWS, :] = ones
    k = kt.T

    logf_t = _log_sigmoid(_dot_nt(wt_ref[3 * hw:3 * hw + n_head, :], h) + bfc_ref[...])
    logft_ref[...] = logf_t
    ct_blk = _lane_cumsum(logf_t, min(tm, 256)) + carry_ref[:, 0:1]
    ct_ref[...] = ct_blk
    carry_ref[...] = jnp.broadcast_to(ct_blk[:, tm - 1:tm], carry_ref.shape)
    cext = _key_ext_cols(_to_columns(ct_blk), n_head)
    for p in range(hw // LANES):
        kaug_ref[p, :, 0:LANES] = k[:, p * LANES:(p + 1) * LANES].astype(BF16)
        kaug_ref[p, :, LANES:2 * LANES] = cext


def _fox_proj_prompt(x3, g, wt, bf_col, tm, n_head):
    bsz, t_len, d = x3.shape
    hw = (wt.shape[0] - n_head) // 3
    n_pair = hw // LANES
    assert t_len % tm == 0 and 3 * n_head + 3 <= LANES
    kern = functools.partial(_fox_proj_prompt_kernel, hw=hw, n_head=n_head,
                             scale=(hw // n_head) ** -0.5 * LOG2E)
    feat = pl.BlockSpec((None, hw, tm), lambda b, t: (b, 0, t))
    head = pl.BlockSpec((None, n_head, tm), lambda b, t: (b, 0, t))
    return pl.pallas_call(
        kern,
        grid=(bsz, t_len // tm),
        in_specs=[
            pl.BlockSpec((None, tm, d), lambda b, t: (b, t, 0)),
            _resident((1, d)),
            _resident(wt.shape),
            _resident(bf_col.shape),
        ],
        out_specs=[
            feat,
            pl.BlockSpec((None, n_pair, tm, 2 * LANES), lambda b, t: (b, 0, t, 0)),
            feat, feat,
            pl.BlockSpec((None, n_pair, VT_ROWS, tm), lambda b, t: (b, 0, 0, t)),
            head, head,
        ],
        out_shape=[
            jax.ShapeDtypeStruct((bsz, hw, t_len), BF16),
            jax.ShapeDtypeStruct((bsz, n_pair, t_len, 2 * LANES), BF16),
            jax.ShapeDtypeStruct((bsz, hw, t_len), F32),
            jax.ShapeDtypeStruct((bsz, hw, t_len), F32),
            jax.ShapeDtypeStruct((bsz, n_pair, VT_ROWS, t_len), BF16),
            jax.ShapeDtypeStruct((bsz, n_head, t_len), F32),
            jax.ShapeDtypeStruct((bsz, n_head, t_len), F32),
        ],
        scratch_shapes=[pltpu.VMEM((n_head, LANES), F32), pltpu.VMEM(wt.shape, BF16)],
        compiler_params=_params(2),
        name="fox_proj_prompt",
    )(x3, g, wt, bf_col)


def _fox_proj_sample_kernel(x_ref, g_ref, wtf_ref, bfc_ref, lct_ref,
                            q_ref, k_ref, v_ref, logft_ref, ccol_ref, cextn_ref, cextc_ref, wt_ref,
                            *, hw, n_head, scale):
    _cast_weights_once(pl.program_id(0) == 0, wtf_ref, wt_ref)
    nb, tm, d = x_ref.shape
    cache_len = lct_ref.shape[2]
    h = _rmsnorm(x_ref[...].reshape(nb * tm, d), g_ref[...]).astype(BF16)
    q_ref[...] = (_dot_nt(h, wt_ref[0:hw, :]) * scale).astype(BF16).reshape(nb, tm, hw)
    k_ref[...] = _dot_nt(h, wt_ref[hw:2 * hw, :]).reshape(nb, tm, hw)
    v_ref[...] = _dot_nt(h, wt_ref[2 * hw:3 * hw, :]).reshape(nb, tm, hw)

    ct_cache = _lane_cumsum(lct_ref[...].reshape(nb * n_head, cache_len), min(cache_len, 256))
    logf_all = _log_sigmoid(_dot_nt(wt_ref[3 * hw:3 * hw + n_head, :], h) + bfc_ref[...])
    for b in range(nb):
        ct_b = ct_cache[b * n_head:(b + 1) * n_head, :]
        cextc_ref[b] = _key_ext_rows(ct_b)
        logf_t = logf_all[:, b * tm:(b + 1) * tm]
        logft_ref[b] = logf_t
        ct_new = _lane_cumsum(logf_t, tm) + ct_b[:, cache_len - 1:cache_len]
        cextn_ref[b] = _key_ext_rows(ct_new)
        ccol_ref[b] = _to_columns(ct_new)


def _fox_proj_sample(x3, g, wt, bf_col, logf_cache_t, n_head, nb):
    bsz, tm, d = x3.shape
    hw = (wt.shape[0] - n_head) // 3
    cache_len = logf_cache_t.shape[2]
    assert 4 * n_head <= LANES and bsz % nb == 0
    kern = functools.partial(_fox_proj_sample_kernel, hw=hw, n_head=n_head,
                             scale=(hw // n_head) ** -0.5 * LOG2E)
    tok = pl.BlockSpec((nb, tm, hw), lambda b: (b, 0, 0))
    return pl.pallas_call(
        kern,
        grid=(bsz // nb,),
        in_specs=[
            pl.BlockSpec((nb, tm, d), lambda b: (b, 0, 0)),
            _resident((1, d)),
            _resident(wt.shape),
            _resident(bf_col.shape),
            pl.BlockSpec((nb, n_head, cache_len), lambda b: (b, 0, 0)),
        ],
        out_specs=[
            tok, tok, tok,
            pl.BlockSpec((nb, n_head, tm), lambda b: (b, 0, 0)),
            pl.BlockSpec((nb, tm, LANES), lambda b: (b, 0, 0)),
            pl.BlockSpec((nb, LANES, tm), lambda b: (b, 0, 0)),
            pl.BlockSpec((nb, LANES, cache_len), lambda b: (b, 0, 0)),
        ],
        out_shape=[
            jax.ShapeDtypeStruct((bsz, tm, hw), BF16),
            jax.ShapeDtypeStruct((bsz, tm, hw), F32),
            jax.ShapeDtypeStruct((bsz, tm, hw), F32),
            jax.ShapeDtypeStruct((bsz, n_head, tm), F32),
            jax.ShapeDtypeStruct((bsz, tm, LANES), F32),
            jax.ShapeDtypeStruct((bsz, LANES, tm), BF16),
            jax.ShapeDtypeStruct((bsz, LANES, cache_len), BF16),
        ],
        scratch_shapes=[pltpu.VMEM(wt.shape, BF16)],
        compiler_params=_params(1),
        name="fox_proj_sample",
    )(x3, g, wt, bf_col, logf_cache_t)


def _attn_setup_pair(p, qt_ref, cq_rows, qaug_ref, *, head0, tq, hd, n_head):
    ext_rows = 4 * n_head
    rr = lax.broadcasted_iota(jnp.int32, (ext_rows, tq), 0)
    for h in range(2):
        cols = slice(h * tq, (h + 1) * tq)
        rows = slice(p * LANES + h * hd, p * LANES + (h + 1) * hd)
        qaug_ref[p, h * hd:(h + 1) * hd, cols] = qt_ref[rows, :]
        qaug_ref[p, LANES:LANES + ext_rows, cols] = _query_ext(rr, head0 + 2 * p + h, n_head, cq_rows[2 * p + h])


def _attn_update(n_pair, kaug_of, vt_of, vt_prev_last, mask, next_kaug0,
                 qaug_ref, m_ref, acc_ref, s0_ref, plast_ref, alast_ref):
    last = n_pair - 1
    s_next = s0_ref[...]
    acc_ref[last] = acc_ref[last] * alast_ref[...] + _dot(vt_prev_last(), plast_ref[...])
    pending = None
    for p in range(n_pair):
        s = s_next
        if p + 1 < n_pair:
            s_next = _dot(kaug_of(p + 1), qaug_ref[p + 1])
        elif next_kaug0 is not None:
            s0_ref[...] = _dot(next_kaug0(), qaug_ref[0])
        if mask is not None:
            s = jnp.where(mask, s, MASK_VALUE)
        m_old = m_ref[p]
        m_new = jnp.maximum(m_old, jnp.max(s, axis=0, keepdims=True))
        alpha = jnp.exp2(m_old - m_new)
        pr = jnp.exp2(s - m_new).astype(BF16)
        m_ref[p] = m_new
        if pending is not None:
            q, pr_q, alpha_q = pending
            acc_ref[q] = acc_ref[q] * alpha_q + _dot(vt_of(q), pr_q)
        pending = (p, pr, alpha)
    plast_ref[...] = pending[1]
    alast_ref[...] = pending[2]


def _attn_finish(o_ref, acc_ref, *, n_pair, tq, hd):
    for p in range(n_pair):
        full = acc_ref[p, 0:LANES, :] * (1.0 / acc_ref[p, LANES:LANES + 1, :])
        z = jnp.concatenate([full[0:hd, 0:tq], full[hd:2 * hd, tq:2 * tq]], axis=0)
        o_ref[:, p * LANES:(p + 1) * LANES] = z.T.astype(o_ref.dtype)


def _fox_attn_prompt_kernel(qt_ref, kaug_ref, vt_ref, ct_ref, o_ref,
                            qaug_ref, m_ref, acc_ref, s0_ref, plast_ref, alast_ref,
                            *, tq, n_pair, hd, n_head):
    g = pl.program_id(1)
    i = pl.program_id(2)
    last = n_pair - 1
    qs = pl.ds(pl.multiple_of(i * tq, tq), tq)
    cq_rows = [ct_ref[h:h + 1, qs] for h in range(2 * n_pair)]
    setup = functools.partial(_attn_setup_pair, qt_ref=qt_ref, cq_rows=cq_rows, qaug_ref=qaug_ref,
                              head0=g * (2 * n_pair), tq=tq, hd=hd, n_head=n_head)
    state = (qaug_ref, m_ref, acc_ref, s0_ref, plast_ref, alast_ref)

    @pl.when((pl.program_id(0) == 0) & (g == 0) & (i == 0))
    def _():
        qaug_ref[...] = jnp.zeros_like(qaug_ref)

    setup(0)
    s0_ref[...] = _dot(kaug_ref[0, 0:tq, :], qaug_ref[0])
    for p in range(1, n_pair):
        setup(p)
    m_ref[...] = jnp.full_like(m_ref, MASK_VALUE)
    acc_ref[...] = jnp.zeros_like(acc_ref)
    plast_ref[...] = jnp.zeros_like(plast_ref)
    alast_ref[...] = jnp.ones_like(alast_ref)

    def keys(j):
        return pl.ds(pl.multiple_of(j * tq, tq), tq)

    def vt_of(p, ks):
        return vt_ref[p, :, ks]

    def kv_block(j, mask, has_next):
        ks = keys(j)
        _attn_update(n_pair, lambda p: kaug_ref[p, ks, :], lambda p: vt_of(p, ks),
                     lambda: vt_of(last, keys(jnp.maximum(j - 1, 0))), mask,
                     (lambda: kaug_ref[0, keys(j + 1), :]) if has_next else None, *state)

    def body(j, carry):
        kv_block(j, None, True)
        return carry

    lax.fori_loop(0, i, body, 0)
    row = lax.broadcasted_iota(jnp.int32, (tq, 2 * tq), 0)
    col = lax.broadcasted_iota(jnp.int32, (tq, 2 * tq), 1) % tq
    kv_block(i, row <= col, False)
    acc_ref[last] = acc_ref[last] * alast_ref[...] + _dot(vt_of(last, keys(i)), plast_ref[...])
    _attn_finish(o_ref, acc_ref, n_pair=n_pair, tq=tq, hd=hd)


def _fox_attn_prompt(qt, kaug, vtb, ct, tq, n_pair, n_head):
    bsz, hw, t_len = qt.shape
    hd = hw // n_head
    assert 2 * hd == LANES and t_len % tq == 0 and tq % LANES == 0
    width = n_pair * LANES
    n_group = hw // width
    ct4 = ct.reshape(bsz, n_group, 2 * n_pair, t_len)
    kern = functools.partial(_fox_attn_prompt_kernel, tq=tq, n_pair=n_pair, hd=hd, n_head=n_head)
    return pl.pallas_call(
        kern,
        grid=(bsz, n_group, t_len // tq),
        in_specs=[
            pl.BlockSpec((None, width, tq), lambda b, g, i: (b, g, i)),
            pl.BlockSpec((None, n_pair, t_len, 2 * LANES), lambda b, g, i: (b, g, 0, 0)),
            pl.BlockSpec((None, n_pair, VT_ROWS, t_len), lambda b, g, i: (b, g, 0, 0)),
            pl.BlockSpec((None, None, 2 * n_pair, t_len), lambda b, g, i: (b, g, 0, 0)),
        ],
        out_specs=pl.BlockSpec((None, tq, width), lambda b, g, i: (b, i, g)),
        out_shape=jax.ShapeDtypeStruct((bsz, t_len, hw), BF16),
        scratch_shapes=[
            pltpu.VMEM((n_pair, 2 * LANES, 2 * tq), BF16),
            pltpu.VMEM((n_pair, 1, 2 * tq), F32),
            pltpu.VMEM((n_pair, VT_ROWS, 2 * tq), F32),
            pltpu.VMEM((tq, 2 * tq), F32),
            pltpu.VMEM((tq, 2 * tq), BF16),
            pltpu.VMEM((1, 2 * tq), F32),
        ],
        compiler_params=_params(3),
        name="fox_attn_prompt",
    )(qt, kaug, vtb, ct4)


def _fox_attn_sample_kernel(q_ref, ccol_ref, kt_ref, vt_ref, cextc_ref, kn_ref, vn_ref, cextn_ref, o_ref,
                            *, n_pair, hd, n_head):
    g = pl.program_id(1)
    tq = q_ref.shape[0]
    eye = _identity(LANES)
    lane = lax.broadcasted_iota(jnp.int32, (tq, LANES), 1)
    row = lax.broadcasted_iota(jnp.int32, (2 * tq, tq), 0) % tq
    col = lax.broadcasted_iota(jnp.int32, (2 * tq, tq), 1)
    causal_new = col <= row
    ccol = ccol_ref[...]

    def scores(p):
        feat = slice(p * LANES, (p + 1) * LANES)
        q_pair = q_ref[:, feat]
        blocks = []
        for h in range(2):
            hh = g * (2 * n_pair) + 2 * p + h
            cq = jnp.sum(jnp.where(lane == hh, ccol, 0.0), axis=1, keepdims=True)
            top = jnp.where(lane // hd == h, q_pair, jnp.zeros_like(q_pair))
            blocks.append(jnp.concatenate([top, _query_ext(lane, hh, n_head, cq)], axis=1))
        qaug = jnp.concatenate(blocks, axis=0)
        s_c = _dot(qaug, jnp.concatenate([kt_ref[feat, :].astype(BF16), cextc_ref[...]], axis=0))
        kn_t = _dot_nt(eye, kn_ref[:, feat].astype(BF16)).astype(BF16)
        s_n = _dot(qaug, jnp.concatenate([kn_t, cextn_ref[...]], axis=0))
        return s_c, jnp.where(causal_new, s_n, MASK_VALUE)

    nxt = scores(0)
    for p in range(n_pair):
        s_c, s_n = nxt
        if p + 1 < n_pair:
            nxt = scores(p + 1)
        feat = slice(p * LANES, (p + 1) * LANES)
        m = jnp.maximum(jnp.max(s_c, axis=1, keepdims=True), jnp.max(s_n, axis=1, keepdims=True))
        p_c = jnp.exp2(s_c - m)
        p_n = jnp.exp2(s_n - m)
        l = jnp.sum(p_c, axis=1, keepdims=True) + jnp.sum(p_n, axis=1, keepdims=True)
        o = _dot_nt(p_c.astype(BF16), vt_ref[feat, :].astype(BF16)) + _dot(p_n.astype(BF16), vn_ref[:, feat].astype(BF16))
        o = o / l
        o_ref[:, feat] = jnp.where(lane < hd, o[0:tq], o[tq:2 * tq]).astype(o_ref.dtype)


def _fox_attn_sample(q, c_col, kt_cache, vt_cache, cext_cache, k_new, v_new, cext_new, n_pair, n_head):
    bsz, tq, hw = q.shape
    cache_len = kt_cache.shape[2]
    hd = hw // n_head
    assert 2 * hd == LANES and 4 * n_head <= LANES
    width = n_pair * LANES
    new_spec = pl.BlockSpec((None, tq, width), lambda b, g: (b, 0, g))
    cache_spec = pl.BlockSpec((None, width, cache_len), lambda b, g: (b, g, 0))
    kern = functools.partial(_fox_attn_sample_kernel, n_pair=n_pair, hd=hd, n_head=n_head)
    return pl.pallas_call(
        kern,
        grid=(bsz, hw // width),
        in_specs=[
            new_spec,
            pl.BlockSpec((None, tq, LANES), lambda b, g: (b, 0, 0)),
            cache_spec, cache_spec,
            pl.BlockSpec((None, LANES, cache_len), lambda b, g: (b, 0, 0)),
            new_spec, new_spec,
            pl.BlockSpec((None, LANES, tq), lambda b, g: (b, 0, 0)),
        ],
        out_specs=new_spec,
        out_shape=jax.ShapeDtypeStruct((bsz, tq, hw), BF16),
        compiler_params=_params(2),
        name="fox_attn_sample",
    )(q, c_col, kt_cache, vt_cache, cext_cache, k_new, v_new, cext_new)


def kernel(x_prompt, x_sample, state_gla, cache_fox_k, cache_fox_v, cache_fox_logf,
           norm_mix, gla_w_in, gla_w_g2, gla_b_g, gla_norm, gla_w_out,
           fox_w_in, fox_b_f, fox_w_out, norm_ffn, ffn_w_in, ffn_w_down, norm_final):
    d = x_prompt.shape[-1]
    depth = norm_mix.shape[0]
    groups = [x_prompt, x_sample]
    shapes = [x.shape for x in groups]
    xs = [x.reshape(-1, d) for x in groups]
    row_tiles = [min(512, x.shape[0]) for x in xs]
    w_ffn_in = ffn_w_in.astype(BF16)
    w_ffn_down = ffn_w_down.astype(BF16)

    gla_states = [[], []]
    fox_k, fox_v, fox_f = [[], []], [[], []], [[], []]
    for i in range(depth):
        j = i // 2
        g_mix = norm_mix[i].reshape(1, d)
        g_ffn = norm_ffn[i].reshape(1, d)
        last = i == depth - 1
        if i % 2 == 0:
            _, n_head, dk, dv = state_gla.shape[1:]
            hk, hv = n_head * dk, n_head * dv
            n_main = 2 * hk + 2 * hv
            w_in = gla_w_in[j]
            wt = w_in.T
            w_g2 = gla_w_g2[j].astype(BF16)
            b_g = gla_b_g[j].reshape(1, hk)
            w_out = gla_w_out[j].astype(BF16)
            norm_g = gla_norm[j].reshape(1, hv)
            s0s = [jnp.zeros((shapes[0][0], n_head, dk, dv), F32), state_gla[j]]
            for gi in range(2):
                bsz, t_len, _ = shapes[gi]
                proj, glog = _gla_proj(xs[gi], g_mix, wt, w_g2, b_g, row_tiles[gi])
                og, s_fin = _gla_mix(proj.reshape(bsz, t_len, n_main), glog.reshape(bsz, t_len, hk),
                                     s0s[gi], norm_g, min(t_len, 256))
                gla_states[gi].append(s_fin)
                xs[gi] = _post(xs[gi], og.reshape(-1, hv), w_out, g_ffn, w_ffn_in, w_ffn_down, i,
                               norm_final.reshape(1, d), row_tiles[gi], last)
        else:
            n_head = fox_b_f.shape[1]
            hw = fox_w_out.shape[1]
            hd = hw // n_head
            wt = fox_w_in[j].T
            bf_col = fox_b_f[j].reshape(n_head, 1)
            w_out = fox_w_out[j].astype(BF16)
            for gi in range(2):
                bsz, t_len, _ = shapes[gi]
                x3 = xs[gi].reshape(bsz, t_len, d)
                if gi == 0:
                    qt, kaug, kt, vt, vtb, logf_t, ct = _fox_proj_prompt(x3, g_mix, wt, bf_col,
                                                                         min(512, t_len), n_head)
                    o = _fox_attn_prompt(qt, kaug, vtb, ct, 256, 8, n_head)
                    k_out = kt.reshape(bsz, n_head, hd, t_len).transpose(0, 3, 1, 2)
                    v_out = vt.reshape(bsz, n_head, hd, t_len).transpose(0, 3, 1, 2)
                else:
                    cache_len = cache_fox_logf.shape[2]
                    q, k, v, logf_t, c_col, cext_new, cext_cache = _fox_proj_sample(
                        x3, g_mix, wt, bf_col, jnp.transpose(cache_fox_logf[j], (0, 2, 1)), n_head,
                        min(8, bsz))
                    kt_cache = jnp.transpose(cache_fox_k[j], (0, 2, 3, 1)).reshape(bsz, hw, cache_len)
                    vt_cache = jnp.transpose(cache_fox_v[j], (0, 2, 3, 1)).reshape(bsz, hw, cache_len)
                    o = _fox_attn_sample(q, c_col, kt_cache, vt_cache, cext_cache, k, v, cext_new, 4, n_head)
                    k_out = k.reshape(bsz, t_len, n_head, hd)
                    v_out = v.reshape(bsz, t_len, n_head, hd)
                fox_k[gi].append(k_out)
                fox_v[gi].append(v_out)
                fox_f[gi].append(jnp.transpose(logf_t, (0, 2, 1)))
                xs[gi] = _post(xs[gi], o.reshape(-1, hw), w_out, g_ffn, w_ffn_in, w_ffn_down, i,
                               norm_final.reshape(1, d), row_tiles[gi], last)

    y_prompt = xs[0].reshape(shapes[0])
    y_sample = xs[1].reshape(shapes[1])
    st = lambda parts: jnp.stack(parts, axis=0)
    return (y_prompt, y_sample, st(gla_states[0]), st(fox_k[0]), st(fox_v[0]), st(fox_f[0]),
            st(gla_states[1]), st(fox_k[1]), st(fox_v[1]), st(fox_f[1]))
```

```python
import functools

import jax
import jax.numpy as jnp
from jax import lax
from jax.experimental import pallas as pl
from jax.experimental.pallas import tpu as pltpu

F32 = jnp.float32
BF16 = jnp.bfloat16

EPS = 1e-6
MASK_VALUE = -1e30
LOG2E = 1.4426950408889634

GLA_HEADS = 4
GLA_CHUNK = 64
GLA_GATE_TAU = 16.0

LANES = 128
VT_ROWS = LANES + 16
VMEM_LIMIT_BYTES = 56 * 1024 * 1024


def _params(n_grid):
    return pltpu.CompilerParams(
        dimension_semantics=("arbitrary",) * n_grid,
        vmem_limit_bytes=VMEM_LIMIT_BYTES,
    )


def _resident(shape):
    nd = len(shape)
    return pl.BlockSpec(shape, lambda *_: (0,) * nd, pipeline_mode=pl.Buffered(1))


def _dot(a, b):
    return jnp.dot(a, b, preferred_element_type=F32)


def _dot_nt(a, b):
    return lax.dot_general(a, b, (((1,), (1,)), ((), ())), preferred_element_type=F32)


def _dot_tn(a, b):
    return lax.dot_general(a, b, (((0,), (0,)), ((), ())), preferred_element_type=F32)


def _split3(x):
    hi = x.astype(BF16)
    r1 = x - hi.astype(F32)
    mid = r1.astype(BF16)
    lo = (r1 - mid.astype(F32)).astype(BF16)
    return hi, mid, lo


def _sum01(dot_fn, x, ones_first, mat01):
    acc = None
    for part in _split3(x):
        term = dot_fn(mat01, part) if ones_first else dot_fn(part, mat01)
        acc = term if acc is None else acc + term
    return acc


def _rmsnorm(x, g):
    var = jnp.mean(x * x, axis=-1, keepdims=True)
    return x * lax.rsqrt(var + EPS) * g


def _log_sigmoid(z):
    return jnp.minimum(z, 0.0) - jnp.log1p(jnp.exp(-jnp.abs(z)))


def _silu(z):
    return z * jax.nn.sigmoid(z)


def _identity(n):
    r = lax.broadcasted_iota(jnp.int32, (n, n), 0)
    c = lax.broadcasted_iota(jnp.int32, (n, n), 1)
    return (r == c).astype(BF16)


def _cast_weights_once(first_step, wt_ref, wtb_ref):
    @pl.when(first_step)
    def _():
        wtb_ref[...] = wt_ref[...].astype(BF16)


def _gla_proj_kernel(x_ref, g_ref, wt_ref, wg2_ref, bg_ref, proj_ref, glog_ref, wtb_ref, *, n_main):
    _cast_weights_once(pl.program_id(0) == 0, wt_ref, wtb_ref)
    rank = wg2_ref.shape[0]
    h = _rmsnorm(x_ref[...], g_ref[...]).astype(BF16)
    proj_ref[...] = _dot_nt(h, wtb_ref[0:n_main, :])
    gl_t = _dot_nt(wtb_ref[n_main:n_main + rank, :], h).astype(BF16)
    z = _dot_tn(gl_t, wg2_ref[...]) + bg_ref[...]
    glog_ref[...] = _log_sigmoid(z) / GLA_GATE_TAU


def _gla_proj(x2d, g, wt, w_g2, b_g, tm):
    n, d = x2d.shape
    rank, hk = w_g2.shape
    n_main = wt.shape[0] - rank
    return pl.pallas_call(
        functools.partial(_gla_proj_kernel, n_main=n_main),
        grid=(n // tm,),
        in_specs=[
            pl.BlockSpec((tm, d), lambda i: (i, 0)),
            _resident((1, d)),
            _resident(wt.shape),
            _resident(w_g2.shape),
            _resident((1, hk)),
        ],
        out_specs=[
            pl.BlockSpec((tm, n_main), lambda i: (i, 0)),
            pl.BlockSpec((tm, hk), lambda i: (i, 0)),
        ],
        out_shape=[
            jax.ShapeDtypeStruct((n, n_main), F32),
            jax.ShapeDtypeStruct((n, hk), F32),
        ],
        scratch_shapes=[pltpu.VMEM(wt.shape, BF16)],
        compiler_params=_params(1),
        name="gla_proj",
    )(x2d, g, wt, w_g2, b_g)


def _gla_mix_kernel(q_ref, k_ref, v_ref, r_ref, glog_ref, s0_ref, ng_ref,
                    og_ref, sfin_ref, s_ref, *, dk, dv):
    t = pl.program_id(1)
    nb, tb, _ = q_ref.shape
    n_chunk = tb // GLA_CHUNK
    seqs = range(nb)
    heads = range(GLA_HEADS)
    units = [(sb, h) for sb in seqs for h in heads]
    chunks = [slice(c * GLA_CHUNK, (c + 1) * GLA_CHUNK) for c in range(n_chunk)]
    ksl = [slice(h * dk, (h + 1) * dk) for h in heads]
    vsl = [slice(h * dv, (h + 1) * dv) for h in heads]

    @pl.when(t == 0)
    def _():
        s_ref[...] = s0_ref[...]

    row = lax.broadcasted_iota(jnp.int32, (tb, tb), 0)
    col = lax.broadcasted_iota(jnp.int32, (tb, tb), 1)
    same_chunk = (row // GLA_CHUNK) == (col // GLA_CHUNK)
    causal = same_chunk & (col <= row)
    cum_mat = causal.astype(BF16)

    b = [_sum01(_dot, glog_ref[sb], True, cum_mat) for sb in seqs]
    qe, ke, kd, dec_t = [], [], [], []
    for sb in seqs:
        tot_rows = [b[sb][(c + 1) * GLA_CHUNK - 1:(c + 1) * GLA_CHUNK, :] for c in range(n_chunk)]
        b_last = jnp.concatenate([jnp.broadcast_to(r, (GLA_CHUNK, r.shape[1])) for r in tot_rows], axis=0)
        pad_rows = [jnp.zeros_like(tot_rows[0])] * (8 - n_chunk % 8 if n_chunk % 8 else 0)
        dec_t.append(jnp.exp(_to_columns(jnp.concatenate(tot_rows + pad_rows, axis=0))))
        q = q_ref[sb]
        k = k_ref[sb]
        qe.append((q * jnp.exp(b[sb]) * (dk ** -0.5)).astype(BF16))
        ke.append((k * jnp.exp(-b[sb])).astype(BF16))
        kd.append((k * jnp.exp(b_last - b[sb])).astype(BF16))

    v_b = {u: v_ref[u[0], :, vsl[u[1]]].astype(BF16) for u in units}
    a_raw = {(sb, h): _dot_nt(qe[sb][:, ksl[h]], ke[sb][:, ksl[h]]) for sb, h in units}
    upd = {(sb, h): [_dot_tn(kd[sb][rs, ksl[h]], v_b[sb, h][rs]) for rs in chunks] for sb, h in units}
    o_intra = {u: _dot(jnp.where(causal, a_raw[u], 0.0).astype(BF16), v_b[u]) for u in units}
    s_in = {}
    for sb, h in units:
        s = s_ref[sb, h]
        s_in[sb, h] = []
        for c in range(n_chunk):
            s_in[sb, h].append(s.astype(BF16))
            s = s * dec_t[sb][ksl[h], c:c + 1] + upd[sb, h][c]
        s_ref[sb, h] = s
    for sb, h in units:
        o_parts = [o_intra[sb, h][rs] + _dot(qe[sb][rs, ksl[h]], s_in[sb, h][c]) for c, rs in enumerate(chunks)]
        o = o_parts[0] if n_chunk == 1 else jnp.concatenate(o_parts, axis=0)
        on = _rmsnorm(o, ng_ref[:, vsl[h]])
        og_ref[sb, :, vsl[h]] = (on * _silu(r_ref[sb, :, vsl[h]])).astype(BF16)

    @pl.when(t == pl.num_programs(1) - 1)
    def _():
        sfin_ref[...] = s_ref[...]


def _gla_mix(proj3, glog3, s0, norm_g, tb, nb):
    bsz, t_len, _ = proj3.shape
    _, n_head, dk, dv = s0.shape
    hk, hv = n_head * dk, n_head * dv
    assert t_len % tb == 0 and tb % GLA_CHUNK == 0 and hv == 2 * hk and bsz % nb == 0
    kern = functools.partial(_gla_mix_kernel, dk=dk, dv=dv)
    state_spec = pl.BlockSpec((nb, n_head, dk, dv), lambda b, t: (b, 0, 0, 0))
    return pl.pallas_call(
        kern,
        grid=(bsz // nb, t_len // tb),
        in_specs=[
            pl.BlockSpec((nb, tb, hk), lambda b, t: (b, t, 0)),
            pl.BlockSpec((nb, tb, hk), lambda b, t: (b, t, 1)),
            pl.BlockSpec((nb, tb, hv), lambda b, t: (b, t, 1)),
            pl.BlockSpec((nb, tb, hv), lambda b, t: (b, t, 2)),
            pl.BlockSpec((nb, tb, hk), lambda b, t: (b, t, 0)),
            state_spec,
            _resident((1, hv)),
        ],
        out_specs=[
            pl.BlockSpec((nb, tb, hv), lambda b, t: (b, t, 0)),
            state_spec,
        ],
        out_shape=[
            jax.ShapeDtypeStruct((bsz, t_len, hv), BF16),
            jax.ShapeDtypeStruct(s0.shape, F32),
        ],
        scratch_shapes=[pltpu.VMEM((nb, n_head, dk, dv), F32)],
        compiler_params=_params(2),
        name="gla_mix",
    )(proj3, proj3, proj3, proj3, glog3, s0, norm_g)


def _post_kernel(x_ref, o_ref, wo_ref, g_ref, win_ref, wdown_ref, gfin_ref, y_ref, act_ref,
                 *, d_ff, ff_tile, final_norm):
    x1 = x_ref[...] + _dot(o_ref[...], wo_ref[...])
    h = _rmsnorm(x1, g_ref[...]).astype(BF16)
    for j in range(d_ff // ff_tile):
        gate = _dot(h, win_ref[:, j * ff_tile:(j + 1) * ff_tile])
        up = _dot(h, win_ref[:, d_ff + j * ff_tile:d_ff + (j + 1) * ff_tile])
        act_ref[:, j * ff_tile:(j + 1) * ff_tile] = (_silu(gate) * up).astype(BF16)
    y = x1 + _dot(act_ref[...], wdown_ref[...])
    if final_norm:
        y = _rmsnorm(y, gfin_ref[...])
    y_ref[...] = y


def _post(x2d, o2d, w_out, g_ffn, w_in_all, w_down_all, layer, g_final, tm, final_norm):
    n, d = x2d.shape
    d_ff = w_down_all.shape[1]
    ff_tile = 256
    assert d_ff % ff_tile == 0 and n % tm == 0
    kern = functools.partial(_post_kernel, d_ff=d_ff, ff_tile=ff_tile, final_norm=final_norm)
    return pl.pallas_call(
        kern,
        grid=(n // tm,),
        in_specs=[
            pl.BlockSpec((tm, d), lambda i: (i, 0)),
            pl.BlockSpec((tm, o2d.shape[1]), lambda i: (i, 0)),
            _resident(w_out.shape),
            _resident((1, d)),
            pl.BlockSpec((None,) + w_in_all.shape[1:], lambda i: (layer, 0, 0), pipeline_mode=pl.Buffered(1)),
            pl.BlockSpec((None,) + w_down_all.shape[1:], lambda i: (layer, 0, 0), pipeline_mode=pl.Buffered(1)),
            _resident((1, d)),
        ],
        out_specs=pl.BlockSpec((tm, d), lambda i: (i, 0)),
        out_shape=jax.ShapeDtypeStruct((n, d), F32),
        scratch_shapes=[pltpu.VMEM((tm, d_ff), BF16)],
        compiler_params=_params(1),
        name="post_final" if final_norm else "post",
    )(x2d, o2d, w_out, g_ffn, w_in_all, w_down_all, g_final)


def _lane_cumsum(x, tile):
    r = lax.broadcasted_iota(jnp.int32, (tile, tile), 0)
    c = lax.broadcasted_iota(jnp.int32, (tile, tile), 1)
    upper = (r <= c).astype(BF16)
    rows = x.shape[0]
    carry = jnp.zeros((rows, 1), F32)
    out = []
    for j in range(x.shape[1] // tile):
        parts = jnp.concatenate(_split3(x[:, j * tile:(j + 1) * tile]), axis=0)
        s = _dot(parts, upper)
        blk = carry + s[0:rows] + s[rows:2 * rows] + s[2 * rows:3 * rows]
        out.append(blk)
        carry = blk[:, tile - 1:tile]
    return out[0] if len(out) == 1 else jnp.concatenate(out, axis=1)


def _to_columns(x_t):
    rows = x_t.shape[0]
    parts = jnp.concatenate(_split3(x_t), axis=0)
    r = lax.broadcasted_iota(jnp.int32, (3 * rows, LANES), 0)
    c = lax.broadcasted_iota(jnp.int32, (3 * rows, LANES), 1)
    place = ((r % rows) == c).astype(BF16)
    return _dot_tn(parts, place)


def _key_ext_cols(c_cols, n_head):
    hi, mid, lo = (p.astype(F32) for p in _split3(c_cols * LOG2E))
    lane = lax.broadcasted_iota(jnp.int32, c_cols.shape, 1)
    ones = ((lane >= 3 * n_head) & (lane < 3 * n_head + 3)).astype(F32)
    ext = hi + pltpu.roll(mid, n_head, 1) + pltpu.roll(lo, 2 * n_head, 1) + ones
    return ext.astype(BF16)


def _key_ext_rows(c_rows):
    n_head, n = c_rows.shape
    hi, mid, lo = _split3(c_rows * LOG2E)
    r = lax.broadcasted_iota(jnp.int32, (n_head, n), 0)
    ones = (r < 3).astype(BF16)
    zeros = jnp.zeros((LANES - 4 * n_head, n), BF16)
    return jnp.concatenate([hi, mid, lo, ones, zeros], axis=0)


def _query_ext(idx, hh, n_head, cq):
    cq_hi, cq_mid, cq_lo = (p.astype(F32) for p in _split3(cq * LOG2E))
    ext = jnp.where((idx == hh) | (idx == n_head + hh) | (idx == 2 * n_head + hh), -1.0, 0.0)
    ext = jnp.where(idx == 3 * n_head, cq_hi, ext)
    ext = jnp.where(idx == 3 * n_head + 1, cq_mid, ext)
    ext = jnp.where(idx == 3 * n_head + 2, cq_lo, ext)
    return ext.astype(BF16)


def _fox_proj_prompt_kernel(x_ref, g_ref, wtf_ref, bfc_ref,
                            qt_ref, kaug_ref, kt_ref, vt_ref, vtb_ref, logft_ref, ct_ref, carry_ref, wt_ref,
                            *, hw, n_head, scale):
    t = pl.program_id(1)
    tm = x_ref.shape[0]
    _cast_weights_once((pl.program_id(0) == 0) & (t == 0), wtf_ref, wt_ref)

    @pl.when(t == 0)
    def _():
        carry_ref[...] = jnp.zeros_like(carry_ref)

    h = _rmsnorm(x_ref[...], g_ref[...]).astype(BF16)
    qt_ref[...] = (_dot_nt(wt_ref[0:hw, :], h) * scale).astype(BF16)
    kt = _dot_nt(wt_ref[hw:2 * hw, :], h)
    kt_ref[...] = kt
    vt = _dot_nt(wt_ref[2 * hw:3 * hw, :], h)
    vt_ref[...] = vt
    ones = jnp.ones((VT_ROWS - LANES, tm), BF16)
    for p in range(hw // LANES):
        vtb_ref[p, 0:LANES, :] = vt[p * LANES:(p + 1) * LANES, :].astype(BF16)
        vtb_ref[p, LANES:VT_ROWS, :] = ones
    k = kt.T

    logf_t = _log_sigmoid(_dot_nt(wt_ref[3 * hw:3 * hw + n_head, :], h) + bfc_ref[...])
    logft_ref[...] = logf_t
    ct_blk = _lane_cumsum(logf_t, min(tm, 256)) + carry_ref[:, 0:1]
    ct_ref[...] = ct_blk
    carry_ref[...] = jnp.broadcast_to(ct_blk[:, tm - 1:tm], carry_ref.shape)
    cext = _key_ext_cols(_to_columns(ct_blk), n_head)
    for p in range(hw // LANES):
        kaug_ref[p, :, 0:LANES] = k[:, p * LANES:(p + 1) * LANES].astype(BF16)
        kaug_ref[p, :, LANES:2 * LANES] = cext


def _fox_proj_prompt(x3, g, wt, bf_col, tm, n_head):
    bsz, t_len, d = x3.shape
    hw = (wt.shape[0] - n_head) // 3
    n_pair = hw // LANES
    assert t_len % tm == 0 and 3 * n_head + 3 <= LANES
    kern = functools.partial(_fox_proj_prompt_kernel, hw=hw, n_head=n_head,
                             scale=(hw // n_head) ** -0.5 * LOG2E)
    feat = pl.BlockSpec((None, hw, tm), lambda b, t: (b, 0, t))
    head = pl.BlockSpec((None, n_head, tm), lambda b, t: (b, 0, t))
    return pl.pallas_call(
        kern,
        grid=(bsz, t_len // tm),
        in_specs=[
            pl.BlockSpec((None, tm, d), lambda b, t: (b, t, 0)),
            _resident((1, d)),
            _resident(wt.shape),
            _resident(bf_col.shape),
        ],
        out_specs=[
            feat,
            pl.BlockSpec((None, n_pair, tm, 2 * LANES), lambda b, t: (b, 0, t, 0)),
            feat, feat,
            pl.BlockSpec((None, n_pair, VT_ROWS, tm), lambda b, t: (b, 0, 0, t)),
            head, head,
        ],
        out_shape=[
            jax.ShapeDtypeStruct((bsz, hw, t_len), BF16),
            jax.ShapeDtypeStruct((bsz, n_pair, t_len, 2 * LANES), BF16),
            jax.ShapeDtypeStruct((bsz, hw, t_len), F32),
            jax.ShapeDtypeStruct((bsz, hw, t_len), F32),
            jax.ShapeDtypeStruct((bsz, n_pair, VT_ROWS, t_len), BF16),
            jax.ShapeDtypeStruct((bsz, n_head, t_len), F32),
            jax.ShapeDtypeStruct((bsz, n_head, t_len), F32),
        ],
        scratch_shapes=[pltpu.VMEM((n_head, LANES), F32), pltpu.VMEM(wt.shape, BF16)],
        compiler_params=_params(2),
        name="fox_proj_prompt",
    )(x3, g, wt, bf_col)


def _fox_proj_sample_kernel(x_ref, g_ref, wtf_ref, bfc_ref, lct_ref,
                            q_ref, k_ref, v_ref, logft_ref, ccol_ref, cextn_ref, cextc_ref, wt_ref,
                            *, hw, n_head, scale):
    _cast_weights_once(pl.program_id(0) == 0, wtf_ref, wt_ref)
    nb, tm, d = x_ref.shape
    cache_len = lct_ref.shape[2]
    h = _rmsnorm(x_ref[...].reshape(nb * tm, d), g_ref[...]).astype(BF16)
    q_ref[...] = (_dot_nt(h, wt_ref[0:hw, :]) * scale).astype(BF16).reshape(nb, tm, hw)
    k_ref[...] = _dot_nt(h, wt_ref[hw:2 * hw, :]).reshape(nb, tm, hw)
    v_ref[...] = _dot_nt(h, wt_ref[2 * hw:3 * hw, :]).reshape(nb, tm, hw)

    ct_cache = _lane_cumsum(lct_ref[...].reshape(nb * n_head, cache_len), min(cache_len, 256))
    logf_all = _log_sigmoid(_dot_nt(wt_ref[3 * hw:3 * hw + n_head, :], h) + bfc_ref[...])
    for b in range(nb):
        ct_b = ct_cache[b * n_head:(b + 1) * n_head, :]
        cextc_ref[b] = _key_ext_rows(ct_b)
        logf_t = logf_all[:, b * tm:(b + 1) * tm]
        logft_ref[b] = logf_t
        ct_new = _lane_cumsum(logf_t, tm) + ct_b[:, cache_len - 1:cache_len]
        cextn_ref[b] = _key_ext_rows(ct_new)
        ccol_ref[b] = _to_columns(ct_new)


def _fox_proj_sample(x3, g, wt, bf_col, logf_cache_t, n_head, nb):
    bsz, tm, d = x3.shape
    hw = (wt.shape[0] - n_head) // 3
    cache_len = logf_cache_t.shape[2]
    assert 4 * n_head <= LANES and bsz % nb == 0
    kern = functools.partial(_fox_proj_sample_kernel, hw=hw, n_head=n_head,
                             scale=(hw // n_head) ** -0.5 * LOG2E)
    tok = pl.BlockSpec((nb, tm, hw), lambda b: (b, 0, 0))
    return pl.pallas_call(
        kern,
        grid=(bsz // nb,),
        in_specs=[
            pl.BlockSpec((nb, tm, d), lambda b: (b, 0, 0)),
            _resident((1, d)),
            _resident(wt.shape),
            _resident(bf_col.shape),
            pl.BlockSpec((nb, n_head, cache_len), lambda b: (b, 0, 0)),
        ],
        out_specs=[
            tok, tok, tok,
            pl.BlockSpec((nb, n_head, tm), lambda b: (b, 0, 0)),
            pl.BlockSpec((nb, tm, LANES), lambda b: (b, 0, 0)),
            pl.BlockSpec((nb, LANES, tm), lambda b: (b, 0, 0)),
            pl.BlockSpec((nb, LANES, cache_len), lambda b: (b, 0, 0)),
        ],
        out_shape=[
            jax.ShapeDtypeStruct((bsz, tm, hw), BF16),
            jax.ShapeDtypeStruct((bsz, tm, hw), F32),
            jax.ShapeDtypeStruct((bsz, tm, hw), F32),
            jax.ShapeDtypeStruct((bsz, n_head, tm), F32),
            jax.ShapeDtypeStruct((bsz, tm, LANES), F32),
            jax.ShapeDtypeStruct((bsz, LANES, tm), BF16),
            jax.ShapeDtypeStruct((bsz, LANES, cache_len), BF16),
        ],
        scratch_shapes=[pltpu.VMEM(wt.shape, BF16)],
        compiler_params=_params(1),
        name="fox_proj_sample",
    )(x3, g, wt, bf_col, logf_cache_t)


def _attn_setup_pair(p, qt_ref, cq_rows, qaug_ref, *, head0, tq, hd, n_head):
    ext_rows = 4 * n_head
    rr = lax.broadcasted_iota(jnp.int32, (ext_rows, tq), 0)
    for h in range(2):
        cols = slice(h * tq, (h + 1) * tq)
        rows = slice(p * LANES + h * hd, p * LANES + (h + 1) * hd)
        qaug_ref[p, h * hd:(h + 1) * hd, cols] = qt_ref[rows, :]
        qaug_ref[p, LANES:LANES + ext_rows, cols] = _query_ext(rr, head0 + 2 * p + h, n_head, cq_rows[2 * p + h])


def _attn_update(n_pair, kaug_of, vt_of, vt_prev_last, mask, next_kaug0,
                 qaug_ref, m_ref, acc_ref, s0_ref, plast_ref, alast_ref):
    last = n_pair - 1
    s_next = s0_ref[...]
    acc_ref[last] = acc_ref[last] * alast_ref[...] + _dot(vt_prev_last(), plast_ref[...])
    pending = None
    for p in range(n_pair):
        s = s_next
        if p + 1 < n_pair:
            s_next = _dot(kaug_of(p + 1), qaug_ref[p + 1])
        elif next_kaug0 is not None:
            s0_ref[...] = _dot(next_kaug0(), qaug_ref[0])
        if mask is not None:
            s = jnp.where(mask, s, MASK_VALUE)
        m_old = m_ref[p]
        m_new = jnp.maximum(m_old, jnp.max(s, axis=0, keepdims=True))
        alpha = jnp.exp2(m_old - m_new)
        pr = jnp.exp2(s - m_new).astype(BF16)
        m_ref[p] = m_new
        if pending is not None:
            q, pr_q, alpha_q = pending
            acc_ref[q] = acc_ref[q] * alpha_q + _dot(vt_of(q), pr_q)
        pending = (p, pr, alpha)
    plast_ref[...] = pending[1]
    alast_ref[...] = pending[2]


def _attn_finish(o_ref, acc_ref, *, n_pair, tq, hd):
    for p in range(n_pair):
        full = acc_ref[p, 0:LANES, :] * (1.0 / acc_ref[p, LANES:LANES + 1, :])
        z = jnp.concatenate([full[0:hd, 0:tq], full[hd:2 * hd, tq:2 * tq]], axis=0)
        o_ref[:, p * LANES:(p + 1) * LANES] = z.T.astype(o_ref.dtype)


def _fox_attn_prompt_kernel(qt_ref, kaug_ref, vt_ref, ct_ref, o_ref,
                            qaug_ref, m_ref, acc_ref, s0_ref, plast_ref, alast_ref,
                            *, tq, n_pair, hd, n_head):
    g = pl.program_id(1)
    i = pl.program_id(2)
    last = n_pair - 1
    qs = pl.ds(pl.multiple_of(i * tq, tq), tq)
    cq_rows = [ct_ref[h:h + 1, qs] for h in range(2 * n_pair)]
    setup = functools.partial(_attn_setup_pair, qt_ref=qt_ref, cq_rows=cq_rows, qaug_ref=qaug_ref,
                              head0=g * (2 * n_pair), tq=tq, hd=hd, n_head=n_head)
    state = (qaug_ref, m_ref, acc_ref, s0_ref, plast_ref, alast_ref)

    @pl.when((pl.program_id(0) == 0) & (g == 0) & (i == 0))
    def _():
        qaug_ref[...] = jnp.zeros_like(qaug_ref)

    setup(0)
    s0_ref[...] = _dot(kaug_ref[0, 0:tq, :], qaug_ref[0])
    for p in range(1, n_pair):
        setup(p)
    m_ref[...] = jnp.full_like(m_ref, MASK_VALUE)
    acc_ref[...] = jnp.zeros_like(acc_ref)
    plast_ref[...] = jnp.zeros_like(plast_ref)
    alast_ref[...] = jnp.ones_like(alast_ref)

    def keys(j):
        return pl.ds(pl.multiple_of(j * tq, tq), tq)

    def vt_of(p, ks):
        return vt_ref[p, :, ks]

    def kv_block(j, mask, has_next):
        ks = keys(j)
        _attn_update(n_pair, lambda p: kaug_ref[p, ks, :], lambda p: vt_of(p, ks),
                     lambda: vt_of(last, keys(jnp.maximum(j - 1, 0))), mask,
                     (lambda: kaug_ref[0, keys(j + 1), :]) if has_next else None, *state)

    def body(j, carry):
        kv_block(j, None, True)
        return carry

    lax.fori_loop(0, i, body, 0)
    row = lax.broadcasted_iota(jnp.int32, (tq, 2 * tq), 0)
    col = lax.broadcasted_iota(jnp.int32, (tq, 2 * tq), 1) % tq
    kv_block(i, row <= col, False)
    acc_ref[last] = acc_ref[last] * alast_ref[...] + _dot(vt_of(last, keys(i)), plast_ref[...])
    _attn_finish(o_ref, acc_ref, n_pair=n_pair, tq=tq, hd=hd)


def _fox_attn_prompt(qt, kaug, vtb, ct, tq, n_pair, n_head):
    bsz, hw, t_len = qt.shape
    hd = hw // n_head
    assert 2 * hd == LANES and t_len % tq == 0 and tq % LANES == 0
    width = n_pair * LANES
    n_group = hw // width
    ct4 = ct.reshape(bsz, n_group, 2 * n_pair, t_len)
    kern = functools.partial(_fox_attn_prompt_kernel, tq=tq, n_pair=n_pair, hd=hd, n_head=n_head)
    return pl.pallas_call(
        kern,
        grid=(bsz, n_group, t_len // tq),
        in_specs=[
            pl.BlockSpec((None, width, tq), lambda b, g, i: (b, g, i)),
            pl.BlockSpec((None, n_pair, t_len, 2 * LANES), lambda b, g, i: (b, g, 0, 0)),
            pl.BlockSpec((None, n_pair, VT_ROWS, t_len), lambda b, g, i: (b, g, 0, 0)),
            pl.BlockSpec((None, None, 2 * n_pair, t_len), lambda b, g, i: (b, g, 0, 0)),
        ],
        out_specs=pl.BlockSpec((None, tq, width), lambda b, g, i: (b, i, g)),
        out_shape=jax.ShapeDtypeStruct((bsz, t_len, hw), BF16),
        scratch_shapes=[
            pltpu.VMEM((n_pair, 2 * LANES, 2 * tq), BF16),
            pltpu.VMEM((n_pair, 1, 2 * tq), F32),
            pltpu.VMEM((n_pair, VT_ROWS, 2 * tq), F32),
            pltpu.VMEM((tq, 2 * tq), F32),
            pltpu.VMEM((tq, 2 * tq), BF16),
            pltpu.VMEM((1, 2 * tq), F32),
        ],
        compiler_params=_params(3),
        name="fox_attn_prompt",
    )(qt, kaug, vtb, ct4)


def _fox_attn_sample_kernel(q_ref, ccol_ref, kt_ref, vt_ref, cextc_ref, kn_ref, vn_ref, cextn_ref, o_ref,
                            *, n_pair, hd, n_head):
    g = pl.program_id(1)
    tq = q_ref.shape[0]
    eye = _identity(LANES)
    lane = lax.broadcasted_iota(jnp.int32, (tq, LANES), 1)
    row = lax.broadcasted_iota(jnp.int32, (2 * tq, tq), 0) % tq
    col = lax.broadcasted_iota(jnp.int32, (2 * tq, tq), 1)
    causal_new = col <= row
    ccol = ccol_ref[...]

    def scores(p):
        feat = slice(p * LANES, (p + 1) * LANES)
        q_pair = q_ref[:, feat]
        blocks = []
        for h in range(2):
            hh = g * (2 * n_pair) + 2 * p + h
            cq = jnp.sum(jnp.where(lane == hh, ccol, 0.0), axis=1, keepdims=True)
            top = jnp.where(lane // hd == h, q_pair, jnp.zeros_like(q_pair))
            blocks.append(jnp.concatenate([top, _query_ext(lane, hh, n_head, cq)], axis=1))
        qaug = jnp.concatenate(blocks, axis=0)
        s_c = _dot(qaug, jnp.concatenate([kt_ref[feat, :].astype(BF16), cextc_ref[...]], axis=0))
        kn_t = _dot_nt(eye, kn_ref[:, feat].astype(BF16)).astype(BF16)
        s_n = _dot(qaug, jnp.concatenate([kn_t, cextn_ref[...]], axis=0))
        return s_c, jnp.where(causal_new, s_n, MASK_VALUE)

    nxt = scores(0)
    for p in range(n_pair):
        s_c, s_n = nxt
        if p + 1 < n_pair:
            nxt = scores(p + 1)
        feat = slice(p * LANES, (p + 1) * LANES)
        m = jnp.maximum(jnp.max(s_c, axis=1, keepdims=True), jnp.max(s_n, axis=1, keepdims=True))
        p_c = jnp.exp2(s_c - m)
        p_n = jnp.exp2(s_n - m)
        l = jnp.sum(p_c, axis=1, keepdims=True) + jnp.sum(p_n, axis=1, keepdims=True)
        o = _dot_nt(p_c.astype(BF16), vt_ref[feat, :].astype(BF16)) + _dot(p_n.astype(BF16), vn_ref[:, feat].astype(BF16))
        o = o / l
        o_ref[:, feat] = jnp.where(lane < hd, o[0:tq], o[tq:2 * tq]).astype(o_ref.dtype)


def _fox_attn_sample(q, c_col, kt_cache, vt_cache, cext_cache, k_new, v_new, cext_new, n_pair, n_head):
    bsz, tq, hw = q.shape
    cache_len = kt_cache.shape[2]
    hd = hw // n_head
    assert 2 * hd == LANES and 4 * n_head <= LANES
    width = n_pair * LANES
    new_spec = pl.BlockSpec((None, tq, width), lambda b, g: (b, 0, g))
    cache_spec = pl.BlockSpec((None, width, cache_len), lambda b, g: (b, g, 0))
    kern = functools.partial(_fox_attn_sample_kernel, n_pair=n_pair, hd=hd, n_head=n_head)
    return pl.pallas_call(
        kern,
        grid=(bsz, hw // width),
        in_specs=[
            new_spec,
            pl.BlockSpec((None, tq, LANES), lambda b, g: (b, 0, 0)),
            cache_spec, cache_spec,
            pl.BlockSpec((None, LANES, cache_len), lambda b, g: (b, 0, 0)),
            new_spec, new_spec,
            pl.BlockSpec((None, LANES, tq), lambda b, g: (b, 0, 0)),
        ],
        out_specs=new_spec,
        out_shape=jax.ShapeDtypeStruct((bsz, tq, hw), BF16),
        compiler_params=_params(2),
        name="fox_attn_sample",
    )(q, c_col, kt_cache, vt_cache, cext_cache, k_new, v_new, cext_new)


def kernel(x_prompt, x_sample, state_gla, cache_fox_k, cache_fox_v, cache_fox_logf,
           norm_mix, gla_w_in, gla_w_g2, gla_b_g, gla_norm, gla_w_out,
           fox_w_in, fox_b_f, fox_w_out, norm_ffn, ffn_w_in, ffn_w_down, norm_final):
    d = x_prompt.shape[-1]
    depth = norm_mix.shape[0]
    groups = [x_prompt, x_sample]
    shapes = [x.shape for x in groups]
    xs = [x.reshape(-1, d) for x in groups]
    row_tiles = [min(512, x.shape[0]) for x in xs]
    w_ffn_in = ffn_w_in.astype(BF16)
    w_ffn_down = ffn_w_down.astype(BF16)

    gla_states = [[], []]
    fox_k, fox_v, fox_f = [[], []], [[], []], [[], []]
    for i in range(depth):
        j = i // 2
        g_mix = norm_mix[i].reshape(1, d)
        g_ffn = norm_ffn[i].reshape(1, d)
        last = i == depth - 1
        if i % 2 == 0:
            _, n_head, dk, dv = state_gla.shape[1:]
            hk, hv = n_head * dk, n_head * dv
            n_main = 2 * hk + 2 * hv
            w_in = gla_w_in[j]
            wt = w_in.T
            w_g2 = gla_w_g2[j].astype(BF16)
            b_g = gla_b_g[j].reshape(1, hk)
            w_out = gla_w_out[j].astype(BF16)
            norm_g = gla_norm[j].reshape(1, hv)
            s0s = [jnp.zeros((shapes[0][0], n_head, dk, dv), F32), state_gla[j]]
            for gi in range(2):
                bsz, t_len, _ = shapes[gi]
                proj, glog = _gla_proj(xs[gi], g_mix, wt, w_g2, b_g, row_tiles[gi])
                og, s_fin = _gla_mix(proj.reshape(bsz, t_len, n_main), glog.reshape(bsz, t_len, hk),
                                     s0s[gi], norm_g, min(t_len, 256), 2 if gi == 0 else min(4, bsz))
                gla_states[gi].append(s_fin)
                xs[gi] = _post(xs[gi], og.reshape(-1, hv), w_out, g_ffn, w_ffn_in, w_ffn_down, i,
                               norm_final.reshape(1, d), row_tiles[gi], last)
        else:
            n_head = fox_b_f.shape[1]
            hw = fox_w_out.shape[1]
            hd = hw // n_head
            wt = fox_w_in[j].T
            bf_col = fox_b_f[j].reshape(n_head, 1)
            w_out = fox_w_out[j].astype(BF16)
            for gi in range(2):
                bsz, t_len, _ = shapes[gi]
                x3 = xs[gi].reshape(bsz, t_len, d)
                if gi == 0:
                    qt, kaug, kt, vt, vtb, logf_t, ct = _fox_proj_prompt(x3, g_mix, wt, bf_col,
                                                                         min(512, t_len), n_head)
                    o = _fox_attn_prompt(qt, kaug, vtb, ct, 256, 8, n_head)
                    k_out = kt.reshape(bsz, n_head, hd, t_len).transpose(0, 3, 1, 2)
                    v_out = vt.reshape(bsz, n_head, hd, t_len).transpose(0, 3, 1, 2)
                else:
                    cache_len = cache_fox_logf.shape[2]
                    q, k, v, logf_t, c_col, cext_new, cext_cache = _fox_proj_sample(
                        x3, g_mix, wt, bf_col, jnp.transpose(cache_fox_logf[j], (0, 2, 1)), n_head,
                        min(8, bsz))
                    kt_cache = jnp.transpose(cache_fox_k[j], (0, 2, 3, 1)).reshape(bsz, hw, cache_len)
                    vt_cache = jnp.transpose(cache_fox_v[j], (0, 2, 3, 1)).reshape(bsz, hw, cache_len)
                    o = _fox_attn_sample(q, c_col, kt_cache, vt_cache, cext_cache, k, v, cext_new, 4, n_head)
                    k_out = k.reshape(bsz, t_len, n_head, hd)
                    v_out = v.reshape(bsz, t_len, n_head, hd)
                fox_k[gi].append(k_out)
                fox_v[gi].append(v_out)
                fox_f[gi].append(jnp.transpose(logf_t, (0, 2, 1)))
                xs[gi] = _post(xs[gi], o.reshape(-1, hw), w_out, g_ffn, w_ffn_in, w_ffn_down, i,
                               norm_final.reshape(1, d), row_tiles[gi], last)

    y_prompt = xs[0].reshape(shapes[0])
    y_sample = xs[1].reshape(shapes[1])
    st = lambda parts: jnp.stack(parts, axis=0)
    return (y_prompt, y_sample, st(gla_states[0]), st(fox_k[0]), st(fox_v[0]), st(fox_f[0]),
            st(gla_states[1]), st(fox_k[1]), st(fox_v[1]), st(fox_f[1]))
```

```python
import functools

import jax
import jax.numpy as jnp
from jax import lax
from jax.experimental import pallas as pl
from jax.experimental.pallas import tpu as pltpu

F32 = jnp.float32
BF16 = jnp.bfloat16

EPS = 1e-6
MASK_VALUE = -1e30
LOG2E = 1.4426950408889634

GLA_HEADS = 4
GLA_CHUNK = 64
GLA_GATE_TAU = 16.0

LANES = 128
VT_ROWS = LANES + 16
VMEM_LIMIT_BYTES = 56 * 1024 * 1024


def _params(n_grid):
    return pltpu.CompilerParams(
        dimension_semantics=("arbitrary",) * n_grid,
        vmem_limit_bytes=VMEM_LIMIT_BYTES,
    )


def _resident(shape):
    nd = len(shape)
    return pl.BlockSpec(shape, lambda *_: (0,) * nd, pipeline_mode=pl.Buffered(1))


def _dot(a, b):
    return jnp.dot(a, b, preferred_element_type=F32)


def _dot_nt(a, b):
    return lax.dot_general(a, b, (((1,), (1,)), ((), ())), preferred_element_type=F32)


def _dot_tn(a, b):
    return lax.dot_general(a, b, (((0,), (0,)), ((), ())), preferred_element_type=F32)


def _split3(x):
    hi = x.astype(BF16)
    r1 = x - hi.astype(F32)
    mid = r1.astype(BF16)
    lo = (r1 - mid.astype(F32)).astype(BF16)
    return hi, mid, lo


def _sum01(dot_fn, x, ones_first, mat01):
    acc = None
    for part in _split3(x):
        term = dot_fn(mat01, part) if ones_first else dot_fn(part, mat01)
        acc = term if acc is None else acc + term
    return acc


def _rmsnorm(x, g):
    var = jnp.mean(x * x, axis=-1, keepdims=True)
    return x * lax.rsqrt(var + EPS) * g


def _log_sigmoid(z):
    return jnp.minimum(z, 0.0) - jnp.log1p(jnp.exp(-jnp.abs(z)))


def _silu(z):
    return z * jax.nn.sigmoid(z)


def _identity(n):
    r = lax.broadcasted_iota(jnp.int32, (n, n), 0)
    c = lax.broadcasted_iota(jnp.int32, (n, n), 1)
    return (r == c).astype(BF16)


def _cast_weights_once(first_step, wt_ref, wtb_ref):
    @pl.when(first_step)
    def _():
        wtb_ref[...] = wt_ref[...].astype(BF16)


def _gla_proj_kernel(x_ref, g_ref, wt_ref, wg2_ref, bg_ref, proj_ref, glog_ref, wtb_ref, *, n_main):
    _cast_weights_once(pl.program_id(0) == 0, wt_ref, wtb_ref)
    rank = wg2_ref.shape[0]
    h = _rmsnorm(x_ref[...], g_ref[...]).astype(BF16)
    proj_ref[...] = _dot_nt(h, wtb_ref[0:n_main, :])
    gl_t = _dot_nt(wtb_ref[n_main:n_main + rank, :], h).astype(BF16)
    z = _dot_tn(gl_t, wg2_ref[...]) + bg_ref[...]
    glog_ref[...] = _log_sigmoid(z) / GLA_GATE_TAU


def _gla_proj(x2d, g, wt, w_g2, b_g, tm):
    n, d = x2d.shape
    rank, hk = w_g2.shape
    n_main = wt.shape[0] - rank
    return pl.pallas_call(
        functools.partial(_gla_proj_kernel, n_main=n_main),
        grid=(n // tm,),
        in_specs=[
            pl.BlockSpec((tm, d), lambda i: (i, 0)),
            _resident((1, d)),
            _resident(wt.shape),
            _resident(w_g2.shape),
            _resident((1, hk)),
        ],
        out_specs=[
            pl.BlockSpec((tm, n_main), lambda i: (i, 0)),
            pl.BlockSpec((tm, hk), lambda i: (i, 0)),
        ],
        out_shape=[
            jax.ShapeDtypeStruct((n, n_main), F32),
            jax.ShapeDtypeStruct((n, hk), F32),
        ],
        scratch_shapes=[pltpu.VMEM(wt.shape, BF16)],
        compiler_params=_params(1),
        name="gla_proj",
    )(x2d, g, wt, w_g2, b_g)


def _gla_mix_kernel(q_ref, k_ref, v_ref, r_ref, glog_ref, s0_ref, ng_ref,
                    og_ref, sfin_ref, s_ref, *, dk, dv):
    t = pl.program_id(1)
    nb, tb, _ = q_ref.shape
    n_chunk = tb // GLA_CHUNK
    seqs = range(nb)
    heads = range(GLA_HEADS)
    units = [(sb, h) for sb in seqs for h in heads]
    chunks = [slice(c * GLA_CHUNK, (c + 1) * GLA_CHUNK) for c in range(n_chunk)]
    ksl = [slice(h * dk, (h + 1) * dk) for h in heads]
    vsl = [slice(h * dv, (h + 1) * dv) for h in heads]

    @pl.when(t == 0)
    def _():
        s_ref[...] = s0_ref[...]

    row = lax.broadcasted_iota(jnp.int32, (tb, tb), 0)
    col = lax.broadcasted_iota(jnp.int32, (tb, tb), 1)
    same_chunk = (row // GLA_CHUNK) == (col // GLA_CHUNK)
    causal = same_chunk & (col <= row)
    cum_mat = causal.astype(BF16)

    b = [_sum01(_dot, glog_ref[sb], True, cum_mat) for sb in seqs]
    qe, ke, kd, dec_t = [], [], [], []
    for sb in seqs:
        tot_rows = [b[sb][(c + 1) * GLA_CHUNK - 1:(c + 1) * GLA_CHUNK, :] for c in range(n_chunk)]
        b_last = jnp.concatenate([jnp.broadcast_to(r, (GLA_CHUNK, r.shape[1])) for r in tot_rows], axis=0)
        pad_rows = [jnp.zeros_like(tot_rows[0])] * (8 - n_chunk % 8 if n_chunk % 8 else 0)
        dec_t.append(jnp.exp(_to_columns(jnp.concatenate(tot_rows + pad_rows, axis=0))))
        q = q_ref[sb]
        k = k_ref[sb]
        qe.append((q * jnp.exp(b[sb]) * (dk ** -0.5)).astype(BF16))
        ke.append((k * jnp.exp(-b[sb])).astype(BF16))
        kd.append((k * jnp.exp(b_last - b[sb])).astype(BF16))

    v_b = {u: v_ref[u[0], :, vsl[u[1]]].astype(BF16) for u in units}
    a_raw = {(sb, h): _dot_nt(qe[sb][:, ksl[h]], ke[sb][:, ksl[h]]) for sb, h in units}
    upd = {(sb, h): [_dot_tn(kd[sb][rs, ksl[h]], v_b[sb, h][rs]) for rs in chunks] for sb, h in units}
    o_intra = {u: _dot(jnp.where(causal, a_raw[u], 0.0).astype(BF16), v_b[u]) for u in units}
    s_in = {}
    for sb, h in units:
        s = s_ref[sb, h]
        s_in[sb, h] = []
        for c in range(n_chunk):
            s_in[sb, h].append(s.astype(BF16))
            s = s * dec_t[sb][ksl[h], c:c + 1] + upd[sb, h][c]
        s_ref[sb, h] = s
    for sb, h in units:
        o_parts = [o_intra[sb, h][rs] + _dot(qe[sb][rs, ksl[h]], s_in[sb, h][c]) for c, rs in enumerate(chunks)]
        o = o_parts[0] if n_chunk == 1 else jnp.concatenate(o_parts, axis=0)
        on = _rmsnorm(o, ng_ref[:, vsl[h]])
        og_ref[sb, :, vsl[h]] = (on * _silu(r_ref[sb, :, vsl[h]])).astype(BF16)

    @pl.when(t == pl.num_programs(1) - 1)
    def _():
        sfin_ref[...] = s_ref[...]


def _gla_mix(proj3, glog3, s0, norm_g, tb, nb):
    bsz, t_len, _ = proj3.shape
    _, n_head, dk, dv = s0.shape
    hk, hv = n_head * dk, n_head * dv
    assert t_len % tb == 0 and tb % GLA_CHUNK == 0 and hv == 2 * hk and bsz % nb == 0
    kern = functools.partial(_gla_mix_kernel, dk=dk, dv=dv)
    state_spec = pl.BlockSpec((nb, n_head, dk, dv), lambda b, t: (b, 0, 0, 0))
    return pl.pallas_call(
        kern,
        grid=(bsz // nb, t_len // tb),
        in_specs=[
            pl.BlockSpec((nb, tb, hk), lambda b, t: (b, t, 0)),
            pl.BlockSpec((nb, tb, hk), lambda b, t: (b, t, 1)),
            pl.BlockSpec((nb, tb, hv), lambda b, t: (b, t, 1)),
            pl.BlockSpec((nb, tb, hv), lambda b, t: (b, t, 2)),
            pl.BlockSpec((nb, tb, hk), lambda b, t: (b, t, 0)),
            state_spec,
            _resident((1, hv)),
        ],
        out_specs=[
            pl.BlockSpec((nb, tb, hv), lambda b, t: (b, t, 0)),
            state_spec,
        ],
        out_shape=[
            jax.ShapeDtypeStruct((bsz, t_len, hv), BF16),
            jax.ShapeDtypeStruct(s0.shape, F32),
        ],
        scratch_shapes=[pltpu.VMEM((nb, n_head, dk, dv), F32)],
        compiler_params=_params(2),
        name="gla_mix",
    )(proj3, proj3, proj3, proj3, glog3, s0, norm_g)


def _post_kernel(x_ref, o_ref, wo_ref, g_ref, win_ref, wdown_ref, gfin_ref, y_ref, act_ref,
                 *, d_ff, ff_tile, final_norm):
    x1 = x_ref[...] + _dot(o_ref[...], wo_ref[...])
    h = _rmsnorm(x1, g_ref[...]).astype(BF16)
    for j in range(d_ff // ff_tile):
        gate = _dot(h, win_ref[:, j * ff_tile:(j + 1) * ff_tile])
        up = _dot(h, win_ref[:, d_ff + j * ff_tile:d_ff + (j + 1) * ff_tile])
        act_ref[:, j * ff_tile:(j + 1) * ff_tile] = (_silu(gate) * up).astype(BF16)
    y = x1 + _dot(act_ref[...], wdown_ref[...])
    if final_norm:
        y = _rmsnorm(y, gfin_ref[...])
    y_ref[...] = y


def _post(x2d, o2d, w_out, g_ffn, w_in_all, w_down_all, layer, g_final, tm, final_norm):
    n, d = x2d.shape
    d_ff = w_down_all.shape[1]
    ff_tile = 256
    assert d_ff % ff_tile == 0 and n % tm == 0
    kern = functools.partial(_post_kernel, d_ff=d_ff, ff_tile=ff_tile, final_norm=final_norm)
    return pl.pallas_call(
        kern,
        grid=(n // tm,),
        in_specs=[
            pl.BlockSpec((tm, d), lambda i: (i, 0)),
            pl.BlockSpec((tm, o2d.shape[1]), lambda i: (i, 0)),
            _resident(w_out.shape),
            _resident((1, d)),
            pl.BlockSpec((None,) + w_in_all.shape[1:], lambda i: (layer, 0, 0), pipeline_mode=pl.Buffered(1)),
            pl.BlockSpec((None,) + w_down_all.shape[1:], lambda i: (layer, 0, 0), pipeline_mode=pl.Buffered(1)),
            _resident((1, d)),
        ],
        out_specs=pl.BlockSpec((tm, d), lambda i: (i, 0)),
        out_shape=jax.ShapeDtypeStruct((n, d), F32),
        scratch_shapes=[pltpu.VMEM((tm, d_ff), BF16)],
        compiler_params=_params(1),
        name="post_final" if final_norm else "post",
    )(x2d, o2d, w_out, g_ffn, w_in_all, w_down_all, g_final)


def _lane_cumsum(x, tile):
    r = lax.broadcasted_iota(jnp.int32, (tile, tile), 0)
    c = lax.broadcasted_iota(jnp.int32, (tile, tile), 1)
    upper = (r <= c).astype(BF16)
    rows = x.shape[0]
    carry = jnp.zeros((rows, 1), F32)
    out = []
    for j in range(x.shape[1] // tile):
        parts = jnp.concatenate(_split3(x[:, j * tile:(j + 1) * tile]), axis=0)
        s = _dot(parts, upper)
        blk = carry + s[0:rows] + s[rows:2 * rows] + s[2 * rows:3 * rows]
        out.append(blk)
        carry = blk[:, tile - 1:tile]
    return out[0] if len(out) == 1 else jnp.concatenate(out, axis=1)


def _to_columns(x_t):
    rows = x_t.shape[0]
    parts = jnp.concatenate(_split3(x_t), axis=0)
    r = lax.broadcasted_iota(jnp.int32, (3 * rows, LANES), 0)
    c = lax.broadcasted_iota(jnp.int32, (3 * rows, LANES), 1)
    place = ((r % rows) == c).astype(BF16)
    return _dot_tn(parts, place)


def _key_ext_cols(c_cols, n_head):
    hi, mid, lo = (p.astype(F32) for p in _split3(c_cols * LOG2E))
    lane = lax.broadcasted_iota(jnp.int32, c_cols.shape, 1)
    ones = ((lane >= 3 * n_head) & (lane < 3 * n_head + 3)).astype(F32)
    ext = hi + pltpu.roll(mid, n_head, 1) + pltpu.roll(lo, 2 * n_head, 1) + ones
    return ext.astype(BF16)


def _key_ext_rows(c_rows):
    n_head, n = c_rows.shape
    hi, mid, lo = _split3(c_rows * LOG2E)
    r = lax.broadcasted_iota(jnp.int32, (n_head, n), 0)
    ones = (r < 3).astype(BF16)
    zeros = jnp.zeros((LANES - 4 * n_head, n), BF16)
    return jnp.concatenate([hi, mid, lo, ones, zeros], axis=0)


def _query_ext(idx, hh, n_head, cq):
    cq_hi, cq_mid, cq_lo = (p.astype(F32) for p in _split3(cq * LOG2E))
    ext = jnp.where((idx == hh) | (idx == n_head + hh) | (idx == 2 * n_head + hh), -1.0, 0.0)
    ext = jnp.where(idx == 3 * n_head, cq_hi, ext)
    ext = jnp.where(idx == 3 * n_head + 1, cq_mid, ext)
    ext = jnp.where(idx == 3 * n_head + 2, cq_lo, ext)
    return ext.astype(BF16)


def _fox_proj_prompt_kernel(x_ref, g_ref, wtf_ref, bfc_ref,
                            qt_ref, kaug_ref, kt_ref, vt_ref, vtb_ref, logft_ref, ct_ref, carry_ref, wt_ref,
                            *, hw, n_head, scale):
    t = pl.program_id(1)
    tm = x_ref.shape[0]
    _cast_weights_once((pl.program_id(0) == 0) & (t == 0), wtf_ref, wt_ref)

    @pl.when(t == 0)
    def _():
        carry_ref[...] = jnp.zeros_like(carry_ref)

    h = _rmsnorm(x_ref[...], g_ref[...]).astype(BF16)
    qt_ref[...] = (_dot_nt(wt_ref[0:hw, :], h) * scale).astype(BF16)
    kt = _dot_nt(wt_ref[hw:2 * hw, :], h)
    kt_ref[...] = kt
    vt = _dot_nt(wt_ref[2 * hw:3 * hw, :], h)
    vt_ref[...] = vt
    ones = jnp.ones((VT_ROWS - LANES, tm), BF16)
    for p in range(hw // LANES):
        vtb_ref[p, 0:LANES, :] = vt[p * LANES:(p + 1) * LANES, :].astype(BF16)
        vtb_ref[p, LANES:VT_ROWS, :] = ones
    k = kt.T

    logf_t = _log_sigmoid(_dot_nt(wt_ref[3 * hw:3 * hw + n_head, :], h) + bfc_ref[...])
    logft_ref[...] = logf_t
    ct_blk = _lane_cumsum(logf_t, min(tm, 256)) + carry_ref[:, 0:1]
    ct_ref[...] = ct_blk
    carry_ref[...] = jnp.broadcast_to(ct_blk[:, tm - 1:tm], carry_ref.shape)
    cext = _key_ext_cols(_to_columns(ct_blk), n_head)
    for p in range(hw // LANES):
        kaug_ref[p, :, 0:LANES] = k[:, p * LANES:(p + 1) * LANES].astype(BF16)
        kaug_ref[p, :, LANES:2 * LANES] = cext


def _fox_proj_prompt(x3, g, wt, bf_col, tm, n_head):
    bsz, t_len, d = x3.shape
    hw = (wt.shape[0] - n_head) // 3
    n_pair = hw // LANES
    assert t_len % tm == 0 and 3 * n_head + 3 <= LANES
    kern = functools.partial(_fox_proj_prompt_kernel, hw=hw, n_head=n_head,
                             scale=(hw // n_head) ** -0.5 * LOG2E)
    feat = pl.BlockSpec((None, hw, tm), lambda b, t: (b, 0, t))
    head = pl.BlockSpec((None, n_head, tm), lambda b, t: (b, 0, t))
    return pl.pallas_call(
        kern,
        grid=(bsz, t_len // tm),
        in_specs=[
            pl.BlockSpec((None, tm, d), lambda b, t: (b, t, 0)),
            _resident((1, d)),
            _resident(wt.shape),
            _resident(bf_col.shape),
        ],
        out_specs=[
            feat,
            pl.BlockSpec((None, n_pair, tm, 2 * LANES), lambda b, t: (b, 0, t, 0)),
            feat, feat,
            pl.BlockSpec((None, n_pair, VT_ROWS, tm), lambda b, t: (b, 0, 0, t)),
            head, head,
        ],
        out_shape=[
            jax.ShapeDtypeStruct((bsz, hw, t_len), BF16),
            jax.ShapeDtypeStruct((bsz, n_pair, t_len, 2 * LANES), BF16),
            jax.ShapeDtypeStruct((bsz, hw, t_len), F32),
            jax.ShapeDtypeStruct((bsz, hw, t_len), F32),
            jax.ShapeDtypeStruct((bsz, n_pair, VT_ROWS, t_len), BF16),
            jax.ShapeDtypeStruct((bsz, n_head, t_len), F32),
            jax.ShapeDtypeStruct((bsz, n_head, t_len), F32),
        ],
        scratch_shapes=[pltpu.VMEM((n_head, LANES), F32), pltpu.VMEM(wt.shape, BF16)],
        compiler_params=_params(2),
        name="fox_proj_prompt",
    )(x3, g, wt, bf_col)


def _fox_proj_sample_kernel(x_ref, g_ref, wtf_ref, bfc_ref, lct_ref,
                            q_ref, k_ref, v_ref, logft_ref, ccol_ref, cextn_ref, cextc_ref, wt_ref,
                            *, hw, n_head, scale):
    _cast_weights_once(pl.program_id(0) == 0, wtf_ref, wt_ref)
    nb, tm, d = x_ref.shape
    cache_len = lct_ref.shape[2]
    h = _rmsnorm(x_ref[...].reshape(nb * tm, d), g_ref[...]).astype(BF16)
    q_ref[...] = (_dot_nt(h, wt_ref[0:hw, :]) * scale).astype(BF16).reshape(nb, tm, hw)
    k_ref[...] = _dot_nt(h, wt_ref[hw:2 * hw, :]).reshape(nb, tm, hw)
    v_ref[...] = _dot_nt(h, wt_ref[2 * hw:3 * hw, :]).reshape(nb, tm, hw)

    ct_cache = _lane_cumsum(lct_ref[...].reshape(nb * n_head, cache_len), min(cache_len, 256))
    logf_all = _log_sigmoid(_dot_nt(wt_ref[3 * hw:3 * hw + n_head, :], h) + bfc_ref[...])
    for b in range(nb):
        ct_b = ct_cache[b * n_head:(b + 1) * n_head, :]
        cextc_ref[b] = _key_ext_rows(ct_b)
        logf_t = logf_all[:, b * tm:(b + 1) * tm]
        logft_ref[b] = logf_t
        ct_new = _lane_cumsum(logf_t, tm) + ct_b[:, cache_len - 1:cache_len]
        cextn_ref[b] = _key_ext_rows(ct_new)
        ccol_ref[b] = _to_columns(ct_new)


def _fox_proj_sample(x3, g, wt, bf_col, logf_cache_t, n_head, nb):
    bsz, tm, d = x3.shape
    hw = (wt.shape[0] - n_head) // 3
    cache_len = logf_cache_t.shape[2]
    assert 4 * n_head <= LANES and bsz % nb == 0
    kern = functools.partial(_fox_proj_sample_kernel, hw=hw, n_head=n_head,
                             scale=(hw // n_head) ** -0.5 * LOG2E)
    tok = pl.BlockSpec((nb, tm, hw), lambda b: (b, 0, 0))
    return pl.pallas_call(
        kern,
        grid=(bsz // nb,),
        in_specs=[
            pl.BlockSpec((nb, tm, d), lambda b: (b, 0, 0)),
            _resident((1, d)),
            _resident(wt.shape),
            _resident(bf_col.shape),
            pl.BlockSpec((nb, n_head, cache_len), lambda b: (b, 0, 0)),
        ],
        out_specs=[
            tok, tok, tok,
            pl.BlockSpec((nb, n_head, tm), lambda b: (b, 0, 0)),
            pl.BlockSpec((nb, tm, LANES), lambda b: (b, 0, 0)),
            pl.BlockSpec((nb, LANES, tm), lambda b: (b, 0, 0)),
            pl.BlockSpec((nb, LANES, cache_len), lambda b: (b, 0, 0)),
        ],
        out_shape=[
            jax.ShapeDtypeStruct((bsz, tm, hw), BF16),
            jax.ShapeDtypeStruct((bsz, tm, hw), F32),
            jax.ShapeDtypeStruct((bsz, tm, hw), F32),
            jax.ShapeDtypeStruct((bsz, n_head, tm), F32),
            jax.ShapeDtypeStruct((bsz, tm, LANES), F32),
            jax.ShapeDtypeStruct((bsz, LANES, tm), BF16),
            jax.ShapeDtypeStruct((bsz, LANES, cache_len), BF16),
        ],
        scratch_shapes=[pltpu.VMEM(wt.shape, BF16)],
        compiler_params=_params(1),
        name="fox_proj_sample",
    )(x3, g, wt, bf_col, logf_cache_t)


def _attn_setup_pair(p, qt_ref, cq_rows, qaug_ref, *, head0, tq, hd, n_head):
    ext_rows = 4 * n_head
    rr = lax.broadcasted_iota(jnp.int32, (ext_rows, tq), 0)
    for h in range(2):
        cols = slice(h * tq, (h + 1) * tq)
        rows = slice(p * LANES + h * hd, p * LANES + (h + 1) * hd)
        qaug_ref[p, h * hd:(h + 1) * hd, cols] = qt_ref[rows, :]
        qaug_ref[p, LANES:LANES + ext_rows, cols] = _query_ext(rr, head0 + 2 * p + h, n_head, cq_rows[2 * p + h])


def _attn_update(n_pair, kaug_of, vt_of, vt_prev_last, mask, next_kaug0,
                 qaug_ref, m_ref, acc_ref, s0_ref, plast_ref, alast_ref):
    last = n_pair - 1
    s_next = s0_ref[...]
    acc_ref[last] = acc_ref[last] * alast_ref[...] + _dot(vt_prev_last(), plast_ref[...])
    pending = None
    for p in range(n_pair):
        s = s_next
        if p + 1 < n_pair:
            s_next = _dot(kaug_of(p + 1), qaug_ref[p + 1])
        elif next_kaug0 is not None:
            s0_ref[...] = _dot(next_kaug0(), qaug_ref[0])
        if mask is not None:
            s = jnp.where(mask, s, MASK_VALUE)
        m_old = m_ref[p]
        m_new = jnp.maximum(m_old, jnp.max(s, axis=0, keepdims=True))
        alpha = jnp.exp2(m_old - m_new)
        pr = jnp.exp2(s - m_new).astype(BF16)
        m_ref[p] = m_new
        if pending is not None:
            q, pr_q, alpha_q = pending
            acc_ref[q] = acc_ref[q] * alpha_q + _dot(vt_of(q), pr_q)
        pending = (p, pr, alpha)
    plast_ref[...] = pending[1]
    alast_ref[...] = pending[2]


def _attn_finish(o_ref, acc_ref, *, n_pair, tq, hd):
    for p in range(n_pair):
        full = acc_ref[p, 0:LANES, :] * (1.0 / acc_ref[p, LANES:LANES + 1, :])
        z = jnp.concatenate([full[0:hd, 0:tq], full[hd:2 * hd, tq:2 * tq]], axis=0)
        o_ref[:, p * LANES:(p + 1) * LANES] = z.T.astype(o_ref.dtype)


def _fox_attn_prompt_kernel(qt_ref, kaug_ref, vt_ref, ct_ref, o_ref,
                            qaug_ref, m_ref, acc_ref, s0_ref, plast_ref, alast_ref,
                            *, tq, n_pair, hd, n_head):
    g = pl.program_id(1)
    i = pl.program_id(2)
    last = n_pair - 1
    qs = pl.ds(pl.multiple_of(i * tq, tq), tq)
    cq_rows = [ct_ref[h:h + 1, qs] for h in range(2 * n_pair)]
    setup = functools.partial(_attn_setup_pair, qt_ref=qt_ref, cq_rows=cq_rows, qaug_ref=qaug_ref,
                              head0=g * (2 * n_pair), tq=tq, hd=hd, n_head=n_head)
    state = (qaug_ref, m_ref, acc_ref, s0_ref, plast_ref, alast_ref)

    @pl.when((pl.program_id(0) == 0) & (g == 0) & (i == 0))
    def _():
        qaug_ref[...] = jnp.zeros_like(qaug_ref)

    setup(0)
    s0_ref[...] = _dot(kaug_ref[0, 0:tq, :], qaug_ref[0])
    for p in range(1, n_pair):
        setup(p)
    m_ref[...] = jnp.full_like(m_ref, MASK_VALUE)
    acc_ref[...] = jnp.zeros_like(acc_ref)
    plast_ref[...] = jnp.zeros_like(plast_ref)
    alast_ref[...] = jnp.ones_like(alast_ref)

    def keys(j):
        return pl.ds(pl.multiple_of(j * tq, tq), tq)

    def vt_of(p, ks):
        return vt_ref[p, :, ks]

    def kv_block(j, mask, has_next):
        ks = keys(j)
        _attn_update(n_pair, lambda p: kaug_ref[p, ks, :], lambda p: vt_of(p, ks),
                     lambda: vt_of(last, keys(jnp.maximum(j - 1, 0))), mask,
                     (lambda: kaug_ref[0, keys(j + 1), :]) if has_next else None, *state)

    def body(j, carry):
        kv_block(j, None, True)
        return carry

    lax.fori_loop(0, i, body, 0)
    row = lax.broadcasted_iota(jnp.int32, (tq, 2 * tq), 0)
    col = lax.broadcasted_iota(jnp.int32, (tq, 2 * tq), 1) % tq
    kv_block(i, row <= col, False)
    acc_ref[last] = acc_ref[last] * alast_ref[...] + _dot(vt_of(last, keys(i)), plast_ref[...])
    _attn_finish(o_ref, acc_ref, n_pair=n_pair, tq=tq, hd=hd)


def _fox_attn_prompt(qt, kaug, vtb, ct, tq, n_pair, n_head):
    bsz, hw, t_len = qt.shape
    hd = hw // n_head
    assert 2 * hd == LANES and t_len % tq == 0 and tq % LANES == 0
    width = n_pair * LANES
    n_group = hw // width
    ct4 = ct.reshape(bsz, n_group, 2 * n_pair, t_len)
    kern = functools.partial(_fox_attn_prompt_kernel, tq=tq, n_pair=n_pair, hd=hd, n_head=n_head)
    return pl.pallas_call(
        kern,
        grid=(bsz, n_group, t_len // tq),
        in_specs=[
            pl.BlockSpec((None, width, tq), lambda b, g, i: (b, g, i)),
            pl.BlockSpec((None, n_pair, t_len, 2 * LANES), lambda b, g, i: (b, g, 0, 0)),
            pl.BlockSpec((None, n_pair, VT_ROWS, t_len), lambda b, g, i: (b, g, 0, 0)),
            pl.BlockSpec((None, None, 2 * n_pair, t_len), lambda b, g, i: (b, g, 0, 0)),
        ],
        out_specs=pl.BlockSpec((None, tq, width), lambda b, g, i: (b, i, g)),
        out_shape=jax.ShapeDtypeStruct((bsz, t_len, hw), BF16),
        scratch_shapes=[
            pltpu.VMEM((n_pair, 2 * LANES, 2 * tq), BF16),
            pltpu.VMEM((n_pair, 1, 2 * tq), F32),
            pltpu.VMEM((n_pair, VT_ROWS, 2 * tq), F32),
            pltpu.VMEM((tq, 2 * tq), F32),
            pltpu.VMEM((tq, 2 * tq), BF16),
            pltpu.VMEM((1, 2 * tq), F32),
        ],
        compiler_params=_params(3),
        name="fox_attn_prompt",
    )(qt, kaug, vtb, ct4)


def _fox_attn_sample_kernel(q_ref, ccol_ref, kt_ref, vt_ref, cextc_ref, kn_ref, vn_ref, cextn_ref, o_ref,
                            *, n_pair, hd, n_head):
    g = pl.program_id(1)
    tq = q_ref.shape[0]
    eye = _identity(LANES)
    lane = lax.broadcasted_iota(jnp.int32, (tq, LANES), 1)
    row = lax.broadcasted_iota(jnp.int32, (2 * tq, tq), 0) % tq
    col = lax.broadcasted_iota(jnp.int32, (2 * tq, tq), 1)
    causal_new = col <= row
    ccol = ccol_ref[...]

    def scores(p):
        feat = slice(p * LANES, (p + 1) * LANES)
        q_pair = q_ref[:, feat]
        blocks = []
        for h in range(2):
            hh = g * (2 * n_pair) + 2 * p + h
            cq = jnp.sum(jnp.where(lane == hh, ccol, 0.0), axis=1, keepdims=True)
            top = jnp.where(lane // hd == h, q_pair, jnp.zeros_like(q_pair))
            blocks.append(jnp.concatenate([top, _query_ext(lane, hh, n_head, cq)], axis=1))
        qaug = jnp.concatenate(blocks, axis=0)
        s_c = _dot(qaug, jnp.concatenate([kt_ref[feat, :].astype(BF16), cextc_ref[...]], axis=0))
        kn_t = _dot_nt(eye, kn_ref[:, feat].astype(BF16)).astype(BF16)
        s_n = _dot(qaug, jnp.concatenate([kn_t, cextn_ref[...]], axis=0))
        return s_c, jnp.where(causal_new, s_n, MASK_VALUE)

    nxt = scores(0)
    for p in range(n_pair):
        s_c, s_n = nxt
        if p + 1 < n_pair:
            nxt = scores(p + 1)
        feat = slice(p * LANES, (p + 1) * LANES)
        m = jnp.maximum(jnp.max(s_c, axis=1, keepdims=True), jnp.max(s_n, axis=1, keepdims=True))
        p_c = jnp.exp2(s_c - m)
        p_n = jnp.exp2(s_n - m)
        l = jnp.sum(p_c, axis=1, keepdims=True) + jnp.sum(p_n, axis=1, keepdims=True)
        o = _dot_nt(p_c.astype(BF16), vt_ref[feat, :].astype(BF16)) + _dot(p_n.astype(BF16), vn_ref[:, feat].astype(BF16))
        o = o / l
        o_ref[:, feat] = jnp.where(lane < hd, o[0:tq], o[tq:2 * tq]).astype(o_ref.dtype)


def _fox_attn_sample(q, c_col, kt_cache, vt_cache, cext_cache, k_new, v_new, cext_new, n_pair, n_head):
    bsz, tq, hw = q.shape
    cache_len = kt_cache.shape[2]
    hd = hw // n_head
    assert 2 * hd == LANES and 4 * n_head <= LANES
    width = n_pair * LANES
    new_spec = pl.BlockSpec((None, tq, width), lambda b, g: (b, 0, g))
    cache_spec = pl.BlockSpec((None, width, cache_len), lambda b, g: (b, g, 0))
    kern = functools.partial(_fox_attn_sample_kernel, n_pair=n_pair, hd=hd, n_head=n_head)
    return pl.pallas_call(
        kern,
        grid=(bsz, hw // width),
        in_specs=[
            new_spec,
            pl.BlockSpec((None, tq, LANES), lambda b, g: (b, 0, 0)),
            cache_spec, cache_spec,
            pl.BlockSpec((None, LANES, cache_len), lambda b, g: (b, 0, 0)),
            new_spec, new_spec,
            pl.BlockSpec((None, LANES, tq), lambda b, g: (b, 0, 0)),
        ],
        out_specs=new_spec,
        out_shape=jax.ShapeDtypeStruct((bsz, tq, hw), BF16),
        compiler_params=_params(2),
        name="fox_attn_sample",
    )(q, c_col, kt_cache, vt_cache, cext_cache, k_new, v_new, cext_new)


def kernel(x_prompt, x_sample, state_gla, cache_fox_k, cache_fox_v, cache_fox_logf,
           norm_mix, gla_w_in, gla_w_g2, gla_b_g, gla_norm, gla_w_out,
           fox_w_in, fox_b_f, fox_w_out, norm_ffn, ffn_w_in, ffn_w_down, norm_final):
    d = x_prompt.shape[-1]
    depth = norm_mix.shape[0]
    groups = [x_prompt, x_sample]
    shapes = [x.shape for x in groups]
    xs = [x.reshape(-1, d) for x in groups]
    row_tiles = [min(512, x.shape[0]) for x in xs]
    w_ffn_in = ffn_w_in.astype(BF16)
    w_ffn_down = ffn_w_down.astype(BF16)

    gla_states = [[], []]
    fox_k, fox_v, fox_f = [[], []], [[], []], [[], []]
    for i in range(depth):
        j = i // 2
        g_mix = norm_mix[i].reshape(1, d)
        g_ffn = norm_ffn[i].reshape(1, d)
        last = i == depth - 1
        if i % 2 == 0:
            _, n_head, dk, dv = state_gla.shape[1:]
            hk, hv = n_head * dk, n_head * dv
            n_main = 2 * hk + 2 * hv
            w_in = gla_w_in[j]
            wt = w_in.T
            w_g2 = gla_w_g2[j].astype(BF16)
            b_g = gla_b_g[j].reshape(1, hk)
            w_out = gla_w_out[j].astype(BF16)
            norm_g = gla_norm[j].reshape(1, hv)
            s0s = [jnp.zeros((shapes[0][0], n_head, dk, dv), F32), state_gla[j]]
            for gi in range(2):
                bsz, t_len, _ = shapes[gi]
                proj, glog = _gla_proj(xs[gi], g_mix, wt, w_g2, b_g, row_tiles[gi])
                og, s_fin = _gla_mix(proj.reshape(bsz, t_len, n_main), glog.reshape(bsz, t_len, hk),
                                     s0s[gi], norm_g, min(t_len, 256), 2 if gi == 0 else min(4, bsz))
                gla_states[gi].append(s_fin)
                xs[gi] = _post(xs[gi], og.reshape(-1, hv), w_out, g_ffn, w_ffn_in, w_ffn_down, i,
                               norm_final.reshape(1, d), row_tiles[gi], last)
        else:
            n_head = fox_b_f.shape[1]
            hw = fox_w_out.shape[1]
            hd = hw // n_head
            wt = fox_w_in[j].T
            bf_col = fox_b_f[j].reshape(n_head, 1)
            w_out = fox_w_out[j].astype(BF16)
            for gi in range(2):
                bsz, t_len, _ = shapes[gi]
                x3 = xs[gi].reshape(bsz, t_len, d)
                if gi == 0:
                    qt, kaug, kt, vt, vtb, logf_t, ct = _fox_proj_prompt(x3, g_mix, wt, bf_col,
                                                                         min(512, t_len), n_head)
                    o = _fox_attn_prompt(qt, kaug, vtb, ct, 256, 8, n_head)
                    k_out = kt.reshape(bsz, n_head, hd, t_len).transpose(0, 3, 1, 2)
                    v_out = vt.reshape(bsz, n_head, hd, t_len).transpose(0, 3, 1, 2)
                else:
                    cache_len = cache_fox_logf.shape[2]
                    q, k, v, logf_t, c_col, cext_new, cext_cache = _fox_proj_sample(
                        x3, g_mix, wt, bf_col, jnp.transpose(cache_fox_logf[j], (0, 2, 1)), n_head,
                        min(8, bsz))
                    kt_cache = jnp.transpose(cache_fox_k[j], (0, 2, 3, 1)).reshape(bsz, hw, cache_len)
                    vt_cache = jnp.transpose(cache_fox_v[j], (0, 2, 3, 1)).reshape(bsz, hw, cache_len)
                    o = _fox_attn_sample(q, c_col, kt_cache, vt_cache, cext_cache, k, v, cext_new, 8, n_head)
                    k_out = k.reshape(bsz, t_len, n_head, hd)
                    v_out = v.reshape(bsz, t_len, n_head, hd)
                fox_k[gi].append(k_out)
                fox_v[gi].append(v_out)
                fox_f[gi].append(jnp.transpose(logf_t, (0, 2, 1)))
                xs[gi] = _post(xs[gi], o.reshape(-1, hw), w_out, g_ffn, w_ffn_in, w_ffn_down, i,
                               norm_final.reshape(1, d), row_tiles[gi], last)

    y_prompt = xs[0].reshape(shapes[0])
    y_sample = xs[1].reshape(shapes[1])
    st = lambda parts: jnp.stack(parts, axis=0)
    return (y_prompt, y_sample, st(gla_states[0]), st(fox_k[0]), st(fox_v[0]), st(fox_f[0]),
            st(gla_states[1]), st(fox_k[1]), st(fox_v[1]), st(fox_f[1]))
```

```python
import functools

import jax
import jax.numpy as jnp
from jax import lax
from jax.experimental import pallas as pl
from jax.experimental.pallas import tpu as pltpu

F32 = jnp.float32
BF16 = jnp.bfloat16

EPS = 1e-6
MASK_VALUE = -1e30
LOG2E = 1.4426950408889634

GLA_HEADS = 4
GLA_CHUNK = 64
GLA_GATE_TAU = 16.0

LANES = 128
VT_ROWS = LANES + 16
VMEM_LIMIT_BYTES = 56 * 1024 * 1024


def _params(n_grid):
    return pltpu.CompilerParams(
        dimension_semantics=("arbitrary",) * n_grid,
        vmem_limit_bytes=VMEM_LIMIT_BYTES,
    )


def _resident(shape):
    nd = len(shape)
    return pl.BlockSpec(shape, lambda *_: (0,) * nd, pipeline_mode=pl.Buffered(1))


def _dot(a, b):
    return jnp.dot(a, b, preferred_element_type=F32)


def _dot_nt(a, b):
    return lax.dot_general(a, b, (((1,), (1,)), ((), ())), preferred_element_type=F32)


def _dot_tn(a, b):
    return lax.dot_general(a, b, (((0,), (0,)), ((), ())), preferred_element_type=F32)


def _split3(x):
    hi = x.astype(BF16)
    r1 = x - hi.astype(F32)
    mid = r1.astype(BF16)
    lo = (r1 - mid.astype(F32)).astype(BF16)
    return hi, mid, lo


def _sum01(dot_fn, x, ones_first, mat01):
    acc = None
    for part in _split3(x):
        term = dot_fn(mat01, part) if ones_first else dot_fn(part, mat01)
        acc = term if acc is None else acc + term
    return acc


def _rmsnorm(x, g):
    var = jnp.mean(x * x, axis=-1, keepdims=True)
    return x * lax.rsqrt(var + EPS) * g


def _log_sigmoid(z):
    return jnp.minimum(z, 0.0) - jnp.log1p(jnp.exp(-jnp.abs(z)))


def _silu(z):
    return z * jax.nn.sigmoid(z)


def _identity(n):
    r = lax.broadcasted_iota(jnp.int32, (n, n), 0)
    c = lax.broadcasted_iota(jnp.int32, (n, n), 1)
    return (r == c).astype(BF16)


def _cast_weights_once(first_step, wt_ref, wtb_ref):
    @pl.when(first_step)
    def _():
        wtb_ref[...] = wt_ref[...].astype(BF16)


def _gla_proj_kernel(x_ref, g_ref, wt_ref, wg2_ref, bg_ref, proj_ref, glog_ref, wtb_ref, *, n_main):
    _cast_weights_once(pl.program_id(0) == 0, wt_ref, wtb_ref)
    rank = wg2_ref.shape[0]
    h = _rmsnorm(x_ref[...], g_ref[...]).astype(BF16)
    gl_t = _dot_nt(wtb_ref[n_main:n_main + rank, :], h).astype(BF16)
    z = _dot_tn(gl_t, wg2_ref[...]) + bg_ref[...]
    proj_ref[...] = _dot_nt(h, wtb_ref[0:n_main, :])
    glog_ref[...] = _log_sigmoid(z) / GLA_GATE_TAU


def _gla_proj(x2d, g, wt, w_g2, b_g, tm):
    n, d = x2d.shape
    rank, hk = w_g2.shape
    n_main = wt.shape[0] - rank
    return pl.pallas_call(
        functools.partial(_gla_proj_kernel, n_main=n_main),
        grid=(n // tm,),
        in_specs=[
            pl.BlockSpec((tm, d), lambda i: (i, 0)),
            _resident((1, d)),
            _resident(wt.shape),
            _resident(w_g2.shape),
            _resident((1, hk)),
        ],
        out_specs=[
            pl.BlockSpec((tm, n_main), lambda i: (i, 0)),
            pl.BlockSpec((tm, hk), lambda i: (i, 0)),
        ],
        out_shape=[
            jax.ShapeDtypeStruct((n, n_main), F32),
            jax.ShapeDtypeStruct((n, hk), F32),
        ],
        scratch_shapes=[pltpu.VMEM(wt.shape, BF16)],
        compiler_params=_params(1),
        name="gla_proj",
    )(x2d, g, wt, w_g2, b_g)


def _gla_mix_kernel(q_ref, k_ref, v_ref, r_ref, glog_ref, s0_ref, ng_ref,
                    og_ref, sfin_ref, s_ref, *, dk, dv):
    t = pl.program_id(1)
    nb, tb, _ = q_ref.shape
    n_chunk = tb // GLA_CHUNK
    seqs = range(nb)
    heads = range(GLA_HEADS)
    units = [(sb, h) for sb in seqs for h in heads]
    chunks = [slice(c * GLA_CHUNK, (c + 1) * GLA_CHUNK) for c in range(n_chunk)]
    ksl = [slice(h * dk, (h + 1) * dk) for h in heads]
    vsl = [slice(h * dv, (h + 1) * dv) for h in heads]

    @pl.when(t == 0)
    def _():
        s_ref[...] = s0_ref[...]

    row = lax.broadcasted_iota(jnp.int32, (tb, tb), 0)
    col = lax.broadcasted_iota(jnp.int32, (tb, tb), 1)
    same_chunk = (row // GLA_CHUNK) == (col // GLA_CHUNK)
    causal = same_chunk & (col <= row)
    cum_mat = causal.astype(BF16)

    b = [_sum01(_dot, glog_ref[sb], True, cum_mat) for sb in seqs]
    qe, ke, kd, dec_t = [], [], [], []
    for sb in seqs:
        tot_rows = [b[sb][(c + 1) * GLA_CHUNK - 1:(c + 1) * GLA_CHUNK, :] for c in range(n_chunk)]
        b_last = jnp.concatenate([jnp.broadcast_to(r, (GLA_CHUNK, r.shape[1])) for r in tot_rows], axis=0)
        pad_rows = [jnp.zeros_like(tot_rows[0])] * (8 - n_chunk % 8 if n_chunk % 8 else 0)
        dec_t.append(jnp.exp(_to_columns(jnp.concatenate(tot_rows + pad_rows, axis=0))))
        q = q_ref[sb]
        k = k_ref[sb]
        qe.append((q * jnp.exp(b[sb]) * (dk ** -0.5)).astype(BF16))
        ke.append((k * jnp.exp(-b[sb])).astype(BF16))
        kd.append((k * jnp.exp(b_last - b[sb])).astype(BF16))

    v_b = {u: v_ref[u[0], :, vsl[u[1]]].astype(BF16) for u in units}
    a_raw = {(sb, h): _dot_nt(qe[sb][:, ksl[h]], ke[sb][:, ksl[h]]) for sb, h in units}
    upd = {(sb, h): [_dot_tn(kd[sb][rs, ksl[h]], v_b[sb, h][rs]) for rs in chunks] for sb, h in units}
    o_intra = {u: _dot(jnp.where(causal, a_raw[u], 0.0).astype(BF16), v_b[u]) for u in units}
    s_in = {}
    for sb, h in units:
        s = s_ref[sb, h]
        s_in[sb, h] = []
        for c in range(n_chunk):
            s_in[sb, h].append(s.astype(BF16))
            s = s * dec_t[sb][ksl[h], c:c + 1] + upd[sb, h][c]
        s_ref[sb, h] = s
    for sb, h in units:
        o_parts = [o_intra[sb, h][rs] + _dot(qe[sb][rs, ksl[h]], s_in[sb, h][c]) for c, rs in enumerate(chunks)]
        o = o_parts[0] if n_chunk == 1 else jnp.concatenate(o_parts, axis=0)
        on = _rmsnorm(o, ng_ref[:, vsl[h]])
        og_ref[sb, :, vsl[h]] = (on * _silu(r_ref[sb, :, vsl[h]])).astype(BF16)

    @pl.when(t == pl.num_programs(1) - 1)
    def _():
        sfin_ref[...] = s_ref[...]


def _gla_mix(proj3, glog3, s0, norm_g, tb, nb):
    bsz, t_len, _ = proj3.shape
    _, n_head, dk, dv = s0.shape
    hk, hv = n_head * dk, n_head * dv
    assert t_len % tb == 0 and tb % GLA_CHUNK == 0 and hv == 2 * hk and bsz % nb == 0
    kern = functools.partial(_gla_mix_kernel, dk=dk, dv=dv)
    state_spec = pl.BlockSpec((nb, n_head, dk, dv), lambda b, t: (b, 0, 0, 0))
    return pl.pallas_call(
        kern,
        grid=(bsz // nb, t_len // tb),
        in_specs=[
            pl.BlockSpec((nb, tb, hk), lambda b, t: (b, t, 0)),
            pl.BlockSpec((nb, tb, hk), lambda b, t: (b, t, 1)),
            pl.BlockSpec((nb, tb, hv), lambda b, t: (b, t, 1)),
            pl.BlockSpec((nb, tb, hv), lambda b, t: (b, t, 2)),
            pl.BlockSpec((nb, tb, hk), lambda b, t: (b, t, 0)),
            state_spec,
            _resident((1, hv)),
        ],
        out_specs=[
            pl.BlockSpec((nb, tb, hv), lambda b, t: (b, t, 0)),
            state_spec,
        ],
        out_shape=[
            jax.ShapeDtypeStruct((bsz, t_len, hv), BF16),
            jax.ShapeDtypeStruct(s0.shape, F32),
        ],
        scratch_shapes=[pltpu.VMEM((nb, n_head, dk, dv), F32)],
        compiler_params=_params(2),
        name="gla_mix",
    )(proj3, proj3, proj3, proj3, glog3, s0, norm_g)


def _post_kernel(x_ref, o_ref, wo_ref, g_ref, win_ref, wdown_ref, gfin_ref, y_ref, act_ref,
                 *, d_ff, ff_tile, final_norm):
    x1 = x_ref[...] + _dot(o_ref[...], wo_ref[...])
    h = _rmsnorm(x1, g_ref[...]).astype(BF16)
    for j in range(d_ff // ff_tile):
        gate = _dot(h, win_ref[:, j * ff_tile:(j + 1) * ff_tile])
        up = _dot(h, win_ref[:, d_ff + j * ff_tile:d_ff + (j + 1) * ff_tile])
        act_ref[:, j * ff_tile:(j + 1) * ff_tile] = (_silu(gate) * up).astype(BF16)
    y = x1 + _dot(act_ref[...], wdown_ref[...])
    if final_norm:
        y = _rmsnorm(y, gfin_ref[...])
    y_ref[...] = y


def _post(x2d, o2d, w_out, g_ffn, w_in_all, w_down_all, layer, g_final, tm, final_norm):
    n, d = x2d.shape
    d_ff = w_down_all.shape[1]
    ff_tile = 256
    assert d_ff % ff_tile == 0 and n % tm == 0
    kern = functools.partial(_post_kernel, d_ff=d_ff, ff_tile=ff_tile, final_norm=final_norm)
    return pl.pallas_call(
        kern,
        grid=(n // tm,),
        in_specs=[
            pl.BlockSpec((tm, d), lambda i: (i, 0)),
            pl.BlockSpec((tm, o2d.shape[1]), lambda i: (i, 0)),
            _resident(w_out.shape),
            _resident((1, d)),
            pl.BlockSpec((None,) + w_in_all.shape[1:], lambda i: (layer, 0, 0), pipeline_mode=pl.Buffered(1)),
            pl.BlockSpec((None,) + w_down_all.shape[1:], lambda i: (layer, 0, 0), pipeline_mode=pl.Buffered(1)),
            _resident((1, d)),
        ],
        out_specs=pl.BlockSpec((tm, d), lambda i: (i, 0)),
        out_shape=jax.ShapeDtypeStruct((n, d), F32),
        scratch_shapes=[pltpu.VMEM((tm, d_ff), BF16)],
        compiler_params=_params(1),
        name="post_final" if final_norm else "post",
    )(x2d, o2d, w_out, g_ffn, w_in_all, w_down_all, g_final)


def _lane_cumsum(x, tile):
    r = lax.broadcasted_iota(jnp.int32, (tile, tile), 0)
    c = lax.broadcasted_iota(jnp.int32, (tile, tile), 1)
    upper = (r <= c).astype(BF16)
    rows = x.shape[0]
    carry = jnp.zeros((rows, 1), F32)
    out = []
    for j in range(x.shape[1] // tile):
        parts = jnp.concatenate(_split3(x[:, j * tile:(j + 1) * tile]), axis=0)
        s = _dot(parts, upper)
        blk = carry + s[0:rows] + s[rows:2 * rows] + s[2 * rows:3 * rows]
        out.append(blk)
        carry = blk[:, tile - 1:tile]
    return out[0] if len(out) == 1 else jnp.concatenate(out, axis=1)


def _to_columns(x_t):
    rows = x_t.shape[0]
    parts = jnp.concatenate(_split3(x_t), axis=0)
    r = lax.broadcasted_iota(jnp.int32, (3 * rows, LANES), 0)
    c = lax.broadcasted_iota(jnp.int32, (3 * rows, LANES), 1)
    place = ((r % rows) == c).astype(BF16)
    return _dot_tn(parts, place)


def _key_ext_cols(c_cols, n_head):
    hi, mid, lo = (p.astype(F32) for p in _split3(c_cols * LOG2E))
    lane = lax.broadcasted_iota(jnp.int32, c_cols.shape, 1)
    ones = ((lane >= 3 * n_head) & (lane < 3 * n_head + 3)).astype(F32)
    ext = hi + pltpu.roll(mid, n_head, 1) + pltpu.roll(lo, 2 * n_head, 1) + ones
    return ext.astype(BF16)


def _key_ext_rows(c_rows):
    n_head, n = c_rows.shape
    hi, mid, lo = _split3(c_rows * LOG2E)
    r = lax.broadcasted_iota(jnp.int32, (n_head, n), 0)
    ones = (r < 3).astype(BF16)
    zeros = jnp.zeros((LANES - 4 * n_head, n), BF16)
    return jnp.concatenate([hi, mid, lo, ones, zeros], axis=0)


def _query_ext(idx, hh, n_head, cq):
    cq_hi, cq_mid, cq_lo = (p.astype(F32) for p in _split3(cq * LOG2E))
    ext = jnp.where((idx == hh) | (idx == n_head + hh) | (idx == 2 * n_head + hh), -1.0, 0.0)
    ext = jnp.where(idx == 3 * n_head, cq_hi, ext)
    ext = jnp.where(idx == 3 * n_head + 1, cq_mid, ext)
    ext = jnp.where(idx == 3 * n_head + 2, cq_lo, ext)
    return ext.astype(BF16)


def _fox_proj_prompt_kernel(x_ref, g_ref, wtf_ref, bfc_ref,
                            qaug_ref, kaug_ref, kt_ref, vt_ref, vtb_ref, logft_ref, carry_ref, wt_ref,
                            *, hw, n_head, scale, tq):
    t = pl.program_id(1)
    tm = x_ref.shape[0]
    _cast_weights_once((pl.program_id(0) == 0) & (t == 0), wtf_ref, wt_ref)

    @pl.when(t == 0)
    def _():
        carry_ref[...] = jnp.zeros_like(carry_ref)

    h = _rmsnorm(x_ref[...], g_ref[...]).astype(BF16)
    logf_t = _log_sigmoid(_dot_nt(wt_ref[3 * hw:3 * hw + n_head, :], h) + bfc_ref[...])
    logft_ref[...] = logf_t
    ct_blk = _lane_cumsum(logf_t, min(tm, 256)) + carry_ref[:, 0:1]
    carry_ref[...] = jnp.broadcast_to(ct_blk[:, tm - 1:tm], carry_ref.shape)
    cext = _key_ext_cols(_to_columns(ct_blk), n_head)
    kt = _dot_nt(wt_ref[hw:2 * hw, :], h)
    vt = _dot_nt(wt_ref[2 * hw:3 * hw, :], h)
    qt = (_dot_nt(wt_ref[0:hw, :], h) * scale).astype(BF16)
    kt_ref[...] = kt
    hd = hw // n_head
    ext_rows = 4 * n_head
    rr = lax.broadcasted_iota(jnp.int32, (ext_rows, tq), 0)
    rq = lax.broadcasted_iota(jnp.int32, (LANES, tq), 0) // hd
    zeros = jnp.zeros((LANES - ext_rows, tq), BF16)
    for p in range(hw // LANES):
        for jq in range(tm // tq):
            qs = slice(jq * tq, (jq + 1) * tq)
            q_pair = qt[p * LANES:(p + 1) * LANES, qs]
            cols = []
            for hh in (2 * p, 2 * p + 1):
                top = jnp.where(rq == hh - 2 * p, q_pair, jnp.zeros_like(q_pair))
                cols.append(jnp.concatenate([top, _query_ext(rr, hh, n_head, ct_blk[hh:hh + 1, qs]), zeros], axis=0))
            qaug_ref[p, jq] = jnp.concatenate(cols, axis=1)
    k = kt.T
    for p in range(hw // LANES):
        kaug_ref[p, :, 0:LANES] = k[:, p * LANES:(p + 1) * LANES].astype(BF16)
        kaug_ref[p, :, LANES:2 * LANES] = cext
    vt_ref[...] = vt
    ones = jnp.ones((VT_ROWS - LANES, tm), BF16)
    for p in range(hw // LANES):
        vtb_ref[p, 0:LANES, :] = vt[p * LANES:(p + 1) * LANES, :].astype(BF16)
        vtb_ref[p, LANES:VT_ROWS, :] = ones


def _fox_proj_prompt(x3, g, wt, bf_col, tm, tq, n_head):
    bsz, t_len, d = x3.shape
    hw = (wt.shape[0] - n_head) // 3
    n_pair = hw // LANES
    assert t_len % tm == 0 and tm % tq == 0 and tq % LANES == 0 and 3 * n_head + 3 <= LANES
    kern = functools.partial(_fox_proj_prompt_kernel, hw=hw, n_head=n_head,
                             scale=(hw // n_head) ** -0.5 * LOG2E, tq=tq)
    feat = pl.BlockSpec((None, hw, tm), lambda b, t: (b, 0, t))
    head = pl.BlockSpec((None, n_head, tm), lambda b, t: (b, 0, t))
    return pl.pallas_call(
        kern,
        grid=(bsz, t_len // tm),
        in_specs=[
            pl.BlockSpec((None, tm, d), lambda b, t: (b, t, 0)),
            _resident((1, d)),
            _resident(wt.shape),
            _resident(bf_col.shape),
        ],
        out_specs=[
            pl.BlockSpec((None, n_pair, tm // tq, 2 * LANES, 2 * tq), lambda b, t: (b, 0, t, 0, 0)),
            pl.BlockSpec((None, n_pair, tm, 2 * LANES), lambda b, t: (b, 0, t, 0)),
            feat, feat,
            pl.BlockSpec((None, n_pair, VT_ROWS, tm), lambda b, t: (b, 0, 0, t)),
            head,
        ],
        out_shape=[
            jax.ShapeDtypeStruct((bsz, n_pair, t_len // tq, 2 * LANES, 2 * tq), BF16),
            jax.ShapeDtypeStruct((bsz, n_pair, t_len, 2 * LANES), BF16),
            jax.ShapeDtypeStruct((bsz, hw, t_len), F32),
            jax.ShapeDtypeStruct((bsz, hw, t_len), F32),
            jax.ShapeDtypeStruct((bsz, n_pair, VT_ROWS, t_len), BF16),
            jax.ShapeDtypeStruct((bsz, n_head, t_len), F32),
        ],
        scratch_shapes=[pltpu.VMEM((n_head, LANES), F32), pltpu.VMEM(wt.shape, BF16)],
        compiler_params=_params(2),
        name="fox_proj_prompt",
    )(x3, g, wt, bf_col)


def _fox_proj_sample_kernel(x_ref, g_ref, wtf_ref, bfc_ref, lct_ref,
                            q_ref, k_ref, v_ref, logft_ref, ccol_ref, cextn_ref, cextc_ref, wt_ref,
                            *, hw, n_head, scale):
    _cast_weights_once(pl.program_id(0) == 0, wtf_ref, wt_ref)
    nb, tm, d = x_ref.shape
    cache_len = lct_ref.shape[2]
    h = _rmsnorm(x_ref[...].reshape(nb * tm, d), g_ref[...]).astype(BF16)
    q_ref[...] = (_dot_nt(h, wt_ref[0:hw, :]) * scale).astype(BF16).reshape(nb, tm, hw)
    k_ref[...] = _dot_nt(h, wt_ref[hw:2 * hw, :]).reshape(nb, tm, hw)
    v_ref[...] = _dot_nt(h, wt_ref[2 * hw:3 * hw, :]).reshape(nb, tm, hw)

    ct_cache = _lane_cumsum(lct_ref[...].reshape(nb * n_head, cache_len), min(cache_len, 256))
    logf_all = _log_sigmoid(_dot_nt(wt_ref[3 * hw:3 * hw + n_head, :], h) + bfc_ref[...])
    for b in range(nb):
        ct_b = ct_cache[b * n_head:(b + 1) * n_head, :]
        cextc_ref[b] = _key_ext_rows(ct_b)
        logf_t = logf_all[:, b * tm:(b + 1) * tm]
        logft_ref[b] = logf_t
        ct_new = _lane_cumsum(logf_t, tm) + ct_b[:, cache_len - 1:cache_len]
        cextn_ref[b] = _key_ext_rows(ct_new)
        ccol_ref[b] = _to_columns(ct_new)


def _fox_proj_sample(x3, g, wt, bf_col, logf_cache_t, n_head, nb):
    bsz, tm, d = x3.shape
    hw = (wt.shape[0] - n_head) // 3
    cache_len = logf_cache_t.shape[2]
    assert 4 * n_head <= LANES and bsz % nb == 0
    kern = functools.partial(_fox_proj_sample_kernel, hw=hw, n_head=n_head,
                             scale=(hw // n_head) ** -0.5 * LOG2E)
    tok = pl.BlockSpec((nb, tm, hw), lambda b: (b, 0, 0))
    return pl.pallas_call(
        kern,
        grid=(bsz // nb,),
        in_specs=[
            pl.BlockSpec((nb, tm, d), lambda b: (b, 0, 0)),
            _resident((1, d)),
            _resident(wt.shape),
            _resident(bf_col.shape),
            pl.BlockSpec((nb, n_head, cache_len), lambda b: (b, 0, 0)),
        ],
        out_specs=[
            tok, tok, tok,
            pl.BlockSpec((nb, n_head, tm), lambda b: (b, 0, 0)),
            pl.BlockSpec((nb, tm, LANES), lambda b: (b, 0, 0)),
            pl.BlockSpec((nb, LANES, tm), lambda b: (b, 0, 0)),
            pl.BlockSpec((nb, LANES, cache_len), lambda b: (b, 0, 0)),
        ],
        out_shape=[
            jax.ShapeDtypeStruct((bsz, tm, hw), BF16),
            jax.ShapeDtypeStruct((bsz, tm, hw), F32),
            jax.ShapeDtypeStruct((bsz, tm, hw), F32),
            jax.ShapeDtypeStruct((bsz, n_head, tm), F32),
            jax.ShapeDtypeStruct((bsz, tm, LANES), F32),
            jax.ShapeDtypeStruct((bsz, LANES, tm), BF16),
            jax.ShapeDtypeStruct((bsz, LANES, cache_len), BF16),
        ],
        scratch_shapes=[pltpu.VMEM(wt.shape, BF16)],
        compiler_params=_params(1),
        name="fox_proj_sample",
    )(x3, g, wt, bf_col, logf_cache_t)


def _attn_update(n_pair, kaug_of, vt_of, vt_prev_last, mask, next_kaug0,
                 qaug_ref, m_ref, acc_ref, s0_ref, plast_ref, alast_ref):
    last = n_pair - 1
    s_next = s0_ref[...]
    acc_ref[last] = acc_ref[last] * alast_ref[...] + _dot(vt_prev_last(), plast_ref[...])
    pending = None
    for p in range(n_pair):
        s = s_next
        if p + 1 < n_pair:
            s_next = _dot(kaug_of(p + 1), qaug_ref[p + 1])
        elif next_kaug0 is not None:
            s0_ref[...] = _dot(next_kaug0(), qaug_ref[0])
        if mask is not None:
            s = jnp.where(mask, s, MASK_VALUE)
        m_old = m_ref[p]
        m_new = jnp.maximum(m_old, jnp.max(s, axis=0, keepdims=True))
        alpha = jnp.exp2(m_old - m_new)
        pr = jnp.exp2(s - m_new).astype(BF16)
        m_ref[p] = m_new
        if pending is not None:
            q, pr_q, alpha_q = pending
            acc_ref[q] = acc_ref[q] * alpha_q + _dot(vt_of(q), pr_q)
        pending = (p, pr, alpha)
    plast_ref[...] = pending[1]
    alast_ref[...] = pending[2]


def _attn_finish(o_ref, acc_ref, *, n_pair, tq, hd):
    for p in range(n_pair):
        full = acc_ref[p, 0:LANES, :] * (1.0 / acc_ref[p, LANES:LANES + 1, :])
        z = jnp.concatenate([full[0:hd, 0:tq], full[hd:2 * hd, tq:2 * tq]], axis=0)
        o_ref[:, p * LANES:(p + 1) * LANES] = z.T.astype(o_ref.dtype)


def _fox_attn_prompt_kernel(qaug_ref, kaug_ref, vt_ref, o_ref,
                            m_ref, acc_ref, s0_ref, plast_ref, alast_ref, *, tq, n_pair, hd):
    i = pl.program_id(2)
    last = n_pair - 1
    state = (qaug_ref, m_ref, acc_ref, s0_ref, plast_ref, alast_ref)

    s0_ref[...] = _dot(kaug_ref[0, 0:tq, :], qaug_ref[0])
    m_ref[...] = jnp.full_like(m_ref, MASK_VALUE)
    acc_ref[...] = jnp.zeros_like(acc_ref)
    plast_ref[...] = jnp.zeros_like(plast_ref)
    alast_ref[...] = jnp.ones_like(alast_ref)

    def keys(j):
        return pl.ds(pl.multiple_of(j * tq, tq), tq)

    def vt_of(p, ks):
        return vt_ref[p, :, ks]

    def kv_block(j, mask, has_next):
        ks = keys(j)
        _attn_update(n_pair, lambda p: kaug_ref[p, ks, :], lambda p: vt_of(p, ks),
                     lambda: vt_of(last, keys(jnp.maximum(j - 1, 0))), mask,
                     (lambda: kaug_ref[0, keys(j + 1), :]) if has_next else None, *state)

    def body(j, carry):
        kv_block(j, None, True)
        return carry

    lax.fori_loop(0, i, body, 0)
    row = lax.broadcasted_iota(jnp.int32, (tq, 2 * tq), 0)
    col = lax.broadcasted_iota(jnp.int32, (tq, 2 * tq), 1) % tq
    kv_block(i, row <= col, False)
    acc_ref[last] = acc_ref[last] * alast_ref[...] + _dot(vt_of(last, keys(i)), plast_ref[...])
    _attn_finish(o_ref, acc_ref, n_pair=n_pair, tq=tq, hd=hd)


def _fox_attn_prompt(qaug, kaug, vtb, n_pair, n_head):
    bsz, n_pair_all, n_qblk, _, tq2 = qaug.shape
    tq = tq2 // 2
    t_len = n_qblk * tq
    hw = n_pair_all * LANES
    hd = hw // n_head
    assert 2 * hd == LANES and n_pair_all % n_pair == 0
    width = n_pair * LANES
    n_group = n_pair_all // n_pair
    kern = functools.partial(_fox_attn_prompt_kernel, tq=tq, n_pair=n_pair, hd=hd)
    return pl.pallas_call(
        kern,
        grid=(bsz, n_group, n_qblk),
        in_specs=[
            pl.BlockSpec((None, n_pair, None, 2 * LANES, 2 * tq), lambda b, g, i: (b, g, i, 0, 0)),
            pl.BlockSpec((None, n_pair, t_len, 2 * LANES), lambda b, g, i: (b, g, 0, 0)),
            pl.BlockSpec((None, n_pair, VT_ROWS, t_len), lambda b, g, i: (b, g, 0, 0)),
        ],
        out_specs=pl.BlockSpec((None, tq, width), lambda b, g, i: (b, i, g)),
        out_shape=jax.ShapeDtypeStruct((bsz, t_len, hw), BF16),
        scratch_shapes=[
            pltpu.VMEM((n_pair, 1, 2 * tq), F32),
            pltpu.VMEM((n_pair, VT_ROWS, 2 * tq), F32),
            pltpu.VMEM((tq, 2 * tq), F32),
            pltpu.VMEM((tq, 2 * tq), BF16),
            pltpu.VMEM((1, 2 * tq), F32),
        ],
        compiler_params=_params(3),
        name="fox_attn_prompt",
    )(qaug, kaug, vtb)


def _fox_attn_sample_kernel(q_ref, ccol_ref, kt_ref, vt_ref, cextc_ref, kn_ref, vn_ref, cextn_ref, o_ref,
                            *, n_pair, hd, n_head):
    g = pl.program_id(1)
    tq = q_ref.shape[0]
    eye = _identity(LANES)
    lane = lax.broadcasted_iota(jnp.int32, (tq, LANES), 1)
    row = lax.broadcasted_iota(jnp.int32, (2 * tq, tq), 0) % tq
    col = lax.broadcasted_iota(jnp.int32, (2 * tq, tq), 1)
    causal_new = col <= row
    ccol = ccol_ref[...]

    def scores(p):
        feat = slice(p * LANES, (p + 1) * LANES)
        q_pair = q_ref[:, feat]
        blocks = []
        for h in range(2):
            hh = g * (2 * n_pair) + 2 * p + h
            cq = jnp.sum(jnp.where(lane == hh, ccol, 0.0), axis=1, keepdims=True)
            top = jnp.where(lane // hd == h, q_pair, jnp.zeros_like(q_pair))
            blocks.append(jnp.concatenate([top, _query_ext(lane, hh, n_head, cq)], axis=1))
        qaug = jnp.concatenate(blocks, axis=0)
        s_c = _dot(qaug, jnp.concatenate([kt_ref[feat, :].astype(BF16), cextc_ref[...]], axis=0))
        kn_t = _dot_nt(eye, kn_ref[:, feat].astype(BF16)).astype(BF16)
        s_n = _dot(qaug, jnp.concatenate([kn_t, cextn_ref[...]], axis=0))
        return s_c, jnp.where(causal_new, s_n, MASK_VALUE)

    nxt = scores(0)
    for p in range(n_pair):
        s_c, s_n = nxt
        if p + 1 < n_pair:
            nxt = scores(p + 1)
        feat = slice(p * LANES, (p + 1) * LANES)
        m = jnp.maximum(jnp.max(s_c, axis=1, keepdims=True), jnp.max(s_n, axis=1, keepdims=True))
        p_c = jnp.exp2(s_c - m)
        p_n = jnp.exp2(s_n - m)
        l = jnp.sum(p_c, axis=1, keepdims=True) + jnp.sum(p_n, axis=1, keepdims=True)
        o = _dot_nt(p_c.astype(BF16), vt_ref[feat, :].astype(BF16)) + _dot(p_n.astype(BF16), vn_ref[:, feat].astype(BF16))
        o = o / l
        o_ref[:, feat] = jnp.where(lane < hd, o[0:tq], o[tq:2 * tq]).astype(o_ref.dtype)


def _fox_attn_sample(q, c_col, kt_cache, vt_cache, cext_cache, k_new, v_new, cext_new, n_pair, n_head):
    bsz, tq, hw = q.shape
    cache_len = kt_cache.shape[2]
    hd = hw // n_head
    assert 2 * hd == LANES and 4 * n_head <= LANES
    width = n_pair * LANES
    new_spec = pl.BlockSpec((None, tq, width), lambda b, g: (b, 0, g))
    cache_spec = pl.BlockSpec((None, width, cache_len), lambda b, g: (b, g, 0))
    kern = functools.partial(_fox_attn_sample_kernel, n_pair=n_pair, hd=hd, n_head=n_head)
    return pl.pallas_call(
        kern,
        grid=(bsz, hw // width),
        in_specs=[
            new_spec,
            pl.BlockSpec((None, tq, LANES), lambda b, g: (b, 0, 0)),
            cache_spec, cache_spec,
            pl.BlockSpec((None, LANES, cache_len), lambda b, g: (b, 0, 0)),
            new_spec, new_spec,
            pl.BlockSpec((None, LANES, tq), lambda b, g: (b, 0, 0)),
        ],
        out_specs=new_spec,
        out_shape=jax.ShapeDtypeStruct((bsz, tq, hw), BF16),
        compiler_params=_params(2),
        name="fox_attn_sample",
    )(q, c_col, kt_cache, vt_cache, cext_cache, k_new, v_new, cext_new)


def kernel(x_prompt, x_sample, state_gla, cache_fox_k, cache_fox_v, cache_fox_logf,
           norm_mix, gla_w_in, gla_w_g2, gla_b_g, gla_norm, gla_w_out,
           fox_w_in, fox_b_f, fox_w_out, norm_ffn, ffn_w_in, ffn_w_down, norm_final):
    d = x_prompt.shape[-1]
    depth = norm_mix.shape[0]
    groups = [x_prompt, x_sample]
    shapes = [x.shape for x in groups]
    xs = [x.reshape(-1, d) for x in groups]
    row_tiles = [min(512, x.shape[0]) for x in xs]
    w_ffn_in = ffn_w_in.astype(BF16)
    w_ffn_down = ffn_w_down.astype(BF16)

    gla_states = [[], []]
    fox_k, fox_v, fox_f = [[], []], [[], []], [[], []]
    for i in range(depth):
        j = i // 2
        g_mix = norm_mix[i].reshape(1, d)
        g_ffn = norm_ffn[i].reshape(1, d)
        last = i == depth - 1
        if i % 2 == 0:
            _, n_head, dk, dv = state_gla.shape[1:]
            hk, hv = n_head * dk, n_head * dv
            n_main = 2 * hk + 2 * hv
            w_in = gla_w_in[j]
            wt = w_in.T
            w_g2 = gla_w_g2[j].astype(BF16)
            b_g = gla_b_g[j].reshape(1, hk)
            w_out = gla_w_out[j].astype(BF16)
            norm_g = gla_norm[j].reshape(1, hv)
            s0s = [jnp.zeros((shapes[0][0], n_head, dk, dv), F32), state_gla[j]]
            for gi in range(2):
                bsz, t_len, _ = shapes[gi]
                proj, glog = _gla_proj(xs[gi], g_mix, wt, w_g2, b_g, row_tiles[gi])
                og, s_fin = _gla_mix(proj.reshape(bsz, t_len, n_main), glog.reshape(bsz, t_len, hk),
                                     s0s[gi], norm_g, min(t_len, 256), 2 if gi == 0 else min(4, bsz))
                gla_states[gi].append(s_fin)
                xs[gi] = _post(xs[gi], og.reshape(-1, hv), w_out, g_ffn, w_ffn_in, w_ffn_down, i,
                               norm_final.reshape(1, d), row_tiles[gi], last)
        else:
            n_head = fox_b_f.shape[1]
            hw = fox_w_out.shape[1]
            hd = hw // n_head
            wt = fox_w_in[j].T
            bf_col = fox_b_f[j].reshape(n_head, 1)
            w_out = fox_w_out[j].astype(BF16)
            for gi in range(2):
                bsz, t_len, _ = shapes[gi]
                x3 = xs[gi].reshape(bsz, t_len, d)
                if gi == 0:
                    qaug, kaug, kt, vt, vtb, logf_t = _fox_proj_prompt(x3, g_mix, wt, bf_col,
                                                                       min(512, t_len), 256, n_head)
                    o = _fox_attn_prompt(qaug, kaug, vtb, 8, n_head)
                    k_out = kt.reshape(bsz, n_head, hd, t_len).transpose(0, 3, 1, 2)
                    v_out = vt.reshape(bsz, n_head, hd, t_len).transpose(0, 3, 1, 2)
                else:
                    cache_len = cache_fox_logf.shape[2]
                    q, k, v, logf_t, c_col, cext_new, cext_cache = _fox_proj_sample(
                        x3, g_mix, wt, bf_col, jnp.transpose(cache_fox_logf[j], (0, 2, 1)), n_head,
                        min(8, bsz))
                    kt_cache = jnp.transpose(cache_fox_k[j], (0, 2, 3, 1)).reshape(bsz, hw, cache_len)
                    vt_cache = jnp.transpose(cache_fox_v[j], (0, 2, 3, 1)).reshape(bsz, hw, cache_len)
                    o = _fox_attn_sample(q, c_col, kt_cache, vt_cache, cext_cache, k, v, cext_new, 8, n_head)
                    k_out = k.reshape(bsz, t_len, n_head, hd)
                    v_out = v.reshape(bsz, t_len, n_head, hd)
                fox_k[gi].append(k_out)
                fox_v[gi].append(v_out)
                fox_f[gi].append(jnp.transpose(logf_t, (0, 2, 1)))
                xs[gi] = _post(xs[gi], o.reshape(-1, hw), w_out, g_ffn, w_ffn_in, w_ffn_down, i,
                               norm_final.reshape(1, d), row_tiles[gi], last)

    y_prompt = xs[0].reshape(shapes[0])
    y_sample = xs[1].reshape(shapes[1])
    st = lambda parts: jnp.stack(parts, axis=0)
    return (y_prompt, y_sample, st(gla_states[0]), st(fox_k[0]), st(fox_v[0]), st(fox_f[0]),
            st(gla_states[1]), st(fox_k[1]), st(fox_v[1]), st(fox_f[1]))
```

```python
import functools

import jax
import jax.numpy as jnp
from jax import lax
from jax.experimental import pallas as pl
from jax.experimental.pallas import tpu as pltpu

F32 = jnp.float32
BF16 = jnp.bfloat16

EPS = 1e-6
MASK_VALUE = -1e30
LOG2E = 1.4426950408889634

GLA_HEADS = 4
GLA_CHUNK = 64
GLA_GATE_TAU = 16.0

LANES = 128
VT_ROWS = LANES + 16
VMEM_LIMIT_BYTES = 56 * 1024 * 1024


def _params(n_grid):
    return pltpu.CompilerParams(
        dimension_semantics=("arbitrary",) * n_grid,
        vmem_limit_bytes=VMEM_LIMIT_BYTES,
    )


def _resident(shape):
    nd = len(shape)
    return pl.BlockSpec(shape, lambda *_: (0,) * nd, pipeline_mode=pl.Buffered(1))


def _dot(a, b):
    return jnp.dot(a, b, preferred_element_type=F32)


def _dot_nt(a, b):
    return lax.dot_general(a, b, (((1,), (1,)), ((), ())), preferred_element_type=F32)


def _dot_tn(a, b):
    return lax.dot_general(a, b, (((0,), (0,)), ((), ())), preferred_element_type=F32)


def _split3(x):
    hi = x.astype(BF16)
    r1 = x - hi.astype(F32)
    mid = r1.astype(BF16)
    lo = (r1 - mid.astype(F32)).astype(BF16)
    return hi, mid, lo


def _sum01(dot_fn, x, ones_first, mat01):
    acc = None
    for part in _split3(x):
        term = dot_fn(mat01, part) if ones_first else dot_fn(part, mat01)
        acc = term if acc is None else acc + term
    return acc


def _rmsnorm(x, g):
    var = jnp.mean(x * x, axis=-1, keepdims=True)
    return x * lax.rsqrt(var + EPS) * g


def _log_sigmoid(z):
    return jnp.minimum(z, 0.0) - jnp.log1p(jnp.exp(-jnp.abs(z)))


def _silu(z):
    return z * jax.nn.sigmoid(z)


def _identity(n):
    r = lax.broadcasted_iota(jnp.int32, (n, n), 0)
    c = lax.broadcasted_iota(jnp.int32, (n, n), 1)
    return (r == c).astype(BF16)


def _cast_weights_once(first_step, wt_ref, wtb_ref):
    @pl.when(first_step)
    def _():
        wtb_ref[...] = wt_ref[...].astype(BF16)


def _gla_proj_kernel(x_ref, g_ref, wt_ref, wg2_ref, bg_ref, proj_ref, glog_ref, wtb_ref, *, n_main):
    _cast_weights_once(pl.program_id(0) == 0, wt_ref, wtb_ref)
    rank = wg2_ref.shape[0]
    h = _rmsnorm(x_ref[...], g_ref[...]).astype(BF16)
    proj_ref[...] = _dot_nt(h, wtb_ref[0:n_main, :])
    gl_t = _dot_nt(wtb_ref[n_main:n_main + rank, :], h).astype(BF16)
    z = _dot_tn(gl_t, wg2_ref[...]) + bg_ref[...]
    glog_ref[...] = _log_sigmoid(z) / GLA_GATE_TAU


def _gla_proj(x2d, g, wt, w_g2, b_g, tm):
    n, d = x2d.shape
    rank, hk = w_g2.shape
    n_main = wt.shape[0] - rank
    return pl.pallas_call(
        functools.partial(_gla_proj_kernel, n_main=n_main),
        grid=(n // tm,),
        in_specs=[
            pl.BlockSpec((tm, d), lambda i: (i, 0)),
            _resident((1, d)),
            _resident(wt.shape),
            _resident(w_g2.shape),
            _resident((1, hk)),
        ],
        out_specs=[
            pl.BlockSpec((tm, n_main), lambda i: (i, 0)),
            pl.BlockSpec((tm, hk), lambda i: (i, 0)),
        ],
        out_shape=[
            jax.ShapeDtypeStruct((n, n_main), F32),
            jax.ShapeDtypeStruct((n, hk), F32),
        ],
        scratch_shapes=[pltpu.VMEM(wt.shape, BF16)],
        compiler_params=_params(1),
        name="gla_proj",
    )(x2d, g, wt, w_g2, b_g)


def _gla_mix_kernel(q_ref, k_ref, v_ref, r_ref, glog_ref, s0_ref, ng_ref,
                    og_ref, sfin_ref, s_ref, *, dk, dv):
    t = pl.program_id(1)
    nb, tb, _ = q_ref.shape
    n_chunk = tb // GLA_CHUNK
    seqs = range(nb)
    heads = range(GLA_HEADS)
    units = [(sb, h) for sb in seqs for h in heads]
    chunks = [slice(c * GLA_CHUNK, (c + 1) * GLA_CHUNK) for c in range(n_chunk)]
    ksl = [slice(h * dk, (h + 1) * dk) for h in heads]
    vsl = [slice(h * dv, (h + 1) * dv) for h in heads]

    @pl.when(t == 0)
    def _():
        s_ref[...] = s0_ref[...]

    row = lax.broadcasted_iota(jnp.int32, (tb, tb), 0)
    col = lax.broadcasted_iota(jnp.int32, (tb, tb), 1)
    same_chunk = (row // GLA_CHUNK) == (col // GLA_CHUNK)
    causal = same_chunk & (col <= row)
    cum_mat = causal.astype(BF16)

    b = [_sum01(_dot, glog_ref[sb], True, cum_mat) for sb in seqs]
    qe, ke, kd, dec_t = [], [], [], []
    for sb in seqs:
        tot_rows = [b[sb][(c + 1) * GLA_CHUNK - 1:(c + 1) * GLA_CHUNK, :] for c in range(n_chunk)]
        b_last = jnp.concatenate([jnp.broadcast_to(r, (GLA_CHUNK, r.shape[1])) for r in tot_rows], axis=0)
        pad_rows = [jnp.zeros_like(tot_rows[0])] * (8 - n_chunk % 8 if n_chunk % 8 else 0)
        dec_t.append(jnp.exp(_to_columns(jnp.concatenate(tot_rows + pad_rows, axis=0))))
        q = q_ref[sb]
        k = k_ref[sb]
        qe.append((q * jnp.exp(b[sb]) * (dk ** -0.5)).astype(BF16))
        ke.append((k * jnp.exp(-b[sb])).astype(BF16))
        kd.append((k * jnp.exp(b_last - b[sb])).astype(BF16))

    v_b = {u: v_ref[u[0], :, vsl[u[1]]].astype(BF16) for u in units}
    a_raw = {(sb, h): _dot_nt(qe[sb][:, ksl[h]], ke[sb][:, ksl[h]]) for sb, h in units}
    upd = {(sb, h): [_dot_tn(kd[sb][rs, ksl[h]], v_b[sb, h][rs]) for rs in chunks] for sb, h in units}
    o_intra = {u: _dot(jnp.where(causal, a_raw[u], 0.0).astype(BF16), v_b[u]) for u in units}
    s_in = {}
    for sb, h in units:
        s = s_ref[sb, h]
        s_in[sb, h] = []
        for c in range(n_chunk):
            s_in[sb, h].append(s.astype(BF16))
            s = s * dec_t[sb][ksl[h], c:c + 1] + upd[sb, h][c]
        s_ref[sb, h] = s
    for sb, h in units:
        o_parts = [o_intra[sb, h][rs] + _dot(qe[sb][rs, ksl[h]], s_in[sb, h][c]) for c, rs in enumerate(chunks)]
        o = o_parts[0] if n_chunk == 1 else jnp.concatenate(o_parts, axis=0)
        on = _rmsnorm(o, ng_ref[:, vsl[h]])
        og_ref[sb, :, vsl[h]] = (on * _silu(r_ref[sb, :, vsl[h]])).astype(BF16)

    @pl.when(t == pl.num_programs(1) - 1)
    def _():
        sfin_ref[...] = s_ref[...]


def _gla_mix(proj3, glog3, s0, norm_g, tb, nb):
    bsz, t_len, _ = proj3.shape
    _, n_head, dk, dv = s0.shape
    hk, hv = n_head * dk, n_head * dv
    assert t_len % tb == 0 and tb % GLA_CHUNK == 0 and hv == 2 * hk and bsz % nb == 0
    kern = functools.partial(_gla_mix_kernel, dk=dk, dv=dv)
    state_spec = pl.BlockSpec((nb, n_head, dk, dv), lambda b, t: (b, 0, 0, 0))
    return pl.pallas_call(
        kern,
        grid=(bsz // nb, t_len // tb),
        in_specs=[
            pl.BlockSpec((nb, tb, hk), lambda b, t: (b, t, 0)),
            pl.BlockSpec((nb, tb, hk), lambda b, t: (b, t, 1)),
            pl.BlockSpec((nb, tb, hv), lambda b, t: (b, t, 1)),
            pl.BlockSpec((nb, tb, hv), lambda b, t: (b, t, 2)),
            pl.BlockSpec((nb, tb, hk), lambda b, t: (b, t, 0)),
            state_spec,
            _resident((1, hv)),
        ],
        out_specs=[
            pl.BlockSpec((nb, tb, hv), lambda b, t: (b, t, 0)),
            state_spec,
        ],
        out_shape=[
            jax.ShapeDtypeStruct((bsz, t_len, hv), BF16),
            jax.ShapeDtypeStruct(s0.shape, F32),
        ],
        scratch_shapes=[pltpu.VMEM((nb, n_head, dk, dv), F32)],
        compiler_params=_params(2),
        name="gla_mix",
    )(proj3, proj3, proj3, proj3, glog3, s0, norm_g)


def _post_kernel(x_ref, o_ref, wo_ref, g_ref, win_ref, wdown_ref, gfin_ref, y_ref, act_ref,
                 *, d_ff, ff_tile, final_norm, hooks=None):
    x1 = x_ref[...] + _dot(o_ref[...], wo_ref[...])
    h = _rmsnorm(x1, g_ref[...]).astype(BF16)
    for j in range(d_ff // ff_tile):
        gate = _dot(h, win_ref[:, j * ff_tile:(j + 1) * ff_tile])
        up = _dot(h, win_ref[:, d_ff + j * ff_tile:d_ff + (j + 1) * ff_tile])
        act_ref[:, j * ff_tile:(j + 1) * ff_tile] = (_silu(gate) * up).astype(BF16)
        if hooks and j in hooks:
            hooks[j]()
    y = x1 + _dot(act_ref[...], wdown_ref[...])
    if final_norm:
        y = _rmsnorm(y, gfin_ref[...])
    y_ref[...] = y


def _post(x2d, o2d, w_out, g_ffn, w_in_all, w_down_all, layer, g_final, tm, final_norm):
    n, d = x2d.shape
    d_ff = w_down_all.shape[1]
    ff_tile = 256
    assert d_ff % ff_tile == 0 and n % tm == 0
    kern = functools.partial(_post_kernel, d_ff=d_ff, ff_tile=ff_tile, final_norm=final_norm)
    return pl.pallas_call(
        kern,
        grid=(n // tm,),
        in_specs=[
            pl.BlockSpec((tm, d), lambda i: (i, 0)),
            pl.BlockSpec((tm, o2d.shape[1]), lambda i: (i, 0)),
            _resident(w_out.shape),
            _resident((1, d)),
            pl.BlockSpec((None,) + w_in_all.shape[1:], lambda i: (layer, 0, 0), pipeline_mode=pl.Buffered(1)),
            pl.BlockSpec((None,) + w_down_all.shape[1:], lambda i: (layer, 0, 0), pipeline_mode=pl.Buffered(1)),
            _resident((1, d)),
        ],
        out_specs=pl.BlockSpec((tm, d), lambda i: (i, 0)),
        out_shape=jax.ShapeDtypeStruct((n, d), F32),
        scratch_shapes=[pltpu.VMEM((tm, d_ff), BF16)],
        compiler_params=_params(1),
        name="post_final" if final_norm else "post",
    )(x2d, o2d, w_out, g_ffn, w_in_all, w_down_all, g_final)


def _lane_cumsum(x, tile):
    r = lax.broadcasted_iota(jnp.int32, (tile, tile), 0)
    c = lax.broadcasted_iota(jnp.int32, (tile, tile), 1)
    upper = (r <= c).astype(BF16)
    rows = x.shape[0]
    carry = jnp.zeros((rows, 1), F32)
    out = []
    for j in range(x.shape[1] // tile):
        parts = jnp.concatenate(_split3(x[:, j * tile:(j + 1) * tile]), axis=0)
        s = _dot(parts, upper)
        blk = carry + s[0:rows] + s[rows:2 * rows] + s[2 * rows:3 * rows]
        out.append(blk)
        carry = blk[:, tile - 1:tile]
    return out[0] if len(out) == 1 else jnp.concatenate(out, axis=1)


def _to_columns(x_t):
    rows = x_t.shape[0]
    parts = jnp.concatenate(_split3(x_t), axis=0)
    r = lax.broadcasted_iota(jnp.int32, (3 * rows, LANES), 0)
    c = lax.broadcasted_iota(jnp.int32, (3 * rows, LANES), 1)
    place = ((r % rows) == c).astype(BF16)
    return _dot_tn(parts, place)


def _key_ext_cols(c_cols, n_head):
    hi, mid, lo = (p.astype(F32) for p in _split3(c_cols * LOG2E))
    lane = lax.broadcasted_iota(jnp.int32, c_cols.shape, 1)
    ones = ((lane >= 3 * n_head) & (lane < 3 * n_head + 3)).astype(F32)
    ext = hi + pltpu.roll(mid, n_head, 1) + pltpu.roll(lo, 2 * n_head, 1) + ones
    return ext.astype(BF16)


def _key_ext_rows(c_rows):
    n_head, n = c_rows.shape
    hi, mid, lo = _split3(c_rows * LOG2E)
    r = lax.broadcasted_iota(jnp.int32, (n_head, n), 0)
    ones = (r < 3).astype(BF16)
    zeros = jnp.zeros((LANES - 4 * n_head, n), BF16)
    return jnp.concatenate([hi, mid, lo, ones, zeros], axis=0)


def _query_ext(idx, hh, n_head, cq):
    cq_hi, cq_mid, cq_lo = (p.astype(F32) for p in _split3(cq * LOG2E))
    ext = jnp.where((idx == hh) | (idx == n_head + hh) | (idx == 2 * n_head + hh), -1.0, 0.0)
    ext = jnp.where(idx == 3 * n_head, cq_hi, ext)
    ext = jnp.where(idx == 3 * n_head + 1, cq_mid, ext)
    ext = jnp.where(idx == 3 * n_head + 2, cq_lo, ext)
    return ext.astype(BF16)


def _fox_proj_prompt_kernel(x_ref, g_ref, wtf_ref, bfc_ref,
                            qaug_ref, kaug_ref, kt_ref, vt_ref, vtb_ref, logft_ref, carry_ref, wt_ref,
                            *, hw, n_head, scale, tq):
    t = pl.program_id(1)
    tm = x_ref.shape[0]
    _cast_weights_once((pl.program_id(0) == 0) & (t == 0), wtf_ref, wt_ref)

    @pl.when(t == 0)
    def _():
        carry_ref[...] = jnp.zeros_like(carry_ref)

    h = _rmsnorm(x_ref[...], g_ref[...]).astype(BF16)
    logf_t = _log_sigmoid(_dot_nt(wt_ref[3 * hw:3 * hw + n_head, :], h) + bfc_ref[...])
    logft_ref[...] = logf_t
    ct_blk = _lane_cumsum(logf_t, min(tm, 256)) + carry_ref[:, 0:1]
    carry_ref[...] = jnp.broadcast_to(ct_blk[:, tm - 1:tm], carry_ref.shape)
    cext = _key_ext_cols(_to_columns(ct_blk), n_head)
    kt = _dot_nt(wt_ref[hw:2 * hw, :], h)
    vt = _dot_nt(wt_ref[2 * hw:3 * hw, :], h)
    qt = (_dot_nt(wt_ref[0:hw, :], h) * scale).astype(BF16)
    kt_ref[...] = kt
    hd = hw // n_head
    ext_rows = 4 * n_head
    rr = lax.broadcasted_iota(jnp.int32, (ext_rows, tq), 0)
    rq = lax.broadcasted_iota(jnp.int32, (LANES, tq), 0) // hd
    zeros = jnp.zeros((LANES - ext_rows, tq), BF16)
    for p in range(hw // LANES):
        for jq in range(tm // tq):
            qs = slice(jq * tq, (jq + 1) * tq)
            q_pair = qt[p * LANES:(p + 1) * LANES, qs]
            cols = []
            for hh in (2 * p, 2 * p + 1):
                top = jnp.where(rq == hh - 2 * p, q_pair, jnp.zeros_like(q_pair))
                cols.append(jnp.concatenate([top, _query_ext(rr, hh, n_head, ct_blk[hh:hh + 1, qs]), zeros], axis=0))
            qaug_ref[p, jq] = jnp.concatenate(cols, axis=1)
    k = kt.T
    for p in range(hw // LANES):
        kaug_ref[p, :, 0:LANES] = k[:, p * LANES:(p + 1) * LANES].astype(BF16)
        kaug_ref[p, :, LANES:2 * LANES] = cext
    vt_ref[...] = vt
    ones = jnp.ones((VT_ROWS - LANES, tm), BF16)
    for p in range(hw // LANES):
        vtb_ref[p, 0:LANES, :] = vt[p * LANES:(p + 1) * LANES, :].astype(BF16)
        vtb_ref[p, LANES:VT_ROWS, :] = ones


def _fox_proj_prompt(x3, g, wt, bf_col, tm, tq, n_head):
    bsz, t_len, d = x3.shape
    hw = (wt.shape[0] - n_head) // 3
    n_pair = hw // LANES
    assert t_len % tm == 0 and tm % tq == 0 and tq % LANES == 0 and 3 * n_head + 3 <= LANES
    kern = functools.partial(_fox_proj_prompt_kernel, hw=hw, n_head=n_head,
                             scale=(hw // n_head) ** -0.5 * LOG2E, tq=tq)
    feat = pl.BlockSpec((None, hw, tm), lambda b, t: (b, 0, t))
    head = pl.BlockSpec((None, n_head, tm), lambda b, t: (b, 0, t))
    return pl.pallas_call(
        kern,
        grid=(bsz, t_len // tm),
        in_specs=[
            pl.BlockSpec((None, tm, d), lambda b, t: (b, t, 0)),
            _resident((1, d)),
            _resident(wt.shape),
            _resident(bf_col.shape),
        ],
        out_specs=[
            pl.BlockSpec((None, n_pair, tm // tq, 2 * LANES, 2 * tq), lambda b, t: (b, 0, t, 0, 0)),
            pl.BlockSpec((None, n_pair, tm, 2 * LANES), lambda b, t: (b, 0, t, 0)),
            feat, feat,
            pl.BlockSpec((None, n_pair, VT_ROWS, tm), lambda b, t: (b, 0, 0, t)),
            head,
        ],
        out_shape=[
            jax.ShapeDtypeStruct((bsz, n_pair, t_len // tq, 2 * LANES, 2 * tq), BF16),
            jax.ShapeDtypeStruct((bsz, n_pair, t_len, 2 * LANES), BF16),
            jax.ShapeDtypeStruct((bsz, hw, t_len), F32),
            jax.ShapeDtypeStruct((bsz, hw, t_len), F32),
            jax.ShapeDtypeStruct((bsz, n_pair, VT_ROWS, t_len), BF16),
            jax.ShapeDtypeStruct((bsz, n_head, t_len), F32),
        ],
        scratch_shapes=[pltpu.VMEM((n_head, LANES), F32), pltpu.VMEM(wt.shape, BF16)],
        compiler_params=_params(2),
        name="fox_proj_prompt",
    )(x3, g, wt, bf_col)


def _fox_proj_sample_kernel(x_ref, g_ref, wtf_ref, bfc_ref, lct_ref,
                            q_ref, k_ref, v_ref, logft_ref, ccol_ref, cextn_ref, cextc_ref, wt_ref,
                            *, hw, n_head, scale):
    _cast_weights_once(pl.program_id(0) == 0, wtf_ref, wt_ref)
    nb, tm, d = x_ref.shape
    cache_len = lct_ref.shape[2]
    h = _rmsnorm(x_ref[...].reshape(nb * tm, d), g_ref[...]).astype(BF16)
    q_ref[...] = (_dot_nt(h, wt_ref[0:hw, :]) * scale).astype(BF16).reshape(nb, tm, hw)
    k_ref[...] = _dot_nt(h, wt_ref[hw:2 * hw, :]).reshape(nb, tm, hw)
    v_ref[...] = _dot_nt(h, wt_ref[2 * hw:3 * hw, :]).reshape(nb, tm, hw)

    ct_cache = _lane_cumsum(lct_ref[...].reshape(nb * n_head, cache_len), min(cache_len, 256))
    logf_all = _log_sigmoid(_dot_nt(wt_ref[3 * hw:3 * hw + n_head, :], h) + bfc_ref[...])
    for b in range(nb):
        ct_b = ct_cache[b * n_head:(b + 1) * n_head, :]
        cextc_ref[b] = _key_ext_rows(ct_b)
        logf_t = logf_all[:, b * tm:(b + 1) * tm]
        logft_ref[b] = logf_t
        ct_new = _lane_cumsum(logf_t, tm) + ct_b[:, cache_len - 1:cache_len]
        cextn_ref[b] = _key_ext_rows(ct_new)
        ccol_ref[b] = _to_columns(ct_new)


def _fox_proj_sample(x3, g, wt, bf_col, logf_cache_t, n_head, nb):
    bsz, tm, d = x3.shape
    hw = (wt.shape[0] - n_head) // 3
    cache_len = logf_cache_t.shape[2]
    assert 4 * n_head <= LANES and bsz % nb == 0
    kern = functools.partial(_fox_proj_sample_kernel, hw=hw, n_head=n_head,
                             scale=(hw // n_head) ** -0.5 * LOG2E)
    tok = pl.BlockSpec((nb, tm, hw), lambda b: (b, 0, 0))
    return pl.pallas_call(
        kern,
        grid=(bsz // nb,),
        in_specs=[
            pl.BlockSpec((nb, tm, d), lambda b: (b, 0, 0)),
            _resident((1, d)),
            _resident(wt.shape),
            _resident(bf_col.shape),
            pl.BlockSpec((nb, n_head, cache_len), lambda b: (b, 0, 0)),
        ],
        out_specs=[
            tok, tok, tok,
            pl.BlockSpec((nb, n_head, tm), lambda b: (b, 0, 0)),
            pl.BlockSpec((nb, tm, LANES), lambda b: (b, 0, 0)),
            pl.BlockSpec((nb, LANES, tm), lambda b: (b, 0, 0)),
            pl.BlockSpec((nb, LANES, cache_len), lambda b: (b, 0, 0)),
        ],
        out_shape=[
            jax.ShapeDtypeStruct((bsz, tm, hw), BF16),
            jax.ShapeDtypeStruct((bsz, tm, hw), F32),
            jax.ShapeDtypeStruct((bsz, tm, hw), F32),
            jax.ShapeDtypeStruct((bsz, n_head, tm), F32),
            jax.ShapeDtypeStruct((bsz, tm, LANES), F32),
            jax.ShapeDtypeStruct((bsz, LANES, tm), BF16),
            jax.ShapeDtypeStruct((bsz, LANES, cache_len), BF16),
        ],
        scratch_shapes=[pltpu.VMEM(wt.shape, BF16)],
        compiler_params=_params(1),
        name="fox_proj_sample",
    )(x3, g, wt, bf_col, logf_cache_t)


def _attn_update(n_pair, kaug_of, vt_of, vt_prev_last, mask, next_kaug0,
                 qaug_ref, m_ref, acc_ref, s0_ref, plast_ref, alast_ref):
    last = n_pair - 1
    s_next = s0_ref[...]
    acc_ref[last] = acc_ref[last] * alast_ref[...] + _dot(vt_prev_last(), plast_ref[...])
    pending = None
    for p in range(n_pair):
        s = s_next
        if p + 1 < n_pair:
            s_next = _dot(kaug_of(p + 1), qaug_ref[p + 1])
        elif next_kaug0 is not None:
            s0_ref[...] = _dot(next_kaug0(), qaug_ref[0])
        if mask is not None:
            s = jnp.where(mask, s, MASK_VALUE)
        m_old = m_ref[p]
        m_new = jnp.maximum(m_old, jnp.max(s, axis=0, keepdims=True))
        alpha = jnp.exp2(m_old - m_new)
        pr = jnp.exp2(s - m_new).astype(BF16)
        m_ref[p] = m_new
        if pending is not None:
            q, pr_q, alpha_q = pending
            acc_ref[q] = acc_ref[q] * alpha_q + _dot(vt_of(q), pr_q)
        pending = (p, pr, alpha)
    plast_ref[...] = pending[1]
    alast_ref[...] = pending[2]


def _attn_finish(o_ref, acc_ref, *, n_pair, tq, hd):
    for p in range(n_pair):
        full = acc_ref[p, 0:LANES, :] * (1.0 / acc_ref[p, LANES:LANES + 1, :])
        z = jnp.concatenate([full[0:hd, 0:tq], full[hd:2 * hd, tq:2 * tq]], axis=0)
        o_ref[:, p * LANES:(p + 1) * LANES] = z.T.astype(o_ref.dtype)


def _fox_attn_prompt_kernel(qaug_ref, kaug_ref, vt_ref, o_ref,
                            m_ref, acc_ref, s0_ref, plast_ref, alast_ref, *, tq, n_pair, hd):
    i = pl.program_id(2)
    last = n_pair - 1
    state = (qaug_ref, m_ref, acc_ref, s0_ref, plast_ref, alast_ref)

    s0_ref[...] = _dot(kaug_ref[0, 0:tq, :], qaug_ref[0])
    m_ref[...] = jnp.full_like(m_ref, MASK_VALUE)
    acc_ref[...] = jnp.zeros_like(acc_ref)
    plast_ref[...] = jnp.zeros_like(plast_ref)
    alast_ref[...] = jnp.ones_like(alast_ref)

    def keys(j):
        return pl.ds(pl.multiple_of(j * tq, tq), tq)

    def vt_of(p, ks):
        return vt_ref[p, :, ks]

    def kv_block(j, mask, has_next):
        ks = keys(j)
        _attn_update(n_pair, lambda p: kaug_ref[p, ks, :], lambda p: vt_of(p, ks),
                     lambda: vt_of(last, keys(jnp.maximum(j - 1, 0))), mask,
                     (lambda: kaug_ref[0, keys(j + 1), :]) if has_next else None, *state)

    def body(j, carry):
        kv_block(j, None, True)
        return carry

    lax.fori_loop(0, i, body, 0)
    row = lax.broadcasted_iota(jnp.int32, (tq, 2 * tq), 0)
    col = lax.broadcasted_iota(jnp.int32, (tq, 2 * tq), 1) % tq
    kv_block(i, row <= col, False)
    acc_ref[last] = acc_ref[last] * alast_ref[...] + _dot(vt_of(last, keys(i)), plast_ref[...])
    _attn_finish(o_ref, acc_ref, n_pair=n_pair, tq=tq, hd=hd)


def _fox_attn_prompt(qaug, kaug, vtb, n_pair, n_head):
    bsz, n_pair_all, n_qblk, _, tq2 = qaug.shape
    tq = tq2 // 2
    t_len = n_qblk * tq
    hw = n_pair_all * LANES
    hd = hw // n_head
    assert 2 * hd == LANES and n_pair_all % n_pair == 0
    width = n_pair * LANES
    n_group = n_pair_all // n_pair
    kern = functools.partial(_fox_attn_prompt_kernel, tq=tq, n_pair=n_pair, hd=hd)
    return pl.pallas_call(
        kern,
        grid=(bsz, n_group, n_qblk),
        in_specs=[
            pl.BlockSpec((None, n_pair, None, 2 * LANES, 2 * tq), lambda b, g, i: (b, g, i, 0, 0)),
            pl.BlockSpec((None, n_pair, t_len, 2 * LANES), lambda b, g, i: (b, g, 0, 0)),
            pl.BlockSpec((None, n_pair, VT_ROWS, t_len), lambda b, g, i: (b, g, 0, 0)),
        ],
        out_specs=pl.BlockSpec((None, tq, width), lambda b, g, i: (b, i, g)),
        out_shape=jax.ShapeDtypeStruct((bsz, t_len, hw), BF16),
        scratch_shapes=[
            pltpu.VMEM((n_pair, 1, 2 * tq), F32),
            pltpu.VMEM((n_pair, VT_ROWS, 2 * tq), F32),
            pltpu.VMEM((tq, 2 * tq), F32),
            pltpu.VMEM((tq, 2 * tq), BF16),
            pltpu.VMEM((1, 2 * tq), F32),
        ],
        compiler_params=_params(3),
        name="fox_attn_prompt",
    )(qaug, kaug, vtb)


def _sample_attn_stages(q_ref, ccol_ref, kt_ref, vt_ref, cextc_ref, kn_ref, vn_ref, cextn_ref, o_ref,
                        *, g, n_pair, hd, n_head):
    tq = q_ref.shape[0]
    eye = _identity(LANES)
    lane = lax.broadcasted_iota(jnp.int32, (tq, LANES), 1)
    row = lax.broadcasted_iota(jnp.int32, (2 * tq, tq), 0) % tq
    col = lax.broadcasted_iota(jnp.int32, (2 * tq, tq), 1)
    causal_new = col <= row
    ccol = ccol_ref[...]

    def scores(p):
        feat = slice(p * LANES, (p + 1) * LANES)
        q_pair = q_ref[:, feat]
        blocks = []
        for h in range(2):
            hh = g * (2 * n_pair) + 2 * p + h
            cq = jnp.sum(jnp.where(lane == hh, ccol, 0.0), axis=1, keepdims=True)
            top = jnp.where(lane // hd == h, q_pair, jnp.zeros_like(q_pair))
            blocks.append(jnp.concatenate([top, _query_ext(lane, hh, n_head, cq)], axis=1))
        qaug = jnp.concatenate(blocks, axis=0)
        s_c = _dot(qaug, jnp.concatenate([kt_ref[feat, :].astype(BF16), cextc_ref[...]], axis=0))
        kn_t = _dot_nt(eye, kn_ref[:, feat].astype(BF16)).astype(BF16)
        s_n = _dot(qaug, jnp.concatenate([kn_t, cextn_ref[...]], axis=0))
        return s_c, jnp.where(causal_new, s_n, MASK_VALUE)

    held = {}

    def score_stage(p):
        held["s", p] = scores(p)

    def softmax_stage(p):
        s_c, s_n = held.pop(("s", p))
        m = jnp.maximum(jnp.max(s_c, axis=1, keepdims=True), jnp.max(s_n, axis=1, keepdims=True))
        p_c = jnp.exp2(s_c - m)
        p_n = jnp.exp2(s_n - m)
        l = jnp.sum(p_c, axis=1, keepdims=True) + jnp.sum(p_n, axis=1, keepdims=True)
        held["p", p] = (p_c.astype(BF16), p_n.astype(BF16), l)

    def value_stage(p):
        p_c, p_n, l = held.pop(("p", p))
        feat = slice(p * LANES, (p + 1) * LANES)
        o = _dot_nt(p_c, vt_ref[feat, :].astype(BF16)) + _dot(p_n, vn_ref[:, feat].astype(BF16))
        o = o / l
        o_ref[:, feat] = jnp.where(lane < hd, o[0:tq], o[tq:2 * tq]).astype(o_ref.dtype)

    return score_stage, softmax_stage, value_stage


def _fox_attn_sample_kernel(*refs, n_pair, hd, n_head):
    score, softmax, value = _sample_attn_stages(*refs, g=pl.program_id(1), n_pair=n_pair, hd=hd, n_head=n_head)
    score(0)
    for p in range(n_pair):
        if p + 1 < n_pair:
            score(p + 1)
        softmax(p)
        value(p)


def _post_attn_kernel(x_ref, o_ref, wo_ref, g_ref, win_ref, wdown_ref, gfin_ref,
                      q_ref, ccol_ref, kt_ref, vt_ref, cextc_ref, kn_ref, vn_ref, cextn_ref,
                      y_ref, os_ref, act_ref, *, d_ff, ff_tile, final_norm, n_group, n_pair, hd, n_head):
    score, softmax, value = _sample_attn_stages(
        q_ref, ccol_ref, kt_ref, vt_ref, cextc_ref, kn_ref, vn_ref, cextn_ref, os_ref,
        g=pl.program_id(0) % n_group, n_pair=n_pair, hd=hd, n_head=n_head)
    n_ff = d_ff // ff_tile
    stride = max(1, (n_ff - 1) // (n_pair + 1))
    hooks = {}

    def at(j, fn):
        j = min(j, n_ff - 1)
        prev = hooks.get(j)
        hooks[j] = fn if prev is None else (lambda: (prev(), fn()))

    for p in range(n_pair):
        at(p * stride, functools.partial(score, p))
        at(p * stride, functools.partial(softmax, p))
        at((p + 2) * stride, functools.partial(value, p))
    _post_kernel(x_ref, o_ref, wo_ref, g_ref, win_ref, wdown_ref, gfin_ref, y_ref, act_ref,
                 d_ff=d_ff, ff_tile=ff_tile, final_norm=final_norm, hooks=hooks)


def _fox_attn_sample(q, c_col, kt_cache, vt_cache, cext_cache, k_new, v_new, cext_new, n_pair, n_head):
    bsz, tq, hw = q.shape
    cache_len = kt_cache.shape[2]
    hd = hw // n_head
    assert 2 * hd == LANES and 4 * n_head <= LANES
    width = n_pair * LANES
    new_spec = pl.BlockSpec((None, tq, width), lambda b, g: (b, 0, g))
    cache_spec = pl.BlockSpec((None, width, cache_len), lambda b, g: (b, g, 0))
    kern = functools.partial(_fox_attn_sample_kernel, n_pair=n_pair, hd=hd, n_head=n_head)
    return pl.pallas_call(
        kern,
        grid=(bsz, hw // width),
        in_specs=[
            new_spec,
            pl.BlockSpec((None, tq, LANES), lambda b, g: (b, 0, 0)),
            cache_spec, cache_spec,
            pl.BlockSpec((None, LANES, cache_len), lambda b, g: (b, 0, 0)),
            new_spec, new_spec,
            pl.BlockSpec((None, LANES, tq), lambda b, g: (b, 0, 0)),
        ],
        out_specs=new_spec,
        out_shape=jax.ShapeDtypeStruct((bsz, tq, hw), BF16),
        compiler_params=_params(2),
        name="fox_attn_sample",
    )(q, c_col, kt_cache, vt_cache, cext_cache, k_new, v_new, cext_new)


def _post_attn(x2d, o2d, w_out, g_ffn, w_in_all, w_down_all, layer, g_final, final_norm,
               q, c_col, kt_cache, vt_cache, cext_cache, k_new, v_new, cext_new, n_pair, n_head):
    n, d = x2d.shape
    bsz, tq, hw = q.shape
    cache_len = kt_cache.shape[2]
    hd = hw // n_head
    width = n_pair * LANES
    n_group = hw // width
    steps = bsz * n_group
    tm = n // steps
    d_ff = w_down_all.shape[1]
    ff_tile = 256
    assert n % steps == 0 and tm % 16 == 0 and d_ff % ff_tile == 0 and 2 * hd == LANES
    kern = functools.partial(_post_attn_kernel, d_ff=d_ff, ff_tile=ff_tile, final_norm=final_norm,
                             n_group=n_group, n_pair=n_pair, hd=hd, n_head=n_head)
    new_spec = pl.BlockSpec((None, tq, width), lambda i: (i // n_group, 0, i % n_group))
    cache_spec = pl.BlockSpec((None, width, cache_len), lambda i: (i // n_group, i % n_group, 0))
    return pl.pallas_call(
        kern,
        grid=(steps,),
        in_specs=[
            pl.BlockSpec((tm, d), lambda i: (i, 0)),
            pl.BlockSpec((tm, o2d.shape[1]), lambda i: (i, 0)),
            _resident(w_out.shape),
            _resident((1, d)),
            pl.BlockSpec((None,) + w_in_all.shape[1:], lambda i: (layer, 0, 0), pipeline_mode=pl.Buffered(1)),
            pl.BlockSpec((None,) + w_down_all.shape[1:], lambda i: (layer, 0, 0), pipeline_mode=pl.Buffered(1)),
            _resident((1, d)),
            new_spec,
            pl.BlockSpec((None, tq, LANES), lambda i: (i // n_group, 0, 0)),
            cache_spec, cache_spec,
            pl.BlockSpec((None, LANES, cache_len), lambda i: (i // n_group, 0, 0)),
            new_spec, new_spec,
            pl.BlockSpec((None, LANES, tq), lambda i: (i // n_group, 0, 0)),
        ],
        out_specs=[pl.BlockSpec((tm, d), lambda i: (i, 0)), new_spec],
        out_shape=[jax.ShapeDtypeStruct((n, d), F32), jax.ShapeDtypeStruct((bsz, tq, hw), BF16)],
        scratch_shapes=[pltpu.VMEM((tm, d_ff), BF16)],
        compiler_params=_params(1),
        name="post_attn",
    )(x2d, o2d, w_out, g_ffn, w_in_all, w_down_all, g_final,
      q, c_col, kt_cache, vt_cache, cext_cache, k_new, v_new, cext_new)


def kernel(x_prompt, x_sample, state_gla, cache_fox_k, cache_fox_v, cache_fox_logf,
           norm_mix, gla_w_in, gla_w_g2, gla_b_g, gla_norm, gla_w_out,
           fox_w_in, fox_b_f, fox_w_out, norm_ffn, ffn_w_in, ffn_w_down, norm_final):
    d = x_prompt.shape[-1]
    depth = norm_mix.shape[0]
    groups = [x_prompt, x_sample]
    shapes = [x.shape for x in groups]
    xs = [x.reshape(-1, d) for x in groups]
    row_tiles = [min(512, x.shape[0]) for x in xs]
    w_ffn_in = ffn_w_in.astype(BF16)
    w_ffn_down = ffn_w_down.astype(BF16)

    gla_states = [[], []]
    fox_k, fox_v, fox_f = [[], []], [[], []], [[], []]
    for i in range(depth):
        j = i // 2
        g_mix = norm_mix[i].reshape(1, d)
        g_ffn = norm_ffn[i].reshape(1, d)
        last = i == depth - 1
        if i % 2 == 0:
            _, n_head, dk, dv = state_gla.shape[1:]
            hk, hv = n_head * dk, n_head * dv
            n_main = 2 * hk + 2 * hv
            w_in = gla_w_in[j]
            wt = w_in.T
            w_g2 = gla_w_g2[j].astype(BF16)
            b_g = gla_b_g[j].reshape(1, hk)
            w_out = gla_w_out[j].astype(BF16)
            norm_g = gla_norm[j].reshape(1, hv)
            s0s = [jnp.zeros((shapes[0][0], n_head, dk, dv), F32), state_gla[j]]
            for gi in range(2):
                bsz, t_len, _ = shapes[gi]
                proj, glog = _gla_proj(xs[gi], g_mix, wt, w_g2, b_g, row_tiles[gi])
                og, s_fin = _gla_mix(proj.reshape(bsz, t_len, n_main), glog.reshape(bsz, t_len, hk),
                                     s0s[gi], norm_g, min(t_len, 256), 2 if gi == 0 else min(4, bsz))
                gla_states[gi].append(s_fin)
                xs[gi] = _post(xs[gi], og.reshape(-1, hv), w_out, g_ffn, w_ffn_in, w_ffn_down, i,
                               norm_final.reshape(1, d), row_tiles[gi], last)
        else:
            n_head = fox_b_f.shape[1]
            hw = fox_w_out.shape[1]
            hd = hw // n_head
            wt = fox_w_in[j].T
            bf_col = fox_b_f[j].reshape(n_head, 1)
            w_out = fox_w_out[j].astype(BF16)
            g_fin = norm_final.reshape(1, d)
            bsz, t_len, _ = shapes[0]
            qaug, kaug, kt, vt, vtb, logf_p = _fox_proj_prompt(xs[0].reshape(bsz, t_len, d), g_mix, wt, bf_col,
                                                               min(512, t_len), 256, n_head)
            o_p = _fox_attn_prompt(qaug, kaug, vtb, 8, n_head)
            fox_k[0].append(kt.reshape(bsz, n_head, hd, t_len).transpose(0, 3, 1, 2))
            fox_v[0].append(vt.reshape(bsz, n_head, hd, t_len).transpose(0, 3, 1, 2))
            fox_f[0].append(jnp.transpose(logf_p, (0, 2, 1)))
            bsz, t_len, _ = shapes[1]
            cache_len = cache_fox_logf.shape[2]
            q, k, v, logf_s, c_col, cext_new, cext_cache = _fox_proj_sample(
                xs[1].reshape(bsz, t_len, d), g_mix, wt, bf_col,
                jnp.transpose(cache_fox_logf[j], (0, 2, 1)), n_head, min(8, bsz))
            kt_cache = jnp.transpose(cache_fox_k[j], (0, 2, 3, 1)).reshape(bsz, hw, cache_len)
            vt_cache = jnp.transpose(cache_fox_v[j], (0, 2, 3, 1)).reshape(bsz, hw, cache_len)
            attn_args = (q, c_col, kt_cache, vt_cache, cext_cache, k, v, cext_new)
            fused_pairs = 4
            steps = bsz * (hw // (fused_pairs * LANES))
            if xs[0].shape[0] % steps == 0 and (xs[0].shape[0] // steps) % 128 == 0:
                xs[0], o_s = _post_attn(xs[0], o_p.reshape(-1, hw), w_out, g_ffn, w_ffn_in, w_ffn_down, i,
                                        g_fin, last, *attn_args, fused_pairs, n_head)
            else:
                xs[0] = _post(xs[0], o_p.reshape(-1, hw), w_out, g_ffn, w_ffn_in, w_ffn_down, i,
                              g_fin, row_tiles[0], last)
                o_s = _fox_attn_sample(*attn_args, 8, n_head)
            fox_k[1].append(k.reshape(bsz, t_len, n_head, hd))
            fox_v[1].append(v.reshape(bsz, t_len, n_head, hd))
            fox_f[1].append(jnp.transpose(logf_s, (0, 2, 1)))
            xs[1] = _post(xs[1], o_s.reshape(-1, hw), w_out, g_ffn, w_ffn_in, w_ffn_down, i,
                          g_fin, row_tiles[1], last)

    y_prompt = xs[0].reshape(shapes[0])
    y_sample = xs[1].reshape(shapes[1])
    st = lambda parts: jnp.stack(parts, axis=0)
    return (y_prompt, y_sample, st(gla_states[0]), st(fox_k[0]), st(fox_v[0]), st(fox_f[0]),
            st(gla_states[1]), st(fox_k[1]), st(fox_v[1]), st(fox_f[1]))
```

```python
import functools

import jax
import jax.numpy as jnp
from jax import lax
from jax.experimental import pallas as pl
from jax.experimental.pallas import tpu as pltpu

F32 = jnp.float32
BF16 = jnp.bfloat16

EPS = 1e-6
MASK_VALUE = -1e30
LOG2E = 1.4426950408889634

GLA_HEADS = 4
GLA_CHUNK = 64
GLA_GATE_TAU = 16.0

LANES = 128
MXU_TILE = 256
VT_ROWS = LANES + 16
VMEM_LIMIT_BYTES = 56 * 1024 * 1024

ROW_TILE = 512
GLA_ROWS = 256
GLA_SEQS_PROMPT = 2
GLA_SEQS_SAMPLE = 4
ATTN_TQ = 256
ATTN_PAIRS = 8
PROJ_SAMPLE_SEQS = 8
FUSED_ATTN_PAIRS = 4


def _params(n_grid):
    return pltpu.CompilerParams(
        dimension_semantics=("arbitrary",) * n_grid,
        vmem_limit_bytes=VMEM_LIMIT_BYTES,
    )


def _resident(shape):
    nd = len(shape)
    return pl.BlockSpec(shape, lambda *_: (0,) * nd, pipeline_mode=pl.Buffered(1))


def _dot(a, b):
    return jnp.dot(a, b, preferred_element_type=F32)


def _dot_nt(a, b):
    return lax.dot_general(a, b, (((1,), (1,)), ((), ())), preferred_element_type=F32)


def _dot_tn(a, b):
    return lax.dot_general(a, b, (((0,), (0,)), ((), ())), preferred_element_type=F32)


def _split3(x):
    hi = x.astype(BF16)
    r1 = x - hi.astype(F32)
    mid = r1.astype(BF16)
    lo = (r1 - mid.astype(F32)).astype(BF16)
    return hi, mid, lo


def _sum01(dot_fn, x, ones_first, mat01):
    acc = None
    for part in _split3(x):
        term = dot_fn(mat01, part) if ones_first else dot_fn(part, mat01)
        acc = term if acc is None else acc + term
    return acc


def _rmsnorm(x, g):
    var = jnp.mean(x * x, axis=-1, keepdims=True)
    return x * lax.rsqrt(var + EPS) * g


def _log_sigmoid(z):
    return jnp.minimum(z, 0.0) - jnp.log1p(jnp.exp(-jnp.abs(z)))


def _silu(z):
    return z * jax.nn.sigmoid(z)


def _identity(n):
    r = lax.broadcasted_iota(jnp.int32, (n, n), 0)
    c = lax.broadcasted_iota(jnp.int32, (n, n), 1)
    return (r == c).astype(BF16)


def _cast_weights_once(first_step, wt_ref, wtb_ref):
    @pl.when(first_step)
    def _():
        wtb_ref[...] = wt_ref[...].astype(BF16)


def _gla_proj_kernel(x_ref, g_ref, wt_ref, wg2_ref, bg_ref, proj_ref, glog_ref, wtb_ref, *, n_main):
    _cast_weights_once(pl.program_id(0) == 0, wt_ref, wtb_ref)
    rank = wg2_ref.shape[0]
    h = _rmsnorm(x_ref[...], g_ref[...]).astype(BF16)
    proj_ref[...] = _dot_nt(h, wtb_ref[0:n_main, :])
    gl_t = _dot_nt(wtb_ref[n_main:n_main + rank, :], h).astype(BF16)
    z = _dot_tn(gl_t, wg2_ref[...]) + bg_ref[...]
    glog_ref[...] = _log_sigmoid(z) / GLA_GATE_TAU


def _gla_proj(x2d, g, wt, w_g2, b_g, tm):
    n, d = x2d.shape
    rank, hk = w_g2.shape
    n_main = wt.shape[0] - rank
    return pl.pallas_call(
        functools.partial(_gla_proj_kernel, n_main=n_main),
        grid=(n // tm,),
        in_specs=[
            pl.BlockSpec((tm, d), lambda i: (i, 0)),
            _resident((1, d)),
            _resident(wt.shape),
            _resident(w_g2.shape),
            _resident((1, hk)),
        ],
        out_specs=[
            pl.BlockSpec((tm, n_main), lambda i: (i, 0)),
            pl.BlockSpec((tm, hk), lambda i: (i, 0)),
        ],
        out_shape=[
            jax.ShapeDtypeStruct((n, n_main), F32),
            jax.ShapeDtypeStruct((n, hk), F32),
        ],
        scratch_shapes=[pltpu.VMEM(wt.shape, BF16)],
        compiler_params=_params(1),
        name="gla_proj",
    )(x2d, g, wt, w_g2, b_g)


def _gla_mix_kernel(q_ref, k_ref, v_ref, r_ref, glog_ref, s0_ref, ng_ref,
                    og_ref, sfin_ref, s_ref, *, dk, dv):
    t = pl.program_id(1)
    nb, tb, _ = q_ref.shape
    n_chunk = tb // GLA_CHUNK
    seqs = range(nb)
    heads = range(GLA_HEADS)
    units = [(sb, h) for sb in seqs for h in heads]
    chunks = [slice(c * GLA_CHUNK, (c + 1) * GLA_CHUNK) for c in range(n_chunk)]
    ksl = [slice(h * dk, (h + 1) * dk) for h in heads]
    vsl = [slice(h * dv, (h + 1) * dv) for h in heads]

    @pl.when(t == 0)
    def _():
        s_ref[...] = s0_ref[...]

    row = lax.broadcasted_iota(jnp.int32, (tb, tb), 0)
    col = lax.broadcasted_iota(jnp.int32, (tb, tb), 1)
    same_chunk = (row // GLA_CHUNK) == (col // GLA_CHUNK)
    causal = same_chunk & (col <= row)
    cum_mat = causal.astype(BF16)

    b = [_sum01(_dot, glog_ref[sb], True, cum_mat) for sb in seqs]
    qe, ke, kd, dec_t = [], [], [], []
    for sb in seqs:
        tot_rows = [b[sb][(c + 1) * GLA_CHUNK - 1:(c + 1) * GLA_CHUNK, :] for c in range(n_chunk)]
        b_last = jnp.concatenate([jnp.broadcast_to(r, (GLA_CHUNK, r.shape[1])) for r in tot_rows], axis=0)
        pad_rows = [jnp.zeros_like(tot_rows[0])] * (8 - n_chunk % 8 if n_chunk % 8 else 0)
        dec_t.append(jnp.exp(_to_columns(jnp.concatenate(tot_rows + pad_rows, axis=0))))
        q = q_ref[sb]
        k = k_ref[sb]
        qe.append((q * jnp.exp(b[sb]) * (dk ** -0.5)).astype(BF16))
        ke.append((k * jnp.exp(-b[sb])).astype(BF16))
        kd.append((k * jnp.exp(b_last - b[sb])).astype(BF16))

    v_b = {u: v_ref[u[0], :, vsl[u[1]]].astype(BF16) for u in units}
    a_raw = {(sb, h): _dot_nt(qe[sb][:, ksl[h]], ke[sb][:, ksl[h]]) for sb, h in units}
    upd = {(sb, h): [_dot_tn(kd[sb][rs, ksl[h]], v_b[sb, h][rs]) for rs in chunks] for sb, h in units}
    o_intra = {u: _dot(jnp.where(causal, a_raw[u], 0.0).astype(BF16), v_b[u]) for u in units}
    s_in = {}
    for sb, h in units:
        s = s_ref[sb, h]
        s_in[sb, h] = []
        for c in range(n_chunk):
            s_in[sb, h].append(s.astype(BF16))
            s = s * dec_t[sb][ksl[h], c:c + 1] + upd[sb, h][c]
        s_ref[sb, h] = s
    for sb, h in units:
        o_parts = [o_intra[sb, h][rs] + _dot(qe[sb][rs, ksl[h]], s_in[sb, h][c]) for c, rs in enumerate(chunks)]
        o = o_parts[0] if n_chunk == 1 else jnp.concatenate(o_parts, axis=0)
        on = _rmsnorm(o, ng_ref[:, vsl[h]])
        og_ref[sb, :, vsl[h]] = (on * _silu(r_ref[sb, :, vsl[h]])).astype(BF16)

    @pl.when(t == pl.num_programs(1) - 1)
    def _():
        sfin_ref[...] = s_ref[...]


def _gla_mix(proj3, glog3, s0, norm_g, tb, nb):
    bsz, t_len, _ = proj3.shape
    _, n_head, dk, dv = s0.shape
    hk, hv = n_head * dk, n_head * dv
    assert t_len % tb == 0 and tb % GLA_CHUNK == 0 and hv == 2 * hk and bsz % nb == 0
    kern = functools.partial(_gla_mix_kernel, dk=dk, dv=dv)
    state_spec = pl.BlockSpec((nb, n_head, dk, dv), lambda b, t: (b, 0, 0, 0))
    return pl.pallas_call(
        kern,
        grid=(bsz // nb, t_len // tb),
        in_specs=[
            pl.BlockSpec((nb, tb, hk), lambda b, t: (b, t, 0)),
            pl.BlockSpec((nb, tb, hk), lambda b, t: (b, t, 1)),
            pl.BlockSpec((nb, tb, hv), lambda b, t: (b, t, 1)),
            pl.BlockSpec((nb, tb, hv), lambda b, t: (b, t, 2)),
            pl.BlockSpec((nb, tb, hk), lambda b, t: (b, t, 0)),
            state_spec,
            _resident((1, hv)),
        ],
        out_specs=[
            pl.BlockSpec((nb, tb, hv), lambda b, t: (b, t, 0)),
            state_spec,
        ],
        out_shape=[
            jax.ShapeDtypeStruct((bsz, t_len, hv), BF16),
            jax.ShapeDtypeStruct(s0.shape, F32),
        ],
        scratch_shapes=[pltpu.VMEM((nb, n_head, dk, dv), F32)],
        compiler_params=_params(2),
        name="gla_mix",
    )(proj3, proj3, proj3, proj3, glog3, s0, norm_g)


def _post_kernel(x_ref, o_ref, wo_ref, g_ref, win_ref, wdown_ref, gfin_ref, y_ref, act_ref,
                 *, d_ff, ff_tile, final_norm, hooks=None):
    x1 = x_ref[...] + _dot(o_ref[...], wo_ref[...])
    h = _rmsnorm(x1, g_ref[...]).astype(BF16)
    for j in range(d_ff // ff_tile):
        gate = _dot(h, win_ref[:, j * ff_tile:(j + 1) * ff_tile])
        up = _dot(h, win_ref[:, d_ff + j * ff_tile:d_ff + (j + 1) * ff_tile])
        act_ref[:, j * ff_tile:(j + 1) * ff_tile] = (_silu(gate) * up).astype(BF16)
        if hooks and j in hooks:
            hooks[j]()
    y = x1 + _dot(act_ref[...], wdown_ref[...])
    if final_norm:
        y = _rmsnorm(y, gfin_ref[...])
    y_ref[...] = y


def _post(x2d, o2d, w_out, g_ffn, w_in_all, w_down_all, layer, g_final, tm, final_norm):
    n, d = x2d.shape
    d_ff = w_down_all.shape[1]
    ff_tile = MXU_TILE
    assert d_ff % ff_tile == 0 and n % tm == 0
    kern = functools.partial(_post_kernel, d_ff=d_ff, ff_tile=ff_tile, final_norm=final_norm)
    return pl.pallas_call(
        kern,
        grid=(n // tm,),
        in_specs=[
            pl.BlockSpec((tm, d), lambda i: (i, 0)),
            pl.BlockSpec((tm, o2d.shape[1]), lambda i: (i, 0)),
            _resident(w_out.shape),
            _resident((1, d)),
            pl.BlockSpec((None,) + w_in_all.shape[1:], lambda i: (layer, 0, 0), pipeline_mode=pl.Buffered(1)),
            pl.BlockSpec((None,) + w_down_all.shape[1:], lambda i: (layer, 0, 0), pipeline_mode=pl.Buffered(1)),
            _resident((1, d)),
        ],
        out_specs=pl.BlockSpec((tm, d), lambda i: (i, 0)),
        out_shape=jax.ShapeDtypeStruct((n, d), F32),
        scratch_shapes=[pltpu.VMEM((tm, d_ff), BF16)],
        compiler_params=_params(1),
        name="post_final" if final_norm else "post",
    )(x2d, o2d, w_out, g_ffn, w_in_all, w_down_all, g_final)


def _lane_cumsum(x, tile):
    r = lax.broadcasted_iota(jnp.int32, (tile, tile), 0)
    c = lax.broadcasted_iota(jnp.int32, (tile, tile), 1)
    upper = (r <= c).astype(BF16)
    rows = x.shape[0]
    carry = jnp.zeros((rows, 1), F32)
    out = []
    for j in range(x.shape[1] // tile):
        parts = jnp.concatenate(_split3(x[:, j * tile:(j + 1) * tile]), axis=0)
        s = _dot(parts, upper)
        blk = carry + s[0:rows] + s[rows:2 * rows] + s[2 * rows:3 * rows]
        out.append(blk)
        carry = blk[:, tile - 1:tile]
    return out[0] if len(out) == 1 else jnp.concatenate(out, axis=1)


def _to_columns(x_t):
    rows = x_t.shape[0]
    parts = jnp.concatenate(_split3(x_t), axis=0)
    r = lax.broadcasted_iota(jnp.int32, (3 * rows, LANES), 0)
    c = lax.broadcasted_iota(jnp.int32, (3 * rows, LANES), 1)
    place = ((r % rows) == c).astype(BF16)
    return _dot_tn(parts, place)


def _key_ext_cols(c_cols, n_head):
    hi, mid, lo = (p.astype(F32) for p in _split3(c_cols * LOG2E))
    lane = lax.broadcasted_iota(jnp.int32, c_cols.shape, 1)
    ones = ((lane >= 3 * n_head) & (lane < 3 * n_head + 3)).astype(F32)
    ext = hi + pltpu.roll(mid, n_head, 1) + pltpu.roll(lo, 2 * n_head, 1) + ones
    return ext.astype(BF16)


def _key_ext_rows(c_rows):
    n_head, n = c_rows.shape
    hi, mid, lo = _split3(c_rows * LOG2E)
    r = lax.broadcasted_iota(jnp.int32, (n_head, n), 0)
    ones = (r < 3).astype(BF16)
    zeros = jnp.zeros((LANES - 4 * n_head, n), BF16)
    return jnp.concatenate([hi, mid, lo, ones, zeros], axis=0)


def _query_ext(idx, hh, n_head, cq):
    cq_hi, cq_mid, cq_lo = (p.astype(F32) for p in _split3(cq * LOG2E))
    ext = jnp.where((idx == hh) | (idx == n_head + hh) | (idx == 2 * n_head + hh), -1.0, 0.0)
    ext = jnp.where(idx == 3 * n_head, cq_hi, ext)
    ext = jnp.where(idx == 3 * n_head + 1, cq_mid, ext)
    ext = jnp.where(idx == 3 * n_head + 2, cq_lo, ext)
    return ext.astype(BF16)


def _fox_proj_prompt_kernel(x_ref, g_ref, wtf_ref, bfc_ref,
                            qaug_ref, kaug_ref, kt_ref, vt_ref, vtb_ref, logft_ref, carry_ref, wt_ref,
                            *, hw, n_head, scale, tq):
    t = pl.program_id(1)
    tm = x_ref.shape[0]
    _cast_weights_once((pl.program_id(0) == 0) & (t == 0), wtf_ref, wt_ref)

    @pl.when(t == 0)
    def _():
        carry_ref[...] = jnp.zeros_like(carry_ref)

    h = _rmsnorm(x_ref[...], g_ref[...]).astype(BF16)
    logf_t = _log_sigmoid(_dot_nt(wt_ref[3 * hw:3 * hw + n_head, :], h) + bfc_ref[...])
    logft_ref[...] = logf_t
    ct_blk = _lane_cumsum(logf_t, min(tm, MXU_TILE)) + carry_ref[:, 0:1]
    carry_ref[...] = jnp.broadcast_to(ct_blk[:, tm - 1:tm], carry_ref.shape)
    cext = _key_ext_cols(_to_columns(ct_blk), n_head)
    kt = _dot_nt(wt_ref[hw:2 * hw, :], h)
    vt = _dot_nt(wt_ref[2 * hw:3 * hw, :], h)
    qt = (_dot_nt(wt_ref[0:hw, :], h) * scale).astype(BF16)
    kt_ref[...] = kt
    hd = hw // n_head
    ext_rows = 4 * n_head
    rr = lax.broadcasted_iota(jnp.int32, (ext_rows, tq), 0)
    rq = lax.broadcasted_iota(jnp.int32, (LANES, tq), 0) // hd
    zeros = jnp.zeros((LANES - ext_rows, tq), BF16)
    for p in range(hw // LANES):
        for jq in range(tm // tq):
            qs = slice(jq * tq, (jq + 1) * tq)
            q_pair = qt[p * LANES:(p + 1) * LANES, qs]
            cols = []
            for hh in (2 * p, 2 * p + 1):
                top = jnp.where(rq == hh - 2 * p, q_pair, jnp.zeros_like(q_pair))
                cols.append(jnp.concatenate([top, _query_ext(rr, hh, n_head, ct_blk[hh:hh + 1, qs]), zeros], axis=0))
            qaug_ref[p, jq] = jnp.concatenate(cols, axis=1)
    k = kt.T
    for p in range(hw // LANES):
        kaug_ref[p, :, 0:LANES] = k[:, p * LANES:(p + 1) * LANES].astype(BF16)
        kaug_ref[p, :, LANES:2 * LANES] = cext
    vt_ref[...] = vt
    ones = jnp.ones((VT_ROWS - LANES, tm), BF16)
    for p in range(hw // LANES):
        vtb_ref[p, 0:LANES, :] = vt[p * LANES:(p + 1) * LANES, :].astype(BF16)
        vtb_ref[p, LANES:VT_ROWS, :] = ones


def _fox_proj_prompt(x3, g, wt, bf_col, tm, tq, n_head):
    bsz, t_len, d = x3.shape
    hw = (wt.shape[0] - n_head) // 3
    n_pair = hw // LANES
    assert t_len % tm == 0 and tm % tq == 0 and tq % LANES == 0 and 3 * n_head + 3 <= LANES
    kern = functools.partial(_fox_proj_prompt_kernel, hw=hw, n_head=n_head,
                             scale=(hw // n_head) ** -0.5 * LOG2E, tq=tq)
    feat = pl.BlockSpec((None, hw, tm), lambda b, t: (b, 0, t))
    head = pl.BlockSpec((None, n_head, tm), lambda b, t: (b, 0, t))
    return pl.pallas_call(
        kern,
        grid=(bsz, t_len // tm),
        in_specs=[
            pl.BlockSpec((None, tm, d), lambda b, t: (b, t, 0)),
            _resident((1, d)),
            _resident(wt.shape),
            _resident(bf_col.shape),
        ],
        out_specs=[
            pl.BlockSpec((None, n_pair, tm // tq, 2 * LANES, 2 * tq), lambda b, t: (b, 0, t, 0, 0)),
            pl.BlockSpec((None, n_pair, tm, 2 * LANES), lambda b, t: (b, 0, t, 0)),
            feat, feat,
            pl.BlockSpec((None, n_pair, VT_ROWS, tm), lambda b, t: (b, 0, 0, t)),
            head,
        ],
        out_shape=[
            jax.ShapeDtypeStruct((bsz, n_pair, t_len // tq, 2 * LANES, 2 * tq), BF16),
            jax.ShapeDtypeStruct((bsz, n_pair, t_len, 2 * LANES), BF16),
            jax.ShapeDtypeStruct((bsz, hw, t_len), F32),
            jax.ShapeDtypeStruct((bsz, hw, t_len), F32),
            jax.ShapeDtypeStruct((bsz, n_pair, VT_ROWS, t_len), BF16),
            jax.ShapeDtypeStruct((bsz, n_head, t_len), F32),
        ],
        scratch_shapes=[pltpu.VMEM((n_head, LANES), F32), pltpu.VMEM(wt.shape, BF16)],
        compiler_params=_params(2),
        name="fox_proj_prompt",
    )(x3, g, wt, bf_col)


def _fox_proj_sample_kernel(x_ref, g_ref, wtf_ref, bfc_ref, lct_ref,
                            q_ref, k_ref, v_ref, logft_ref, ccol_ref, cextn_ref, cextc_ref, wt_ref,
                            *, hw, n_head, scale):
    _cast_weights_once(pl.program_id(0) == 0, wtf_ref, wt_ref)
    nb, tm, d = x_ref.shape
    cache_len = lct_ref.shape[2]
    h = _rmsnorm(x_ref[...].reshape(nb * tm, d), g_ref[...]).astype(BF16)
    q_ref[...] = (_dot_nt(h, wt_ref[0:hw, :]) * scale).astype(BF16).reshape(nb, tm, hw)
    k_ref[...] = _dot_nt(h, wt_ref[hw:2 * hw, :]).reshape(nb, tm, hw)
    v_ref[...] = _dot_nt(h, wt_ref[2 * hw:3 * hw, :]).reshape(nb, tm, hw)

    ct_cache = _lane_cumsum(lct_ref[...].reshape(nb * n_head, cache_len), min(cache_len, MXU_TILE))
    logf_all = _log_sigmoid(_dot_nt(wt_ref[3 * hw:3 * hw + n_head, :], h) + bfc_ref[...])
    for b in range(nb):
        ct_b = ct_cache[b * n_head:(b + 1) * n_head, :]
        cextc_ref[b] = _key_ext_rows(ct_b)
        logf_t = logf_all[:, b * tm:(b + 1) * tm]
        logft_ref[b] = logf_t
        ct_new = _lane_cumsum(logf_t, tm) + ct_b[:, cache_len - 1:cache_len]
        cextn_ref[b] = _key_ext_rows(ct_new)
        ccol_ref[b] = _to_columns(ct_new)


def _fox_proj_sample(x3, g, wt, bf_col, logf_cache_t, n_head, nb):
    bsz, tm, d = x3.shape
    hw = (wt.shape[0] - n_head) // 3
    cache_len = logf_cache_t.shape[2]
    assert 4 * n_head <= LANES and bsz % nb == 0
    kern = functools.partial(_fox_proj_sample_kernel, hw=hw, n_head=n_head,
                             scale=(hw // n_head) ** -0.5 * LOG2E)
    tok = pl.BlockSpec((nb, tm, hw), lambda b: (b, 0, 0))
    return pl.pallas_call(
        kern,
        grid=(bsz // nb,),
        in_specs=[
            pl.BlockSpec((nb, tm, d), lambda b: (b, 0, 0)),
            _resident((1, d)),
            _resident(wt.shape),
            _resident(bf_col.shape),
            pl.BlockSpec((nb, n_head, cache_len), lambda b: (b, 0, 0)),
        ],
        out_specs=[
            tok, tok, tok,
            pl.BlockSpec((nb, n_head, tm), lambda b: (b, 0, 0)),
            pl.BlockSpec((nb, tm, LANES), lambda b: (b, 0, 0)),
            pl.BlockSpec((nb, LANES, tm), lambda b: (b, 0, 0)),
            pl.BlockSpec((nb, LANES, cache_len), lambda b: (b, 0, 0)),
        ],
        out_shape=[
            jax.ShapeDtypeStruct((bsz, tm, hw), BF16),
            jax.ShapeDtypeStruct((bsz, tm, hw), F32),
            jax.ShapeDtypeStruct((bsz, tm, hw), F32),
            jax.ShapeDtypeStruct((bsz, n_head, tm), F32),
            jax.ShapeDtypeStruct((bsz, tm, LANES), F32),
            jax.ShapeDtypeStruct((bsz, LANES, tm), BF16),
            jax.ShapeDtypeStruct((bsz, LANES, cache_len), BF16),
        ],
        scratch_shapes=[pltpu.VMEM(wt.shape, BF16)],
        compiler_params=_params(1),
        name="fox_proj_sample",
    )(x3, g, wt, bf_col, logf_cache_t)


def _attn_update(n_pair, kaug_of, vt_of, vt_prev_last, mask, next_kaug0,
                 qaug_ref, m_ref, acc_ref, s0_ref, plast_ref, alast_ref):
    last = n_pair - 1
    s_next = s0_ref[...]
    acc_ref[last] = acc_ref[last] * alast_ref[...] + _dot(vt_prev_last(), plast_ref[...])
    pending = None
    for p in range(n_pair):
        s = s_next
        if p + 1 < n_pair:
            s_next = _dot(kaug_of(p + 1), qaug_ref[p + 1])
        elif next_kaug0 is not None:
            s0_ref[...] = _dot(next_kaug0(), qaug_ref[0])
        if mask is not None:
            s = jnp.where(mask, s, MASK_VALUE)
        m_old = m_ref[p]
        m_new = jnp.maximum(m_old, jnp.max(s, axis=0, keepdims=True))
        alpha = jnp.exp2(m_old - m_new)
        pr = jnp.exp2(s - m_new).astype(BF16)
        m_ref[p] = m_new
        if pending is not None:
            q, pr_q, alpha_q = pending
            acc_ref[q] = acc_ref[q] * alpha_q + _dot(vt_of(q), pr_q)
        pending = (p, pr, alpha)
    plast_ref[...] = pending[1]
    alast_ref[...] = pending[2]


def _attn_finish(o_ref, acc_ref, *, n_pair, tq, hd):
    for p in range(n_pair):
        full = acc_ref[p, 0:LANES, :] * (1.0 / acc_ref[p, LANES:LANES + 1, :])
        z = jnp.concatenate([full[0:hd, 0:tq], full[hd:2 * hd, tq:2 * tq]], axis=0)
        o_ref[:, p * LANES:(p + 1) * LANES] = z.T.astype(o_ref.dtype)


def _fox_attn_prompt_kernel(qaug_ref, kaug_ref, vt_ref, o_ref,
                            m_ref, acc_ref, s0_ref, plast_ref, alast_ref, *, tq, n_pair, hd):
    i = pl.program_id(2)
    last = n_pair - 1
    state = (qaug_ref, m_ref, acc_ref, s0_ref, plast_ref, alast_ref)

    s0_ref[...] = _dot(kaug_ref[0, 0:tq, :], qaug_ref[0])
    m_ref[...] = jnp.full_like(m_ref, MASK_VALUE)
    acc_ref[...] = jnp.zeros_like(acc_ref)
    plast_ref[...] = jnp.zeros_like(plast_ref)
    alast_ref[...] = jnp.ones_like(alast_ref)

    def keys(j):
        return pl.ds(pl.multiple_of(j * tq, tq), tq)

    def vt_of(p, ks):
        return vt_ref[p, :, ks]

    def kv_block(j, mask, has_next):
        ks = keys(j)
        _attn_update(n_pair, lambda p: kaug_ref[p, ks, :], lambda p: vt_of(p, ks),
                     lambda: vt_of(last, keys(jnp.maximum(j - 1, 0))), mask,
                     (lambda: kaug_ref[0, keys(j + 1), :]) if has_next else None, *state)

    def body(j, carry):
        kv_block(j, None, True)
        return carry

    lax.fori_loop(0, i, body, 0)
    row = lax.broadcasted_iota(jnp.int32, (tq, 2 * tq), 0)
    col = lax.broadcasted_iota(jnp.int32, (tq, 2 * tq), 1) % tq
    kv_block(i, row <= col, False)
    acc_ref[last] = acc_ref[last] * alast_ref[...] + _dot(vt_of(last, keys(i)), plast_ref[...])
    _attn_finish(o_ref, acc_ref, n_pair=n_pair, tq=tq, hd=hd)


def _fox_attn_prompt(qaug, kaug, vtb, n_pair, n_head):
    bsz, n_pair_all, n_qblk, _, tq2 = qaug.shape
    tq = tq2 // 2
    t_len = n_qblk * tq
    hw = n_pair_all * LANES
    hd = hw // n_head
    assert 2 * hd == LANES and n_pair_all % n_pair == 0
    width = n_pair * LANES
    n_group = n_pair_all // n_pair
    kern = functools.partial(_fox_attn_prompt_kernel, tq=tq, n_pair=n_pair, hd=hd)
    return pl.pallas_call(
        kern,
        grid=(bsz, n_group, n_qblk),
        in_specs=[
            pl.BlockSpec((None, n_pair, None, 2 * LANES, 2 * tq), lambda b, g, i: (b, g, i, 0, 0)),
            pl.BlockSpec((None, n_pair, t_len, 2 * LANES), lambda b, g, i: (b, g, 0, 0)),
            pl.BlockSpec((None, n_pair, VT_ROWS, t_len), lambda b, g, i: (b, g, 0, 0)),
        ],
        out_specs=pl.BlockSpec((None, tq, width), lambda b, g, i: (b, i, g)),
        out_shape=jax.ShapeDtypeStruct((bsz, t_len, hw), BF16),
        scratch_shapes=[
            pltpu.VMEM((n_pair, 1, 2 * tq), F32),
            pltpu.VMEM((n_pair, VT_ROWS, 2 * tq), F32),
            pltpu.VMEM((tq, 2 * tq), F32),
            pltpu.VMEM((tq, 2 * tq), BF16),
            pltpu.VMEM((1, 2 * tq), F32),
        ],
        compiler_params=_params(3),
        name="fox_attn_prompt",
    )(qaug, kaug, vtb)


def _sample_attn_stages(q_ref, ccol_ref, kt_ref, vt_ref, cextc_ref, kn_ref, vn_ref, cextn_ref, o_ref,
                        *, g, n_pair, hd, n_head):
    tq = q_ref.shape[0]
    eye = _identity(LANES)
    lane = lax.broadcasted_iota(jnp.int32, (tq, LANES), 1)
    row = lax.broadcasted_iota(jnp.int32, (2 * tq, tq), 0) % tq
    col = lax.broadcasted_iota(jnp.int32, (2 * tq, tq), 1)
    causal_new = col <= row
    ccol = ccol_ref[...]

    def scores(p):
        feat = slice(p * LANES, (p + 1) * LANES)
        q_pair = q_ref[:, feat]
        blocks = []
        for h in range(2):
            hh = g * (2 * n_pair) + 2 * p + h
            cq = jnp.sum(jnp.where(lane == hh, ccol, 0.0), axis=1, keepdims=True)
            top = jnp.where(lane // hd == h, q_pair, jnp.zeros_like(q_pair))
            blocks.append(jnp.concatenate([top, _query_ext(lane, hh, n_head, cq)], axis=1))
        qaug = jnp.concatenate(blocks, axis=0)
        s_c = _dot(qaug, jnp.concatenate([kt_ref[feat, :].astype(BF16), cextc_ref[...]], axis=0))
        kn_t = _dot_nt(eye, kn_ref[:, feat].astype(BF16)).astype(BF16)
        s_n = _dot(qaug, jnp.concatenate([kn_t, cextn_ref[...]], axis=0))
        return s_c, jnp.where(causal_new, s_n, MASK_VALUE)

    held = {}

    def score_stage(p):
        held["s", p] = scores(p)

    def softmax_stage(p):
        s_c, s_n = held.pop(("s", p))
        m = jnp.maximum(jnp.max(s_c, axis=1, keepdims=True), jnp.max(s_n, axis=1, keepdims=True))
        p_c = jnp.exp2(s_c - m)
        p_n = jnp.exp2(s_n - m)
        l = jnp.sum(p_c, axis=1, keepdims=True) + jnp.sum(p_n, axis=1, keepdims=True)
        held["p", p] = (p_c.astype(BF16), p_n.astype(BF16), l)

    def value_stage(p):
        p_c, p_n, l = held.pop(("p", p))
        feat = slice(p * LANES, (p + 1) * LANES)
        o = _dot_nt(p_c, vt_ref[feat, :].astype(BF16)) + _dot(p_n, vn_ref[:, feat].astype(BF16))
        o = o / l
        o_ref[:, feat] = jnp.where(lane < hd, o[0:tq], o[tq:2 * tq]).astype(o_ref.dtype)

    return score_stage, softmax_stage, value_stage


def _fox_attn_sample_kernel(*refs, n_pair, hd, n_head):
    score, softmax, value = _sample_attn_stages(*refs, g=pl.program_id(1), n_pair=n_pair, hd=hd, n_head=n_head)
    score(0)
    for p in range(n_pair):
        if p + 1 < n_pair:
            score(p + 1)
        softmax(p)
        value(p)


def _post_attn_kernel(x_ref, o_ref, wo_ref, g_ref, win_ref, wdown_ref, gfin_ref,
                      q_ref, ccol_ref, kt_ref, vt_ref, cextc_ref, kn_ref, vn_ref, cextn_ref,
                      y_ref, os_ref, act_ref, *, d_ff, ff_tile, final_norm, n_group, n_pair, hd, n_head):
    score, softmax, value = _sample_attn_stages(
        q_ref, ccol_ref, kt_ref, vt_ref, cextc_ref, kn_ref, vn_ref, cextn_ref, os_ref,
        g=pl.program_id(0) % n_group, n_pair=n_pair, hd=hd, n_head=n_head)
    n_ff = d_ff // ff_tile
    stride = max(1, (n_ff - 1) // (n_pair + 1))
    hooks = {}

    def at(j, fn):
        j = min(j, n_ff - 1)
        prev = hooks.get(j)
        hooks[j] = fn if prev is None else (lambda: (prev(), fn()))

    for p in range(n_pair):
        at(p * stride, functools.partial(score, p))
        at(p * stride, functools.partial(softmax, p))
        at((p + 2) * stride, functools.partial(value, p))
    _post_kernel(x_ref, o_ref, wo_ref, g_ref, win_ref, wdown_ref, gfin_ref, y_ref, act_ref,
                 d_ff=d_ff, ff_tile=ff_tile, final_norm=final_norm, hooks=hooks)


def _fox_attn_sample(q, c_col, kt_cache, vt_cache, cext_cache, k_new, v_new, cext_new, n_pair, n_head):
    bsz, tq, hw = q.shape
    cache_len = kt_cache.shape[2]
    hd = hw // n_head
    assert 2 * hd == LANES and 4 * n_head <= LANES
    width = n_pair * LANES
    new_spec = pl.BlockSpec((None, tq, width), lambda b, g: (b, 0, g))
    cache_spec = pl.BlockSpec((None, width, cache_len), lambda b, g: (b, g, 0))
    kern = functools.partial(_fox_attn_sample_kernel, n_pair=n_pair, hd=hd, n_head=n_head)
    return pl.pallas_call(
        kern,
        grid=(bsz, hw // width),
        in_specs=[
            new_spec,
            pl.BlockSpec((None, tq, LANES), lambda b, g: (b, 0, 0)),
            cache_spec, cache_spec,
            pl.BlockSpec((None, LANES, cache_len), lambda b, g: (b, 0, 0)),
            new_spec, new_spec,
            pl.BlockSpec((None, LANES, tq), lambda b, g: (b, 0, 0)),
        ],
        out_specs=new_spec,
        out_shape=jax.ShapeDtypeStruct((bsz, tq, hw), BF16),
        compiler_params=_params(2),
        name="fox_attn_sample",
    )(q, c_col, kt_cache, vt_cache, cext_cache, k_new, v_new, cext_new)


def _post_attn(x2d, o2d, w_out, g_ffn, w_in_all, w_down_all, layer, g_final, final_norm,
               q, c_col, kt_cache, vt_cache, cext_cache, k_new, v_new, cext_new, n_pair, n_head):
    n, d = x2d.shape
    bsz, tq, hw = q.shape
    cache_len = kt_cache.shape[2]
    hd = hw // n_head
    width = n_pair * LANES
    n_group = hw // width
    steps = bsz * n_group
    tm = n // steps
    d_ff = w_down_all.shape[1]
    ff_tile = MXU_TILE
    assert n % steps == 0 and tm % 16 == 0 and d_ff % ff_tile == 0 and 2 * hd == LANES
    kern = functools.partial(_post_attn_kernel, d_ff=d_ff, ff_tile=ff_tile, final_norm=final_norm,
                             n_group=n_group, n_pair=n_pair, hd=hd, n_head=n_head)
    new_spec = pl.BlockSpec((None, tq, width), lambda i: (i // n_group, 0, i % n_group))
    cache_spec = pl.BlockSpec((None, width, cache_len), lambda i: (i // n_group, i % n_group, 0))
    return pl.pallas_call(
        kern,
        grid=(steps,),
        in_specs=[
            pl.BlockSpec((tm, d), lambda i: (i, 0)),
            pl.BlockSpec((tm, o2d.shape[1]), lambda i: (i, 0)),
            _resident(w_out.shape),
            _resident((1, d)),
            pl.BlockSpec((None,) + w_in_all.shape[1:], lambda i: (layer, 0, 0), pipeline_mode=pl.Buffered(1)),
            pl.BlockSpec((None,) + w_down_all.shape[1:], lambda i: (layer, 0, 0), pipeline_mode=pl.Buffered(1)),
            _resident((1, d)),
            new_spec,
            pl.BlockSpec((None, tq, LANES), lambda i: (i // n_group, 0, 0)),
            cache_spec, cache_spec,
            pl.BlockSpec((None, LANES, cache_len), lambda i: (i // n_group, 0, 0)),
            new_spec, new_spec,
            pl.BlockSpec((None, LANES, tq), lambda i: (i // n_group, 0, 0)),
        ],
        out_specs=[pl.BlockSpec((tm, d), lambda i: (i, 0)), new_spec],
        out_shape=[jax.ShapeDtypeStruct((n, d), F32), jax.ShapeDtypeStruct((bsz, tq, hw), BF16)],
        scratch_shapes=[pltpu.VMEM((tm, d_ff), BF16)],
        compiler_params=_params(1),
        name="post_attn",
    )(x2d, o2d, w_out, g_ffn, w_in_all, w_down_all, g_final,
      q, c_col, kt_cache, vt_cache, cext_cache, k_new, v_new, cext_new)


def kernel(x_prompt, x_sample, state_gla, cache_fox_k, cache_fox_v, cache_fox_logf,
           norm_mix, gla_w_in, gla_w_g2, gla_b_g, gla_norm, gla_w_out,
           fox_w_in, fox_b_f, fox_w_out, norm_ffn, ffn_w_in, ffn_w_down, norm_final):
    d = x_prompt.shape[-1]
    depth = norm_mix.shape[0]
    groups = [x_prompt, x_sample]
    shapes = [x.shape for x in groups]
    xs = [x.reshape(-1, d) for x in groups]
    row_tiles = [min(ROW_TILE, x.shape[0]) for x in xs]
    w_ffn_in = ffn_w_in.astype(BF16)
    w_ffn_down = ffn_w_down.astype(BF16)

    gla_states = [[], []]
    fox_k, fox_v, fox_f = [[], []], [[], []], [[], []]
    for i in range(depth):
        j = i // 2
        g_mix = norm_mix[i].reshape(1, d)
        g_ffn = norm_ffn[i].reshape(1, d)
        last = i == depth - 1
        if i % 2 == 0:
            _, n_head, dk, dv = state_gla.shape[1:]
            hk, hv = n_head * dk, n_head * dv
            n_main = 2 * hk + 2 * hv
            w_in = gla_w_in[j]
            wt = w_in.T
            w_g2 = gla_w_g2[j].astype(BF16)
            b_g = gla_b_g[j].reshape(1, hk)
            w_out = gla_w_out[j].astype(BF16)
            norm_g = gla_norm[j].reshape(1, hv)
            s0s = [jnp.zeros((shapes[0][0], n_head, dk, dv), F32), state_gla[j]]
            for gi in range(2):
                bsz, t_len, _ = shapes[gi]
                proj, glog = _gla_proj(xs[gi], g_mix, wt, w_g2, b_g, row_tiles[gi])
                og, s_fin = _gla_mix(proj.reshape(bsz, t_len, n_main), glog.reshape(bsz, t_len, hk),
                                     s0s[gi], norm_g, min(t_len, GLA_ROWS),
                                     min(bsz, GLA_SEQS_PROMPT if gi == 0 else GLA_SEQS_SAMPLE))
                gla_states[gi].append(s_fin)
                xs[gi] = _post(xs[gi], og.reshape(-1, hv), w_out, g_ffn, w_ffn_in, w_ffn_down, i,
                               norm_final.reshape(1, d), row_tiles[gi], last)
        else:
            n_head = fox_b_f.shape[1]
            hw = fox_w_out.shape[1]
            hd = hw // n_head
            wt = fox_w_in[j].T
            bf_col = fox_b_f[j].reshape(n_head, 1)
            w_out = fox_w_out[j].astype(BF16)
            g_fin = norm_final.reshape(1, d)
            bsz, t_len, _ = shapes[0]
            qaug, kaug, kt, vt, vtb, logf_p = _fox_proj_prompt(xs[0].reshape(bsz, t_len, d), g_mix, wt, bf_col,
                                                               min(ROW_TILE, t_len), ATTN_TQ, n_head)
            o_p = _fox_attn_prompt(qaug, kaug, vtb, ATTN_PAIRS, n_head)
            fox_k[0].append(kt.reshape(bsz, n_head, hd, t_len).transpose(0, 3, 1, 2))
            fox_v[0].append(vt.reshape(bsz, n_head, hd, t_len).transpose(0, 3, 1, 2))
            fox_f[0].append(jnp.transpose(logf_p, (0, 2, 1)))
            bsz, t_len, _ = shapes[1]
            cache_len = cache_fox_logf.shape[2]
            q, k, v, logf_s, c_col, cext_new, cext_cache = _fox_proj_sample(
                xs[1].reshape(bsz, t_len, d), g_mix, wt, bf_col,
                jnp.transpose(cache_fox_logf[j], (0, 2, 1)), n_head, min(PROJ_SAMPLE_SEQS, bsz))
            kt_cache = jnp.transpose(cache_fox_k[j], (0, 2, 3, 1)).reshape(bsz, hw, cache_len)
            vt_cache = jnp.transpose(cache_fox_v[j], (0, 2, 3, 1)).reshape(bsz, hw, cache_len)
            attn_args = (q, c_col, kt_cache, vt_cache, cext_cache, k, v, cext_new)
            steps = bsz * (hw // (FUSED_ATTN_PAIRS * LANES))
            if xs[0].shape[0] % steps == 0 and (xs[0].shape[0] // steps) % LANES == 0:
                xs[0], o_s = _post_attn(xs[0], o_p.reshape(-1, hw), w_out, g_ffn, w_ffn_in, w_ffn_down, i,
                                        g_fin, last, *attn_args, FUSED_ATTN_PAIRS, n_head)
            else:
                xs[0] = _post(xs[0], o_p.reshape(-1, hw), w_out, g_ffn, w_ffn_in, w_ffn_down, i,
                              g_fin, row_tiles[0], last)
                o_s = _fox_attn_sample(*attn_args, hw // LANES, n_head)
            fox_k[1].append(k.reshape(bsz, t_len, n_head, hd))
            fox_v[1].append(v.reshape(bsz, t_len, n_head, hd))
            fox_f[1].append(jnp.transpose(logf_s, (0, 2, 1)))
            xs[1] = _post(xs[1], o_s.reshape(-1, hw), w_out, g_ffn, w_ffn_in, w_ffn_down, i,
                          g_fin, row_tiles[1], last)

    y_prompt = xs[0].reshape(shapes[0])
    y_sample = xs[1].reshape(shapes[1])
    st = lambda parts: jnp.stack(parts, axis=0)
    return (y_prompt, y_sample, st(gla_states[0]), st(fox_k[0]), st(fox_v[0]), st(fox_f[0]),
            st(gla_states[1]), st(fox_k[1]), st(fox_v[1]), st(fox_f[1]))
```

```python
import functools

import jax
import jax.numpy as jnp
from jax import lax
from jax.experimental import pallas as pl
from jax.experimental.pallas import tpu as pltpu

F32 = jnp.float32
BF16 = jnp.bfloat16

EPS = 1e-6
MASK_VALUE = -1e30
LOG2E = 1.4426950408889634

GLA_HEADS = 4
GLA_CHUNK = 64
GLA_GATE_TAU = 16.0

LANES = 128
MXU_TILE = 256
VT_ROWS = LANES + 16
VMEM_LIMIT_BYTES = 56 * 1024 * 1024

ROW_TILE = 512
GLA_ROWS = 256
GLA_SEQS_PROMPT = 2
GLA_SEQS_SAMPLE = 4
ATTN_TQ = 256
ATTN_PAIRS = 8
PROJ_SAMPLE_SEQS = 8
FUSED_ATTN_PAIRS = 4


def _params(n_grid):
    return pltpu.CompilerParams(
        dimension_semantics=("arbitrary",) * n_grid,
        vmem_limit_bytes=VMEM_LIMIT_BYTES,
    )


def _resident(shape):
    nd = len(shape)
    return pl.BlockSpec(shape, lambda *_: (0,) * nd, pipeline_mode=pl.Buffered(1))


def _dot(a, b):
    return jnp.dot(a, b, preferred_element_type=F32)


def _dot_nt(a, b):
    return lax.dot_general(a, b, (((1,), (1,)), ((), ())), preferred_element_type=F32)


def _dot_tn(a, b):
    return lax.dot_general(a, b, (((0,), (0,)), ((), ())), preferred_element_type=F32)


def _split3(x):
    hi = x.astype(BF16)
    r1 = x - hi.astype(F32)
    mid = r1.astype(BF16)
    lo = (r1 - mid.astype(F32)).astype(BF16)
    return hi, mid, lo


def _sum01(dot_fn, x, ones_first, mat01):
    acc = None
    for part in _split3(x):
        term = dot_fn(mat01, part) if ones_first else dot_fn(part, mat01)
        acc = term if acc is None else acc + term
    return acc


def _rmsnorm(x, g):
    var = jnp.mean(x * x, axis=-1, keepdims=True)
    return x * lax.rsqrt(var + EPS) * g


def _log_sigmoid(z):
    return jnp.minimum(z, 0.0) - jnp.log1p(jnp.exp(-jnp.abs(z)))


def _silu(z):
    return z * jax.nn.sigmoid(z)


def _identity(n):
    r = lax.broadcasted_iota(jnp.int32, (n, n), 0)
    c = lax.broadcasted_iota(jnp.int32, (n, n), 1)
    return (r == c).astype(BF16)


def _cast_weights_once(first_step, wt_ref, wtb_ref):
    @pl.when(first_step)
    def _():
        wtb_ref[...] = wt_ref[...].astype(BF16)


def _gla_proj_kernel(x_ref, g_ref, wt_ref, wg2_ref, bg_ref, proj_ref, glog_ref, wtb_ref, *, n_main):
    _cast_weights_once(pl.program_id(0) == 0, wt_ref, wtb_ref)
    rank = wg2_ref.shape[0]
    h = _rmsnorm(x_ref[...], g_ref[...]).astype(BF16)
    proj_ref[...] = _dot_nt(h, wtb_ref[0:n_main, :])
    gl_t = _dot_nt(wtb_ref[n_main:n_main + rank, :], h).astype(BF16)
    z = _dot_tn(gl_t, wg2_ref[...]) + bg_ref[...]
    glog_ref[...] = _log_sigmoid(z) / GLA_GATE_TAU


def _gla_proj(x2d, g, wt, w_g2, b_g, tm):
    n, d = x2d.shape
    rank, hk = w_g2.shape
    n_main = wt.shape[0] - rank
    return pl.pallas_call(
        functools.partial(_gla_proj_kernel, n_main=n_main),
        grid=(n // tm,),
        in_specs=[
            pl.BlockSpec((tm, d), lambda i: (i, 0)),
            _resident((1, d)),
            _resident(wt.shape),
            _resident(w_g2.shape),
            _resident((1, hk)),
        ],
        out_specs=[
            pl.BlockSpec((tm, n_main), lambda i: (i, 0)),
            pl.BlockSpec((tm, hk), lambda i: (i, 0)),
        ],
        out_shape=[
            jax.ShapeDtypeStruct((n, n_main), F32),
            jax.ShapeDtypeStruct((n, hk), F32),
        ],
        scratch_shapes=[pltpu.VMEM(wt.shape, BF16)],
        compiler_params=_params(1),
        name="gla_proj",
    )(x2d, g, wt, w_g2, b_g)


def _gla_mix_kernel(q_ref, k_ref, v_ref, r_ref, glog_ref, s0_ref, ng_ref,
                    og_ref, sfin_ref, s_ref, *, dk, dv):
    t = pl.program_id(1)
    nb, tb, _ = q_ref.shape
    n_chunk = tb // GLA_CHUNK
    seqs = range(nb)
    heads = range(GLA_HEADS)
    units = [(sb, h) for sb in seqs for h in heads]
    chunks = [slice(c * GLA_CHUNK, (c + 1) * GLA_CHUNK) for c in range(n_chunk)]
    ksl = [slice(h * dk, (h + 1) * dk) for h in heads]
    vsl = [slice(h * dv, (h + 1) * dv) for h in heads]

    @pl.when(t == 0)
    def _():
        s_ref[...] = s0_ref[...]

    row = lax.broadcasted_iota(jnp.int32, (tb, tb), 0)
    col = lax.broadcasted_iota(jnp.int32, (tb, tb), 1)
    same_chunk = (row // GLA_CHUNK) == (col // GLA_CHUNK)
    causal = same_chunk & (col <= row)
    cum_mat = causal.astype(BF16)

    b = [_sum01(_dot, glog_ref[sb], True, cum_mat) for sb in seqs]
    qe, ke, kd, dec_t = [], [], [], []
    for sb in seqs:
        tot_rows = [b[sb][(c + 1) * GLA_CHUNK - 1:(c + 1) * GLA_CHUNK, :] for c in range(n_chunk)]
        b_last = jnp.concatenate([jnp.broadcast_to(r, (GLA_CHUNK, r.shape[1])) for r in tot_rows], axis=0)
        pad_rows = [jnp.zeros_like(tot_rows[0])] * (8 - n_chunk % 8 if n_chunk % 8 else 0)
        dec_t.append(jnp.exp(_to_columns(jnp.concatenate(tot_rows + pad_rows, axis=0))))
        q = q_ref[sb]
        k = k_ref[sb]
        qe.append((q * jnp.exp(b[sb]) * (dk ** -0.5)).astype(BF16))
        ke.append((k * jnp.exp(-b[sb])).astype(BF16))
        kd.append((k * jnp.exp(b_last - b[sb])).astype(BF16))

    v_b = {u: v_ref[u[0], :, vsl[u[1]]].astype(BF16) for u in units}
    a_raw = {(sb, h): _dot_nt(qe[sb][:, ksl[h]], ke[sb][:, ksl[h]]) for sb, h in units}
    upd = {(sb, h): [_dot_tn(kd[sb][rs, ksl[h]], v_b[sb, h][rs]) for rs in chunks] for sb, h in units}
    o_intra = {u: _dot(jnp.where(causal, a_raw[u], 0.0).astype(BF16), v_b[u]) for u in units}
    s_in = {}
    for sb, h in units:
        s = s_ref[sb, h]
        s_in[sb, h] = []
        for c in range(n_chunk):
            s_in[sb, h].append(s.astype(BF16))
            s = s * dec_t[sb][ksl[h], c:c + 1] + upd[sb, h][c]
        s_ref[sb, h] = s
    for sb, h in units:
        o_parts = [o_intra[sb, h][rs] + _dot(qe[sb][rs, ksl[h]], s_in[sb, h][c]) for c, rs in enumerate(chunks)]
        o = o_parts[0] if n_chunk == 1 else jnp.concatenate(o_parts, axis=0)
        on = _rmsnorm(o, ng_ref[:, vsl[h]])
        og_ref[sb, :, vsl[h]] = (on * _silu(r_ref[sb, :, vsl[h]])).astype(BF16)

    @pl.when(t == pl.num_programs(1) - 1)
    def _():
        sfin_ref[...] = s_ref[...]


def _gla_mix(proj3, glog3, s0, norm_g, tb, nb):
    bsz, t_len, _ = proj3.shape
    _, n_head, dk, dv = s0.shape
    hk, hv = n_head * dk, n_head * dv
    assert t_len % tb == 0 and tb % GLA_CHUNK == 0 and hv == 2 * hk and bsz % nb == 0
    kern = functools.partial(_gla_mix_kernel, dk=dk, dv=dv)
    state_spec = pl.BlockSpec((nb, n_head, dk, dv), lambda b, t: (b, 0, 0, 0))
    return pl.pallas_call(
        kern,
        grid=(bsz // nb, t_len // tb),
        in_specs=[
            pl.BlockSpec((nb, tb, hk), lambda b, t: (b, t, 0)),
            pl.BlockSpec((nb, tb, hk), lambda b, t: (b, t, 1)),
            pl.BlockSpec((nb, tb, hv), lambda b, t: (b, t, 1)),
            pl.BlockSpec((nb, tb, hv), lambda b, t: (b, t, 2)),
            pl.BlockSpec((nb, tb, hk), lambda b, t: (b, t, 0)),
            state_spec,
            _resident((1, hv)),
        ],
        out_specs=[
            pl.BlockSpec((nb, tb, hv), lambda b, t: (b, t, 0)),
            state_spec,
        ],
        out_shape=[
            jax.ShapeDtypeStruct((bsz, t_len, hv), BF16),
            jax.ShapeDtypeStruct(s0.shape, F32),
        ],
        scratch_shapes=[pltpu.VMEM((nb, n_head, dk, dv), F32)],
        compiler_params=_params(2),
        name="gla_mix",
    )(proj3, proj3, proj3, proj3, glog3, s0, norm_g)


def _post_kernel(x_ref, o_ref, wo_ref, g_ref, win_ref, wdown_ref, gfin_ref, y_ref, act_ref,
                 *, d_ff, ff_tile, final_norm, hooks=None):
    x1 = x_ref[...] + _dot(o_ref[...], wo_ref[...])
    if hooks and -1 in hooks:
        hooks[-1]()
    h = _rmsnorm(x1, g_ref[...]).astype(BF16)
    for j in range(d_ff // ff_tile):
        gate = _dot(h, win_ref[:, j * ff_tile:(j + 1) * ff_tile])
        up = _dot(h, win_ref[:, d_ff + j * ff_tile:d_ff + (j + 1) * ff_tile])
        act_ref[:, j * ff_tile:(j + 1) * ff_tile] = (_silu(gate) * up).astype(BF16)
        if hooks and j in hooks:
            hooks[j]()
    y = x1 + _dot(act_ref[...], wdown_ref[...])
    if final_norm:
        y = _rmsnorm(y, gfin_ref[...])
    y_ref[...] = y


def _post(x2d, o2d, w_out, g_ffn, w_in_all, w_down_all, layer, g_final, tm, final_norm):
    n, d = x2d.shape
    d_ff = w_down_all.shape[1]
    ff_tile = MXU_TILE
    assert d_ff % ff_tile == 0 and n % tm == 0
    kern = functools.partial(_post_kernel, d_ff=d_ff, ff_tile=ff_tile, final_norm=final_norm)
    return pl.pallas_call(
        kern,
        grid=(n // tm,),
        in_specs=[
            pl.BlockSpec((tm, d), lambda i: (i, 0)),
            pl.BlockSpec((tm, o2d.shape[1]), lambda i: (i, 0)),
            _resident(w_out.shape),
            _resident((1, d)),
            pl.BlockSpec((None,) + w_in_all.shape[1:], lambda i: (layer, 0, 0), pipeline_mode=pl.Buffered(1)),
            pl.BlockSpec((None,) + w_down_all.shape[1:], lambda i: (layer, 0, 0), pipeline_mode=pl.Buffered(1)),
            _resident((1, d)),
        ],
        out_specs=pl.BlockSpec((tm, d), lambda i: (i, 0)),
        out_shape=jax.ShapeDtypeStruct((n, d), F32),
        scratch_shapes=[pltpu.VMEM((tm, d_ff), BF16)],
        compiler_params=_params(1),
        name="post_final" if final_norm else "post",
    )(x2d, o2d, w_out, g_ffn, w_in_all, w_down_all, g_final)


def _lane_cumsum(x, tile):
    r = lax.broadcasted_iota(jnp.int32, (tile, tile), 0)
    c = lax.broadcasted_iota(jnp.int32, (tile, tile), 1)
    upper = (r <= c).astype(BF16)
    rows = x.shape[0]
    carry = jnp.zeros((rows, 1), F32)
    out = []
    for j in range(x.shape[1] // tile):
        parts = jnp.concatenate(_split3(x[:, j * tile:(j + 1) * tile]), axis=0)
        s = _dot(parts, upper)
        blk = carry + s[0:rows] + s[rows:2 * rows] + s[2 * rows:3 * rows]
        out.append(blk)
        carry = blk[:, tile - 1:tile]
    return out[0] if len(out) == 1 else jnp.concatenate(out, axis=1)


def _to_columns(x_t):
    rows = x_t.shape[0]
    parts = jnp.concatenate(_split3(x_t), axis=0)
    r = lax.broadcasted_iota(jnp.int32, (3 * rows, LANES), 0)
    c = lax.broadcasted_iota(jnp.int32, (3 * rows, LANES), 1)
    place = ((r % rows) == c).astype(BF16)
    return _dot_tn(parts, place)


def _key_ext_cols(c_cols, n_head):
    hi, mid, lo = (p.astype(F32) for p in _split3(c_cols * LOG2E))
    lane = lax.broadcasted_iota(jnp.int32, c_cols.shape, 1)
    ones = ((lane >= 3 * n_head) & (lane < 3 * n_head + 3)).astype(F32)
    ext = hi + pltpu.roll(mid, n_head, 1) + pltpu.roll(lo, 2 * n_head, 1) + ones
    return ext.astype(BF16)


def _key_ext_rows(c_rows):
    n_head, n = c_rows.shape
    hi, mid, lo = _split3(c_rows * LOG2E)
    r = lax.broadcasted_iota(jnp.int32, (n_head, n), 0)
    ones = (r < 3).astype(BF16)
    zeros = jnp.zeros((LANES - 4 * n_head, n), BF16)
    return jnp.concatenate([hi, mid, lo, ones, zeros], axis=0)


def _query_ext(idx, hh, n_head, cq):
    cq_hi, cq_mid, cq_lo = (p.astype(F32) for p in _split3(cq * LOG2E))
    ext = jnp.where((idx == hh) | (idx == n_head + hh) | (idx == 2 * n_head + hh), -1.0, 0.0)
    ext = jnp.where(idx == 3 * n_head, cq_hi, ext)
    ext = jnp.where(idx == 3 * n_head + 1, cq_mid, ext)
    ext = jnp.where(idx == 3 * n_head + 2, cq_lo, ext)
    return ext.astype(BF16)


def _fox_proj_prompt_kernel(x_ref, g_ref, wtf_ref, bfc_ref,
                            qaug_ref, kaug_ref, kt_ref, vt_ref, vtb_ref, logft_ref, carry_ref, wt_ref,
                            *, hw, n_head, scale, tq):
    t = pl.program_id(1)
    tm = x_ref.shape[0]
    _cast_weights_once((pl.program_id(0) == 0) & (t == 0), wtf_ref, wt_ref)

    @pl.when(t == 0)
    def _():
        carry_ref[...] = jnp.zeros_like(carry_ref)

    h = _rmsnorm(x_ref[...], g_ref[...]).astype(BF16)
    logf_t = _log_sigmoid(_dot_nt(wt_ref[3 * hw:3 * hw + n_head, :], h) + bfc_ref[...])
    logft_ref[...] = logf_t
    ct_blk = _lane_cumsum(logf_t, min(tm, MXU_TILE)) + carry_ref[:, 0:1]
    carry_ref[...] = jnp.broadcast_to(ct_blk[:, tm - 1:tm], carry_ref.shape)
    cext = _key_ext_cols(_to_columns(ct_blk), n_head)
    kt = _dot_nt(wt_ref[hw:2 * hw, :], h)
    vt = _dot_nt(wt_ref[2 * hw:3 * hw, :], h)
    qt = (_dot_nt(wt_ref[0:hw, :], h) * scale).astype(BF16)
    kt_ref[...] = kt
    hd = hw // n_head
    ext_rows = 4 * n_head
    rr = lax.broadcasted_iota(jnp.int32, (ext_rows, tq), 0)
    rq = lax.broadcasted_iota(jnp.int32, (LANES, tq), 0) // hd
    zeros = jnp.zeros((LANES - ext_rows, tq), BF16)
    for p in range(hw // LANES):
        for jq in range(tm // tq):
            qs = slice(jq * tq, (jq + 1) * tq)
            q_pair = qt[p * LANES:(p + 1) * LANES, qs]
            cols = []
            for hh in (2 * p, 2 * p + 1):
                top = jnp.where(rq == hh - 2 * p, q_pair, jnp.zeros_like(q_pair))
                cols.append(jnp.concatenate([top, _query_ext(rr, hh, n_head, ct_blk[hh:hh + 1, qs]), zeros], axis=0))
            qaug_ref[p, jq] = jnp.concatenate(cols, axis=1)
    k = kt.T
    for p in range(hw // LANES):
        kaug_ref[p, :, 0:LANES] = k[:, p * LANES:(p + 1) * LANES].astype(BF16)
        kaug_ref[p, :, LANES:2 * LANES] = cext
    vt_ref[...] = vt
    ones = jnp.ones((VT_ROWS - LANES, tm), BF16)
    for p in range(hw // LANES):
        vtb_ref[p, 0:LANES, :] = vt[p * LANES:(p + 1) * LANES, :].astype(BF16)
        vtb_ref[p, LANES:VT_ROWS, :] = ones


def _fox_proj_prompt(x3, g, wt, bf_col, tm, tq, n_head):
    bsz, t_len, d = x3.shape
    hw = (wt.shape[0] - n_head) // 3
    n_pair = hw // LANES
    assert t_len % tm == 0 and tm % tq == 0 and tq % LANES == 0 and 3 * n_head + 3 <= LANES
    kern = functools.partial(_fox_proj_prompt_kernel, hw=hw, n_head=n_head,
                             scale=(hw // n_head) ** -0.5 * LOG2E, tq=tq)
    feat = pl.BlockSpec((None, hw, tm), lambda b, t: (b, 0, t))
    head = pl.BlockSpec((None, n_head, tm), lambda b, t: (b, 0, t))
    return pl.pallas_call(
        kern,
        grid=(bsz, t_len // tm),
        in_specs=[
            pl.BlockSpec((None, tm, d), lambda b, t: (b, t, 0)),
            _resident((1, d)),
            _resident(wt.shape),
            _resident(bf_col.shape),
        ],
        out_specs=[
            pl.BlockSpec((None, n_pair, tm // tq, 2 * LANES, 2 * tq), lambda b, t: (b, 0, t, 0, 0)),
            pl.BlockSpec((None, n_pair, tm, 2 * LANES), lambda b, t: (b, 0, t, 0)),
            feat, feat,
            pl.BlockSpec((None, n_pair, VT_ROWS, tm), lambda b, t: (b, 0, 0, t)),
            head,
        ],
        out_shape=[
            jax.ShapeDtypeStruct((bsz, n_pair, t_len // tq, 2 * LANES, 2 * tq), BF16),
            jax.ShapeDtypeStruct((bsz, n_pair, t_len, 2 * LANES), BF16),
            jax.ShapeDtypeStruct((bsz, hw, t_len), F32),
            jax.ShapeDtypeStruct((bsz, hw, t_len), F32),
            jax.ShapeDtypeStruct((bsz, n_pair, VT_ROWS, t_len), BF16),
            jax.ShapeDtypeStruct((bsz, n_head, t_len), F32),
        ],
        scratch_shapes=[pltpu.VMEM((n_head, LANES), F32), pltpu.VMEM(wt.shape, BF16)],
        compiler_params=_params(2),
        name="fox_proj_prompt",
    )(x3, g, wt, bf_col)


def _fox_proj_sample_kernel(x_ref, g_ref, wtf_ref, bfc_ref, lct_ref,
                            q_ref, k_ref, v_ref, logft_ref, ccol_ref, cextn_ref, cextc_ref, wt_ref,
                            *, hw, n_head, scale):
    _cast_weights_once(pl.program_id(0) == 0, wtf_ref, wt_ref)
    nb, tm, d = x_ref.shape
    cache_len = lct_ref.shape[2]
    h = _rmsnorm(x_ref[...].reshape(nb * tm, d), g_ref[...]).astype(BF16)
    q_ref[...] = (_dot_nt(h, wt_ref[0:hw, :]) * scale).astype(BF16).reshape(nb, tm, hw)
    k_ref[...] = _dot_nt(h, wt_ref[hw:2 * hw, :]).reshape(nb, tm, hw)
    v_ref[...] = _dot_nt(h, wt_ref[2 * hw:3 * hw, :]).reshape(nb, tm, hw)

    ct_cache = _lane_cumsum(lct_ref[...].reshape(nb * n_head, cache_len), min(cache_len, MXU_TILE))
    logf_all = _log_sigmoid(_dot_nt(wt_ref[3 * hw:3 * hw + n_head, :], h) + bfc_ref[...])
    for b in range(nb):
        ct_b = ct_cache[b * n_head:(b + 1) * n_head, :]
        cextc_ref[b] = _key_ext_rows(ct_b)
        logf_t = logf_all[:, b * tm:(b + 1) * tm]
        logft_ref[b] = logf_t
        ct_new = _lane_cumsum(logf_t, tm) + ct_b[:, cache_len - 1:cache_len]
        cextn_ref[b] = _key_ext_rows(ct_new)
        ccol_ref[b] = _to_columns(ct_new)


def _fox_proj_sample(x3, g, wt, bf_col, logf_cache_t, n_head, nb):
    bsz, tm, d = x3.shape
    hw = (wt.shape[0] - n_head) // 3
    cache_len = logf_cache_t.shape[2]
    assert 4 * n_head <= LANES and bsz % nb == 0
    kern = functools.partial(_fox_proj_sample_kernel, hw=hw, n_head=n_head,
                             scale=(hw // n_head) ** -0.5 * LOG2E)
    tok = pl.BlockSpec((nb, tm, hw), lambda b: (b, 0, 0))
    return pl.pallas_call(
        kern,
        grid=(bsz // nb,),
        in_specs=[
            pl.BlockSpec((nb, tm, d), lambda b: (b, 0, 0)),
            _resident((1, d)),
            _resident(wt.shape),
            _resident(bf_col.shape),
            pl.BlockSpec((nb, n_head, cache_len), lambda b: (b, 0, 0)),
        ],
        out_specs=[
            tok, tok, tok,
            pl.BlockSpec((nb, n_head, tm), lambda b: (b, 0, 0)),
            pl.BlockSpec((nb, tm, LANES), lambda b: (b, 0, 0)),
            pl.BlockSpec((nb, LANES, tm), lambda b: (b, 0, 0)),
            pl.BlockSpec((nb, LANES, cache_len), lambda b: (b, 0, 0)),
        ],
        out_shape=[
            jax.ShapeDtypeStruct((bsz, tm, hw), BF16),
            jax.ShapeDtypeStruct((bsz, tm, hw), F32),
            jax.ShapeDtypeStruct((bsz, tm, hw), F32),
            jax.ShapeDtypeStruct((bsz, n_head, tm), F32),
            jax.ShapeDtypeStruct((bsz, tm, LANES), F32),
            jax.ShapeDtypeStruct((bsz, LANES, tm), BF16),
            jax.ShapeDtypeStruct((bsz, LANES, cache_len), BF16),
        ],
        scratch_shapes=[pltpu.VMEM(wt.shape, BF16)],
        compiler_params=_params(1),
        name="fox_proj_sample",
    )(x3, g, wt, bf_col, logf_cache_t)


def _attn_update(n_pair, kaug_of, vt_of, vt_prev_last, mask, next_kaug0,
                 qaug_ref, m_ref, acc_ref, s0_ref, plast_ref, alast_ref):
    last = n_pair - 1
    s_next = s0_ref[...]
    acc_ref[last] = acc_ref[last] * alast_ref[...] + _dot(vt_prev_last(), plast_ref[...])
    pending = None
    for p in range(n_pair):
        s = s_next
        if p + 1 < n_pair:
            s_next = _dot(kaug_of(p + 1), qaug_ref[p + 1])
        elif next_kaug0 is not None:
            s0_ref[...] = _dot(next_kaug0(), qaug_ref[0])
        if mask is not None:
            s = jnp.where(mask, s, MASK_VALUE)
        m_old = m_ref[p]
        m_new = jnp.maximum(m_old, jnp.max(s, axis=0, keepdims=True))
        alpha = jnp.exp2(m_old - m_new)
        pr = jnp.exp2(s - m_new).astype(BF16)
        m_ref[p] = m_new
        if pending is not None:
            q, pr_q, alpha_q = pending
            acc_ref[q] = acc_ref[q] * alpha_q + _dot(vt_of(q), pr_q)
        pending = (p, pr, alpha)
    plast_ref[...] = pending[1]
    alast_ref[...] = pending[2]


def _attn_finish(o_ref, acc_ref, *, n_pair, tq, hd):
    for p in range(n_pair):
        full = acc_ref[p, 0:LANES, :] * (1.0 / acc_ref[p, LANES:LANES + 1, :])
        z = jnp.concatenate([full[0:hd, 0:tq], full[hd:2 * hd, tq:2 * tq]], axis=0)
        o_ref[:, p * LANES:(p + 1) * LANES] = z.T.astype(o_ref.dtype)


def _fox_attn_prompt_kernel(qaug_ref, kaug_ref, vt_ref, o_ref,
                            m_ref, acc_ref, s0_ref, plast_ref, alast_ref, *, tq, n_pair, hd):
    i = pl.program_id(2)
    last = n_pair - 1
    state = (qaug_ref, m_ref, acc_ref, s0_ref, plast_ref, alast_ref)

    s0_ref[...] = _dot(kaug_ref[0, 0:tq, :], qaug_ref[0])
    m_ref[...] = jnp.full_like(m_ref, MASK_VALUE)
    acc_ref[...] = jnp.zeros_like(acc_ref)
    plast_ref[...] = jnp.zeros_like(plast_ref)
    alast_ref[...] = jnp.ones_like(alast_ref)

    def keys(j):
        return pl.ds(pl.multiple_of(j * tq, tq), tq)

    def vt_of(p, ks):
        return vt_ref[p, :, ks]

    def kv_block(j, mask, has_next):
        ks = keys(j)
        _attn_update(n_pair, lambda p: kaug_ref[p, ks, :], lambda p: vt_of(p, ks),
                     lambda: vt_of(last, keys(jnp.maximum(j - 1, 0))), mask,
                     (lambda: kaug_ref[0, keys(j + 1), :]) if has_next else None, *state)

    def body(j, carry):
        kv_block(j, None, True)
        return carry

    lax.fori_loop(0, i, body, 0)
    row = lax.broadcasted_iota(jnp.int32, (tq, 2 * tq), 0)
    col = lax.broadcasted_iota(jnp.int32, (tq, 2 * tq), 1) % tq
    kv_block(i, row <= col, False)
    acc_ref[last] = acc_ref[last] * alast_ref[...] + _dot(vt_of(last, keys(i)), plast_ref[...])
    _attn_finish(o_ref, acc_ref, n_pair=n_pair, tq=tq, hd=hd)


def _fox_attn_prompt(qaug, kaug, vtb, n_pair, n_head):
    bsz, n_pair_all, n_qblk, _, tq2 = qaug.shape
    tq = tq2 // 2
    t_len = n_qblk * tq
    hw = n_pair_all * LANES
    hd = hw // n_head
    assert 2 * hd == LANES and n_pair_all % n_pair == 0
    width = n_pair * LANES
    n_group = n_pair_all // n_pair
    kern = functools.partial(_fox_attn_prompt_kernel, tq=tq, n_pair=n_pair, hd=hd)
    return pl.pallas_call(
        kern,
        grid=(bsz, n_group, n_qblk),
        in_specs=[
            pl.BlockSpec((None, n_pair, None, 2 * LANES, 2 * tq), lambda b, g, i: (b, g, i, 0, 0)),
            pl.BlockSpec((None, n_pair, t_len, 2 * LANES), lambda b, g, i: (b, g, 0, 0)),
            pl.BlockSpec((None, n_pair, VT_ROWS, t_len), lambda b, g, i: (b, g, 0, 0)),
        ],
        out_specs=pl.BlockSpec((None, tq, width), lambda b, g, i: (b, i, g)),
        out_shape=jax.ShapeDtypeStruct((bsz, t_len, hw), BF16),
        scratch_shapes=[
            pltpu.VMEM((n_pair, 1, 2 * tq), F32),
            pltpu.VMEM((n_pair, VT_ROWS, 2 * tq), F32),
            pltpu.VMEM((tq, 2 * tq), F32),
            pltpu.VMEM((tq, 2 * tq), BF16),
            pltpu.VMEM((1, 2 * tq), F32),
        ],
        compiler_params=_params(3),
        name="fox_attn_prompt",
    )(qaug, kaug, vtb)


def _sample_attn_stages(q_ref, ccol_ref, kt_ref, vt_ref, cextc_ref, kn_ref, vn_ref, cextn_ref, o_ref,
                        *, g, n_pair, hd, n_head):
    tq = q_ref.shape[0]
    eye = _identity(LANES)
    lane = lax.broadcasted_iota(jnp.int32, (tq, LANES), 1)
    row = lax.broadcasted_iota(jnp.int32, (2 * tq, tq), 0) % tq
    col = lax.broadcasted_iota(jnp.int32, (2 * tq, tq), 1)
    causal_new = col <= row
    ccol = ccol_ref[...]

    def scores(p):
        feat = slice(p * LANES, (p + 1) * LANES)
        q_pair = q_ref[:, feat]
        blocks = []
        for h in range(2):
            hh = g * (2 * n_pair) + 2 * p + h
            cq = jnp.sum(jnp.where(lane == hh, ccol, 0.0), axis=1, keepdims=True)
            top = jnp.where(lane // hd == h, q_pair, jnp.zeros_like(q_pair))
            blocks.append(jnp.concatenate([top, _query_ext(lane, hh, n_head, cq)], axis=1))
        qaug = jnp.concatenate(blocks, axis=0)
        s_c = _dot(qaug, jnp.concatenate([kt_ref[feat, :].astype(BF16), cextc_ref[...]], axis=0))
        kn_t = _dot_nt(eye, kn_ref[:, feat].astype(BF16)).astype(BF16)
        s_n = _dot(qaug, jnp.concatenate([kn_t, cextn_ref[...]], axis=0))
        return s_c, jnp.where(causal_new, s_n, MASK_VALUE)

    held = {}

    def score_stage(p):
        held["s", p] = scores(p)

    def softmax_stage(p):
        s_c, s_n = held.pop(("s", p))
        m = jnp.maximum(jnp.max(s_c, axis=1, keepdims=True), jnp.max(s_n, axis=1, keepdims=True))
        p_c = jnp.exp2(s_c - m)
        p_n = jnp.exp2(s_n - m)
        l = jnp.sum(p_c, axis=1, keepdims=True) + jnp.sum(p_n, axis=1, keepdims=True)
        held["p", p] = (p_c.astype(BF16), p_n.astype(BF16), l)

    def value_stage(p):
        p_c, p_n, l = held.pop(("p", p))
        feat = slice(p * LANES, (p + 1) * LANES)
        o = _dot_nt(p_c, vt_ref[feat, :].astype(BF16)) + _dot(p_n, vn_ref[:, feat].astype(BF16))
        o = o / l
        o_ref[:, feat] = jnp.where(lane < hd, o[0:tq], o[tq:2 * tq]).astype(o_ref.dtype)

    return score_stage, softmax_stage, value_stage


def _fox_attn_sample_kernel(*refs, n_pair, hd, n_head):
    score, softmax, value = _sample_attn_stages(*refs, g=pl.program_id(1), n_pair=n_pair, hd=hd, n_head=n_head)
    score(0)
    for p in range(n_pair):
        if p + 1 < n_pair:
            score(p + 1)
        softmax(p)
        value(p)


def _post_attn_kernel(x_ref, o_ref, wo_ref, g_ref, win_ref, wdown_ref, gfin_ref,
                      q_ref, ccol_ref, kt_ref, vt_ref, cextc_ref, kn_ref, vn_ref, cextn_ref,
                      y_ref, os_ref, act_ref, *, d_ff, ff_tile, final_norm, n_group, n_pair, hd, n_head):
    score, softmax, value = _sample_attn_stages(
        q_ref, ccol_ref, kt_ref, vt_ref, cextc_ref, kn_ref, vn_ref, cextn_ref, os_ref,
        g=pl.program_id(0) % n_group, n_pair=n_pair, hd=hd, n_head=n_head)
    n_ff = d_ff // ff_tile
    lag = max(1, n_ff - 1 - n_pair) // 2 + 2
    hooks = {}

    def at(j, fn):
        j = min(j, n_ff - 1)
        prev = hooks.get(j)
        hooks[j] = fn if prev is None else (lambda: (prev(), fn()))

    for p in range(n_pair):
        at(p - 1, functools.partial(score, p))
        at(p - 1, functools.partial(softmax, p))
        at(p - 1 + lag, functools.partial(value, p))
    _post_kernel(x_ref, o_ref, wo_ref, g_ref, win_ref, wdown_ref, gfin_ref, y_ref, act_ref,
                 d_ff=d_ff, ff_tile=ff_tile, final_norm=final_norm, hooks=hooks)


def _fox_attn_sample(q, c_col, kt_cache, vt_cache, cext_cache, k_new, v_new, cext_new, n_pair, n_head):
    bsz, tq, hw = q.shape
    cache_len = kt_cache.shape[2]
    hd = hw // n_head
    assert 2 * hd == LANES and 4 * n_head <= LANES
    width = n_pair * LANES
    new_spec = pl.BlockSpec((None, tq, width), lambda b, g: (b, 0, g))
    cache_spec = pl.BlockSpec((None, width, cache_len), lambda b, g: (b, g, 0))
    kern = functools.partial(_fox_attn_sample_kernel, n_pair=n_pair, hd=hd, n_head=n_head)
    return pl.pallas_call(
        kern,
        grid=(bsz, hw // width),
        in_specs=[
            new_spec,
            pl.BlockSpec((None, tq, LANES), lambda b, g: (b, 0, 0)),
            cache_spec, cache_spec,
            pl.BlockSpec((None, LANES, cache_len), lambda b, g: (b, 0, 0)),
            new_spec, new_spec,
            pl.BlockSpec((None, LANES, tq), lambda b, g: (b, 0, 0)),
        ],
        out_specs=new_spec,
        out_shape=jax.ShapeDtypeStruct((bsz, tq, hw), BF16),
        compiler_params=_params(2),
        name="fox_attn_sample",
    )(q, c_col, kt_cache, vt_cache, cext_cache, k_new, v_new, cext_new)


def _post_attn(x2d, o2d, w_out, g_ffn, w_in_all, w_down_all, layer, g_final, final_norm,
               q, c_col, kt_cache, vt_cache, cext_cache, k_new, v_new, cext_new, n_pair, n_head):
    n, d = x2d.shape
    bsz, tq, hw = q.shape
    cache_len = kt_cache.shape[2]
    hd = hw // n_head
    width = n_pair * LANES
    n_group = hw // width
    steps = bsz * n_group
    tm = n // steps
    d_ff = w_down_all.shape[1]
    ff_tile = MXU_TILE
    assert n % steps == 0 and tm % 16 == 0 and d_ff % ff_tile == 0 and 2 * hd == LANES
    kern = functools.partial(_post_attn_kernel, d_ff=d_ff, ff_tile=ff_tile, final_norm=final_norm,
                             n_group=n_group, n_pair=n_pair, hd=hd, n_head=n_head)
    new_spec = pl.BlockSpec((None, tq, width), lambda i: (i // n_group, 0, i % n_group))
    cache_spec = pl.BlockSpec((None, width, cache_len), lambda i: (i // n_group, i % n_group, 0))
    return pl.pallas_call(
        kern,
        grid=(steps,),
        in_specs=[
            pl.BlockSpec((tm, d), lambda i: (i, 0)),
            pl.BlockSpec((tm, o2d.shape[1]), lambda i: (i, 0)),
            _resident(w_out.shape),
            _resident((1, d)),
            pl.BlockSpec((None,) + w_in_all.shape[1:], lambda i: (layer, 0, 0), pipeline_mode=pl.Buffered(1)),
            pl.BlockSpec((None,) + w_down_all.shape[1:], lambda i: (layer, 0, 0), pipeline_mode=pl.Buffered(1)),
            _resident((1, d)),
            new_spec,
            pl.BlockSpec((None, tq, LANES), lambda i: (i // n_group, 0, 0)),
            cache_spec, cache_spec,
            pl.BlockSpec((None, LANES, cache_len), lambda i: (i // n_group, 0, 0)),
            new_spec, new_spec,
            pl.BlockSpec((None, LANES, tq), lambda i: (i // n_group, 0, 0)),
        ],
        out_specs=[pl.BlockSpec((tm, d), lambda i: (i, 0)), new_spec],
        out_shape=[jax.ShapeDtypeStruct((n, d), F32), jax.ShapeDtypeStruct((bsz, tq, hw), BF16)],
        scratch_shapes=[pltpu.VMEM((tm, d_ff), BF16)],
        compiler_params=_params(1),
        name="post_attn",
    )(x2d, o2d, w_out, g_ffn, w_in_all, w_down_all, g_final,
      q, c_col, kt_cache, vt_cache, cext_cache, k_new, v_new, cext_new)


def kernel(x_prompt, x_sample, state_gla, cache_fox_k, cache_fox_v, cache_fox_logf,
           norm_mix, gla_w_in, gla_w_g2, gla_b_g, gla_norm, gla_w_out,
           fox_w_in, fox_b_f, fox_w_out, norm_ffn, ffn_w_in, ffn_w_down, norm_final):
    d = x_prompt.shape[-1]
    depth = norm_mix.shape[0]
    groups = [x_prompt, x_sample]
    shapes = [x.shape for x in groups]
    xs = [x.reshape(-1, d) for x in groups]
    row_tiles = [min(ROW_TILE, x.shape[0]) for x in xs]
    w_ffn_in = ffn_w_in.astype(BF16)
    w_ffn_down = ffn_w_down.astype(BF16)

    gla_states = [[], []]
    fox_k, fox_v, fox_f = [[], []], [[], []], [[], []]
    for i in range(depth):
        j = i // 2
        g_mix = norm_mix[i].reshape(1, d)
        g_ffn = norm_ffn[i].reshape(1, d)
        last = i == depth - 1
        if i % 2 == 0:
            _, n_head, dk, dv = state_gla.shape[1:]
            hk, hv = n_head * dk, n_head * dv
            n_main = 2 * hk + 2 * hv
            w_in = gla_w_in[j]
            wt = w_in.T
            w_g2 = gla_w_g2[j].astype(BF16)
            b_g = gla_b_g[j].reshape(1, hk)
            w_out = gla_w_out[j].astype(BF16)
            norm_g = gla_norm[j].reshape(1, hv)
            s0s = [jnp.zeros((shapes[0][0], n_head, dk, dv), F32), state_gla[j]]
            for gi in range(2):
                bsz, t_len, _ = shapes[gi]
                proj, glog = _gla_proj(xs[gi], g_mix, wt, w_g2, b_g, row_tiles[gi])
                og, s_fin = _gla_mix(proj.reshape(bsz, t_len, n_main), glog.reshape(bsz, t_len, hk),
                                     s0s[gi], norm_g, min(t_len, GLA_ROWS),
                                     min(bsz, GLA_SEQS_PROMPT if gi == 0 else GLA_SEQS_SAMPLE))
                gla_states[gi].append(s_fin)
                xs[gi] = _post(xs[gi], og.reshape(-1, hv), w_out, g_ffn, w_ffn_in, w_ffn_down, i,
                               norm_final.reshape(1, d), row_tiles[gi], last)
        else:
            n_head = fox_b_f.shape[1]
            hw = fox_w_out.shape[1]
            hd = hw // n_head
            wt = fox_w_in[j].T
            bf_col = fox_b_f[j].reshape(n_head, 1)
            w_out = fox_w_out[j].astype(BF16)
            g_fin = norm_final.reshape(1, d)
            bsz, t_len, _ = shapes[0]
            qaug, kaug, kt, vt, vtb, logf_p = _fox_proj_prompt(xs[0].reshape(bsz, t_len, d), g_mix, wt, bf_col,
                                                               min(ROW_TILE, t_len), ATTN_TQ, n_head)
            o_p = _fox_attn_prompt(qaug, kaug, vtb, ATTN_PAIRS, n_head)
            fox_k[0].append(kt.reshape(bsz, n_head, hd, t_len).transpose(0, 3, 1, 2))
            fox_v[0].append(vt.reshape(bsz, n_head, hd, t_len).transpose(0, 3, 1, 2))
            fox_f[0].append(jnp.transpose(logf_p, (0, 2, 1)))
            bsz, t_len, _ = shapes[1]
            cache_len = cache_fox_logf.shape[2]
            q, k, v, logf_s, c_col, cext_new, cext_cache = _fox_proj_sample(
                xs[1].reshape(bsz, t_len, d), g_mix, wt, bf_col,
                jnp.transpose(cache_fox_logf[j], (0, 2, 1)), n_head, min(PROJ_SAMPLE_SEQS, bsz))
            kt_cache = jnp.transpose(cache_fox_k[j], (0, 2, 3, 1)).reshape(bsz, hw, cache_len)
            vt_cache = jnp.transpose(cache_fox_v[j], (0, 2, 3, 1)).reshape(bsz, hw, cache_len)
            attn_args = (q, c_col, kt_cache, vt_cache, cext_cache, k, v, cext_new)
            steps = bsz * (hw // (FUSED_ATTN_PAIRS * LANES))
            if xs[0].shape[0] % steps == 0 and (xs[0].shape[0] // steps) % LANES == 0:
                xs[0], o_s = _post_attn(xs[0], o_p.reshape(-1, hw), w_out, g_ffn, w_ffn_in, w_ffn_down, i,
                                        g_fin, last, *attn_args, FUSED_ATTN_PAIRS, n_head)
            else:
                xs[0] = _post(xs[0], o_p.reshape(-1, hw), w_out, g_ffn, w_ffn_in, w_ffn_down, i,
                              g_fin, row_tiles[0], last)
                o_s = _fox_attn_sample(*attn_args, hw // LANES, n_head)
            fox_k[1].append(k.reshape(bsz, t_len, n_head, hd))
            fox_v[1].append(v.reshape(bsz, t_len, n_head, hd))
            fox_f[1].append(jnp.transpose(logf_s, (0, 2, 1)))
            xs[1] = _post(xs[1], o_s.reshape(-1, hw), w_out, g_ffn, w_ffn_in, w_ffn_down, i,
                          g_fin, row_tiles[1], last)

    y_prompt = xs[0].reshape(shapes[0])
    y_sample = xs[1].reshape(shapes[1])
    st = lambda parts: jnp.stack(parts, axis=0)
    return (y_prompt, y_sample, st(gla_states[0]), st(fox_k[0]), st(fox_v[0]), st(fox_f[0]),
            st(gla_states[1]), st(fox_k[1]), st(fox_v[1]), st(fox_f[1]))
```

```python
import functools

import jax
import jax.numpy as jnp
from jax import lax
from jax.experimental import pallas as pl
from jax.experimental.pallas import tpu as pltpu

F32 = jnp.float32
BF16 = jnp.bfloat16

EPS = 1e-6
MASK_VALUE = -1e30
LOG2E = 1.4426950408889634

GLA_HEADS = 4
GLA_CHUNK = 64
GLA_GATE_TAU = 16.0

LANES = 128
MXU_TILE = 256
VT_ROWS = LANES + 16
VMEM_LIMIT_BYTES = 56 * 1024 * 1024

ROW_TILE = 512
GLA_ROWS = 256
GLA_SEQS_PROMPT = 2
GLA_SEQS_SAMPLE = 4
ATTN_TQ = 256
ATTN_PAIRS = 8
PROJ_SAMPLE_SEQS = 8
FUSED_ATTN_PAIRS = 4


def _params(n_grid):
    return pltpu.CompilerParams(
        dimension_semantics=("arbitrary",) * n_grid,
        vmem_limit_bytes=VMEM_LIMIT_BYTES,
    )


def _resident(shape):
    nd = len(shape)
    return pl.BlockSpec(shape, lambda *_: (0,) * nd, pipeline_mode=pl.Buffered(1))


def _dot(a, b):
    return jnp.dot(a, b, preferred_element_type=F32)


def _dot_nt(a, b):
    return lax.dot_general(a, b, (((1,), (1,)), ((), ())), preferred_element_type=F32)


def _dot_tn(a, b):
    return lax.dot_general(a, b, (((0,), (0,)), ((), ())), preferred_element_type=F32)


def _split3(x):
    hi = x.astype(BF16)
    r1 = x - hi.astype(F32)
    mid = r1.astype(BF16)
    lo = (r1 - mid.astype(F32)).astype(BF16)
    return hi, mid, lo


def _sum01(dot_fn, x, ones_first, mat01):
    acc = None
    for part in _split3(x):
        term = dot_fn(mat01, part) if ones_first else dot_fn(part, mat01)
        acc = term if acc is None else acc + term
    return acc


def _rmsnorm(x, g):
    var = jnp.mean(x * x, axis=-1, keepdims=True)
    return x * lax.rsqrt(var + EPS) * g


def _log_sigmoid(z):
    return jnp.minimum(z, 0.0) - jnp.log1p(jnp.exp(-jnp.abs(z)))


def _silu(z):
    return z * jax.nn.sigmoid(z)


def _identity(n):
    r = lax.broadcasted_iota(jnp.int32, (n, n), 0)
    c = lax.broadcasted_iota(jnp.int32, (n, n), 1)
    return (r == c).astype(BF16)


def _cast_weights_once(first_step, wt_ref, wtb_ref):
    @pl.when(first_step)
    def _():
        wtb_ref[...] = wt_ref[...].astype(BF16)


def _gla_proj_kernel(x_ref, g_ref, wt_ref, wg2_ref, bg_ref, *rest, n_main, n_ride):
    ride_in, (proj_ref, glog_ref) = rest[:n_ride], rest[n_ride:n_ride + 2]
    ride_out, wtb_ref = rest[n_ride + 2:2 * n_ride + 2], rest[-1]
    _cast_weights_once(pl.program_id(0) == 0, wt_ref, wtb_ref)
    rank = wg2_ref.shape[0]
    h = _rmsnorm(x_ref[...], g_ref[...]).astype(BF16)
    proj_ref[...] = _dot_nt(h, wtb_ref[0:n_main, :])
    gl_t = _dot_nt(wtb_ref[n_main:n_main + rank, :], h).astype(BF16)
    z = _dot_tn(gl_t, wg2_ref[...]) + bg_ref[...]
    glog_ref[...] = _log_sigmoid(z) / GLA_GATE_TAU
    for src, dst in zip(ride_in, ride_out):
        dst[...] = src[...].astype(BF16)


def _rider_rows(a, steps):
    layers, rows, _ = a.shape
    if steps % layers or rows % (steps // layers) or (rows // (steps // layers)) % 16:
        return None
    return rows // (steps // layers)


def _gla_proj(x2d, g, wt, w_g2, b_g, tm, riders=()):
    n, d = x2d.shape
    rank, hk = w_g2.shape
    n_main = wt.shape[0] - rank
    steps = n // tm
    ride_specs = []
    for a in riders:
        per = steps // a.shape[0]
        ride_specs.append(pl.BlockSpec((1, _rider_rows(a, steps), a.shape[2]),
                                       lambda i, per=per: (i // per, i % per, 0)))
    return pl.pallas_call(
        functools.partial(_gla_proj_kernel, n_main=n_main, n_ride=len(riders)),
        grid=(steps,),
        in_specs=[
            pl.BlockSpec((tm, d), lambda i: (i, 0)),
            _resident((1, d)),
            _resident(wt.shape),
            _resident(w_g2.shape),
            _resident((1, hk)),
        ] + ride_specs,
        out_specs=[
            pl.BlockSpec((tm, n_main), lambda i: (i, 0)),
            pl.BlockSpec((tm, hk), lambda i: (i, 0)),
        ] + ride_specs,
        out_shape=[
            jax.ShapeDtypeStruct((n, n_main), F32),
            jax.ShapeDtypeStruct((n, hk), F32),
        ] + [jax.ShapeDtypeStruct(a.shape, BF16) for a in riders],
        scratch_shapes=[pltpu.VMEM(wt.shape, BF16)],
        compiler_params=_params(1),
        name="gla_proj",
    )(x2d, g, wt, w_g2, b_g, *riders)


def _gla_mix_kernel(q_ref, k_ref, v_ref, r_ref, glog_ref, s0_ref, ng_ref,
                    og_ref, sfin_ref, s_ref, *, dk, dv):
    t = pl.program_id(1)
    nb, tb, _ = q_ref.shape
    n_chunk = tb // GLA_CHUNK
    seqs = range(nb)
    heads = range(GLA_HEADS)
    units = [(sb, h) for sb in seqs for h in heads]
    chunks = [slice(c * GLA_CHUNK, (c + 1) * GLA_CHUNK) for c in range(n_chunk)]
    ksl = [slice(h * dk, (h + 1) * dk) for h in heads]
    vsl = [slice(h * dv, (h + 1) * dv) for h in heads]

    @pl.when(t == 0)
    def _():
        s_ref[...] = s0_ref[...]

    row = lax.broadcasted_iota(jnp.int32, (tb, tb), 0)
    col = lax.broadcasted_iota(jnp.int32, (tb, tb), 1)
    same_chunk = (row // GLA_CHUNK) == (col // GLA_CHUNK)
    causal = same_chunk & (col <= row)
    cum_mat = causal.astype(BF16)

    b = [_sum01(_dot, glog_ref[sb], True, cum_mat) for sb in seqs]
    qe, ke, kd, dec_t = [], [], [], []
    for sb in seqs:
        tot_rows = [b[sb][(c + 1) * GLA_CHUNK - 1:(c + 1) * GLA_CHUNK, :] for c in range(n_chunk)]
        b_last = jnp.concatenate([jnp.broadcast_to(r, (GLA_CHUNK, r.shape[1])) for r in tot_rows], axis=0)
        pad_rows = [jnp.zeros_like(tot_rows[0])] * (8 - n_chunk % 8 if n_chunk % 8 else 0)
        dec_t.append(jnp.exp(_to_columns(jnp.concatenate(tot_rows + pad_rows, axis=0))))
        q = q_ref[sb]
        k = k_ref[sb]
        qe.append((q * jnp.exp(b[sb]) * (dk ** -0.5)).astype(BF16))
        ke.append((k * jnp.exp(-b[sb])).astype(BF16))
        kd.append((k * jnp.exp(b_last - b[sb])).astype(BF16))

    v_b = {u: v_ref[u[0], :, vsl[u[1]]].astype(BF16) for u in units}
    a_raw = {(sb, h): _dot_nt(qe[sb][:, ksl[h]], ke[sb][:, ksl[h]]) for sb, h in units}
    upd = {(sb, h): [_dot_tn(kd[sb][rs, ksl[h]], v_b[sb, h][rs]) for rs in chunks] for sb, h in units}
    o_intra = {u: _dot(jnp.where(causal, a_raw[u], 0.0).astype(BF16), v_b[u]) for u in units}
    s_in = {}
    for sb, h in units:
        s = s_ref[sb, h]
        s_in[sb, h] = []
        for c in range(n_chunk):
            s_in[sb, h].append(s.astype(BF16))
            s = s * dec_t[sb][ksl[h], c:c + 1] + upd[sb, h][c]
        s_ref[sb, h] = s
    for sb, h in units:
        o_parts = [o_intra[sb, h][rs] + _dot(qe[sb][rs, ksl[h]], s_in[sb, h][c]) for c, rs in enumerate(chunks)]
        o = o_parts[0] if n_chunk == 1 else jnp.concatenate(o_parts, axis=0)
        on = _rmsnorm(o, ng_ref[:, vsl[h]])
        og_ref[sb, :, vsl[h]] = (on * _silu(r_ref[sb, :, vsl[h]])).astype(BF16)

    @pl.when(t == pl.num_programs(1) - 1)
    def _():
        sfin_ref[...] = s_ref[...]


def _gla_mix(proj3, glog3, s0, norm_g, tb, nb):
    bsz, t_len, _ = proj3.shape
    _, n_head, dk, dv = s0.shape
    hk, hv = n_head * dk, n_head * dv
    assert t_len % tb == 0 and tb % GLA_CHUNK == 0 and hv == 2 * hk and bsz % nb == 0
    kern = functools.partial(_gla_mix_kernel, dk=dk, dv=dv)
    state_spec = pl.BlockSpec((nb, n_head, dk, dv), lambda b, t: (b, 0, 0, 0))
    return pl.pallas_call(
        kern,
        grid=(bsz // nb, t_len // tb),
        in_specs=[
            pl.BlockSpec((nb, tb, hk), lambda b, t: (b, t, 0)),
            pl.BlockSpec((nb, tb, hk), lambda b, t: (b, t, 1)),
            pl.BlockSpec((nb, tb, hv), lambda b, t: (b, t, 1)),
            pl.BlockSpec((nb, tb, hv), lambda b, t: (b, t, 2)),
            pl.BlockSpec((nb, tb, hk), lambda b, t: (b, t, 0)),
            state_spec,
            _resident((1, hv)),
        ],
        out_specs=[
            pl.BlockSpec((nb, tb, hv), lambda b, t: (b, t, 0)),
            state_spec,
        ],
        out_shape=[
            jax.ShapeDtypeStruct((bsz, t_len, hv), BF16),
            jax.ShapeDtypeStruct(s0.shape, F32),
        ],
        scratch_shapes=[pltpu.VMEM((nb, n_head, dk, dv), F32)],
        compiler_params=_params(2),
        name="gla_mix",
    )(proj3, proj3, proj3, proj3, glog3, s0, norm_g)


def _post_kernel(x_ref, o_ref, wo_ref, g_ref, win_ref, wdown_ref, gfin_ref, y_ref, act_ref,
                 *, d_ff, ff_tile, final_norm, hooks=None):
    x1 = x_ref[...] + _dot(o_ref[...], wo_ref[...])
    if hooks and -1 in hooks:
        hooks[-1]()
    h = _rmsnorm(x1, g_ref[...]).astype(BF16)
    for j in range(d_ff // ff_tile):
        gate = _dot(h, win_ref[:, j * ff_tile:(j + 1) * ff_tile])
        up = _dot(h, win_ref[:, d_ff + j * ff_tile:d_ff + (j + 1) * ff_tile])
        act_ref[:, j * ff_tile:(j + 1) * ff_tile] = (_silu(gate) * up).astype(BF16)
        if hooks and j in hooks:
            hooks[j]()
    y = x1 + _dot(act_ref[...], wdown_ref[...])
    if final_norm:
        y = _rmsnorm(y, gfin_ref[...])
    y_ref[...] = y


def _post(x2d, o2d, w_out, g_ffn, w_in_all, w_down_all, layer, g_final, tm, final_norm):
    n, d = x2d.shape
    d_ff = w_down_all.shape[1]
    ff_tile = MXU_TILE
    assert d_ff % ff_tile == 0 and n % tm == 0
    kern = functools.partial(_post_kernel, d_ff=d_ff, ff_tile=ff_tile, final_norm=final_norm)
    return pl.pallas_call(
        kern,
        grid=(n // tm,),
        in_specs=[
            pl.BlockSpec((tm, d), lambda i: (i, 0)),
            pl.BlockSpec((tm, o2d.shape[1]), lambda i: (i, 0)),
            _resident(w_out.shape),
            _resident((1, d)),
            pl.BlockSpec((None,) + w_in_all.shape[1:], lambda i: (layer, 0, 0), pipeline_mode=pl.Buffered(1)),
            pl.BlockSpec((None,) + w_down_all.shape[1:], lambda i: (layer, 0, 0), pipeline_mode=pl.Buffered(1)),
            _resident((1, d)),
        ],
        out_specs=pl.BlockSpec((tm, d), lambda i: (i, 0)),
        out_shape=jax.ShapeDtypeStruct((n, d), F32),
        scratch_shapes=[pltpu.VMEM((tm, d_ff), BF16)],
        compiler_params=_params(1),
        name="post_final" if final_norm else "post",
    )(x2d, o2d, w_out, g_ffn, w_in_all, w_down_all, g_final)


def _lane_cumsum(x, tile):
    r = lax.broadcasted_iota(jnp.int32, (tile, tile), 0)
    c = lax.broadcasted_iota(jnp.int32, (tile, tile), 1)
    upper = (r <= c).astype(BF16)
    rows = x.shape[0]
    carry = jnp.zeros((rows, 1), F32)
    out = []
    for j in range(x.shape[1] // tile):
        parts = jnp.concatenate(_split3(x[:, j * tile:(j + 1) * tile]), axis=0)
        s = _dot(parts, upper)
        blk = carry + s[0:rows] + s[rows:2 * rows] + s[2 * rows:3 * rows]
        out.append(blk)
        carry = blk[:, tile - 1:tile]
    return out[0] if len(out) == 1 else jnp.concatenate(out, axis=1)


def _to_columns(x_t):
    rows = x_t.shape[0]
    parts = jnp.concatenate(_split3(x_t), axis=0)
    r = lax.broadcasted_iota(jnp.int32, (3 * rows, LANES), 0)
    c = lax.broadcasted_iota(jnp.int32, (3 * rows, LANES), 1)
    place = ((r % rows) == c).astype(BF16)
    return _dot_tn(parts, place)


def _key_ext_cols(c_cols, n_head):
    hi, mid, lo = (p.astype(F32) for p in _split3(c_cols * LOG2E))
    lane = lax.broadcasted_iota(jnp.int32, c_cols.shape, 1)
    ones = ((lane >= 3 * n_head) & (lane < 3 * n_head + 3)).astype(F32)
    ext = hi + pltpu.roll(mid, n_head, 1) + pltpu.roll(lo, 2 * n_head, 1) + ones
    return ext.astype(BF16)


def _key_ext_rows(c_rows):
    n_head, n = c_rows.shape
    hi, mid, lo = _split3(c_rows * LOG2E)
    r = lax.broadcasted_iota(jnp.int32, (n_head, n), 0)
    ones = (r < 3).astype(BF16)
    zeros = jnp.zeros((LANES - 4 * n_head, n), BF16)
    return jnp.concatenate([hi, mid, lo, ones, zeros], axis=0)


def _query_ext(idx, hh, n_head, cq):
    cq_hi, cq_mid, cq_lo = (p.astype(F32) for p in _split3(cq * LOG2E))
    ext = jnp.where((idx == hh) | (idx == n_head + hh) | (idx == 2 * n_head + hh), -1.0, 0.0)
    ext = jnp.where(idx == 3 * n_head, cq_hi, ext)
    ext = jnp.where(idx == 3 * n_head + 1, cq_mid, ext)
    ext = jnp.where(idx == 3 * n_head + 2, cq_lo, ext)
    return ext.astype(BF16)


def _fox_proj_prompt_kernel(x_ref, g_ref, wtf_ref, bfc_ref,
                            qaug_ref, kaug_ref, kt_ref, vt_ref, vtb_ref, logft_ref, carry_ref, wt_ref,
                            *, hw, n_head, scale, tq):
    t = pl.program_id(1)
    tm = x_ref.shape[0]
    _cast_weights_once((pl.program_id(0) == 0) & (t == 0), wtf_ref, wt_ref)

    @pl.when(t == 0)
    def _():
        carry_ref[...] = jnp.zeros_like(carry_ref)

    h = _rmsnorm(x_ref[...], g_ref[...]).astype(BF16)
    logf_t = _log_sigmoid(_dot_nt(wt_ref[3 * hw:3 * hw + n_head, :], h) + bfc_ref[...])
    logft_ref[...] = logf_t
    ct_blk = _lane_cumsum(logf_t, min(tm, MXU_TILE)) + carry_ref[:, 0:1]
    carry_ref[...] = jnp.broadcast_to(ct_blk[:, tm - 1:tm], carry_ref.shape)
    cext = _key_ext_cols(_to_columns(ct_blk), n_head)
    kt = _dot_nt(wt_ref[hw:2 * hw, :], h)
    vt = _dot_nt(wt_ref[2 * hw:3 * hw, :], h)
    qt = (_dot_nt(wt_ref[0:hw, :], h) * scale).astype(BF16)
    kt_ref[...] = kt
    hd = hw // n_head
    ext_rows = 4 * n_head
    rr = lax.broadcasted_iota(jnp.int32, (ext_rows, tq), 0)
    rq = lax.broadcasted_iota(jnp.int32, (LANES, tq), 0) // hd
    zeros = jnp.zeros((LANES - ext_rows, tq), BF16)
    for p in range(hw // LANES):
        for jq in range(tm // tq):
            qs = slice(jq * tq, (jq + 1) * tq)
            q_pair = qt[p * LANES:(p + 1) * LANES, qs]
            cols = []
            for hh in (2 * p, 2 * p + 1):
                top = jnp.where(rq == hh - 2 * p, q_pair, jnp.zeros_like(q_pair))
                cols.append(jnp.concatenate([top, _query_ext(rr, hh, n_head, ct_blk[hh:hh + 1, qs]), zeros], axis=0))
            qaug_ref[p, jq] = jnp.concatenate(cols, axis=1)
    k = kt.T
    for p in range(hw // LANES):
        kaug_ref[p, :, 0:LANES] = k[:, p * LANES:(p + 1) * LANES].astype(BF16)
        kaug_ref[p, :, LANES:2 * LANES] = cext
    vt_ref[...] = vt
    ones = jnp.ones((VT_ROWS - LANES, tm), BF16)
    for p in range(hw // LANES):
        vtb_ref[p, 0:LANES, :] = vt[p * LANES:(p + 1) * LANES, :].astype(BF16)
        vtb_ref[p, LANES:VT_ROWS, :] = ones


def _fox_proj_prompt(x3, g, wt, bf_col, tm, tq, n_head):
    bsz, t_len, d = x3.shape
    hw = (wt.shape[0] - n_head) // 3
    n_pair = hw // LANES
    assert t_len % tm == 0 and tm % tq == 0 and tq % LANES == 0 and 3 * n_head + 3 <= LANES
    kern = functools.partial(_fox_proj_prompt_kernel, hw=hw, n_head=n_head,
                             scale=(hw // n_head) ** -0.5 * LOG2E, tq=tq)
    feat = pl.BlockSpec((None, hw, tm), lambda b, t: (b, 0, t))
    head = pl.BlockSpec((None, n_head, tm), lambda b, t: (b, 0, t))
    return pl.pallas_call(
        kern,
        grid=(bsz, t_len // tm),
        in_specs=[
            pl.BlockSpec((None, tm, d), lambda b, t: (b, t, 0)),
            _resident((1, d)),
            _resident(wt.shape),
            _resident(bf_col.shape),
        ],
        out_specs=[
            pl.BlockSpec((None, n_pair, tm // tq, 2 * LANES, 2 * tq), lambda b, t: (b, 0, t, 0, 0)),
            pl.BlockSpec((None, n_pair, tm, 2 * LANES), lambda b, t: (b, 0, t, 0)),
            feat, feat,
            pl.BlockSpec((None, n_pair, VT_ROWS, tm), lambda b, t: (b, 0, 0, t)),
            head,
        ],
        out_shape=[
            jax.ShapeDtypeStruct((bsz, n_pair, t_len // tq, 2 * LANES, 2 * tq), BF16),
            jax.ShapeDtypeStruct((bsz, n_pair, t_len, 2 * LANES), BF16),
            jax.ShapeDtypeStruct((bsz, hw, t_len), F32),
            jax.ShapeDtypeStruct((bsz, hw, t_len), F32),
            jax.ShapeDtypeStruct((bsz, n_pair, VT_ROWS, t_len), BF16),
            jax.ShapeDtypeStruct((bsz, n_head, t_len), F32),
        ],
        scratch_shapes=[pltpu.VMEM((n_head, LANES), F32), pltpu.VMEM(wt.shape, BF16)],
        compiler_params=_params(2),
        name="fox_proj_prompt",
    )(x3, g, wt, bf_col)


def _fox_proj_sample_kernel(x_ref, g_ref, wtf_ref, bfc_ref, lct_ref,
                            q_ref, k_ref, v_ref, logft_ref, ccol_ref, cextn_ref, cextc_ref, wt_ref,
                            *, hw, n_head, scale):
    _cast_weights_once(pl.program_id(0) == 0, wtf_ref, wt_ref)
    nb, tm, d = x_ref.shape
    cache_len = lct_ref.shape[2]
    h = _rmsnorm(x_ref[...].reshape(nb * tm, d), g_ref[...]).astype(BF16)
    q_ref[...] = (_dot_nt(h, wt_ref[0:hw, :]) * scale).astype(BF16).reshape(nb, tm, hw)
    k_ref[...] = _dot_nt(h, wt_ref[hw:2 * hw, :]).reshape(nb, tm, hw)
    v_ref[...] = _dot_nt(h, wt_ref[2 * hw:3 * hw, :]).reshape(nb, tm, hw)

    ct_cache = _lane_cumsum(lct_ref[...].reshape(nb * n_head, cache_len), min(cache_len, MXU_TILE))
    logf_all = _log_sigmoid(_dot_nt(wt_ref[3 * hw:3 * hw + n_head, :], h) + bfc_ref[...])
    for b in range(nb):
        ct_b = ct_cache[b * n_head:(b + 1) * n_head, :]
        cextc_ref[b] = _key_ext_rows(ct_b)
        logf_t = logf_all[:, b * tm:(b + 1) * tm]
        logft_ref[b] = logf_t
        ct_new = _lane_cumsum(logf_t, tm) + ct_b[:, cache_len - 1:cache_len]
        cextn_ref[b] = _key_ext_rows(ct_new)
        ccol_ref[b] = _to_columns(ct_new)


def _fox_proj_sample(x3, g, wt, bf_col, logf_cache_t, n_head, nb):
    bsz, tm, d = x3.shape
    hw = (wt.shape[0] - n_head) // 3
    cache_len = logf_cache_t.shape[2]
    assert 4 * n_head <= LANES and bsz % nb == 0
    kern = functools.partial(_fox_proj_sample_kernel, hw=hw, n_head=n_head,
                             scale=(hw // n_head) ** -0.5 * LOG2E)
    tok = pl.BlockSpec((nb, tm, hw), lambda b: (b, 0, 0))
    return pl.pallas_call(
        kern,
        grid=(bsz // nb,),
        in_specs=[
            pl.BlockSpec((nb, tm, d), lambda b: (b, 0, 0)),
            _resident((1, d)),
            _resident(wt.shape),
            _resident(bf_col.shape),
            pl.BlockSpec((nb, n_head, cache_len), lambda b: (b, 0, 0)),
        ],
        out_specs=[
            tok, tok, tok,
            pl.BlockSpec((nb, n_head, tm), lambda b: (b, 0, 0)),
            pl.BlockSpec((nb, tm, LANES), lambda b: (b, 0, 0)),
            pl.BlockSpec((nb, LANES, tm), lambda b: (b, 0, 0)),
            pl.BlockSpec((nb, LANES, cache_len), lambda b: (b, 0, 0)),
        ],
        out_shape=[
            jax.ShapeDtypeStruct((bsz, tm, hw), BF16),
            jax.ShapeDtypeStruct((bsz, tm, hw), F32),
            jax.ShapeDtypeStruct((bsz, tm, hw), F32),
            jax.ShapeDtypeStruct((bsz, n_head, tm), F32),
            jax.ShapeDtypeStruct((bsz, tm, LANES), F32),
            jax.ShapeDtypeStruct((bsz, LANES, tm), BF16),
            jax.ShapeDtypeStruct((bsz, LANES, cache_len), BF16),
        ],
        scratch_shapes=[pltpu.VMEM(wt.shape, BF16)],
        compiler_params=_params(1),
        name="fox_proj_sample",
    )(x3, g, wt, bf_col, logf_cache_t)


def _attn_update(n_pair, kaug_of, vt_of, vt_prev_last, mask, next_kaug0,
                 qaug_ref, m_ref, acc_ref, s0_ref, plast_ref, alast_ref):
    last = n_pair - 1
    s_next = s0_ref[...]
    acc_ref[last] = acc_ref[last] * alast_ref[...] + _dot(vt_prev_last(), plast_ref[...])
    pending = None
    for p in range(n_pair):
        s = s_next
        if p + 1 < n_pair:
            s_next = _dot(kaug_of(p + 1), qaug_ref[p + 1])
        elif next_kaug0 is not None:
            s0_ref[...] = _dot(next_kaug0(), qaug_ref[0])
        if mask is not None:
            s = jnp.where(mask, s, MASK_VALUE)
        m_old = m_ref[p]
        m_new = jnp.maximum(m_old, jnp.max(s, axis=0, keepdims=True))
        alpha = jnp.exp2(m_old - m_new)
        pr = jnp.exp2(s - m_new).astype(BF16)
        m_ref[p] = m_new
        if pending is not None:
            q, pr_q, alpha_q = pending
            acc_ref[q] = acc_ref[q] * alpha_q + _dot(vt_of(q), pr_q)
        pending = (p, pr, alpha)
    plast_ref[...] = pending[1]
    alast_ref[...] = pending[2]


def _attn_finish(o_ref, acc_ref, *, n_pair, tq, hd):
    for p in range(n_pair):
        full = acc_ref[p, 0:LANES, :] * (1.0 / acc_ref[p, LANES:LANES + 1, :])
        z = jnp.concatenate([full[0:hd, 0:tq], full[hd:2 * hd, tq:2 * tq]], axis=0)
        o_ref[:, p * LANES:(p + 1) * LANES] = z.T.astype(o_ref.dtype)


def _fox_attn_prompt_kernel(qaug_ref, kaug_ref, vt_ref, o_ref,
                            m_ref, acc_ref, s0_ref, plast_ref, alast_ref, *, tq, n_pair, hd):
    i = pl.program_id(2)
    last = n_pair - 1
    state = (qaug_ref, m_ref, acc_ref, s0_ref, plast_ref, alast_ref)

    s0_ref[...] = _dot(kaug_ref[0, 0:tq, :], qaug_ref[0])
    m_ref[...] = jnp.full_like(m_ref, MASK_VALUE)
    acc_ref[...] = jnp.zeros_like(acc_ref)
    plast_ref[...] = jnp.zeros_like(plast_ref)
    alast_ref[...] = jnp.ones_like(alast_ref)

    def keys(j):
        return pl.ds(pl.multiple_of(j * tq, tq), tq)

    def vt_of(p, ks):
        return vt_ref[p, :, ks]

    def kv_block(j, mask, has_next):
        ks = keys(j)
        _attn_update(n_pair, lambda p: kaug_ref[p, ks, :], lambda p: vt_of(p, ks),
                     lambda: vt_of(last, keys(jnp.maximum(j - 1, 0))), mask,
                     (lambda: kaug_ref[0, keys(j + 1), :]) if has_next else None, *state)

    def body(j, carry):
        kv_block(j, None, True)
        return carry

    lax.fori_loop(0, i, body, 0)
    row = lax.broadcasted_iota(jnp.int32, (tq, 2 * tq), 0)
    col = lax.broadcasted_iota(jnp.int32, (tq, 2 * tq), 1) % tq
    kv_block(i, row <= col, False)
    acc_ref[last] = acc_ref[last] * alast_ref[...] + _dot(vt_of(last, keys(i)), plast_ref[...])
    _attn_finish(o_ref, acc_ref, n_pair=n_pair, tq=tq, hd=hd)


def _fox_attn_prompt(qaug, kaug, vtb, n_pair, n_head):
    bsz, n_pair_all, n_qblk, _, tq2 = qaug.shape
    tq = tq2 // 2
    t_len = n_qblk * tq
    hw = n_pair_all * LANES
    hd = hw // n_head
    assert 2 * hd == LANES and n_pair_all % n_pair == 0
    width = n_pair * LANES
    n_group = n_pair_all // n_pair
    kern = functools.partial(_fox_attn_prompt_kernel, tq=tq, n_pair=n_pair, hd=hd)
    return pl.pallas_call(
        kern,
        grid=(bsz, n_group, n_qblk),
        in_specs=[
            pl.BlockSpec((None, n_pair, None, 2 * LANES, 2 * tq), lambda b, g, i: (b, g, i, 0, 0)),
            pl.BlockSpec((None, n_pair, t_len, 2 * LANES), lambda b, g, i: (b, g, 0, 0)),
            pl.BlockSpec((None, n_pair, VT_ROWS, t_len), lambda b, g, i: (b, g, 0, 0)),
        ],
        out_specs=pl.BlockSpec((None, tq, width), lambda b, g, i: (b, i, g)),
        out_shape=jax.ShapeDtypeStruct((bsz, t_len, hw), BF16),
        scratch_shapes=[
            pltpu.VMEM((n_pair, 1, 2 * tq), F32),
            pltpu.VMEM((n_pair, VT_ROWS, 2 * tq), F32),
            pltpu.VMEM((tq, 2 * tq), F32),
            pltpu.VMEM((tq, 2 * tq), BF16),
            pltpu.VMEM((1, 2 * tq), F32),
        ],
        compiler_params=_params(3),
        name="fox_attn_prompt",
    )(qaug, kaug, vtb)


def _sample_attn_stages(q_ref, ccol_ref, kt_ref, vt_ref, cextc_ref, kn_ref, vn_ref, cextn_ref, o_ref,
                        *, g, n_pair, hd, n_head):
    tq = q_ref.shape[0]
    eye = _identity(LANES)
    lane = lax.broadcasted_iota(jnp.int32, (tq, LANES), 1)
    row = lax.broadcasted_iota(jnp.int32, (2 * tq, tq), 0) % tq
    col = lax.broadcasted_iota(jnp.int32, (2 * tq, tq), 1)
    causal_new = col <= row
    ccol = ccol_ref[...]

    def scores(p):
        feat = slice(p * LANES, (p + 1) * LANES)
        q_pair = q_ref[:, feat]
        blocks = []
        for h in range(2):
            hh = g * (2 * n_pair) + 2 * p + h
            cq = jnp.sum(jnp.where(lane == hh, ccol, 0.0), axis=1, keepdims=True)
            top = jnp.where(lane // hd == h, q_pair, jnp.zeros_like(q_pair))
            blocks.append(jnp.concatenate([top, _query_ext(lane, hh, n_head, cq)], axis=1))
        qaug = jnp.concatenate(blocks, axis=0)
        s_c = _dot(qaug, jnp.concatenate([kt_ref[feat, :].astype(BF16), cextc_ref[...]], axis=0))
        kn_t = _dot_nt(eye, kn_ref[:, feat].astype(BF16)).astype(BF16)
        s_n = _dot(qaug, jnp.concatenate([kn_t, cextn_ref[...]], axis=0))
        return s_c, jnp.where(causal_new, s_n, MASK_VALUE)

    held = {}

    def score_stage(p):
        held["s", p] = scores(p)

    def softmax_stage(p):
        s_c, s_n = held.pop(("s", p))
        m = jnp.maximum(jnp.max(s_c, axis=1, keepdims=True), jnp.max(s_n, axis=1, keepdims=True))
        p_c = jnp.exp2(s_c - m)
        p_n = jnp.exp2(s_n - m)
        l = jnp.sum(p_c, axis=1, keepdims=True) + jnp.sum(p_n, axis=1, keepdims=True)
        held["p", p] = (p_c.astype(BF16), p_n.astype(BF16), l)

    def value_stage(p):
        p_c, p_n, l = held.pop(("p", p))
        feat = slice(p * LANES, (p + 1) * LANES)
        o = _dot_nt(p_c, vt_ref[feat, :].astype(BF16)) + _dot(p_n, vn_ref[:, feat].astype(BF16))
        o = o / l
        o_ref[:, feat] = jnp.where(lane < hd, o[0:tq], o[tq:2 * tq]).astype(o_ref.dtype)

    return score_stage, softmax_stage, value_stage


def _fox_attn_sample_kernel(*refs, n_pair, hd, n_head):
    score, softmax, value = _sample_attn_stages(*refs, g=pl.program_id(1), n_pair=n_pair, hd=hd, n_head=n_head)
    score(0)
    for p in range(n_pair):
        if p + 1 < n_pair:
            score(p + 1)
        softmax(p)
        value(p)


def _post_attn_kernel(x_ref, o_ref, wo_ref, g_ref, win_ref, wdown_ref, gfin_ref,
                      q_ref, ccol_ref, kt_ref, vt_ref, cextc_ref, kn_ref, vn_ref, cextn_ref,
                      y_ref, os_ref, act_ref, *, d_ff, ff_tile, final_norm, n_group, n_pair, hd, n_head):
    score, softmax, value = _sample_attn_stages(
        q_ref, ccol_ref, kt_ref, vt_ref, cextc_ref, kn_ref, vn_ref, cextn_ref, os_ref,
        g=pl.program_id(0) % n_group, n_pair=n_pair, hd=hd, n_head=n_head)
    n_ff = d_ff // ff_tile
    lag = max(1, n_ff - 1 - n_pair) // 2 + 2
    hooks = {}

    def at(j, fn):
        j = min(j, n_ff - 1)
        prev = hooks.get(j)
        hooks[j] = fn if prev is None else (lambda: (prev(), fn()))

    for p in range(n_pair):
        at(p - 1, functools.partial(score, p))
        at(p - 1, functools.partial(softmax, p))
        at(p - 1 + lag, functools.partial(value, p))
    _post_kernel(x_ref, o_ref, wo_ref, g_ref, win_ref, wdown_ref, gfin_ref, y_ref, act_ref,
                 d_ff=d_ff, ff_tile=ff_tile, final_norm=final_norm, hooks=hooks)


def _fox_attn_sample(q, c_col, kt_cache, vt_cache, cext_cache, k_new, v_new, cext_new, n_pair, n_head):
    bsz, tq, hw = q.shape
    cache_len = kt_cache.shape[2]
    hd = hw // n_head
    assert 2 * hd == LANES and 4 * n_head <= LANES
    width = n_pair * LANES
    new_spec = pl.BlockSpec((None, tq, width), lambda b, g: (b, 0, g))
    cache_spec = pl.BlockSpec((None, width, cache_len), lambda b, g: (b, g, 0))
    kern = functools.partial(_fox_attn_sample_kernel, n_pair=n_pair, hd=hd, n_head=n_head)
    return pl.pallas_call(
        kern,
        grid=(bsz, hw // width),
        in_specs=[
            new_spec,
            pl.BlockSpec((None, tq, LANES), lambda b, g: (b, 0, 0)),
            cache_spec, cache_spec,
            pl.BlockSpec((None, LANES, cache_len), lambda b, g: (b, 0, 0)),
            new_spec, new_spec,
            pl.BlockSpec((None, LANES, tq), lambda b, g: (b, 0, 0)),
        ],
        out_specs=new_spec,
        out_shape=jax.ShapeDtypeStruct((bsz, tq, hw), BF16),
        compiler_params=_params(2),
        name="fox_attn_sample",
    )(q, c_col, kt_cache, vt_cache, cext_cache, k_new, v_new, cext_new)


def _post_attn(x2d, o2d, w_out, g_ffn, w_in_all, w_down_all, layer, g_final, final_norm,
               q, c_col, kt_cache, vt_cache, cext_cache, k_new, v_new, cext_new, n_pair, n_head):
    n, d = x2d.shape
    bsz, tq, hw = q.shape
    cache_len = kt_cache.shape[2]
    hd = hw // n_head
    width = n_pair * LANES
    n_group = hw // width
    steps = bsz * n_group
    tm = n // steps
    d_ff = w_down_all.shape[1]
    ff_tile = MXU_TILE
    assert n % steps == 0 and tm % 16 == 0 and d_ff % ff_tile == 0 and 2 * hd == LANES
    kern = functools.partial(_post_attn_kernel, d_ff=d_ff, ff_tile=ff_tile, final_norm=final_norm,
                             n_group=n_group, n_pair=n_pair, hd=hd, n_head=n_head)
    new_spec = pl.BlockSpec((None, tq, width), lambda i: (i // n_group, 0, i % n_group))
    cache_spec = pl.BlockSpec((None, width, cache_len), lambda i: (i // n_group, i % n_group, 0))
    return pl.pallas_call(
        kern,
        grid=(steps,),
        in_specs=[
            pl.BlockSpec((tm, d), lambda i: (i, 0)),
            pl.BlockSpec((tm, o2d.shape[1]), lambda i: (i, 0)),
            _resident(w_out.shape),
            _resident((1, d)),
            pl.BlockSpec((None,) + w_in_all.shape[1:], lambda i: (layer, 0, 0), pipeline_mode=pl.Buffered(1)),
            pl.BlockSpec((None,) + w_down_all.shape[1:], lambda i: (layer, 0, 0), pipeline_mode=pl.Buffered(1)),
            _resident((1, d)),
            new_spec,
            pl.BlockSpec((None, tq, LANES), lambda i: (i // n_group, 0, 0)),
            cache_spec, cache_spec,
            pl.BlockSpec((None, LANES, cache_len), lambda i: (i // n_group, 0, 0)),
            new_spec, new_spec,
            pl.BlockSpec((None, LANES, tq), lambda i: (i // n_group, 0, 0)),
        ],
        out_specs=[pl.BlockSpec((tm, d), lambda i: (i, 0)), new_spec],
        out_shape=[jax.ShapeDtypeStruct((n, d), F32), jax.ShapeDtypeStruct((bsz, tq, hw), BF16)],
        scratch_shapes=[pltpu.VMEM((tm, d_ff), BF16)],
        compiler_params=_params(1),
        name="post_attn",
    )(x2d, o2d, w_out, g_ffn, w_in_all, w_down_all, g_final,
      q, c_col, kt_cache, vt_cache, cext_cache, k_new, v_new, cext_new)


def kernel(x_prompt, x_sample, state_gla, cache_fox_k, cache_fox_v, cache_fox_logf,
           norm_mix, gla_w_in, gla_w_g2, gla_b_g, gla_norm, gla_w_out,
           fox_w_in, fox_b_f, fox_w_out, norm_ffn, ffn_w_in, ffn_w_down, norm_final):
    d = x_prompt.shape[-1]
    depth = norm_mix.shape[0]
    groups = [x_prompt, x_sample]
    shapes = [x.shape for x in groups]
    xs = [x.reshape(-1, d) for x in groups]
    row_tiles = [min(ROW_TILE, x.shape[0]) for x in xs]
    ffn_steps = xs[0].shape[0] // row_tiles[0]
    ride_ffn = all(_rider_rows(a, ffn_steps) is not None for a in (ffn_w_in, ffn_w_down))
    w_ffn_in = w_ffn_down = None
    if not ride_ffn:
        w_ffn_in = ffn_w_in.astype(BF16)
        w_ffn_down = ffn_w_down.astype(BF16)

    gla_states = [[], []]
    fox_k, fox_v, fox_f = [[], []], [[], []], [[], []]
    for i in range(depth):
        j = i // 2
        g_mix = norm_mix[i].reshape(1, d)
        g_ffn = norm_ffn[i].reshape(1, d)
        last = i == depth - 1
        if i % 2 == 0:
            _, n_head, dk, dv = state_gla.shape[1:]
            hk, hv = n_head * dk, n_head * dv
            n_main = 2 * hk + 2 * hv
            w_in = gla_w_in[j]
            wt = w_in.T
            w_g2 = gla_w_g2[j].astype(BF16)
            b_g = gla_b_g[j].reshape(1, hk)
            w_out = gla_w_out[j].astype(BF16)
            norm_g = gla_norm[j].reshape(1, hv)
            s0s = [jnp.zeros((shapes[0][0], n_head, dk, dv), F32), state_gla[j]]
            for gi in range(2):
                bsz, t_len, _ = shapes[gi]
                if w_ffn_in is None:
                    proj, glog, w_ffn_in, w_ffn_down = _gla_proj(xs[gi], g_mix, wt, w_g2, b_g, row_tiles[gi],
                                                                 (ffn_w_in, ffn_w_down))
                else:
                    proj, glog = _gla_proj(xs[gi], g_mix, wt, w_g2, b_g, row_tiles[gi])
                og, s_fin = _gla_mix(proj.reshape(bsz, t_len, n_main), glog.reshape(bsz, t_len, hk),
                                     s0s[gi], norm_g, min(t_len, GLA_ROWS),
                                     min(bsz, GLA_SEQS_PROMPT if gi == 0 else GLA_SEQS_SAMPLE))
                gla_states[gi].append(s_fin)
                xs[gi] = _post(xs[gi], og.reshape(-1, hv), w_out, g_ffn, w_ffn_in, w_ffn_down, i,
                               norm_final.reshape(1, d), row_tiles[gi], last)
        else:
            n_head = fox_b_f.shape[1]
            hw = fox_w_out.shape[1]
            hd = hw // n_head
            wt = fox_w_in[j].T
            bf_col = fox_b_f[j].reshape(n_head, 1)
            w_out = fox_w_out[j].astype(BF16)
            g_fin = norm_final.reshape(1, d)
            bsz, t_len, _ = shapes[0]
            qaug, kaug, kt, vt, vtb, logf_p = _fox_proj_prompt(xs[0].reshape(bsz, t_len, d), g_mix, wt, bf_col,
                                                               min(ROW_TILE, t_len), ATTN_TQ, n_head)
            o_p = _fox_attn_prompt(qaug, kaug, vtb, ATTN_PAIRS, n_head)
            fox_k[0].append(kt.reshape(bsz, n_head, hd, t_len).transpose(0, 3, 1, 2))
            fox_v[0].append(vt.reshape(bsz, n_head, hd, t_len).transpose(0, 3, 1, 2))
            fox_f[0].append(jnp.transpose(logf_p, (0, 2, 1)))
            bsz, t_len, _ = shapes[1]
            cache_len = cache_fox_logf.shape[2]
            q, k, v, logf_s, c_col, cext_new, cext_cache = _fox_proj_sample(
                xs[1].reshape(bsz, t_len, d), g_mix, wt, bf_col,
                jnp.transpose(cache_fox_logf[j], (0, 2, 1)), n_head, min(PROJ_SAMPLE_SEQS, bsz))
            kt_cache = jnp.transpose(cache_fox_k[j], (0, 2, 3, 1)).reshape(bsz, hw, cache_len)
            vt_cache = jnp.transpose(cache_fox_v[j], (0, 2, 3, 1)).reshape(bsz, hw, cache_len)
            attn_args = (q, c_col, kt_cache, vt_cache, cext_cache, k, v, cext_new)
            steps = bsz * (hw // (FUSED_ATTN_PAIRS * LANES))
            if xs[0].shape[0] % steps == 0 and (xs[0].shape[0] // steps) % LANES == 0:
                xs[0], o_s = _post_attn(xs[0], o_p.reshape(-1, hw), w_out, g_ffn, w_ffn_in, w_ffn_down, i,
                                        g_fin, last, *attn_args, FUSED_ATTN_PAIRS, n_head)
            else:
                xs[0] = _post(xs[0], o_p.reshape(-1, hw), w_out, g_ffn, w_ffn_in, w_ffn_down, i,
                              g_fin, row_tiles[0], last)
                o_s = _fox_attn_sample(*attn_args, hw // LANES, n_head)
            fox_k[1].append(k.reshape(bsz, t_len, n_head, hd))
            fox_v[1].append(v.reshape(bsz, t_len, n_head, hd))
            fox_f[1].append(jnp.transpose(logf_s, (0, 2, 1)))
            xs[1] = _post(xs[1], o_s.reshape(-1, hw), w_out, g_ffn, w_ffn_in, w_ffn_down, i,
                          g_fin, row_tiles[1], last)

    y_prompt = xs[0].reshape(shapes[0])
    y_sample = xs[1].reshape(shapes[1])
    st = lambda parts: jnp.stack(parts, axis=0)
    return (y_prompt, y_sample, st(gla_states[0]), st(fox_k[0]), st(fox_v[0]), st(fox_f[0]),
            st(gla_states[1]), st(fox_k[1]), st(fox_v[1]), st(fox_f[1]))
```

```python
import functools

import jax
import jax.numpy as jnp
from jax import lax
from jax.experimental import pallas as pl
from jax.experimental.pallas import tpu as pltpu

F32 = jnp.float32
BF16 = jnp.bfloat16

EPS = 1e-6
MASK_VALUE = -1e30
LOG2E = 1.4426950408889634

GLA_HEADS = 4
GLA_CHUNK = 64
GLA_GATE_TAU = 16.0

LANES = 128
MXU_TILE = 256
VT_ROWS = LANES + 16
VMEM_LIMIT_BYTES = 56 * 1024 * 1024

ROW_TILE = 512
GLA_ROWS = 256
GLA_SEQS_PROMPT = 2
GLA_SEQS_SAMPLE = 4
ATTN_TQ = 256
ATTN_PAIRS = 8
PROJ_SAMPLE_SEQS = 4
FUSED_ATTN_PAIRS = 4


def _params(n_grid):
    return pltpu.CompilerParams(
        dimension_semantics=("arbitrary",) * n_grid,
        vmem_limit_bytes=VMEM_LIMIT_BYTES,
    )


def _resident(shape):
    nd = len(shape)
    return pl.BlockSpec(shape, lambda *_: (0,) * nd, pipeline_mode=pl.Buffered(1))


def _dot(a, b):
    return jnp.dot(a, b, preferred_element_type=F32)


def _dot_nt(a, b):
    return lax.dot_general(a, b, (((1,), (1,)), ((), ())), preferred_element_type=F32)


def _dot_tn(a, b):
    return lax.dot_general(a, b, (((0,), (0,)), ((), ())), preferred_element_type=F32)


def _split3(x):
    hi = x.astype(BF16)
    r1 = x - hi.astype(F32)
    mid = r1.astype(BF16)
    lo = (r1 - mid.astype(F32)).astype(BF16)
    return hi, mid, lo


def _sum01(dot_fn, x, ones_first, mat01):
    acc = None
    for part in _split3(x):
        term = dot_fn(mat01, part) if ones_first else dot_fn(part, mat01)
        acc = term if acc is None else acc + term
    return acc


def _rmsnorm(x, g):
    var = jnp.mean(x * x, axis=-1, keepdims=True)
    return x * lax.rsqrt(var + EPS) * g


def _log_sigmoid(z):
    return jnp.minimum(z, 0.0) - jnp.log1p(jnp.exp(-jnp.abs(z)))


def _silu(z):
    return z * jax.nn.sigmoid(z)


def _identity(n):
    r = lax.broadcasted_iota(jnp.int32, (n, n), 0)
    c = lax.broadcasted_iota(jnp.int32, (n, n), 1)
    return (r == c).astype(BF16)


def _cast_weights_once(first_step, wt_ref, wtb_ref):
    @pl.when(first_step)
    def _():
        wtb_ref[...] = wt_ref[...].astype(BF16)


def _gla_proj_kernel(x_ref, g_ref, wt_ref, wg2_ref, bg_ref, *rest, n_main, n_ride):
    ride_in, (proj_ref, glog_ref) = rest[:n_ride], rest[n_ride:n_ride + 2]
    ride_out, wtb_ref = rest[n_ride + 2:2 * n_ride + 2], rest[-1]
    _cast_weights_once(pl.program_id(0) == 0, wt_ref, wtb_ref)
    rank = wg2_ref.shape[0]
    h = _rmsnorm(x_ref[...], g_ref[...]).astype(BF16)
    proj_ref[...] = _dot_nt(h, wtb_ref[0:n_main, :])
    gl_t = _dot_nt(wtb_ref[n_main:n_main + rank, :], h).astype(BF16)
    z = _dot_tn(gl_t, wg2_ref[...]) + bg_ref[...]
    glog_ref[...] = _log_sigmoid(z) / GLA_GATE_TAU
    for src, dst in zip(ride_in, ride_out):
        dst[...] = src[...].astype(BF16)


def _rider_rows(a, steps):
    layers, rows, _ = a.shape
    if steps % layers or rows % (steps // layers) or (rows // (steps // layers)) % 16:
        return None
    return rows // (steps // layers)


def _gla_proj(x2d, g, wt, w_g2, b_g, tm, riders=()):
    n, d = x2d.shape
    rank, hk = w_g2.shape
    n_main = wt.shape[0] - rank
    steps = n // tm
    ride_specs = []
    for a in riders:
        per = steps // a.shape[0]
        ride_specs.append(pl.BlockSpec((1, _rider_rows(a, steps), a.shape[2]),
                                       lambda i, per=per: (i // per, i % per, 0)))
    return pl.pallas_call(
        functools.partial(_gla_proj_kernel, n_main=n_main, n_ride=len(riders)),
        grid=(steps,),
        in_specs=[
            pl.BlockSpec((tm, d), lambda i: (i, 0)),
            _resident((1, d)),
            _resident(wt.shape),
            _resident(w_g2.shape),
            _resident((1, hk)),
        ] + ride_specs,
        out_specs=[
            pl.BlockSpec((tm, n_main), lambda i: (i, 0)),
            pl.BlockSpec((tm, hk), lambda i: (i, 0)),
        ] + ride_specs,
        out_shape=[
            jax.ShapeDtypeStruct((n, n_main), F32),
            jax.ShapeDtypeStruct((n, hk), F32),
        ] + [jax.ShapeDtypeStruct(a.shape, BF16) for a in riders],
        scratch_shapes=[pltpu.VMEM(wt.shape, BF16)],
        compiler_params=_params(1),
        name="gla_proj",
    )(x2d, g, wt, w_g2, b_g, *riders)


def _gla_mix_kernel(q_ref, k_ref, v_ref, r_ref, glog_ref, s0_ref, ng_ref,
                    og_ref, sfin_ref, s_ref, *, dk, dv):
    t = pl.program_id(1)
    nb, tb, _ = q_ref.shape
    n_chunk = tb // GLA_CHUNK
    seqs = range(nb)
    heads = range(GLA_HEADS)
    units = [(sb, h) for sb in seqs for h in heads]
    chunks = [slice(c * GLA_CHUNK, (c + 1) * GLA_CHUNK) for c in range(n_chunk)]
    ksl = [slice(h * dk, (h + 1) * dk) for h in heads]
    vsl = [slice(h * dv, (h + 1) * dv) for h in heads]

    @pl.when(t == 0)
    def _():
        s_ref[...] = s0_ref[...]

    row = lax.broadcasted_iota(jnp.int32, (tb, tb), 0)
    col = lax.broadcasted_iota(jnp.int32, (tb, tb), 1)
    same_chunk = (row // GLA_CHUNK) == (col // GLA_CHUNK)
    causal = same_chunk & (col <= row)
    cum_mat = causal.astype(BF16)

    b = [_sum01(_dot, glog_ref[sb], True, cum_mat) for sb in seqs]
    qe, ke, kd, dec_t = [], [], [], []
    for sb in seqs:
        tot_rows = [b[sb][(c + 1) * GLA_CHUNK - 1:(c + 1) * GLA_CHUNK, :] for c in range(n_chunk)]
        b_last = jnp.concatenate([jnp.broadcast_to(r, (GLA_CHUNK, r.shape[1])) for r in tot_rows], axis=0)
        pad_rows = [jnp.zeros_like(tot_rows[0])] * (8 - n_chunk % 8 if n_chunk % 8 else 0)
        dec_t.append(jnp.exp(_to_columns(jnp.concatenate(tot_rows + pad_rows, axis=0))))
        q = q_ref[sb]
        k = k_ref[sb]
        qe.append((q * jnp.exp(b[sb]) * (dk ** -0.5)).astype(BF16))
        ke.append((k * jnp.exp(-b[sb])).astype(BF16))
        kd.append((k * jnp.exp(b_last - b[sb])).astype(BF16))

    v_b = {u: v_ref[u[0], :, vsl[u[1]]].astype(BF16) for u in units}
    a_raw = {(sb, h): _dot_nt(qe[sb][:, ksl[h]], ke[sb][:, ksl[h]]) for sb, h in units}
    upd = {(sb, h): [_dot_tn(kd[sb][rs, ksl[h]], v_b[sb, h][rs]) for rs in chunks] for sb, h in units}
    o_intra = {u: _dot(jnp.where(causal, a_raw[u], 0.0).astype(BF16), v_b[u]) for u in units}
    s_in = {}
    for sb, h in units:
        s = s_ref[sb, h]
        s_in[sb, h] = []
        for c in range(n_chunk):
            s_in[sb, h].append(s.astype(BF16))
            s = s * dec_t[sb][ksl[h], c:c + 1] + upd[sb, h][c]
        s_ref[sb, h] = s
    for sb, h in units:
        o_parts = [o_intra[sb, h][rs] + _dot(qe[sb][rs, ksl[h]], s_in[sb, h][c]) for c, rs in enumerate(chunks)]
        o = o_parts[0] if n_chunk == 1 else jnp.concatenate(o_parts, axis=0)
        on = _rmsnorm(o, ng_ref[:, vsl[h]])
        og_ref[sb, :, vsl[h]] = (on * _silu(r_ref[sb, :, vsl[h]])).astype(BF16)

    @pl.when(t == pl.num_programs(1) - 1)
    def _():
        sfin_ref[...] = s_ref[...]


def _gla_mix(proj3, glog3, s0, norm_g, tb, nb):
    bsz, t_len, _ = proj3.shape
    _, n_head, dk, dv = s0.shape
    hk, hv = n_head * dk, n_head * dv
    assert t_len % tb == 0 and tb % GLA_CHUNK == 0 and hv == 2 * hk and bsz % nb == 0
    kern = functools.partial(_gla_mix_kernel, dk=dk, dv=dv)
    state_spec = pl.BlockSpec((nb, n_head, dk, dv), lambda b, t: (b, 0, 0, 0))
    return pl.pallas_call(
        kern,
        grid=(bsz // nb, t_len // tb),
        in_specs=[
            pl.BlockSpec((nb, tb, hk), lambda b, t: (b, t, 0)),
            pl.BlockSpec((nb, tb, hk), lambda b, t: (b, t, 1)),
            pl.BlockSpec((nb, tb, hv), lambda b, t: (b, t, 1)),
            pl.BlockSpec((nb, tb, hv), lambda b, t: (b, t, 2)),
            pl.BlockSpec((nb, tb, hk), lambda b, t: (b, t, 0)),
            state_spec,
            _resident((1, hv)),
        ],
        out_specs=[
            pl.BlockSpec((nb, tb, hv), lambda b, t: (b, t, 0)),
            state_spec,
        ],
        out_shape=[
            jax.ShapeDtypeStruct((bsz, t_len, hv), BF16),
            jax.ShapeDtypeStruct(s0.shape, F32),
        ],
        scratch_shapes=[pltpu.VMEM((nb, n_head, dk, dv), F32)],
        compiler_params=_params(2),
        name="gla_mix",
    )(proj3, proj3, proj3, proj3, glog3, s0, norm_g)


def _post_kernel(x_ref, o_ref, wo_ref, g_ref, win_ref, wdown_ref, gfin_ref, y_ref, act_ref,
                 *, d_ff, ff_tile, final_norm, hooks=None):
    x1 = x_ref[...] + _dot(o_ref[...], wo_ref[...])
    if hooks and -1 in hooks:
        hooks[-1]()
    h = _rmsnorm(x1, g_ref[...]).astype(BF16)
    for j in range(d_ff // ff_tile):
        gate = _dot(h, win_ref[:, j * ff_tile:(j + 1) * ff_tile])
        up = _dot(h, win_ref[:, d_ff + j * ff_tile:d_ff + (j + 1) * ff_tile])
        act_ref[:, j * ff_tile:(j + 1) * ff_tile] = (_silu(gate) * up).astype(BF16)
        if hooks and j in hooks:
            hooks[j]()
    y = x1 + _dot(act_ref[...], wdown_ref[...])
    if final_norm:
        y = _rmsnorm(y, gfin_ref[...])
    y_ref[...] = y


def _post(x2d, o2d, w_out, g_ffn, w_in_all, w_down_all, layer, g_final, tm, final_norm):
    n, d = x2d.shape
    d_ff = w_down_all.shape[1]
    ff_tile = MXU_TILE
    assert d_ff % ff_tile == 0 and n % tm == 0
    kern = functools.partial(_post_kernel, d_ff=d_ff, ff_tile=ff_tile, final_norm=final_norm)
    return pl.pallas_call(
        kern,
        grid=(n // tm,),
        in_specs=[
            pl.BlockSpec((tm, d), lambda i: (i, 0)),
            pl.BlockSpec((tm, o2d.shape[1]), lambda i: (i, 0)),
            _resident(w_out.shape),
            _resident((1, d)),
            pl.BlockSpec((None,) + w_in_all.shape[1:], lambda i: (layer, 0, 0), pipeline_mode=pl.Buffered(1)),
            pl.BlockSpec((None,) + w_down_all.shape[1:], lambda i: (layer, 0, 0), pipeline_mode=pl.Buffered(1)),
            _resident((1, d)),
        ],
        out_specs=pl.BlockSpec((tm, d), lambda i: (i, 0)),
        out_shape=jax.ShapeDtypeStruct((n, d), F32),
        scratch_shapes=[pltpu.VMEM((tm, d_ff), BF16)],
        compiler_params=_params(1),
        name="post_final" if final_norm else "post",
    )(x2d, o2d, w_out, g_ffn, w_in_all, w_down_all, g_final)


def _lane_cumsum(x, tile):
    r = lax.broadcasted_iota(jnp.int32, (tile, tile), 0)
    c = lax.broadcasted_iota(jnp.int32, (tile, tile), 1)
    upper = (r <= c).astype(BF16)
    rows = x.shape[0]
    carry = jnp.zeros((rows, 1), F32)
    out = []
    for j in range(x.shape[1] // tile):
        parts = jnp.concatenate(_split3(x[:, j * tile:(j + 1) * tile]), axis=0)
        s = _dot(parts, upper)
        blk = carry + s[0:rows] + s[rows:2 * rows] + s[2 * rows:3 * rows]
        out.append(blk)
        carry = blk[:, tile - 1:tile]
    return out[0] if len(out) == 1 else jnp.concatenate(out, axis=1)


def _to_columns(x_t):
    rows = x_t.shape[0]
    parts = jnp.concatenate(_split3(x_t), axis=0)
    r = lax.broadcasted_iota(jnp.int32, (3 * rows, LANES), 0)
    c = lax.broadcasted_iota(jnp.int32, (3 * rows, LANES), 1)
    place = ((r % rows) == c).astype(BF16)
    return _dot_tn(parts, place)


def _key_ext_cols(c_cols, n_head):
    hi, mid, lo = (p.astype(F32) for p in _split3(c_cols * LOG2E))
    lane = lax.broadcasted_iota(jnp.int32, c_cols.shape, 1)
    ones = ((lane >= 3 * n_head) & (lane < 3 * n_head + 3)).astype(F32)
    ext = hi + pltpu.roll(mid, n_head, 1) + pltpu.roll(lo, 2 * n_head, 1) + ones
    return ext.astype(BF16)


def _key_ext_rows(c_rows):
    n_head, n = c_rows.shape
    hi, mid, lo = _split3(c_rows * LOG2E)
    r = lax.broadcasted_iota(jnp.int32, (n_head, n), 0)
    ones = (r < 3).astype(BF16)
    zeros = jnp.zeros((LANES - 4 * n_head, n), BF16)
    return jnp.concatenate([hi, mid, lo, ones, zeros], axis=0)


def _query_ext(idx, hh, n_head, cq):
    cq_hi, cq_mid, cq_lo = (p.astype(F32) for p in _split3(cq * LOG2E))
    ext = jnp.where((idx == hh) | (idx == n_head + hh) | (idx == 2 * n_head + hh), -1.0, 0.0)
    ext = jnp.where(idx == 3 * n_head, cq_hi, ext)
    ext = jnp.where(idx == 3 * n_head + 1, cq_mid, ext)
    ext = jnp.where(idx == 3 * n_head + 2, cq_lo, ext)
    return ext.astype(BF16)


def _fox_proj_prompt_kernel(x_ref, g_ref, wtf_ref, bfc_ref,
                            qaug_ref, kaug_ref, kt_ref, vt_ref, vtb_ref, logft_ref, carry_ref, wt_ref,
                            *, hw, n_head, scale, tq):
    t = pl.program_id(1)
    tm = x_ref.shape[0]
    _cast_weights_once((pl.program_id(0) == 0) & (t == 0), wtf_ref, wt_ref)

    @pl.when(t == 0)
    def _():
        carry_ref[...] = jnp.zeros_like(carry_ref)

    h = _rmsnorm(x_ref[...], g_ref[...]).astype(BF16)
    logf_t = _log_sigmoid(_dot_nt(wt_ref[3 * hw:3 * hw + n_head, :], h) + bfc_ref[...])
    logft_ref[...] = logf_t
    ct_blk = _lane_cumsum(logf_t, min(tm, MXU_TILE)) + carry_ref[:, 0:1]
    carry_ref[...] = jnp.broadcast_to(ct_blk[:, tm - 1:tm], carry_ref.shape)
    cext = _key_ext_cols(_to_columns(ct_blk), n_head)
    kt = _dot_nt(wt_ref[hw:2 * hw, :], h)
    vt = _dot_nt(wt_ref[2 * hw:3 * hw, :], h)
    qt = (_dot_nt(wt_ref[0:hw, :], h) * scale).astype(BF16)
    kt_ref[...] = kt
    hd = hw // n_head
    ext_rows = 4 * n_head
    rr = lax.broadcasted_iota(jnp.int32, (ext_rows, tq), 0)
    rq = lax.broadcasted_iota(jnp.int32, (LANES, tq), 0) // hd
    zeros = jnp.zeros((LANES - ext_rows, tq), BF16)
    for p in range(hw // LANES):
        for jq in range(tm // tq):
            qs = slice(jq * tq, (jq + 1) * tq)
            q_pair = qt[p * LANES:(p + 1) * LANES, qs]
            cols = []
            for hh in (2 * p, 2 * p + 1):
                top = jnp.where(rq == hh - 2 * p, q_pair, jnp.zeros_like(q_pair))
                cols.append(jnp.concatenate([top, _query_ext(rr, hh, n_head, ct_blk[hh:hh + 1, qs]), zeros], axis=0))
            qaug_ref[p, jq] = jnp.concatenate(cols, axis=1)
    k = kt.T
    for p in range(hw // LANES):
        kaug_ref[p, :, 0:LANES] = k[:, p * LANES:(p + 1) * LANES].astype(BF16)
        kaug_ref[p, :, LANES:2 * LANES] = cext
    vt_ref[...] = vt
    ones = jnp.ones((VT_ROWS - LANES, tm), BF16)
    for p in range(hw // LANES):
        vtb_ref[p, 0:LANES, :] = vt[p * LANES:(p + 1) * LANES, :].astype(BF16)
        vtb_ref[p, LANES:VT_ROWS, :] = ones


def _fox_proj_prompt(x3, g, wt, bf_col, tm, tq, n_head):
    bsz, t_len, d = x3.shape
    hw = (wt.shape[0] - n_head) // 3
    n_pair = hw // LANES
    assert t_len % tm == 0 and tm % tq == 0 and tq % LANES == 0 and 3 * n_head + 3 <= LANES
    kern = functools.partial(_fox_proj_prompt_kernel, hw=hw, n_head=n_head,
                             scale=(hw // n_head) ** -0.5 * LOG2E, tq=tq)
    feat = pl.BlockSpec((None, hw, tm), lambda b, t: (b, 0, t))
    head = pl.BlockSpec((None, n_head, tm), lambda b, t: (b, 0, t))
    return pl.pallas_call(
        kern,
        grid=(bsz, t_len // tm),
        in_specs=[
            pl.BlockSpec((None, tm, d), lambda b, t: (b, t, 0)),
            _resident((1, d)),
            _resident(wt.shape),
            _resident(bf_col.shape),
        ],
        out_specs=[
            pl.BlockSpec((None, n_pair, tm // tq, 2 * LANES, 2 * tq), lambda b, t: (b, 0, t, 0, 0)),
            pl.BlockSpec((None, n_pair, tm, 2 * LANES), lambda b, t: (b, 0, t, 0)),
            feat, feat,
            pl.BlockSpec((None, n_pair, VT_ROWS, tm), lambda b, t: (b, 0, 0, t)),
            head,
        ],
        out_shape=[
            jax.ShapeDtypeStruct((bsz, n_pair, t_len // tq, 2 * LANES, 2 * tq), BF16),
            jax.ShapeDtypeStruct((bsz, n_pair, t_len, 2 * LANES), BF16),
            jax.ShapeDtypeStruct((bsz, hw, t_len), F32),
            jax.ShapeDtypeStruct((bsz, hw, t_len), F32),
            jax.ShapeDtypeStruct((bsz, n_pair, VT_ROWS, t_len), BF16),
            jax.ShapeDtypeStruct((bsz, n_head, t_len), F32),
        ],
        scratch_shapes=[pltpu.VMEM((n_head, LANES), F32), pltpu.VMEM(wt.shape, BF16)],
        compiler_params=_params(2),
        name="fox_proj_prompt",
    )(x3, g, wt, bf_col)


def _fox_proj_sample_kernel(x_ref, g_ref, wtf_ref, bfc_ref, lct_ref,
                            q_ref, k_ref, v_ref, logft_ref, ccol_ref, cextn_ref, cextc_ref, k4_ref, v4_ref, wt_ref,
                            *, hw, n_head, scale):
    _cast_weights_once(pl.program_id(0) == 0, wtf_ref, wt_ref)
    nb, tm, d = x_ref.shape
    cache_len = lct_ref.shape[2]
    hd = hw // n_head
    h = _rmsnorm(x_ref[...].reshape(nb * tm, d), g_ref[...]).astype(BF16)
    q_ref[...] = (_dot_nt(h, wt_ref[0:hw, :]) * scale).astype(BF16).reshape(nb, tm, hw)
    for src_rows, tok_ref, row_ref in ((slice(hw, 2 * hw), k_ref, k4_ref), (slice(2 * hw, 3 * hw), v_ref, v4_ref)):
        kv = _dot_nt(h, wt_ref[src_rows, :])
        tok_ref[...] = kv.reshape(nb, tm, hw)
        for b in range(nb):
            for hh in range(n_head):
                row_ref[b, pl.ds(hh, tm, stride=n_head), :] = kv[b * tm:(b + 1) * tm, hh * hd:(hh + 1) * hd]

    ct_cache = _lane_cumsum(lct_ref[...].reshape(nb * n_head, cache_len), min(cache_len, MXU_TILE))
    logf_all = _log_sigmoid(_dot_nt(wt_ref[3 * hw:3 * hw + n_head, :], h) + bfc_ref[...])
    for b in range(nb):
        ct_b = ct_cache[b * n_head:(b + 1) * n_head, :]
        cextc_ref[b] = _key_ext_rows(ct_b)
        logf_t = logf_all[:, b * tm:(b + 1) * tm]
        logft_ref[b] = logf_t
        ct_new = _lane_cumsum(logf_t, tm) + ct_b[:, cache_len - 1:cache_len]
        cextn_ref[b] = _key_ext_rows(ct_new)
        ccol_ref[b] = _to_columns(ct_new)


def _fox_proj_sample(x3, g, wt, bf_col, logf_cache_t, n_head, nb):
    bsz, tm, d = x3.shape
    hw = (wt.shape[0] - n_head) // 3
    cache_len = logf_cache_t.shape[2]
    assert 4 * n_head <= LANES and bsz % nb == 0
    kern = functools.partial(_fox_proj_sample_kernel, hw=hw, n_head=n_head,
                             scale=(hw // n_head) ** -0.5 * LOG2E)
    tok = pl.BlockSpec((nb, tm, hw), lambda b: (b, 0, 0))
    return pl.pallas_call(
        kern,
        grid=(bsz // nb,),
        in_specs=[
            pl.BlockSpec((nb, tm, d), lambda b: (b, 0, 0)),
            _resident((1, d)),
            _resident(wt.shape),
            _resident(bf_col.shape),
            pl.BlockSpec((nb, n_head, cache_len), lambda b: (b, 0, 0)),
        ],
        out_specs=[
            tok, tok, tok,
            pl.BlockSpec((nb, n_head, tm), lambda b: (b, 0, 0)),
            pl.BlockSpec((nb, tm, LANES), lambda b: (b, 0, 0)),
            pl.BlockSpec((nb, LANES, tm), lambda b: (b, 0, 0)),
            pl.BlockSpec((nb, LANES, cache_len), lambda b: (b, 0, 0)),
            pl.BlockSpec((nb, tm * n_head, hw // n_head), lambda b: (b, 0, 0)),
            pl.BlockSpec((nb, tm * n_head, hw // n_head), lambda b: (b, 0, 0)),
        ],
        out_shape=[
            jax.ShapeDtypeStruct((bsz, tm, hw), BF16),
            jax.ShapeDtypeStruct((bsz, tm, hw), F32),
            jax.ShapeDtypeStruct((bsz, tm, hw), F32),
            jax.ShapeDtypeStruct((bsz, n_head, tm), F32),
            jax.ShapeDtypeStruct((bsz, tm, LANES), F32),
            jax.ShapeDtypeStruct((bsz, LANES, tm), BF16),
            jax.ShapeDtypeStruct((bsz, LANES, cache_len), BF16),
            jax.ShapeDtypeStruct((bsz, tm * n_head, hw // n_head), F32),
            jax.ShapeDtypeStruct((bsz, tm * n_head, hw // n_head), F32),
        ],
        scratch_shapes=[pltpu.VMEM(wt.shape, BF16)],
        compiler_params=_params(1),
        name="fox_proj_sample",
    )(x3, g, wt, bf_col, logf_cache_t)


def _attn_update(n_pair, kaug_of, vt_of, vt_prev_last, mask, next_kaug0,
                 qaug_ref, m_ref, acc_ref, s0_ref, plast_ref, alast_ref):
    last = n_pair - 1
    s_next = s0_ref[...]
    acc_ref[last] = acc_ref[last] * alast_ref[...] + _dot(vt_prev_last(), plast_ref[...])
    pending = None
    for p in range(n_pair):
        s = s_next
        if p + 1 < n_pair:
            s_next = _dot(kaug_of(p + 1), qaug_ref[p + 1])
        elif next_kaug0 is not None:
            s0_ref[...] = _dot(next_kaug0(), qaug_ref[0])
        if mask is not None:
            s = jnp.where(mask, s, MASK_VALUE)
        m_old = m_ref[p]
        m_new = jnp.maximum(m_old, jnp.max(s, axis=0, keepdims=True))
        alpha = jnp.exp2(m_old - m_new)
        pr = jnp.exp2(s - m_new).astype(BF16)
        m_ref[p] = m_new
        if pending is not None:
            q, pr_q, alpha_q = pending
            acc_ref[q] = acc_ref[q] * alpha_q + _dot(vt_of(q), pr_q)
        pending = (p, pr, alpha)
    plast_ref[...] = pending[1]
    alast_ref[...] = pending[2]


def _attn_finish(o_ref, acc_ref, *, n_pair, tq, hd):
    for p in range(n_pair):
        full = acc_ref[p, 0:LANES, :] * (1.0 / acc_ref[p, LANES:LANES + 1, :])
        z = jnp.concatenate([full[0:hd, 0:tq], full[hd:2 * hd, tq:2 * tq]], axis=0)
        o_ref[:, p * LANES:(p + 1) * LANES] = z.T.astype(o_ref.dtype)


def _fox_attn_prompt_kernel(qaug_ref, kaug_ref, vt_ref, o_ref,
                            m_ref, acc_ref, s0_ref, plast_ref, alast_ref, *, tq, n_pair, hd):
    i = pl.program_id(2)
    last = n_pair - 1
    state = (qaug_ref, m_ref, acc_ref, s0_ref, plast_ref, alast_ref)

    s0_ref[...] = _dot(kaug_ref[0, 0:tq, :], qaug_ref[0])
    m_ref[...] = jnp.full_like(m_ref, MASK_VALUE)
    acc_ref[...] = jnp.zeros_like(acc_ref)
    plast_ref[...] = jnp.zeros_like(plast_ref)
    alast_ref[...] = jnp.ones_like(alast_ref)

    def keys(j):
        return pl.ds(pl.multiple_of(j * tq, tq), tq)

    def vt_of(p, ks):
        return vt_ref[p, :, ks]

    def kv_block(j, mask, has_next):
        ks = keys(j)
        _attn_update(n_pair, lambda p: kaug_ref[p, ks, :], lambda p: vt_of(p, ks),
                     lambda: vt_of(last, keys(jnp.maximum(j - 1, 0))), mask,
                     (lambda: kaug_ref[0, keys(j + 1), :]) if has_next else None, *state)

    def body(j, carry):
        kv_block(j, None, True)
        return carry

    lax.fori_loop(0, i, body, 0)
    row = lax.broadcasted_iota(jnp.int32, (tq, 2 * tq), 0)
    col = lax.broadcasted_iota(jnp.int32, (tq, 2 * tq), 1) % tq
    kv_block(i, row <= col, False)
    acc_ref[last] = acc_ref[last] * alast_ref[...] + _dot(vt_of(last, keys(i)), plast_ref[...])
    _attn_finish(o_ref, acc_ref, n_pair=n_pair, tq=tq, hd=hd)


def _fox_attn_prompt(qaug, kaug, vtb, n_pair, n_head):
    bsz, n_pair_all, n_qblk, _, tq2 = qaug.shape
    tq = tq2 // 2
    t_len = n_qblk * tq
    hw = n_pair_all * LANES
    hd = hw // n_head
    assert 2 * hd == LANES and n_pair_all % n_pair == 0
    width = n_pair * LANES
    n_group = n_pair_all // n_pair
    kern = functools.partial(_fox_attn_prompt_kernel, tq=tq, n_pair=n_pair, hd=hd)
    return pl.pallas_call(
        kern,
        grid=(bsz, n_group, n_qblk),
        in_specs=[
            pl.BlockSpec((None, n_pair, None, 2 * LANES, 2 * tq), lambda b, g, i: (b, g, i, 0, 0)),
            pl.BlockSpec((None, n_pair, t_len, 2 * LANES), lambda b, g, i: (b, g, 0, 0)),
            pl.BlockSpec((None, n_pair, VT_ROWS, t_len), lambda b, g, i: (b, g, 0, 0)),
        ],
        out_specs=pl.BlockSpec((None, tq, width), lambda b, g, i: (b, i, g)),
        out_shape=jax.ShapeDtypeStruct((bsz, t_len, hw), BF16),
        scratch_shapes=[
            pltpu.VMEM((n_pair, 1, 2 * tq), F32),
            pltpu.VMEM((n_pair, VT_ROWS, 2 * tq), F32),
            pltpu.VMEM((tq, 2 * tq), F32),
            pltpu.VMEM((tq, 2 * tq), BF16),
            pltpu.VMEM((1, 2 * tq), F32),
        ],
        compiler_params=_params(3),
        name="fox_attn_prompt",
    )(qaug, kaug, vtb)


def _sample_attn_stages(q_ref, ccol_ref, kt_ref, vt_ref, cextc_ref, kn_ref, vn_ref, cextn_ref, o_ref,
                        *, g, n_pair, hd, n_head):
    tq = q_ref.shape[0]
    eye = _identity(LANES)
    lane = lax.broadcasted_iota(jnp.int32, (tq, LANES), 1)
    row = lax.broadcasted_iota(jnp.int32, (2 * tq, tq), 0) % tq
    col = lax.broadcasted_iota(jnp.int32, (2 * tq, tq), 1)
    causal_new = col <= row
    ccol = ccol_ref[...]

    def scores(p):
        feat = slice(p * LANES, (p + 1) * LANES)
        q_pair = q_ref[:, feat]
        blocks = []
        for h in range(2):
            hh = g * (2 * n_pair) + 2 * p + h
            cq = jnp.sum(jnp.where(lane == hh, ccol, 0.0), axis=1, keepdims=True)
            top = jnp.where(lane // hd == h, q_pair, jnp.zeros_like(q_pair))
            blocks.append(jnp.concatenate([top, _query_ext(lane, hh, n_head, cq)], axis=1))
        qaug = jnp.concatenate(blocks, axis=0)
        s_c = _dot(qaug, jnp.concatenate([kt_ref[feat, :].astype(BF16), cextc_ref[...]], axis=0))
        kn_t = _dot_nt(eye, kn_ref[:, feat].astype(BF16)).astype(BF16)
        s_n = _dot(qaug, jnp.concatenate([kn_t, cextn_ref[...]], axis=0))
        return s_c, jnp.where(causal_new, s_n, MASK_VALUE)

    held = {}

    def score_stage(p):
        held["s", p] = scores(p)

    def softmax_stage(p):
        s_c, s_n = held.pop(("s", p))
        m = jnp.maximum(jnp.max(s_c, axis=1, keepdims=True), jnp.max(s_n, axis=1, keepdims=True))
        p_c = jnp.exp2(s_c - m)
        p_n = jnp.exp2(s_n - m)
        l = jnp.sum(p_c, axis=1, keepdims=True) + jnp.sum(p_n, axis=1, keepdims=True)
        held["p", p] = (p_c.astype(BF16), p_n.astype(BF16), l)

    def value_stage(p):
        p_c, p_n, l = held.pop(("p", p))
        feat = slice(p * LANES, (p + 1) * LANES)
        o = _dot_nt(p_c, vt_ref[feat, :].astype(BF16)) + _dot(p_n, vn_ref[:, feat].astype(BF16))
        o = o / l
        o_ref[:, feat] = jnp.where(lane < hd, o[0:tq], o[tq:2 * tq]).astype(o_ref.dtype)

    return score_stage, softmax_stage, value_stage


def _fox_attn_sample_kernel(*refs, n_pair, hd, n_head):
    score, softmax, value = _sample_attn_stages(*refs, g=pl.program_id(1), n_pair=n_pair, hd=hd, n_head=n_head)
    score(0)
    for p in range(n_pair):
        if p + 1 < n_pair:
            score(p + 1)
        softmax(p)
        value(p)


def _post_attn_kernel(x_ref, o_ref, wo_ref, g_ref, win_ref, wdown_ref, gfin_ref,
                      q_ref, ccol_ref, kt_ref, vt_ref, cextc_ref, kn_ref, vn_ref, cextn_ref,
                      y_ref, os_ref, act_ref, *, d_ff, ff_tile, final_norm, n_group, n_pair, hd, n_head):
    score, softmax, value = _sample_attn_stages(
        q_ref, ccol_ref, kt_ref, vt_ref, cextc_ref, kn_ref, vn_ref, cextn_ref, os_ref,
        g=pl.program_id(0) % n_group, n_pair=n_pair, hd=hd, n_head=n_head)
    n_ff = d_ff // ff_tile
    lag = max(1, n_ff - 1 - n_pair) // 2 + 2
    hooks = {}

    def at(j, fn):
        j = min(j, n_ff - 1)
        prev = hooks.get(j)
        hooks[j] = fn if prev is None else (lambda: (prev(), fn()))

    for p in range(n_pair):
        at(p - 1, functools.partial(score, p))
        at(p - 1, functools.partial(softmax, p))
        at(p - 1 + lag, functools.partial(value, p))
    _post_kernel(x_ref, o_ref, wo_ref, g_ref, win_ref, wdown_ref, gfin_ref, y_ref, act_ref,
                 d_ff=d_ff, ff_tile=ff_tile, final_norm=final_norm, hooks=hooks)


def _fox_attn_sample(q, c_col, kt_cache, vt_cache, cext_cache, k_new, v_new, cext_new, n_pair, n_head):
    bsz, tq, hw = q.shape
    cache_len = kt_cache.shape[2]
    hd = hw // n_head
    assert 2 * hd == LANES and 4 * n_head <= LANES
    width = n_pair * LANES
    new_spec = pl.BlockSpec((None, tq, width), lambda b, g: (b, 0, g))
    cache_spec = pl.BlockSpec((None, width, cache_len), lambda b, g: (b, g, 0))
    kern = functools.partial(_fox_attn_sample_kernel, n_pair=n_pair, hd=hd, n_head=n_head)
    return pl.pallas_call(
        kern,
        grid=(bsz, hw // width),
        in_specs=[
            new_spec,
            pl.BlockSpec((None, tq, LANES), lambda b, g: (b, 0, 0)),
            cache_spec, cache_spec,
            pl.BlockSpec((None, LANES, cache_len), lambda b, g: (b, 0, 0)),
            new_spec, new_spec,
            pl.BlockSpec((None, LANES, tq), lambda b, g: (b, 0, 0)),
        ],
        out_specs=new_spec,
        out_shape=jax.ShapeDtypeStruct((bsz, tq, hw), BF16),
        compiler_params=_params(2),
        name="fox_attn_sample",
    )(q, c_col, kt_cache, vt_cache, cext_cache, k_new, v_new, cext_new)


def _post_attn(x2d, o2d, w_out, g_ffn, w_in_all, w_down_all, layer, g_final, final_norm,
               q, c_col, kt_cache, vt_cache, cext_cache, k_new, v_new, cext_new, n_pair, n_head):
    n, d = x2d.shape
    bsz, tq, hw = q.shape
    cache_len = kt_cache.shape[2]
    hd = hw // n_head
    width = n_pair * LANES
    n_group = hw // width
    steps = bsz * n_group
    tm = n // steps
    d_ff = w_down_all.shape[1]
    ff_tile = MXU_TILE
    assert n % steps == 0 and tm % 16 == 0 and d_ff % ff_tile == 0 and 2 * hd == LANES
    kern = functools.partial(_post_attn_kernel, d_ff=d_ff, ff_tile=ff_tile, final_norm=final_norm,
                             n_group=n_group, n_pair=n_pair, hd=hd, n_head=n_head)
    new_spec = pl.BlockSpec((None, tq, width), lambda i: (i // n_group, 0, i % n_group))
    cache_spec = pl.BlockSpec((None, width, cache_len), lambda i: (i // n_group, i % n_group, 0))
    return pl.pallas_call(
        kern,
        grid=(steps,),
        in_specs=[
            pl.BlockSpec((tm, d), lambda i: (i, 0)),
            pl.BlockSpec((tm, o2d.shape[1]), lambda i: (i, 0)),
            _resident(w_out.shape),
            _resident((1, d)),
            pl.BlockSpec((None,) + w_in_all.shape[1:], lambda i: (layer, 0, 0), pipeline_mode=pl.Buffered(1)),
            pl.BlockSpec((None,) + w_down_all.shape[1:], lambda i: (layer, 0, 0), pipeline_mode=pl.Buffered(1)),
            _resident((1, d)),
            new_spec,
            pl.BlockSpec((None, tq, LANES), lambda i: (i // n_group, 0, 0)),
            cache_spec, cache_spec,
            pl.BlockSpec((None, LANES, cache_len), lambda i: (i // n_group, 0, 0)),
            new_spec, new_spec,
            pl.BlockSpec((None, LANES, tq), lambda i: (i // n_group, 0, 0)),
        ],
        out_specs=[pl.BlockSpec((tm, d), lambda i: (i, 0)), new_spec],
        out_shape=[jax.ShapeDtypeStruct((n, d), F32), jax.ShapeDtypeStruct((bsz, tq, hw), BF16)],
        scratch_shapes=[pltpu.VMEM((tm, d_ff), BF16)],
        compiler_params=_params(1),
        name="post_attn",
    )(x2d, o2d, w_out, g_ffn, w_in_all, w_down_all, g_final,
      q, c_col, kt_cache, vt_cache, cext_cache, k_new, v_new, cext_new)


def kernel(x_prompt, x_sample, state_gla, cache_fox_k, cache_fox_v, cache_fox_logf,
           norm_mix, gla_w_in, gla_w_g2, gla_b_g, gla_norm, gla_w_out,
           fox_w_in, fox_b_f, fox_w_out, norm_ffn, ffn_w_in, ffn_w_down, norm_final):
    d = x_prompt.shape[-1]
    depth = norm_mix.shape[0]
    groups = [x_prompt, x_sample]
    shapes = [x.shape for x in groups]
    xs = [x.reshape(-1, d) for x in groups]
    row_tiles = [min(ROW_TILE, x.shape[0]) for x in xs]
    ffn_steps = xs[0].shape[0] // row_tiles[0]
    ride_ffn = all(_rider_rows(a, ffn_steps) is not None for a in (ffn_w_in, ffn_w_down))
    w_ffn_in = w_ffn_down = None
    if not ride_ffn:
        w_ffn_in = ffn_w_in.astype(BF16)
        w_ffn_down = ffn_w_down.astype(BF16)

    gla_states = [[], []]
    fox_k, fox_v, fox_f = [[], []], [[], []], [[], []]
    for i in range(depth):
        j = i // 2
        g_mix = norm_mix[i].reshape(1, d)
        g_ffn = norm_ffn[i].reshape(1, d)
        last = i == depth - 1
        if i % 2 == 0:
            _, n_head, dk, dv = state_gla.shape[1:]
            hk, hv = n_head * dk, n_head * dv
            n_main = 2 * hk + 2 * hv
            w_in = gla_w_in[j]
            wt = w_in.T
            w_g2 = gla_w_g2[j].astype(BF16)
            b_g = gla_b_g[j].reshape(1, hk)
            w_out = gla_w_out[j].astype(BF16)
            norm_g = gla_norm[j].reshape(1, hv)
            s0s = [jnp.zeros((shapes[0][0], n_head, dk, dv), F32), state_gla[j]]
            for gi in range(2):
                bsz, t_len, _ = shapes[gi]
                if w_ffn_in is None:
                    proj, glog, w_ffn_in, w_ffn_down = _gla_proj(xs[gi], g_mix, wt, w_g2, b_g, row_tiles[gi],
                                                                 (ffn_w_in, ffn_w_down))
                else:
                    proj, glog = _gla_proj(xs[gi], g_mix, wt, w_g2, b_g, row_tiles[gi])
                og, s_fin = _gla_mix(proj.reshape(bsz, t_len, n_main), glog.reshape(bsz, t_len, hk),
                                     s0s[gi], norm_g, min(t_len, GLA_ROWS),
                                     min(bsz, GLA_SEQS_PROMPT if gi == 0 else GLA_SEQS_SAMPLE))
                gla_states[gi].append(s_fin)
                xs[gi] = _post(xs[gi], og.reshape(-1, hv), w_out, g_ffn, w_ffn_in, w_ffn_down, i,
                               norm_final.reshape(1, d), row_tiles[gi], last)
        else:
            n_head = fox_b_f.shape[1]
            hw = fox_w_out.shape[1]
            hd = hw // n_head
            wt = fox_w_in[j].T
            bf_col = fox_b_f[j].reshape(n_head, 1)
            w_out = fox_w_out[j].astype(BF16)
            g_fin = norm_final.reshape(1, d)
            bsz, t_len, _ = shapes[0]
            qaug, kaug, kt, vt, vtb, logf_p = _fox_proj_prompt(xs[0].reshape(bsz, t_len, d), g_mix, wt, bf_col,
                                                               min(ROW_TILE, t_len), ATTN_TQ, n_head)
            o_p = _fox_attn_prompt(qaug, kaug, vtb, ATTN_PAIRS, n_head)
            fox_k[0].append(kt.reshape(bsz, n_head, hd, t_len).transpose(0, 3, 1, 2))
            fox_v[0].append(vt.reshape(bsz, n_head, hd, t_len).transpose(0, 3, 1, 2))
            fox_f[0].append(jnp.transpose(logf_p, (0, 2, 1)))
            bsz, t_len, _ = shapes[1]
            cache_len = cache_fox_logf.shape[2]
            q, k, v, logf_s, c_col, cext_new, cext_cache, k_rows, v_rows = _fox_proj_sample(
                xs[1].reshape(bsz, t_len, d), g_mix, wt, bf_col,
                jnp.transpose(cache_fox_logf[j], (0, 2, 1)), n_head, min(PROJ_SAMPLE_SEQS, bsz))
            kt_cache = jnp.transpose(cache_fox_k[j], (0, 2, 3, 1)).reshape(bsz, hw, cache_len)
            vt_cache = jnp.transpose(cache_fox_v[j], (0, 2, 3, 1)).reshape(bsz, hw, cache_len)
            attn_args = (q, c_col, kt_cache, vt_cache, cext_cache, k, v, cext_new)
            steps = bsz * (hw // (FUSED_ATTN_PAIRS * LANES))
            if xs[0].shape[0] % steps == 0 and (xs[0].shape[0] // steps) % LANES == 0:
                xs[0], o_s = _post_attn(xs[0], o_p.reshape(-1, hw), w_out, g_ffn, w_ffn_in, w_ffn_down, i,
                                        g_fin, last, *attn_args, FUSED_ATTN_PAIRS, n_head)
            else:
                xs[0] = _post(xs[0], o_p.reshape(-1, hw), w_out, g_ffn, w_ffn_in, w_ffn_down, i,
                              g_fin, row_tiles[0], last)
                o_s = _fox_attn_sample(*attn_args, hw // LANES, n_head)
            fox_k[1].append(k_rows.reshape(bsz, t_len, n_head, hd))
            fox_v[1].append(v_rows.reshape(bsz, t_len, n_head, hd))
            fox_f[1].append(jnp.transpose(logf_s, (0, 2, 1)))
            xs[1] = _post(xs[1], o_s.reshape(-1, hw), w_out, g_ffn, w_ffn_in, w_ffn_down, i,
                          g_fin, row_tiles[1], last)

    y_prompt = xs[0].reshape(shapes[0])
    y_sample = xs[1].reshape(shapes[1])
    st = lambda parts: jnp.stack(parts, axis=0)
    return (y_prompt, y_sample, st(gla_states[0]), st(fox_k[0]), st(fox_v[0]), st(fox_f[0]),
            st(gla_states[1]), st(fox_k[1]), st(fox_v[1]), st(fox_f[1]))
```

```python
import functools

import jax
import jax.numpy as jnp
from jax import lax
from jax.experimental import pallas as pl
from jax.experimental.pallas import tpu as pltpu

F32 = jnp.float32
BF16 = jnp.bfloat16

EPS = 1e-6
MASK_VALUE = -1e30
LOG2E = 1.4426950408889634

GLA_HEADS = 4
GLA_CHUNK = 64
GLA_GATE_TAU = 16.0

LANES = 128
MXU_TILE = 256
VT_ROWS = LANES + 16
VMEM_LIMIT_BYTES = 56 * 1024 * 1024

ROW_TILE = 512
GLA_ROWS = 256
GLA_SEQS_PROMPT = 2
GLA_SEQS_SAMPLE = 4
ATTN_TQ = 256
ATTN_PAIRS = 8
PROJ_SAMPLE_SEQS = 4
FUSED_ATTN_PAIRS = 4


def _params(n_grid):
    return pltpu.CompilerParams(
        dimension_semantics=("arbitrary",) * n_grid,
        vmem_limit_bytes=VMEM_LIMIT_BYTES,
    )


def _resident(shape):
    nd = len(shape)
    return pl.BlockSpec(shape, lambda *_: (0,) * nd, pipeline_mode=pl.Buffered(1))


def _dot(a, b):
    return jnp.dot(a, b, preferred_element_type=F32)


def _dot_nt(a, b):
    return lax.dot_general(a, b, (((1,), (1,)), ((), ())), preferred_element_type=F32)


def _dot_tn(a, b):
    return lax.dot_general(a, b, (((0,), (0,)), ((), ())), preferred_element_type=F32)


def _split3(x):
    hi = x.astype(BF16)
    r1 = x - hi.astype(F32)
    mid = r1.astype(BF16)
    lo = (r1 - mid.astype(F32)).astype(BF16)
    return hi, mid, lo


def _sum01(dot_fn, x, ones_first, mat01):
    acc = None
    for part in _split3(x):
        term = dot_fn(mat01, part) if ones_first else dot_fn(part, mat01)
        acc = term if acc is None else acc + term
    return acc


def _rmsnorm(x, g):
    var = jnp.mean(x * x, axis=-1, keepdims=True)
    return x * lax.rsqrt(var + EPS) * g


def _log_sigmoid(z):
    return jnp.minimum(z, 0.0) - jnp.log1p(jnp.exp(-jnp.abs(z)))


def _silu(z):
    return z * jax.nn.sigmoid(z)


def _identity(n):
    r = lax.broadcasted_iota(jnp.int32, (n, n), 0)
    c = lax.broadcasted_iota(jnp.int32, (n, n), 1)
    return (r == c).astype(BF16)


def _cast_weights_once(first_step, wt_ref, wtb_ref):
    @pl.when(first_step)
    def _():
        wtb_ref[...] = wt_ref[...].astype(BF16)


def _gla_proj_kernel(x_ref, g_ref, wt_ref, wg2_ref, bg_ref, *rest, n_main, n_ride):
    ride_in, (proj_ref, glog_ref) = rest[:n_ride], rest[n_ride:n_ride + 2]
    ride_out, wtb_ref = rest[n_ride + 2:2 * n_ride + 2], rest[-1]
    _cast_weights_once(pl.program_id(0) == 0, wt_ref, wtb_ref)
    rank = wg2_ref.shape[0]
    h = _rmsnorm(x_ref[...], g_ref[...]).astype(BF16)
    proj_ref[...] = _dot_nt(h, wtb_ref[0:n_main, :])
    gl_t = _dot_nt(wtb_ref[n_main:n_main + rank, :], h).astype(BF16)
    z = _dot_tn(gl_t, wg2_ref[...]) + bg_ref[...]
    glog_ref[...] = _log_sigmoid(z) / GLA_GATE_TAU
    for src, dst in zip(ride_in, ride_out):
        dst[...] = src[...].astype(BF16)


def _rider_rows(a, steps):
    layers, rows, _ = a.shape
    if steps % layers or rows % (steps // layers) or (rows // (steps // layers)) % 16:
        return None
    return rows // (steps // layers)


def _gla_proj(x2d, g, wt, w_g2, b_g, tm, riders=()):
    n, d = x2d.shape
    rank, hk = w_g2.shape
    n_main = wt.shape[0] - rank
    steps = n // tm
    ride_specs = []
    for a in riders:
        per = steps // a.shape[0]
        ride_specs.append(pl.BlockSpec((1, _rider_rows(a, steps), a.shape[2]),
                                       lambda i, per=per: (i // per, i % per, 0)))
    return pl.pallas_call(
        functools.partial(_gla_proj_kernel, n_main=n_main, n_ride=len(riders)),
        grid=(steps,),
        in_specs=[
            pl.BlockSpec((tm, d), lambda i: (i, 0)),
            _resident((1, d)),
            _resident(wt.shape),
            _resident(w_g2.shape),
            _resident((1, hk)),
        ] + ride_specs,
        out_specs=[
            pl.BlockSpec((tm, n_main), lambda i: (i, 0)),
            pl.BlockSpec((tm, hk), lambda i: (i, 0)),
        ] + ride_specs,
        out_shape=[
            jax.ShapeDtypeStruct((n, n_main), F32),
            jax.ShapeDtypeStruct((n, hk), F32),
        ] + [jax.ShapeDtypeStruct(a.shape, BF16) for a in riders],
        scratch_shapes=[pltpu.VMEM(wt.shape, BF16)],
        compiler_params=_params(1),
        name="gla_proj",
    )(x2d, g, wt, w_g2, b_g, *riders)


def _gla_mix_kernel(q_ref, k_ref, v_ref, r_ref, glog_ref, s0_ref, ng_ref,
                    og_ref, sfin_ref, s_ref, *, dk, dv, whole_seq=False):
    always = lambda fn: fn()
    first = always if whole_seq else pl.when(pl.program_id(1) == 0)
    final = always if whole_seq else pl.when(pl.program_id(1) == pl.num_programs(1) - 1)
    nb, tb, _ = q_ref.shape
    n_chunk = tb // GLA_CHUNK
    seqs = range(nb)
    heads = range(GLA_HEADS)
    units = [(sb, h) for sb in seqs for h in heads]
    chunks = [slice(c * GLA_CHUNK, (c + 1) * GLA_CHUNK) for c in range(n_chunk)]
    ksl = [slice(h * dk, (h + 1) * dk) for h in heads]
    vsl = [slice(h * dv, (h + 1) * dv) for h in heads]

    @first
    def _():
        s_ref[...] = s0_ref[...]

    row = lax.broadcasted_iota(jnp.int32, (tb, tb), 0)
    col = lax.broadcasted_iota(jnp.int32, (tb, tb), 1)
    same_chunk = (row // GLA_CHUNK) == (col // GLA_CHUNK)
    causal = same_chunk & (col <= row)
    cum_mat = causal.astype(BF16)

    b = [_sum01(_dot, glog_ref[sb], True, cum_mat) for sb in seqs]
    qe, ke, kd, dec_t = [], [], [], []
    for sb in seqs:
        tot_rows = [b[sb][(c + 1) * GLA_CHUNK - 1:(c + 1) * GLA_CHUNK, :] for c in range(n_chunk)]
        b_last = jnp.concatenate([jnp.broadcast_to(r, (GLA_CHUNK, r.shape[1])) for r in tot_rows], axis=0)
        pad_rows = [jnp.zeros_like(tot_rows[0])] * (8 - n_chunk % 8 if n_chunk % 8 else 0)
        dec_t.append(jnp.exp(_to_columns(jnp.concatenate(tot_rows + pad_rows, axis=0))))
        q = q_ref[sb]
        k = k_ref[sb]
        qe.append((q * jnp.exp(b[sb]) * (dk ** -0.5)).astype(BF16))
        ke.append((k * jnp.exp(-b[sb])).astype(BF16))
        kd.append((k * jnp.exp(b_last - b[sb])).astype(BF16))

    v_b = {u: v_ref[u[0], :, vsl[u[1]]].astype(BF16) for u in units}
    a_raw = {(sb, h): _dot_nt(qe[sb][:, ksl[h]], ke[sb][:, ksl[h]]) for sb, h in units}
    upd = {(sb, h): [_dot_tn(kd[sb][rs, ksl[h]], v_b[sb, h][rs]) for rs in chunks] for sb, h in units}
    o_intra = {u: _dot(jnp.where(causal, a_raw[u], 0.0).astype(BF16), v_b[u]) for u in units}
    s_in = {}
    for sb, h in units:
        s = s_ref[sb, h]
        s_in[sb, h] = []
        for c in range(n_chunk):
            s_in[sb, h].append(s.astype(BF16))
            s = s * dec_t[sb][ksl[h], c:c + 1] + upd[sb, h][c]
        s_ref[sb, h] = s
    for sb, h in units:
        o_parts = [o_intra[sb, h][rs] + _dot(qe[sb][rs, ksl[h]], s_in[sb, h][c]) for c, rs in enumerate(chunks)]
        o = o_parts[0] if n_chunk == 1 else jnp.concatenate(o_parts, axis=0)
        on = _rmsnorm(o, ng_ref[:, vsl[h]])
        og_ref[sb, :, vsl[h]] = (on * _silu(r_ref[sb, :, vsl[h]])).astype(BF16)

    @final
    def _():
        sfin_ref[...] = s_ref[...]


def _gla_proj_mix_kernel(x_ref, g_ref, wt_ref, wg2_ref, bg_ref, s0_ref, ng_ref,
                         og_ref, sfin_ref, wtb_ref, proj_ref, glog_ref, s_ref, *, n_main, dk, dv):
    nb, tb, d = x_ref.shape
    hk, hv = GLA_HEADS * dk, GLA_HEADS * dv
    _cast_weights_once(pl.program_id(0) == 0, wt_ref, wtb_ref)
    rank = wg2_ref.shape[0]
    h = _rmsnorm(x_ref[...].reshape(nb * tb, d), g_ref[...]).astype(BF16)
    proj_ref[...] = _dot_nt(h, wtb_ref[0:n_main, :]).reshape(nb, tb, n_main)
    gl_t = _dot_nt(wtb_ref[n_main:n_main + rank, :], h).astype(BF16)
    z = _dot_tn(gl_t, wg2_ref[...]) + bg_ref[...]
    glog_ref[...] = (_log_sigmoid(z) / GLA_GATE_TAU).reshape(nb, tb, hk)
    _gla_mix_kernel(proj_ref.at[:, :, 0:hk], proj_ref.at[:, :, hk:2 * hk], proj_ref.at[:, :, 2 * hk:2 * hk + hv],
                    proj_ref.at[:, :, 2 * hk + hv:2 * hk + 2 * hv], glog_ref, s0_ref, ng_ref,
                    og_ref, sfin_ref, s_ref, dk=dk, dv=dv, whole_seq=True)


def _gla_proj_mix(x3, g, wt, w_g2, b_g, s0, norm_g, nb):
    bsz, tb, d = x3.shape
    _, n_head, dk, dv = s0.shape
    rank, hk = w_g2.shape
    hv = n_head * dv
    n_main = wt.shape[0] - rank
    assert bsz % nb == 0 and tb % GLA_CHUNK == 0 and n_main == 2 * hk + 2 * hv
    state_spec = pl.BlockSpec((nb, n_head, dk, dv), lambda b: (b, 0, 0, 0))
    return pl.pallas_call(
        functools.partial(_gla_proj_mix_kernel, n_main=n_main, dk=dk, dv=dv),
        grid=(bsz // nb,),
        in_specs=[
            pl.BlockSpec((nb, tb, d), lambda b: (b, 0, 0)),
            _resident((1, d)),
            _resident(wt.shape),
            _resident(w_g2.shape),
            _resident((1, hk)),
            state_spec,
            _resident((1, hv)),
        ],
        out_specs=[pl.BlockSpec((nb, tb, hv), lambda b: (b, 0, 0)), state_spec],
        out_shape=[jax.ShapeDtypeStruct((bsz, tb, hv), BF16), jax.ShapeDtypeStruct(s0.shape, F32)],
        scratch_shapes=[
            pltpu.VMEM(wt.shape, BF16),
            pltpu.VMEM((nb, tb, n_main), F32),
            pltpu.VMEM((nb, tb, hk), F32),
            pltpu.VMEM((nb, n_head, dk, dv), F32),
        ],
        compiler_params=_params(1),
        name="gla_proj_mix",
    )(x3, g, wt, w_g2, b_g, s0, norm_g)


def _gla_mix(proj3, glog3, s0, norm_g, tb, nb):
    bsz, t_len, _ = proj3.shape
    _, n_head, dk, dv = s0.shape
    hk, hv = n_head * dk, n_head * dv
    assert t_len % tb == 0 and tb % GLA_CHUNK == 0 and hv == 2 * hk and bsz % nb == 0
    kern = functools.partial(_gla_mix_kernel, dk=dk, dv=dv)
    state_spec = pl.BlockSpec((nb, n_head, dk, dv), lambda b, t: (b, 0, 0, 0))
    return pl.pallas_call(
        kern,
        grid=(bsz // nb, t_len // tb),
        in_specs=[
            pl.BlockSpec((nb, tb, hk), lambda b, t: (b, t, 0)),
            pl.BlockSpec((nb, tb, hk), lambda b, t: (b, t, 1)),
            pl.BlockSpec((nb, tb, hv), lambda b, t: (b, t, 1)),
            pl.BlockSpec((nb, tb, hv), lambda b, t: (b, t, 2)),
            pl.BlockSpec((nb, tb, hk), lambda b, t: (b, t, 0)),
            state_spec,
            _resident((1, hv)),
        ],
        out_specs=[
            pl.BlockSpec((nb, tb, hv), lambda b, t: (b, t, 0)),
            state_spec,
        ],
        out_shape=[
            jax.ShapeDtypeStruct((bsz, t_len, hv), BF16),
            jax.ShapeDtypeStruct(s0.shape, F32),
        ],
        scratch_shapes=[pltpu.VMEM((nb, n_head, dk, dv), F32)],
        compiler_params=_params(2),
        name="gla_mix",
    )(proj3, proj3, proj3, proj3, glog3, s0, norm_g)


def _post_kernel(x_ref, o_ref, wo_ref, g_ref, win_ref, wdown_ref, gfin_ref, y_ref, act_ref,
                 *, d_ff, ff_tile, final_norm, hooks=None):
    x1 = x_ref[...] + _dot(o_ref[...], wo_ref[...])
    if hooks and -1 in hooks:
        hooks[-1]()
    h = _rmsnorm(x1, g_ref[...]).astype(BF16)
    for j in range(d_ff // ff_tile):
        gate = _dot(h, win_ref[:, j * ff_tile:(j + 1) * ff_tile])
        up = _dot(h, win_ref[:, d_ff + j * ff_tile:d_ff + (j + 1) * ff_tile])
        act_ref[:, j * ff_tile:(j + 1) * ff_tile] = (_silu(gate) * up).astype(BF16)
        if hooks and j in hooks:
            hooks[j]()
    y = x1 + _dot(act_ref[...], wdown_ref[...])
    if final_norm:
        y = _rmsnorm(y, gfin_ref[...])
    y_ref[...] = y


def _post(x2d, o2d, w_out, g_ffn, w_in_all, w_down_all, layer, g_final, tm, final_norm):
    n, d = x2d.shape
    d_ff = w_down_all.shape[1]
    ff_tile = MXU_TILE
    assert d_ff % ff_tile == 0 and n % tm == 0
    kern = functools.partial(_post_kernel, d_ff=d_ff, ff_tile=ff_tile, final_norm=final_norm)
    return pl.pallas_call(
        kern,
        grid=(n // tm,),
        in_specs=[
            pl.BlockSpec((tm, d), lambda i: (i, 0)),
            pl.BlockSpec((tm, o2d.shape[1]), lambda i: (i, 0)),
            _resident(w_out.shape),
            _resident((1, d)),
            pl.BlockSpec((None,) + w_in_all.shape[1:], lambda i: (layer, 0, 0), pipeline_mode=pl.Buffered(1)),
            pl.BlockSpec((None,) + w_down_all.shape[1:], lambda i: (layer, 0, 0), pipeline_mode=pl.Buffered(1)),
            _resident((1, d)),
        ],
        out_specs=pl.BlockSpec((tm, d), lambda i: (i, 0)),
        out_shape=jax.ShapeDtypeStruct((n, d), F32),
        scratch_shapes=[pltpu.VMEM((tm, d_ff), BF16)],
        compiler_params=_params(1),
        name="post_final" if final_norm else "post",
    )(x2d, o2d, w_out, g_ffn, w_in_all, w_down_all, g_final)


def _lane_cumsum(x, tile):
    r = lax.broadcasted_iota(jnp.int32, (tile, tile), 0)
    c = lax.broadcasted_iota(jnp.int32, (tile, tile), 1)
    upper = (r <= c).astype(BF16)
    rows = x.shape[0]
    carry = jnp.zeros((rows, 1), F32)
    out = []
    for j in range(x.shape[1] // tile):
        parts = jnp.concatenate(_split3(x[:, j * tile:(j + 1) * tile]), axis=0)
        s = _dot(parts, upper)
        blk = carry + s[0:rows] + s[rows:2 * rows] + s[2 * rows:3 * rows]
        out.append(blk)
        carry = blk[:, tile - 1:tile]
    return out[0] if len(out) == 1 else jnp.concatenate(out, axis=1)


def _to_columns(x_t):
    rows = x_t.shape[0]
    parts = jnp.concatenate(_split3(x_t), axis=0)
    r = lax.broadcasted_iota(jnp.int32, (3 * rows, LANES), 0)
    c = lax.broadcasted_iota(jnp.int32, (3 * rows, LANES), 1)
    place = ((r % rows) == c).astype(BF16)
    return _dot_tn(parts, place)


def _key_ext_cols(c_cols, n_head):
    hi, mid, lo = (p.astype(F32) for p in _split3(c_cols * LOG2E))
    lane = lax.broadcasted_iota(jnp.int32, c_cols.shape, 1)
    ones = ((lane >= 3 * n_head) & (lane < 3 * n_head + 3)).astype(F32)
    ext = hi + pltpu.roll(mid, n_head, 1) + pltpu.roll(lo, 2 * n_head, 1) + ones
    return ext.astype(BF16)


def _key_ext_rows(c_rows):
    n_head, n = c_rows.shape
    hi, mid, lo = _split3(c_rows * LOG2E)
    r = lax.broadcasted_iota(jnp.int32, (n_head, n), 0)
    ones = (r < 3).astype(BF16)
    zeros = jnp.zeros((LANES - 4 * n_head, n), BF16)
    return jnp.concatenate([hi, mid, lo, ones, zeros], axis=0)


def _query_ext(idx, hh, n_head, cq):
    cq_hi, cq_mid, cq_lo = (p.astype(F32) for p in _split3(cq * LOG2E))
    ext = jnp.where((idx == hh) | (idx == n_head + hh) | (idx == 2 * n_head + hh), -1.0, 0.0)
    ext = jnp.where(idx == 3 * n_head, cq_hi, ext)
    ext = jnp.where(idx == 3 * n_head + 1, cq_mid, ext)
    ext = jnp.where(idx == 3 * n_head + 2, cq_lo, ext)
    return ext.astype(BF16)


def _fox_proj_prompt_kernel(x_ref, g_ref, wtf_ref, bfc_ref,
                            qaug_ref, kaug_ref, kt_ref, vt_ref, vtb_ref, logft_ref, carry_ref, wt_ref,
                            *, hw, n_head, scale, tq):
    t = pl.program_id(1)
    tm = x_ref.shape[0]
    _cast_weights_once((pl.program_id(0) == 0) & (t == 0), wtf_ref, wt_ref)

    @pl.when(t == 0)
    def _():
        carry_ref[...] = jnp.zeros_like(carry_ref)

    h = _rmsnorm(x_ref[...], g_ref[...]).astype(BF16)
    logf_t = _log_sigmoid(_dot_nt(wt_ref[3 * hw:3 * hw + n_head, :], h) + bfc_ref[...])
    logft_ref[...] = logf_t
    ct_blk = _lane_cumsum(logf_t, min(tm, MXU_TILE)) + carry_ref[:, 0:1]
    carry_ref[...] = jnp.broadcast_to(ct_blk[:, tm - 1:tm], carry_ref.shape)
    cext = _key_ext_cols(_to_columns(ct_blk), n_head)
    kt = _dot_nt(wt_ref[hw:2 * hw, :], h)
    vt = _dot_nt(wt_ref[2 * hw:3 * hw, :], h)
    qt = (_dot_nt(wt_ref[0:hw, :], h) * scale).astype(BF16)
    kt_ref[...] = kt
    hd = hw // n_head
    ext_rows = 4 * n_head
    rr = lax.broadcasted_iota(jnp.int32, (ext_rows, tq), 0)
    rq = lax.broadcasted_iota(jnp.int32, (LANES, tq), 0) // hd
    zeros = jnp.zeros((LANES - ext_rows, tq), BF16)
    for p in range(hw // LANES):
        for jq in range(tm // tq):
            qs = slice(jq * tq, (jq + 1) * tq)
            q_pair = qt[p * LANES:(p + 1) * LANES, qs]
            cols = []
            for hh in (2 * p, 2 * p + 1):
                top = jnp.where(rq == hh - 2 * p, q_pair, jnp.zeros_like(q_pair))
                cols.append(jnp.concatenate([top, _query_ext(rr, hh, n_head, ct_blk[hh:hh + 1, qs]), zeros], axis=0))
            qaug_ref[p, jq] = jnp.concatenate(cols, axis=1)
    k = kt.T
    for p in range(hw // LANES):
        kaug_ref[p, :, 0:LANES] = k[:, p * LANES:(p + 1) * LANES].astype(BF16)
        kaug_ref[p, :, LANES:2 * LANES] = cext
    vt_ref[...] = vt
    ones = jnp.ones((VT_ROWS - LANES, tm), BF16)
    for p in range(hw // LANES):
        vtb_ref[p, 0:LANES, :] = vt[p * LANES:(p + 1) * LANES, :].astype(BF16)
        vtb_ref[p, LANES:VT_ROWS, :] = ones


def _fox_proj_prompt(x3, g, wt, bf_col, tm, tq, n_head):
    bsz, t_len, d = x3.shape
    hw = (wt.shape[0] - n_head) // 3
    n_pair = hw // LANES
    assert t_len % tm == 0 and tm % tq == 0 and tq % LANES == 0 and 3 * n_head + 3 <= LANES
    kern = functools.partial(_fox_proj_prompt_kernel, hw=hw, n_head=n_head,
                             scale=(hw // n_head) ** -0.5 * LOG2E, tq=tq)
    feat = pl.BlockSpec((None, hw, tm), lambda b, t: (b, 0, t))
    head = pl.BlockSpec((None, n_head, tm), lambda b, t: (b, 0, t))
    return pl.pallas_call(
        kern,
        grid=(bsz, t_len // tm),
        in_specs=[
            pl.BlockSpec((None, tm, d), lambda b, t: (b, t, 0)),
            _resident((1, d)),
            _resident(wt.shape),
            _resident(bf_col.shape),
        ],
        out_specs=[
            pl.BlockSpec((None, n_pair, tm // tq, 2 * LANES, 2 * tq), lambda b, t: (b, 0, t, 0, 0)),
            pl.BlockSpec((None, n_pair, tm, 2 * LANES), lambda b, t: (b, 0, t, 0)),
            feat, feat,
            pl.BlockSpec((None, n_pair, VT_ROWS, tm), lambda b, t: (b, 0, 0, t)),
            head,
        ],
        out_shape=[
            jax.ShapeDtypeStruct((bsz, n_pair, t_len // tq, 2 * LANES, 2 * tq), BF16),
            jax.ShapeDtypeStruct((bsz, n_pair, t_len, 2 * LANES), BF16),
            jax.ShapeDtypeStruct((bsz, hw, t_len), F32),
            jax.ShapeDtypeStruct((bsz, hw, t_len), F32),
            jax.ShapeDtypeStruct((bsz, n_pair, VT_ROWS, t_len), BF16),
            jax.ShapeDtypeStruct((bsz, n_head, t_len), F32),
        ],
        scratch_shapes=[pltpu.VMEM((n_head, LANES), F32), pltpu.VMEM(wt.shape, BF16)],
        compiler_params=_params(2),
        name="fox_proj_prompt",
    )(x3, g, wt, bf_col)


def _fox_proj_sample_kernel(x_ref, g_ref, wtf_ref, bfc_ref, lct_ref,
                            q_ref, k_ref, v_ref, logft_ref, ccol_ref, cextn_ref, cextc_ref, k4_ref, v4_ref, wt_ref,
                            *, hw, n_head, scale):
    _cast_weights_once(pl.program_id(0) == 0, wtf_ref, wt_ref)
    nb, tm, d = x_ref.shape
    cache_len = lct_ref.shape[2]
    hd = hw // n_head
    h = _rmsnorm(x_ref[...].reshape(nb * tm, d), g_ref[...]).astype(BF16)
    q_ref[...] = (_dot_nt(h, wt_ref[0:hw, :]) * scale).astype(BF16).reshape(nb, tm, hw)
    for src_rows, tok_ref, row_ref in ((slice(hw, 2 * hw), k_ref, k4_ref), (slice(2 * hw, 3 * hw), v_ref, v4_ref)):
        kv = _dot_nt(h, wt_ref[src_rows, :])
        tok_ref[...] = kv.reshape(nb, tm, hw)
        for b in range(nb):
            for hh in range(n_head):
                row_ref[b, pl.ds(hh, tm, stride=n_head), :] = kv[b * tm:(b + 1) * tm, hh * hd:(hh + 1) * hd]

    ct_cache = _lane_cumsum(lct_ref[...].reshape(nb * n_head, cache_len), min(cache_len, MXU_TILE))
    logf_all = _log_sigmoid(_dot_nt(wt_ref[3 * hw:3 * hw + n_head, :], h) + bfc_ref[...])
    for b in range(nb):
        ct_b = ct_cache[b * n_head:(b + 1) * n_head, :]
        cextc_ref[b] = _key_ext_rows(ct_b)
        logf_t = logf_all[:, b * tm:(b + 1) * tm]
        logft_ref[b] = logf_t
        ct_new = _lane_cumsum(logf_t, tm) + ct_b[:, cache_len - 1:cache_len]
        cextn_ref[b] = _key_ext_rows(ct_new)
        ccol_ref[b] = _to_columns(ct_new)


def _fox_proj_sample(x3, g, wt, bf_col, logf_cache_t, n_head, nb):
    bsz, tm, d = x3.shape
    hw = (wt.shape[0] - n_head) // 3
    cache_len = logf_cache_t.shape[2]
    assert 4 * n_head <= LANES and bsz % nb == 0
    kern = functools.partial(_fox_proj_sample_kernel, hw=hw, n_head=n_head,
                             scale=(hw // n_head) ** -0.5 * LOG2E)
    tok = pl.BlockSpec((nb, tm, hw), lambda b: (b, 0, 0))
    return pl.pallas_call(
        kern,
        grid=(bsz // nb,),
        in_specs=[
            pl.BlockSpec((nb, tm, d), lambda b: (b, 0, 0)),
            _resident((1, d)),
            _resident(wt.shape),
            _resident(bf_col.shape),
            pl.BlockSpec((nb, n_head, cache_len), lambda b: (b, 0, 0)),
        ],
        out_specs=[
            tok, tok, tok,
            pl.BlockSpec((nb, n_head, tm), lambda b: (b, 0, 0)),
            pl.BlockSpec((nb, tm, LANES), lambda b: (b, 0, 0)),
            pl.BlockSpec((nb, LANES, tm), lambda b: (b, 0, 0)),
            pl.BlockSpec((nb, LANES, cache_len), lambda b: (b, 0, 0)),
            pl.BlockSpec((nb, tm * n_head, hw // n_head), lambda b: (b, 0, 0)),
            pl.BlockSpec((nb, tm * n_head, hw // n_head), lambda b: (b, 0, 0)),
        ],
        out_shape=[
            jax.ShapeDtypeStruct((bsz, tm, hw), BF16),
            jax.ShapeDtypeStruct((bsz, tm, hw), F32),
            jax.ShapeDtypeStruct((bsz, tm, hw), F32),
            jax.ShapeDtypeStruct((bsz, n_head, tm), F32),
            jax.ShapeDtypeStruct((bsz, tm, LANES), F32),
            jax.ShapeDtypeStruct((bsz, LANES, tm), BF16),
            jax.ShapeDtypeStruct((bsz, LANES, cache_len), BF16),
            jax.ShapeDtypeStruct((bsz, tm * n_head, hw // n_head), F32),
            jax.ShapeDtypeStruct((bsz, tm * n_head, hw // n_head), F32),
        ],
        scratch_shapes=[pltpu.VMEM(wt.shape, BF16)],
        compiler_params=_params(1),
        name="fox_proj_sample",
    )(x3, g, wt, bf_col, logf_cache_t)


def _attn_update(n_pair, kaug_of, vt_of, vt_prev_last, mask, next_kaug0,
                 qaug_ref, m_ref, acc_ref, s0_ref, plast_ref, alast_ref):
    last = n_pair - 1
    s_next = s0_ref[...]
    acc_ref[last] = acc_ref[last] * alast_ref[...] + _dot(vt_prev_last(), plast_ref[...])
    pending = None
    for p in range(n_pair):
        s = s_next
        if p + 1 < n_pair:
            s_next = _dot(kaug_of(p + 1), qaug_ref[p + 1])
        elif next_kaug0 is not None:
            s0_ref[...] = _dot(next_kaug0(), qaug_ref[0])
        if mask is not None:
            s = jnp.where(mask, s, MASK_VALUE)
        m_old = m_ref[p]
        m_new = jnp.maximum(m_old, jnp.max(s, axis=0, keepdims=True))
        alpha = jnp.exp2(m_old - m_new)
        pr = jnp.exp2(s - m_new).astype(BF16)
        m_ref[p] = m_new
        if pending is not None:
            q, pr_q, alpha_q = pending
            acc_ref[q] = acc_ref[q] * alpha_q + _dot(vt_of(q), pr_q)
        pending = (p, pr, alpha)
    plast_ref[...] = pending[1]
    alast_ref[...] = pending[2]


def _attn_finish(o_ref, acc_ref, *, n_pair, tq, hd):
    for p in range(n_pair):
        full = acc_ref[p, 0:LANES, :] * (1.0 / acc_ref[p, LANES:LANES + 1, :])
        z = jnp.concatenate([full[0:hd, 0:tq], full[hd:2 * hd, tq:2 * tq]], axis=0)
        o_ref[:, p * LANES:(p + 1) * LANES] = z.T.astype(o_ref.dtype)


def _fox_attn_prompt_kernel(qaug_ref, kaug_ref, vt_ref, o_ref,
                            m_ref, acc_ref, s0_ref, plast_ref, alast_ref, *, tq, n_pair, hd):
    i = pl.program_id(2)
    last = n_pair - 1
    state = (qaug_ref, m_ref, acc_ref, s0_ref, plast_ref, alast_ref)

    s0_ref[...] = _dot(kaug_ref[0, 0:tq, :], qaug_ref[0])
    m_ref[...] = jnp.full_like(m_ref, MASK_VALUE)
    acc_ref[...] = jnp.zeros_like(acc_ref)
    plast_ref[...] = jnp.zeros_like(plast_ref)
    alast_ref[...] = jnp.ones_like(alast_ref)

    def keys(j):
        return pl.ds(pl.multiple_of(j * tq, tq), tq)

    def vt_of(p, ks):
        return vt_ref[p, :, ks]

    def kv_block(j, mask, has_next):
        ks = keys(j)
        _attn_update(n_pair, lambda p: kaug_ref[p, ks, :], lambda p: vt_of(p, ks),
                     lambda: vt_of(last, keys(jnp.maximum(j - 1, 0))), mask,
                     (lambda: kaug_ref[0, keys(j + 1), :]) if has_next else None, *state)

    def body(j, carry):
        kv_block(j, None, True)
        return carry

    lax.fori_loop(0, i, body, 0)
    row = lax.broadcasted_iota(jnp.int32, (tq, 2 * tq), 0)
    col = lax.broadcasted_iota(jnp.int32, (tq, 2 * tq), 1) % tq
    kv_block(i, row <= col, False)
    acc_ref[last] = acc_ref[last] * alast_ref[...] + _dot(vt_of(last, keys(i)), plast_ref[...])
    _attn_finish(o_ref, acc_ref, n_pair=n_pair, tq=tq, hd=hd)


def _fox_attn_prompt(qaug, kaug, vtb, n_pair, n_head):
    bsz, n_pair_all, n_qblk, _, tq2 = qaug.shape
    tq = tq2 // 2
    t_len = n_qblk * tq
    hw = n_pair_all * LANES
    hd = hw // n_head
    assert 2 * hd == LANES and n_pair_all % n_pair == 0
    width = n_pair * LANES
    n_group = n_pair_all // n_pair
    kern = functools.partial(_fox_attn_prompt_kernel, tq=tq, n_pair=n_pair, hd=hd)
    return pl.pallas_call(
        kern,
        grid=(bsz, n_group, n_qblk),
        in_specs=[
            pl.BlockSpec((None, n_pair, None, 2 * LANES, 2 * tq), lambda b, g, i: (b, g, i, 0, 0)),
            pl.BlockSpec((None, n_pair, t_len, 2 * LANES), lambda b, g, i: (b, g, 0, 0)),
            pl.BlockSpec((None, n_pair, VT_ROWS, t_len), lambda b, g, i: (b, g, 0, 0)),
        ],
        out_specs=pl.BlockSpec((None, tq, width), lambda b, g, i: (b, i, g)),
        out_shape=jax.ShapeDtypeStruct((bsz, t_len, hw), BF16),
        scratch_shapes=[
            pltpu.VMEM((n_pair, 1, 2 * tq), F32),
            pltpu.VMEM((n_pair, VT_ROWS, 2 * tq), F32),
            pltpu.VMEM((tq, 2 * tq), F32),
            pltpu.VMEM((tq, 2 * tq), BF16),
            pltpu.VMEM((1, 2 * tq), F32),
        ],
        compiler_params=_params(3),
        name="fox_attn_prompt",
    )(qaug, kaug, vtb)


def _sample_attn_stages(q_ref, ccol_ref, kt_ref, vt_ref, cextc_ref, kn_ref, vn_ref, cextn_ref, o_ref,
                        *, g, n_pair, hd, n_head):
    tq = q_ref.shape[0]
    eye = _identity(LANES)
    lane = lax.broadcasted_iota(jnp.int32, (tq, LANES), 1)
    row = lax.broadcasted_iota(jnp.int32, (2 * tq, tq), 0) % tq
    col = lax.broadcasted_iota(jnp.int32, (2 * tq, tq), 1)
    causal_new = col <= row
    ccol = ccol_ref[...]

    def scores(p):
        feat = slice(p * LANES, (p + 1) * LANES)
        q_pair = q_ref[:, feat]
        blocks = []
        for h in range(2):
            hh = g * (2 * n_pair) + 2 * p + h
            cq = jnp.sum(jnp.where(lane == hh, ccol, 0.0), axis=1, keepdims=True)
            top = jnp.where(lane // hd == h, q_pair, jnp.zeros_like(q_pair))
            blocks.append(jnp.concatenate([top, _query_ext(lane, hh, n_head, cq)], axis=1))
        qaug = jnp.concatenate(blocks, axis=0)
        s_c = _dot(qaug, jnp.concatenate([kt_ref[feat, :].astype(BF16), cextc_ref[...]], axis=0))
        kn_t = _dot_nt(eye, kn_ref[:, feat].astype(BF16)).astype(BF16)
        s_n = _dot(qaug, jnp.concatenate([kn_t, cextn_ref[...]], axis=0))
        return s_c, jnp.where(causal_new, s_n, MASK_VALUE)

    held = {}

    def score_stage(p):
        held["s", p] = scores(p)

    def softmax_stage(p):
        s_c, s_n = held.pop(("s", p))
        m = jnp.maximum(jnp.max(s_c, axis=1, keepdims=True), jnp.max(s_n, axis=1, keepdims=True))
        p_c = jnp.exp2(s_c - m)
        p_n = jnp.exp2(s_n - m)
        l = jnp.sum(p_c, axis=1, keepdims=True) + jnp.sum(p_n, axis=1, keepdims=True)
        held["p", p] = (p_c.astype(BF16), p_n.astype(BF16), l)

    def value_stage(p):
        p_c, p_n, l = held.pop(("p", p))
        feat = slice(p * LANES, (p + 1) * LANES)
        o = _dot_nt(p_c, vt_ref[feat, :].astype(BF16)) + _dot(p_n, vn_ref[:, feat].astype(BF16))
        o = o / l
        o_ref[:, feat] = jnp.where(lane < hd, o[0:tq], o[tq:2 * tq]).astype(o_ref.dtype)

    return score_stage, softmax_stage, value_stage


def _fox_attn_sample_kernel(*refs, n_pair, hd, n_head):
    score, softmax, value = _sample_attn_stages(*refs, g=pl.program_id(1), n_pair=n_pair, hd=hd, n_head=n_head)
    score(0)
    for p in range(n_pair):
        if p + 1 < n_pair:
            score(p + 1)
        softmax(p)
        value(p)


def _post_attn_kernel(x_ref, o_ref, wo_ref, g_ref, win_ref, wdown_ref, gfin_ref,
                      q_ref, ccol_ref, kt_ref, vt_ref, cextc_ref, kn_ref, vn_ref, cextn_ref,
                      y_ref, os_ref, act_ref, *, d_ff, ff_tile, final_norm, n_group, n_pair, hd, n_head):
    score, softmax, value = _sample_attn_stages(
        q_ref, ccol_ref, kt_ref, vt_ref, cextc_ref, kn_ref, vn_ref, cextn_ref, os_ref,
        g=pl.program_id(0) % n_group, n_pair=n_pair, hd=hd, n_head=n_head)
    n_ff = d_ff // ff_tile
    lag = max(1, n_ff - 1 - n_pair) // 2 + 2
    hooks = {}

    def at(j, fn):
        j = min(j, n_ff - 1)
        prev = hooks.get(j)
        hooks[j] = fn if prev is None else (lambda: (prev(), fn()))

    for p in range(n_pair):
        at(p - 1, functools.partial(score, p))
        at(p - 1, functools.partial(softmax, p))
        at(p - 1 + lag, functools.partial(value, p))
    _post_kernel(x_ref, o_ref, wo_ref, g_ref, win_ref, wdown_ref, gfin_ref, y_ref, act_ref,
                 d_ff=d_ff, ff_tile=ff_tile, final_norm=final_norm, hooks=hooks)


def _fox_attn_sample(q, c_col, kt_cache, vt_cache, cext_cache, k_new, v_new, cext_new, n_pair, n_head):
    bsz, tq, hw = q.shape
    cache_len = kt_cache.shape[2]
    hd = hw // n_head
    assert 2 * hd == LANES and 4 * n_head <= LANES
    width = n_pair * LANES
    new_spec = pl.BlockSpec((None, tq, width), lambda b, g: (b, 0, g))
    cache_spec = pl.BlockSpec((None, width, cache_len), lambda b, g: (b, g, 0))
    kern = functools.partial(_fox_attn_sample_kernel, n_pair=n_pair, hd=hd, n_head=n_head)
    return pl.pallas_call(
        kern,
        grid=(bsz, hw // width),
        in_specs=[
            new_spec,
            pl.BlockSpec((None, tq, LANES), lambda b, g: (b, 0, 0)),
            cache_spec, cache_spec,
            pl.BlockSpec((None, LANES, cache_len), lambda b, g: (b, 0, 0)),
            new_spec, new_spec,
            pl.BlockSpec((None, LANES, tq), lambda b, g: (b, 0, 0)),
        ],
        out_specs=new_spec,
        out_shape=jax.ShapeDtypeStruct((bsz, tq, hw), BF16),
        compiler_params=_params(2),
        name="fox_attn_sample",
    )(q, c_col, kt_cache, vt_cache, cext_cache, k_new, v_new, cext_new)


def _post_attn(x2d, o2d, w_out, g_ffn, w_in_all, w_down_all, layer, g_final, final_norm,
               q, c_col, kt_cache, vt_cache, cext_cache, k_new, v_new, cext_new, n_pair, n_head):
    n, d = x2d.shape
    bsz, tq, hw = q.shape
    cache_len = kt_cache.shape[2]
    hd = hw // n_head
    width = n_pair * LANES
    n_group = hw // width
    steps = bsz * n_group
    tm = n // steps
    d_ff = w_down_all.shape[1]
    ff_tile = MXU_TILE
    assert n % steps == 0 and tm % 16 == 0 and d_ff % ff_tile == 0 and 2 * hd == LANES
    kern = functools.partial(_post_attn_kernel, d_ff=d_ff, ff_tile=ff_tile, final_norm=final_norm,
                             n_group=n_group, n_pair=n_pair, hd=hd, n_head=n_head)
    new_spec = pl.BlockSpec((None, tq, width), lambda i: (i // n_group, 0, i % n_group))
    cache_spec = pl.BlockSpec((None, width, cache_len), lambda i: (i // n_group, i % n_group, 0))
    return pl.pallas_call(
        kern,
        grid=(steps,),
        in_specs=[
            pl.BlockSpec((tm, d), lambda i: (i, 0)),
            pl.BlockSpec((tm, o2d.shape[1]), lambda i: (i, 0)),
            _resident(w_out.shape),
            _resident((1, d)),
            pl.BlockSpec((None,) + w_in_all.shape[1:], lambda i: (layer, 0, 0), pipeline_mode=pl.Buffered(1)),
            pl.BlockSpec((None,) + w_down_all.shape[1:], lambda i: (layer, 0, 0), pipeline_mode=pl.Buffered(1)),
            _resident((1, d)),
            new_spec,
            pl.BlockSpec((None, tq, LANES), lambda i: (i // n_group, 0, 0)),
            cache_spec, cache_spec,
            pl.BlockSpec((None, LANES, cache_len), lambda i: (i // n_group, 0, 0)),
            new_spec, new_spec,
            pl.BlockSpec((None, LANES, tq), lambda i: (i // n_group, 0, 0)),
        ],
        out_specs=[pl.BlockSpec((tm, d), lambda i: (i, 0)), new_spec],
        out_shape=[jax.ShapeDtypeStruct((n, d), F32), jax.ShapeDtypeStruct((bsz, tq, hw), BF16)],
        scratch_shapes=[pltpu.VMEM((tm, d_ff), BF16)],
        compiler_params=_params(1),
        name="post_attn",
    )(x2d, o2d, w_out, g_ffn, w_in_all, w_down_all, g_final,
      q, c_col, kt_cache, vt_cache, cext_cache, k_new, v_new, cext_new)


def kernel(x_prompt, x_sample, state_gla, cache_fox_k, cache_fox_v, cache_fox_logf,
           norm_mix, gla_w_in, gla_w_g2, gla_b_g, gla_norm, gla_w_out,
           fox_w_in, fox_b_f, fox_w_out, norm_ffn, ffn_w_in, ffn_w_down, norm_final):
    d = x_prompt.shape[-1]
    depth = norm_mix.shape[0]
    groups = [x_prompt, x_sample]
    shapes = [x.shape for x in groups]
    xs = [x.reshape(-1, d) for x in groups]
    row_tiles = [min(ROW_TILE, x.shape[0]) for x in xs]
    ffn_steps = xs[0].shape[0] // row_tiles[0]
    ride_ffn = all(_rider_rows(a, ffn_steps) is not None for a in (ffn_w_in, ffn_w_down))
    w_ffn_in = w_ffn_down = None
    if not ride_ffn:
        w_ffn_in = ffn_w_in.astype(BF16)
        w_ffn_down = ffn_w_down.astype(BF16)

    gla_states = [[], []]
    fox_k, fox_v, fox_f = [[], []], [[], []], [[], []]
    for i in range(depth):
        j = i // 2
        g_mix = norm_mix[i].reshape(1, d)
        g_ffn = norm_ffn[i].reshape(1, d)
        last = i == depth - 1
        if i % 2 == 0:
            _, n_head, dk, dv = state_gla.shape[1:]
            hk, hv = n_head * dk, n_head * dv
            n_main = 2 * hk + 2 * hv
            w_in = gla_w_in[j]
            wt = w_in.T
            w_g2 = gla_w_g2[j].astype(BF16)
            b_g = gla_b_g[j].reshape(1, hk)
            w_out = gla_w_out[j].astype(BF16)
            norm_g = gla_norm[j].reshape(1, hv)
            s0s = [jnp.zeros((shapes[0][0], n_head, dk, dv), F32), state_gla[j]]
            for gi in range(2):
                bsz, t_len, _ = shapes[gi]
                n_seq = min(bsz, GLA_SEQS_PROMPT if gi == 0 else GLA_SEQS_SAMPLE)
                if w_ffn_in is not None and t_len <= GLA_ROWS:
                    og, s_fin = _gla_proj_mix(xs[gi].reshape(bsz, t_len, d), g_mix, wt, w_g2, b_g,
                                              s0s[gi], norm_g, n_seq)
                else:
                    if w_ffn_in is None:
                        proj, glog, w_ffn_in, w_ffn_down = _gla_proj(xs[gi], g_mix, wt, w_g2, b_g,
                                                                     row_tiles[gi], (ffn_w_in, ffn_w_down))
                    else:
                        proj, glog = _gla_proj(xs[gi], g_mix, wt, w_g2, b_g, row_tiles[gi])
                    og, s_fin = _gla_mix(proj.reshape(bsz, t_len, n_main), glog.reshape(bsz, t_len, hk),
                                         s0s[gi], norm_g, min(t_len, GLA_ROWS), n_seq)
                gla_states[gi].append(s_fin)
                xs[gi] = _post(xs[gi], og.reshape(-1, hv), w_out, g_ffn, w_ffn_in, w_ffn_down, i,
                               norm_final.reshape(1, d), row_tiles[gi], last)
        else:
            n_head = fox_b_f.shape[1]
            hw = fox_w_out.shape[1]
            hd = hw // n_head
            wt = fox_w_in[j].T
            bf_col = fox_b_f[j].reshape(n_head, 1)
            w_out = fox_w_out[j].astype(BF16)
            g_fin = norm_final.reshape(1, d)
            bsz, t_len, _ = shapes[0]
            qaug, kaug, kt, vt, vtb, logf_p = _fox_proj_prompt(xs[0].reshape(bsz, t_len, d), g_mix, wt, bf_col,
                                                               min(ROW_TILE, t_len), ATTN_TQ, n_head)
            o_p = _fox_attn_prompt(qaug, kaug, vtb, ATTN_PAIRS, n_head)
            fox_k[0].append(kt.reshape(bsz, n_head, hd, t_len).transpose(0, 3, 1, 2))
            fox_v[0].append(vt.reshape(bsz, n_head, hd, t_len).transpose(0, 3, 1, 2))
            fox_f[0].append(jnp.transpose(logf_p, (0, 2, 1)))
            bsz, t_len, _ = shapes[1]
            cache_len = cache_fox_logf.shape[2]
            q, k, v, logf_s, c_col, cext_new, cext_cache, k_rows, v_rows = _fox_proj_sample(
                xs[1].reshape(bsz, t_len, d), g_mix, wt, bf_col,
                jnp.transpose(cache_fox_logf[j], (0, 2, 1)), n_head, min(PROJ_SAMPLE_SEQS, bsz))
            kt_cache = jnp.transpose(cache_fox_k[j], (0, 2, 3, 1)).reshape(bsz, hw, cache_len)
            vt_cache = jnp.transpose(cache_fox_v[j], (0, 2, 3, 1)).reshape(bsz, hw, cache_len)
            attn_args = (q, c_col, kt_cache, vt_cache, cext_cache, k, v, cext_new)
            steps = bsz * (hw // (FUSED_ATTN_PAIRS * LANES))
            if xs[0].shape[0] % steps == 0 and (xs[0].shape[0] // steps) % LANES == 0:
                xs[0], o_s = _post_attn(xs[0], o_p.reshape(-1, hw), w_out, g_ffn, w_ffn_in, w_ffn_down, i,
                                        g_fin, last, *attn_args, FUSED_ATTN_PAIRS, n_head)
            else:
                xs[0] = _post(xs[0], o_p.reshape(-1, hw), w_out, g_ffn, w_ffn_in, w_ffn_down, i,
                              g_fin, row_tiles[0], last)
                o_s = _fox_attn_sample(*attn_args, hw // LANES, n_head)
            fox_k[1].append(k_rows.reshape(bsz, t_len, n_head, hd))
            fox_v[1].append(v_rows.reshape(bsz, t_len, n_head, hd))
            fox_f[1].append(jnp.transpose(logf_s, (0, 2, 1)))
            xs[1] = _post(xs[1], o_s.reshape(-1, hw), w_out, g_ffn, w_ffn_in, w_ffn_down, i,
                          g_fin, row_tiles[1], last)

    y_prompt = xs[0].reshape(shapes[0])
    y_sample = xs[1].reshape(shapes[1])
    st = lambda parts: jnp.stack(parts, axis=0)
    return (y_prompt, y_sample, st(gla_states[0]), st(fox_k[0]), st(fox_v[0]), st(fox_f[0]),
            st(gla_states[1]), st(fox_k[1]), st(fox_v[1]), st(fox_f[1]))
```

```python
import functools

import jax
import jax.numpy as jnp
from jax import lax
from jax.experimental import pallas as pl
from jax.experimental.pallas import tpu as pltpu

F32 = jnp.float32
BF16 = jnp.bfloat16

EPS = 1e-6
MASK_VALUE = -1e30
LOG2E = 1.4426950408889634

GLA_HEADS = 4
GLA_CHUNK = 64
GLA_GATE_TAU = 16.0

LANES = 128
MXU_TILE = 256
VT_ROWS = LANES + 16
VMEM_LIMIT_BYTES = 56 * 1024 * 1024

ROW_TILE = 512
GLA_ROWS = 256
GLA_SEQS_PROMPT = 2
GLA_SEQS_SAMPLE = 8
ATTN_TQ = 256
ATTN_PAIRS = 8
PROJ_SAMPLE_SEQS = 4
FUSED_ATTN_PAIRS = 4


def _params(n_grid):
    return pltpu.CompilerParams(
        dimension_semantics=("arbitrary",) * n_grid,
        vmem_limit_bytes=VMEM_LIMIT_BYTES,
    )


def _resident(shape):
    nd = len(shape)
    return pl.BlockSpec(shape, lambda *_: (0,) * nd, pipeline_mode=pl.Buffered(1))


def _dot(a, b):
    return jnp.dot(a, b, preferred_element_type=F32)


def _dot_nt(a, b):
    return lax.dot_general(a, b, (((1,), (1,)), ((), ())), preferred_element_type=F32)


def _dot_tn(a, b):
    return lax.dot_general(a, b, (((0,), (0,)), ((), ())), preferred_element_type=F32)


def _split3(x):
    hi = x.astype(BF16)
    r1 = x - hi.astype(F32)
    mid = r1.astype(BF16)
    lo = (r1 - mid.astype(F32)).astype(BF16)
    return hi, mid, lo


def _sum01(dot_fn, x, ones_first, mat01):
    acc = None
    for part in _split3(x):
        term = dot_fn(mat01, part) if ones_first else dot_fn(part, mat01)
        acc = term if acc is None else acc + term
    return acc


def _rmsnorm(x, g):
    var = jnp.mean(x * x, axis=-1, keepdims=True)
    return x * lax.rsqrt(var + EPS) * g


def _log_sigmoid(z):
    return jnp.minimum(z, 0.0) - jnp.log1p(jnp.exp(-jnp.abs(z)))


def _silu(z):
    return z * jax.nn.sigmoid(z)


def _identity(n):
    r = lax.broadcasted_iota(jnp.int32, (n, n), 0)
    c = lax.broadcasted_iota(jnp.int32, (n, n), 1)
    return (r == c).astype(BF16)


def _cast_weights_once(first_step, wt_ref, wtb_ref):
    @pl.when(first_step)
    def _():
        wtb_ref[...] = wt_ref[...].astype(BF16)


def _gla_proj_kernel(x_ref, g_ref, wt_ref, wg2_ref, bg_ref, *rest, n_main, n_ride):
    ride_in, (proj_ref, glog_ref) = rest[:n_ride], rest[n_ride:n_ride + 2]
    ride_out, wtb_ref = rest[n_ride + 2:2 * n_ride + 2], rest[-1]
    _cast_weights_once(pl.program_id(0) == 0, wt_ref, wtb_ref)
    rank = wg2_ref.shape[0]
    h = _rmsnorm(x_ref[...], g_ref[...]).astype(BF16)
    proj_ref[...] = _dot_nt(h, wtb_ref[0:n_main, :])
    gl_t = _dot_nt(wtb_ref[n_main:n_main + rank, :], h).astype(BF16)
    z = _dot_tn(gl_t, wg2_ref[...]) + bg_ref[...]
    glog_ref[...] = _log_sigmoid(z) / GLA_GATE_TAU
    for src, dst in zip(ride_in, ride_out):
        dst[...] = src[...].astype(BF16)


def _rider_rows(a, steps):
    layers, rows, _ = a.shape
    if steps % layers or rows % (steps // layers) or (rows // (steps // layers)) % 16:
        return None
    return rows // (steps // layers)


def _gla_proj(x2d, g, wt, w_g2, b_g, tm, riders=()):
    n, d = x2d.shape
    rank, hk = w_g2.shape
    n_main = wt.shape[0] - rank
    steps = n // tm
    ride_specs = []
    for a in riders:
        per = steps // a.shape[0]
        ride_specs.append(pl.BlockSpec((1, _rider_rows(a, steps), a.shape[2]),
                                       lambda i, per=per: (i // per, i % per, 0)))
    return pl.pallas_call(
        functools.partial(_gla_proj_kernel, n_main=n_main, n_ride=len(riders)),
        grid=(steps,),
        in_specs=[
            pl.BlockSpec((tm, d), lambda i: (i, 0)),
            _resident((1, d)),
            _resident(wt.shape),
            _resident(w_g2.shape),
            _resident((1, hk)),
        ] + ride_specs,
        out_specs=[
            pl.BlockSpec((tm, n_main), lambda i: (i, 0)),
            pl.BlockSpec((tm, hk), lambda i: (i, 0)),
        ] + ride_specs,
        out_shape=[
            jax.ShapeDtypeStruct((n, n_main), F32),
            jax.ShapeDtypeStruct((n, hk), F32),
        ] + [jax.ShapeDtypeStruct(a.shape, BF16) for a in riders],
        scratch_shapes=[pltpu.VMEM(wt.shape, BF16)],
        compiler_params=_params(1),
        name="gla_proj",
    )(x2d, g, wt, w_g2, b_g, *riders)


def _gla_mix_kernel(q_ref, k_ref, v_ref, r_ref, glog_ref, s0_ref, ng_ref,
                    og_ref, sfin_ref, s_ref, *, dk, dv, whole_seq=False):
    always = lambda fn: fn()
    first = always if whole_seq else pl.when(pl.program_id(1) == 0)
    final = always if whole_seq else pl.when(pl.program_id(1) == pl.num_programs(1) - 1)
    nb, tb, _ = q_ref.shape
    n_chunk = tb // GLA_CHUNK
    seqs = range(nb)
    heads = range(GLA_HEADS)
    units = [(sb, h) for sb in seqs for h in heads]
    chunks = [slice(c * GLA_CHUNK, (c + 1) * GLA_CHUNK) for c in range(n_chunk)]
    ksl = [slice(h * dk, (h + 1) * dk) for h in heads]
    vsl = [slice(h * dv, (h + 1) * dv) for h in heads]

    @first
    def _():
        s_ref[...] = s0_ref[...]

    row = lax.broadcasted_iota(jnp.int32, (tb, tb), 0)
    col = lax.broadcasted_iota(jnp.int32, (tb, tb), 1)
    same_chunk = (row // GLA_CHUNK) == (col // GLA_CHUNK)
    causal = same_chunk & (col <= row)
    cum_mat = causal.astype(BF16)

    b = [_sum01(_dot, glog_ref[sb], True, cum_mat) for sb in seqs]
    qe, ke, kd, dec_t = [], [], [], []
    for sb in seqs:
        tot_rows = [b[sb][(c + 1) * GLA_CHUNK - 1:(c + 1) * GLA_CHUNK, :] for c in range(n_chunk)]
        b_last = jnp.concatenate([jnp.broadcast_to(r, (GLA_CHUNK, r.shape[1])) for r in tot_rows], axis=0)
        pad_rows = [jnp.zeros_like(tot_rows[0])] * (8 - n_chunk % 8 if n_chunk % 8 else 0)
        dec_t.append(jnp.exp(_to_columns(jnp.concatenate(tot_rows + pad_rows, axis=0))))
        q = q_ref[sb]
        k = k_ref[sb]
        qe.append((q * jnp.exp(b[sb]) * (dk ** -0.5)).astype(BF16))
        ke.append((k * jnp.exp(-b[sb])).astype(BF16))
        kd.append((k * jnp.exp(b_last - b[sb])).astype(BF16))

    v_b = {u: v_ref[u[0], :, vsl[u[1]]].astype(BF16) for u in units}
    a_raw = {(sb, h): _dot_nt(qe[sb][:, ksl[h]], ke[sb][:, ksl[h]]) for sb, h in units}
    upd = {(sb, h): [_dot_tn(kd[sb][rs, ksl[h]], v_b[sb, h][rs]) for rs in chunks] for sb, h in units}
    o_intra = {u: _dot(jnp.where(causal, a_raw[u], 0.0).astype(BF16), v_b[u]) for u in units}
    s_in = {}
    for sb, h in units:
        s = s_ref[sb, h]
        s_in[sb, h] = []
        for c in range(n_chunk):
            s_in[sb, h].append(s.astype(BF16))
            s = s * dec_t[sb][ksl[h], c:c + 1] + upd[sb, h][c]
        s_ref[sb, h] = s
    for sb, h in units:
        o_parts = [o_intra[sb, h][rs] + _dot(qe[sb][rs, ksl[h]], s_in[sb, h][c]) for c, rs in enumerate(chunks)]
        o = o_parts[0] if n_chunk == 1 else jnp.concatenate(o_parts, axis=0)
        on = _rmsnorm(o, ng_ref[:, vsl[h]])
        og_ref[sb, :, vsl[h]] = (on * _silu(r_ref[sb, :, vsl[h]])).astype(BF16)

    @final
    def _():
        sfin_ref[...] = s_ref[...]


def _gla_proj_mix_kernel(x_ref, g_ref, wt_ref, wg2_ref, bg_ref, s0_ref, ng_ref,
                         og_ref, sfin_ref, wtb_ref, proj_ref, glog_ref, s_ref, *, n_main, dk, dv):
    nb, tb, d = x_ref.shape
    hk, hv = GLA_HEADS * dk, GLA_HEADS * dv
    _cast_weights_once(pl.program_id(0) == 0, wt_ref, wtb_ref)
    rank = wg2_ref.shape[0]
    h = _rmsnorm(x_ref[...].reshape(nb * tb, d), g_ref[...]).astype(BF16)
    proj_ref[...] = _dot_nt(h, wtb_ref[0:n_main, :]).reshape(nb, tb, n_main)
    gl_t = _dot_nt(wtb_ref[n_main:n_main + rank, :], h).astype(BF16)
    z = _dot_tn(gl_t, wg2_ref[...]) + bg_ref[...]
    glog_ref[...] = (_log_sigmoid(z) / GLA_GATE_TAU).reshape(nb, tb, hk)
    _gla_mix_kernel(proj_ref.at[:, :, 0:hk], proj_ref.at[:, :, hk:2 * hk], proj_ref.at[:, :, 2 * hk:2 * hk + hv],
                    proj_ref.at[:, :, 2 * hk + hv:2 * hk + 2 * hv], glog_ref, s0_ref, ng_ref,
                    og_ref, sfin_ref, s_ref, dk=dk, dv=dv, whole_seq=True)


def _gla_proj_mix(x3, g, wt, w_g2, b_g, s0, norm_g, nb):
    bsz, tb, d = x3.shape
    _, n_head, dk, dv = s0.shape
    rank, hk = w_g2.shape
    hv = n_head * dv
    n_main = wt.shape[0] - rank
    assert bsz % nb == 0 and tb % GLA_CHUNK == 0 and n_main == 2 * hk + 2 * hv
    state_spec = pl.BlockSpec((nb, n_head, dk, dv), lambda b: (b, 0, 0, 0))
    return pl.pallas_call(
        functools.partial(_gla_proj_mix_kernel, n_main=n_main, dk=dk, dv=dv),
        grid=(bsz // nb,),
        in_specs=[
            pl.BlockSpec((nb, tb, d), lambda b: (b, 0, 0)),
            _resident((1, d)),
            _resident(wt.shape),
            _resident(w_g2.shape),
            _resident((1, hk)),
            state_spec,
            _resident((1, hv)),
        ],
        out_specs=[pl.BlockSpec((nb, tb, hv), lambda b: (b, 0, 0)), state_spec],
        out_shape=[jax.ShapeDtypeStruct((bsz, tb, hv), BF16), jax.ShapeDtypeStruct(s0.shape, F32)],
        scratch_shapes=[
            pltpu.VMEM(wt.shape, BF16),
            pltpu.VMEM((nb, tb, n_main), F32),
            pltpu.VMEM((nb, tb, hk), F32),
            pltpu.VMEM((nb, n_head, dk, dv), F32),
        ],
        compiler_params=_params(1),
        name="gla_proj_mix",
    )(x3, g, wt, w_g2, b_g, s0, norm_g)


def _gla_mix(proj3, glog3, s0, norm_g, tb, nb):
    bsz, t_len, _ = proj3.shape
    _, n_head, dk, dv = s0.shape
    hk, hv = n_head * dk, n_head * dv
    assert t_len % tb == 0 and tb % GLA_CHUNK == 0 and hv == 2 * hk and bsz % nb == 0
    kern = functools.partial(_gla_mix_kernel, dk=dk, dv=dv)
    state_spec = pl.BlockSpec((nb, n_head, dk, dv), lambda b, t: (b, 0, 0, 0))
    return pl.pallas_call(
        kern,
        grid=(bsz // nb, t_len // tb),
        in_specs=[
            pl.BlockSpec((nb, tb, hk), lambda b, t: (b, t, 0)),
            pl.BlockSpec((nb, tb, hk), lambda b, t: (b, t, 1)),
            pl.BlockSpec((nb, tb, hv), lambda b, t: (b, t, 1)),
            pl.BlockSpec((nb, tb, hv), lambda b, t: (b, t, 2)),
            pl.BlockSpec((nb, tb, hk), lambda b, t: (b, t, 0)),
            state_spec,
            _resident((1, hv)),
        ],
        out_specs=[
            pl.BlockSpec((nb, tb, hv), lambda b, t: (b, t, 0)),
            state_spec,
        ],
        out_shape=[
            jax.ShapeDtypeStruct((bsz, t_len, hv), BF16),
            jax.ShapeDtypeStruct(s0.shape, F32),
        ],
        scratch_shapes=[pltpu.VMEM((nb, n_head, dk, dv), F32)],
        compiler_params=_params(2),
        name="gla_mix",
    )(proj3, proj3, proj3, proj3, glog3, s0, norm_g)


def _post_kernel(x_ref, o_ref, wo_ref, g_ref, win_ref, wdown_ref, gfin_ref, y_ref, act_ref,
                 *, d_ff, ff_tile, final_norm, hooks=None):
    x1 = x_ref[...] + _dot(o_ref[...], wo_ref[...])
    if hooks and -1 in hooks:
        hooks[-1]()
    h = _rmsnorm(x1, g_ref[...]).astype(BF16)
    for j in range(d_ff // ff_tile):
        gate = _dot(h, win_ref[:, j * ff_tile:(j + 1) * ff_tile])
        up = _dot(h, win_ref[:, d_ff + j * ff_tile:d_ff + (j + 1) * ff_tile])
        act_ref[:, j * ff_tile:(j + 1) * ff_tile] = (_silu(gate) * up).astype(BF16)
        if hooks and j in hooks:
            hooks[j]()
    y = x1 + _dot(act_ref[...], wdown_ref[...])
    if final_norm:
        y = _rmsnorm(y, gfin_ref[...])
    y_ref[...] = y


def _post(x2d, o2d, w_out, g_ffn, w_in_all, w_down_all, layer, g_final, tm, final_norm):
    n, d = x2d.shape
    d_ff = w_down_all.shape[1]
    ff_tile = MXU_TILE
    assert d_ff % ff_tile == 0 and n % tm == 0
    kern = functools.partial(_post_kernel, d_ff=d_ff, ff_tile=ff_tile, final_norm=final_norm)
    return pl.pallas_call(
        kern,
        grid=(n // tm,),
        in_specs=[
            pl.BlockSpec((tm, d), lambda i: (i, 0)),
            pl.BlockSpec((tm, o2d.shape[1]), lambda i: (i, 0)),
            _resident(w_out.shape),
            _resident((1, d)),
            pl.BlockSpec((None,) + w_in_all.shape[1:], lambda i: (layer, 0, 0), pipeline_mode=pl.Buffered(1)),
            pl.BlockSpec((None,) + w_down_all.shape[1:], lambda i: (layer, 0, 0), pipeline_mode=pl.Buffered(1)),
            _resident((1, d)),
        ],
        out_specs=pl.BlockSpec((tm, d), lambda i: (i, 0)),
        out_shape=jax.ShapeDtypeStruct((n, d), F32),
        scratch_shapes=[pltpu.VMEM((tm, d_ff), BF16)],
        compiler_params=_params(1),
        name="post_final" if final_norm else "post",
    )(x2d, o2d, w_out, g_ffn, w_in_all, w_down_all, g_final)


def _lane_cumsum(x, tile):
    r = lax.broadcasted_iota(jnp.int32, (tile, tile), 0)
    c = lax.broadcasted_iota(jnp.int32, (tile, tile), 1)
    upper = (r <= c).astype(BF16)
    rows = x.shape[0]
    carry = jnp.zeros((rows, 1), F32)
    out = []
    for j in range(x.shape[1] // tile):
        parts = jnp.concatenate(_split3(x[:, j * tile:(j + 1) * tile]), axis=0)
        s = _dot(parts, upper)
        blk = carry + s[0:rows] + s[rows:2 * rows] + s[2 * rows:3 * rows]
        out.append(blk)
        carry = blk[:, tile - 1:tile]
    return out[0] if len(out) == 1 else jnp.concatenate(out, axis=1)


def _to_columns(x_t):
    rows = x_t.shape[0]
    parts = jnp.concatenate(_split3(x_t), axis=0)
    r = lax.broadcasted_iota(jnp.int32, (3 * rows, LANES), 0)
    c = lax.broadcasted_iota(jnp.int32, (3 * rows, LANES), 1)
    place = ((r % rows) == c).astype(BF16)
    return _dot_tn(parts, place)


def _key_ext_cols(c_cols, n_head):
    hi, mid, lo = (p.astype(F32) for p in _split3(c_cols * LOG2E))
    lane = lax.broadcasted_iota(jnp.int32, c_cols.shape, 1)
    ones = ((lane >= 3 * n_head) & (lane < 3 * n_head + 3)).astype(F32)
    ext = hi + pltpu.roll(mid, n_head, 1) + pltpu.roll(lo, 2 * n_head, 1) + ones
    return ext.astype(BF16)


def _key_ext_rows(c_rows):
    n_head, n = c_rows.shape
    hi, mid, lo = _split3(c_rows * LOG2E)
    r = lax.broadcasted_iota(jnp.int32, (n_head, n), 0)
    ones = (r < 3).astype(BF16)
    zeros = jnp.zeros((LANES - 4 * n_head, n), BF16)
    return jnp.concatenate([hi, mid, lo, ones, zeros], axis=0)


def _query_ext(idx, hh, n_head, cq):
    cq_hi, cq_mid, cq_lo = (p.astype(F32) for p in _split3(cq * LOG2E))
    ext = jnp.where((idx == hh) | (idx == n_head + hh) | (idx == 2 * n_head + hh), -1.0, 0.0)
    ext = jnp.where(idx == 3 * n_head, cq_hi, ext)
    ext = jnp.where(idx == 3 * n_head + 1, cq_mid, ext)
    ext = jnp.where(idx == 3 * n_head + 2, cq_lo, ext)
    return ext.astype(BF16)


def _fox_proj_prompt_kernel(x_ref, g_ref, wtf_ref, bfc_ref,
                            qaug_ref, kaug_ref, kt_ref, vt_ref, vtb_ref, logft_ref, carry_ref, wt_ref,
                            *, hw, n_head, scale, tq):
    t = pl.program_id(1)
    tm = x_ref.shape[0]
    _cast_weights_once((pl.program_id(0) == 0) & (t == 0), wtf_ref, wt_ref)

    @pl.when(t == 0)
    def _():
        carry_ref[...] = jnp.zeros_like(carry_ref)

    h = _rmsnorm(x_ref[...], g_ref[...]).astype(BF16)
    logf_t = _log_sigmoid(_dot_nt(wt_ref[3 * hw:3 * hw + n_head, :], h) + bfc_ref[...])
    logft_ref[...] = logf_t
    ct_blk = _lane_cumsum(logf_t, min(tm, MXU_TILE)) + carry_ref[:, 0:1]
    carry_ref[...] = jnp.broadcast_to(ct_blk[:, tm - 1:tm], carry_ref.shape)
    cext = _key_ext_cols(_to_columns(ct_blk), n_head)
    kt = _dot_nt(wt_ref[hw:2 * hw, :], h)
    vt = _dot_nt(wt_ref[2 * hw:3 * hw, :], h)
    qt = (_dot_nt(wt_ref[0:hw, :], h) * scale).astype(BF16)
    kt_ref[...] = kt
    hd = hw // n_head
    ext_rows = 4 * n_head
    rr = lax.broadcasted_iota(jnp.int32, (ext_rows, tq), 0)
    rq = lax.broadcasted_iota(jnp.int32, (LANES, tq), 0) // hd
    zeros = jnp.zeros((LANES - ext_rows, tq), BF16)
    for p in range(hw // LANES):
        for jq in range(tm // tq):
            qs = slice(jq * tq, (jq + 1) * tq)
            q_pair = qt[p * LANES:(p + 1) * LANES, qs]
            cols = []
            for hh in (2 * p, 2 * p + 1):
                top = jnp.where(rq == hh - 2 * p, q_pair, jnp.zeros_like(q_pair))
                cols.append(jnp.concatenate([top, _query_ext(rr, hh, n_head, ct_blk[hh:hh + 1, qs]), zeros], axis=0))
            qaug_ref[p, jq] = jnp.concatenate(cols, axis=1)
    k = kt.T
    for p in range(hw // LANES):
        kaug_ref[p, :, 0:LANES] = k[:, p * LANES:(p + 1) * LANES].astype(BF16)
        kaug_ref[p, :, LANES:2 * LANES] = cext
    vt_ref[...] = vt
    ones = jnp.ones((VT_ROWS - LANES, tm), BF16)
    for p in range(hw // LANES):
        vtb_ref[p, 0:LANES, :] = vt[p * LANES:(p + 1) * LANES, :].astype(BF16)
        vtb_ref[p, LANES:VT_ROWS, :] = ones


def _fox_proj_prompt(x3, g, wt, bf_col, tm, tq, n_head):
    bsz, t_len, d = x3.shape
    hw = (wt.shape[0] - n_head) // 3
    n_pair = hw // LANES
    assert t_len % tm == 0 and tm % tq == 0 and tq % LANES == 0 and 3 * n_head + 3 <= LANES
    kern = functools.partial(_fox_proj_prompt_kernel, hw=hw, n_head=n_head,
                             scale=(hw // n_head) ** -0.5 * LOG2E, tq=tq)
    feat = pl.BlockSpec((None, hw, tm), lambda b, t: (b, 0, t))
    head = pl.BlockSpec((None, n_head, tm), lambda b, t: (b, 0, t))
    return pl.pallas_call(
        kern,
        grid=(bsz, t_len // tm),
        in_specs=[
            pl.BlockSpec((None, tm, d), lambda b, t: (b, t, 0)),
            _resident((1, d)),
            _resident(wt.shape),
            _resident(bf_col.shape),
        ],
        out_specs=[
            pl.BlockSpec((None, n_pair, tm // tq, 2 * LANES, 2 * tq), lambda b, t: (b, 0, t, 0, 0)),
            pl.BlockSpec((None, n_pair, tm, 2 * LANES), lambda b, t: (b, 0, t, 0)),
            feat, feat,
            pl.BlockSpec((None, n_pair, VT_ROWS, tm), lambda b, t: (b, 0, 0, t)),
            head,
        ],
        out_shape=[
            jax.ShapeDtypeStruct((bsz, n_pair, t_len // tq, 2 * LANES, 2 * tq), BF16),
            jax.ShapeDtypeStruct((bsz, n_pair, t_len, 2 * LANES), BF16),
            jax.ShapeDtypeStruct((bsz, hw, t_len), F32),
            jax.ShapeDtypeStruct((bsz, hw, t_len), F32),
            jax.ShapeDtypeStruct((bsz, n_pair, VT_ROWS, t_len), BF16),
            jax.ShapeDtypeStruct((bsz, n_head, t_len), F32),
        ],
        scratch_shapes=[pltpu.VMEM((n_head, LANES), F32), pltpu.VMEM(wt.shape, BF16)],
        compiler_params=_params(2),
        name="fox_proj_prompt",
    )(x3, g, wt, bf_col)


def _fox_proj_sample_kernel(x_ref, g_ref, wtf_ref, bfc_ref, lct_ref,
                            q_ref, k_ref, v_ref, logft_ref, ccol_ref, cextn_ref, cextc_ref, k4_ref, v4_ref, wt_ref,
                            *, hw, n_head, scale):
    _cast_weights_once(pl.program_id(0) == 0, wtf_ref, wt_ref)
    nb, tm, d = x_ref.shape
    cache_len = lct_ref.shape[2]
    hd = hw // n_head
    h = _rmsnorm(x_ref[...].reshape(nb * tm, d), g_ref[...]).astype(BF16)
    q_ref[...] = (_dot_nt(h, wt_ref[0:hw, :]) * scale).astype(BF16).reshape(nb, tm, hw)
    for src_rows, tok_ref, row_ref in ((slice(hw, 2 * hw), k_ref, k4_ref), (slice(2 * hw, 3 * hw), v_ref, v4_ref)):
        kv = _dot_nt(h, wt_ref[src_rows, :])
        tok_ref[...] = kv.reshape(nb, tm, hw)
        for b in range(nb):
            for hh in range(n_head):
                row_ref[b, pl.ds(hh, tm, stride=n_head), :] = kv[b * tm:(b + 1) * tm, hh * hd:(hh + 1) * hd]

    ct_cache = _lane_cumsum(lct_ref[...].reshape(nb * n_head, cache_len), min(cache_len, MXU_TILE))
    logf_all = _log_sigmoid(_dot_nt(wt_ref[3 * hw:3 * hw + n_head, :], h) + bfc_ref[...])
    for b in range(nb):
        ct_b = ct_cache[b * n_head:(b + 1) * n_head, :]
        cextc_ref[b] = _key_ext_rows(ct_b)
        logf_t = logf_all[:, b * tm:(b + 1) * tm]
        logft_ref[b] = logf_t
        ct_new = _lane_cumsum(logf_t, tm) + ct_b[:, cache_len - 1:cache_len]
        cextn_ref[b] = _key_ext_rows(ct_new)
        ccol_ref[b] = _to_columns(ct_new)


def _fox_proj_sample(x3, g, wt, bf_col, logf_cache_t, n_head, nb):
    bsz, tm, d = x3.shape
    hw = (wt.shape[0] - n_head) // 3
    cache_len = logf_cache_t.shape[2]
    assert 4 * n_head <= LANES and bsz % nb == 0
    kern = functools.partial(_fox_proj_sample_kernel, hw=hw, n_head=n_head,
                             scale=(hw // n_head) ** -0.5 * LOG2E)
    tok = pl.BlockSpec((nb, tm, hw), lambda b: (b, 0, 0))
    return pl.pallas_call(
        kern,
        grid=(bsz // nb,),
        in_specs=[
            pl.BlockSpec((nb, tm, d), lambda b: (b, 0, 0)),
            _resident((1, d)),
            _resident(wt.shape),
            _resident(bf_col.shape),
            pl.BlockSpec((nb, n_head, cache_len), lambda b: (b, 0, 0)),
        ],
        out_specs=[
            tok, tok, tok,
            pl.BlockSpec((nb, n_head, tm), lambda b: (b, 0, 0)),
            pl.BlockSpec((nb, tm, LANES), lambda b: (b, 0, 0)),
            pl.BlockSpec((nb, LANES, tm), lambda b: (b, 0, 0)),
            pl.BlockSpec((nb, LANES, cache_len), lambda b: (b, 0, 0)),
            pl.BlockSpec((nb, tm * n_head, hw // n_head), lambda b: (b, 0, 0)),
            pl.BlockSpec((nb, tm * n_head, hw // n_head), lambda b: (b, 0, 0)),
        ],
        out_shape=[
            jax.ShapeDtypeStruct((bsz, tm, hw), BF16),
            jax.ShapeDtypeStruct((bsz, tm, hw), F32),
            jax.ShapeDtypeStruct((bsz, tm, hw), F32),
            jax.ShapeDtypeStruct((bsz, n_head, tm), F32),
            jax.ShapeDtypeStruct((bsz, tm, LANES), F32),
            jax.ShapeDtypeStruct((bsz, LANES, tm), BF16),
            jax.ShapeDtypeStruct((bsz, LANES, cache_len), BF16),
            jax.ShapeDtypeStruct((bsz, tm * n_head, hw // n_head), F32),
            jax.ShapeDtypeStruct((bsz, tm * n_head, hw // n_head), F32),
        ],
        scratch_shapes=[pltpu.VMEM(wt.shape, BF16)],
        compiler_params=_params(1),
        name="fox_proj_sample",
    )(x3, g, wt, bf_col, logf_cache_t)


def _attn_update(n_pair, kaug_of, vt_of, vt_prev_last, mask, next_kaug0,
                 qaug_ref, m_ref, acc_ref, s0_ref, plast_ref, alast_ref):
    last = n_pair - 1
    s_next = s0_ref[...]
    acc_ref[last] = acc_ref[last] * alast_ref[...] + _dot(vt_prev_last(), plast_ref[...])
    pending = None
    for p in range(n_pair):
        s = s_next
        if p + 1 < n_pair:
            s_next = _dot(kaug_of(p + 1), qaug_ref[p + 1])
        elif next_kaug0 is not None:
            s0_ref[...] = _dot(next_kaug0(), qaug_ref[0])
        if mask is not None:
            s = jnp.where(mask, s, MASK_VALUE)
        m_old = m_ref[p]
        m_new = jnp.maximum(m_old, jnp.max(s, axis=0, keepdims=True))
        alpha = jnp.exp2(m_old - m_new)
        pr = jnp.exp2(s - m_new).astype(BF16)
        m_ref[p] = m_new
        if pending is not None:
            q, pr_q, alpha_q = pending
            acc_ref[q] = acc_ref[q] * alpha_q + _dot(vt_of(q), pr_q)
        pending = (p, pr, alpha)
    plast_ref[...] = pending[1]
    alast_ref[...] = pending[2]


def _attn_finish(o_ref, acc_ref, *, n_pair, tq, hd):
    for p in range(n_pair):
        full = acc_ref[p, 0:LANES, :] * (1.0 / acc_ref[p, LANES:LANES + 1, :])
        z = jnp.concatenate([full[0:hd, 0:tq], full[hd:2 * hd, tq:2 * tq]], axis=0)
        o_ref[:, p * LANES:(p + 1) * LANES] = z.T.astype(o_ref.dtype)


def _fox_attn_prompt_kernel(qaug_ref, kaug_ref, vt_ref, o_ref,
                            m_ref, acc_ref, s0_ref, plast_ref, alast_ref, *, tq, n_pair, hd):
    i = pl.program_id(2)
    last = n_pair - 1
    state = (qaug_ref, m_ref, acc_ref, s0_ref, plast_ref, alast_ref)

    s0_ref[...] = _dot(kaug_ref[0, 0:tq, :], qaug_ref[0])
    m_ref[...] = jnp.full_like(m_ref, MASK_VALUE)
    acc_ref[...] = jnp.zeros_like(acc_ref)
    plast_ref[...] = jnp.zeros_like(plast_ref)
    alast_ref[...] = jnp.ones_like(alast_ref)

    def keys(j):
        return pl.ds(pl.multiple_of(j * tq, tq), tq)

    def vt_of(p, ks):
        return vt_ref[p, :, ks]

    def kv_block(j, mask, has_next):
        ks = keys(j)
        _attn_update(n_pair, lambda p: kaug_ref[p, ks, :], lambda p: vt_of(p, ks),
                     lambda: vt_of(last, keys(jnp.maximum(j - 1, 0))), mask,
                     (lambda: kaug_ref[0, keys(j + 1), :]) if has_next else None, *state)

    def body(j, carry):
        kv_block(j, None, True)
        return carry

    lax.fori_loop(0, i, body, 0)
    row = lax.broadcasted_iota(jnp.int32, (tq, 2 * tq), 0)
    col = lax.broadcasted_iota(jnp.int32, (tq, 2 * tq), 1) % tq
    kv_block(i, row <= col, False)
    acc_ref[last] = acc_ref[last] * alast_ref[...] + _dot(vt_of(last, keys(i)), plast_ref[...])
    _attn_finish(o_ref, acc_ref, n_pair=n_pair, tq=tq, hd=hd)


def _fox_attn_prompt(qaug, kaug, vtb, n_pair, n_head):
    bsz, n_pair_all, n_qblk, _, tq2 = qaug.shape
    tq = tq2 // 2
    t_len = n_qblk * tq
    hw = n_pair_all * LANES
    hd = hw // n_head
    assert 2 * hd == LANES and n_pair_all % n_pair == 0
    width = n_pair * LANES
    n_group = n_pair_all // n_pair
    kern = functools.partial(_fox_attn_prompt_kernel, tq=tq, n_pair=n_pair, hd=hd)
    return pl.pallas_call(
        kern,
        grid=(bsz, n_group, n_qblk),
        in_specs=[
            pl.BlockSpec((None, n_pair, None, 2 * LANES, 2 * tq), lambda b, g, i: (b, g, i, 0, 0)),
            pl.BlockSpec((None, n_pair, t_len, 2 * LANES), lambda b, g, i: (b, g, 0, 0)),
            pl.BlockSpec((None, n_pair, VT_ROWS, t_len), lambda b, g, i: (b, g, 0, 0)),
        ],
        out_specs=pl.BlockSpec((None, tq, width), lambda b, g, i: (b, i, g)),
        out_shape=jax.ShapeDtypeStruct((bsz, t_len, hw), BF16),
        scratch_shapes=[
            pltpu.VMEM((n_pair, 1, 2 * tq), F32),
            pltpu.VMEM((n_pair, VT_ROWS, 2 * tq), F32),
            pltpu.VMEM((tq, 2 * tq), F32),
            pltpu.VMEM((tq, 2 * tq), BF16),
            pltpu.VMEM((1, 2 * tq), F32),
        ],
        compiler_params=_params(3),
        name="fox_attn_prompt",
    )(qaug, kaug, vtb)


def _sample_attn_stages(q_ref, ccol_ref, kt_ref, vt_ref, cextc_ref, kn_ref, vn_ref, cextn_ref, o_ref,
                        *, g, n_pair, hd, n_head):
    tq = q_ref.shape[0]
    eye = _identity(LANES)
    lane = lax.broadcasted_iota(jnp.int32, (tq, LANES), 1)
    row = lax.broadcasted_iota(jnp.int32, (2 * tq, tq), 0) % tq
    col = lax.broadcasted_iota(jnp.int32, (2 * tq, tq), 1)
    causal_new = col <= row
    ccol = ccol_ref[...]

    def scores(p):
        feat = slice(p * LANES, (p + 1) * LANES)
        q_pair = q_ref[:, feat]
        blocks = []
        for h in range(2):
            hh = g * (2 * n_pair) + 2 * p + h
            cq = jnp.sum(jnp.where(lane == hh, ccol, 0.0), axis=1, keepdims=True)
            top = jnp.where(lane // hd == h, q_pair, jnp.zeros_like(q_pair))
            blocks.append(jnp.concatenate([top, _query_ext(lane, hh, n_head, cq)], axis=1))
        qaug = jnp.concatenate(blocks, axis=0)
        s_c = _dot(qaug, jnp.concatenate([kt_ref[feat, :].astype(BF16), cextc_ref[...]], axis=0))
        kn_t = _dot_nt(eye, kn_ref[:, feat].astype(BF16)).astype(BF16)
        s_n = _dot(qaug, jnp.concatenate([kn_t, cextn_ref[...]], axis=0))
        return s_c, jnp.where(causal_new, s_n, MASK_VALUE)

    held = {}

    def score_stage(p):
        held["s", p] = scores(p)

    def softmax_stage(p):
        s_c, s_n = held.pop(("s", p))
        m = jnp.maximum(jnp.max(s_c, axis=1, keepdims=True), jnp.max(s_n, axis=1, keepdims=True))
        p_c = jnp.exp2(s_c - m)
        p_n = jnp.exp2(s_n - m)
        l = jnp.sum(p_c, axis=1, keepdims=True) + jnp.sum(p_n, axis=1, keepdims=True)
        held["p", p] = (p_c.astype(BF16), p_n.astype(BF16), l)

    def value_stage(p):
        p_c, p_n, l = held.pop(("p", p))
        feat = slice(p * LANES, (p + 1) * LANES)
        o = _dot_nt(p_c, vt_ref[feat, :].astype(BF16)) + _dot(p_n, vn_ref[:, feat].astype(BF16))
        o = o / l
        o_ref[:, feat] = jnp.where(lane < hd, o[0:tq], o[tq:2 * tq]).astype(o_ref.dtype)

    return score_stage, softmax_stage, value_stage


def _fox_attn_sample_kernel(*refs, n_pair, hd, n_head):
    score, softmax, value = _sample_attn_stages(*refs, g=pl.program_id(1), n_pair=n_pair, hd=hd, n_head=n_head)
    score(0)
    for p in range(n_pair):
        if p + 1 < n_pair:
            score(p + 1)
        softmax(p)
        value(p)


def _post_attn_kernel(x_ref, o_ref, wo_ref, g_ref, win_ref, wdown_ref, gfin_ref,
                      q_ref, ccol_ref, kt_ref, vt_ref, cextc_ref, kn_ref, vn_ref, cextn_ref,
                      y_ref, os_ref, act_ref, *, d_ff, ff_tile, final_norm, n_group, n_pair, hd, n_head):
    score, softmax, value = _sample_attn_stages(
        q_ref, ccol_ref, kt_ref, vt_ref, cextc_ref, kn_ref, vn_ref, cextn_ref, os_ref,
        g=pl.program_id(0) % n_group, n_pair=n_pair, hd=hd, n_head=n_head)
    n_ff = d_ff // ff_tile
    lag = max(1, n_ff - 1 - n_pair) // 2 + 2
    hooks = {}

    def at(j, fn):
        j = min(j, n_ff - 1)
        prev = hooks.get(j)
        hooks[j] = fn if prev is None else (lambda: (prev(), fn()))

    for p in range(n_pair):
        at(p - 1, functools.partial(score, p))
        at(p - 1, functools.partial(softmax, p))
        at(p - 1 + lag, functools.partial(value, p))
    _post_kernel(x_ref, o_ref, wo_ref, g_ref, win_ref, wdown_ref, gfin_ref, y_ref, act_ref,
                 d_ff=d_ff, ff_tile=ff_tile, final_norm=final_norm, hooks=hooks)


def _fox_attn_sample(q, c_col, kt_cache, vt_cache, cext_cache, k_new, v_new, cext_new, n_pair, n_head):
    bsz, tq, hw = q.shape
    cache_len = kt_cache.shape[2]
    hd = hw // n_head
    assert 2 * hd == LANES and 4 * n_head <= LANES
    width = n_pair * LANES
    new_spec = pl.BlockSpec((None, tq, width), lambda b, g: (b, 0, g))
    cache_spec = pl.BlockSpec((None, width, cache_len), lambda b, g: (b, g, 0))
    kern = functools.partial(_fox_attn_sample_kernel, n_pair=n_pair, hd=hd, n_head=n_head)
    return pl.pallas_call(
        kern,
        grid=(bsz, hw // width),
        in_specs=[
            new_spec,
            pl.BlockSpec((None, tq, LANES), lambda b, g: (b, 0, 0)),
            cache_spec, cache_spec,
            pl.BlockSpec((None, LANES, cache_len), lambda b, g: (b, 0, 0)),
            new_spec, new_spec,
            pl.BlockSpec((None, LANES, tq), lambda b, g: (b, 0, 0)),
        ],
        out_specs=new_spec,
        out_shape=jax.ShapeDtypeStruct((bsz, tq, hw), BF16),
        compiler_params=_params(2),
        name="fox_attn_sample",
    )(q, c_col, kt_cache, vt_cache, cext_cache, k_new, v_new, cext_new)


def _post_attn(x2d, o2d, w_out, g_ffn, w_in_all, w_down_all, layer, g_final, final_norm,
               q, c_col, kt_cache, vt_cache, cext_cache, k_new, v_new, cext_new, n_pair, n_head):
    n, d = x2d.shape
    bsz, tq, hw = q.shape
    cache_len = kt_cache.shape[2]
    hd = hw // n_head
    width = n_pair * LANES
    n_group = hw // width
    steps = bsz * n_group
    tm = n // steps
    d_ff = w_down_all.shape[1]
    ff_tile = MXU_TILE
    assert n % steps == 0 and tm % 16 == 0 and d_ff % ff_tile == 0 and 2 * hd == LANES
    kern = functools.partial(_post_attn_kernel, d_ff=d_ff, ff_tile=ff_tile, final_norm=final_norm,
                             n_group=n_group, n_pair=n_pair, hd=hd, n_head=n_head)
    new_spec = pl.BlockSpec((None, tq, width), lambda i: (i // n_group, 0, i % n_group))
    cache_spec = pl.BlockSpec((None, width, cache_len), lambda i: (i // n_group, i % n_group, 0))
    return pl.pallas_call(
        kern,
        grid=(steps,),
        in_specs=[
            pl.BlockSpec((tm, d), lambda i: (i, 0)),
            pl.BlockSpec((tm, o2d.shape[1]), lambda i: (i, 0)),
            _resident(w_out.shape),
            _resident((1, d)),
            pl.BlockSpec((None,) + w_in_all.shape[1:], lambda i: (layer, 0, 0), pipeline_mode=pl.Buffered(1)),
            pl.BlockSpec((None,) + w_down_all.shape[1:], lambda i: (layer, 0, 0), pipeline_mode=pl.Buffered(1)),
            _resident((1, d)),
            new_spec,
            pl.BlockSpec((None, tq, LANES), lambda i: (i // n_group, 0, 0)),
            cache_spec, cache_spec,
            pl.BlockSpec((None, LANES, cache_len), lambda i: (i // n_group, 0, 0)),
            new_spec, new_spec,
            pl.BlockSpec((None, LANES, tq), lambda i: (i // n_group, 0, 0)),
        ],
        out_specs=[pl.BlockSpec((tm, d), lambda i: (i, 0)), new_spec],
        out_shape=[jax.ShapeDtypeStruct((n, d), F32), jax.ShapeDtypeStruct((bsz, tq, hw), BF16)],
        scratch_shapes=[pltpu.VMEM((tm, d_ff), BF16)],
        compiler_params=_params(1),
        name="post_attn",
    )(x2d, o2d, w_out, g_ffn, w_in_all, w_down_all, g_final,
      q, c_col, kt_cache, vt_cache, cext_cache, k_new, v_new, cext_new)


def kernel(x_prompt, x_sample, state_gla, cache_fox_k, cache_fox_v, cache_fox_logf,
           norm_mix, gla_w_in, gla_w_g2, gla_b_g, gla_norm, gla_w_out,
           fox_w_in, fox_b_f, fox_w_out, norm_ffn, ffn_w_in, ffn_w_down, norm_final):
    d = x_prompt.shape[-1]
    depth = norm_mix.shape[0]
    groups = [x_prompt, x_sample]
    shapes = [x.shape for x in groups]
    xs = [x.reshape(-1, d) for x in groups]
    row_tiles = [min(ROW_TILE, x.shape[0]) for x in xs]
    ffn_steps = xs[0].shape[0] // row_tiles[0]
    ride_ffn = all(_rider_rows(a, ffn_steps) is not None for a in (ffn_w_in, ffn_w_down))
    w_ffn_in = w_ffn_down = None
    if not ride_ffn:
        w_ffn_in = ffn_w_in.astype(BF16)
        w_ffn_down = ffn_w_down.astype(BF16)

    gla_states = [[], []]
    fox_k, fox_v, fox_f = [[], []], [[], []], [[], []]
    for i in range(depth):
        j = i // 2
        g_mix = norm_mix[i].reshape(1, d)
        g_ffn = norm_ffn[i].reshape(1, d)
        last = i == depth - 1
        if i % 2 == 0:
            _, n_head, dk, dv = state_gla.shape[1:]
            hk, hv = n_head * dk, n_head * dv
            n_main = 2 * hk + 2 * hv
            w_in = gla_w_in[j]
            wt = w_in.T
            w_g2 = gla_w_g2[j].astype(BF16)
            b_g = gla_b_g[j].reshape(1, hk)
            w_out = gla_w_out[j].astype(BF16)
            norm_g = gla_norm[j].reshape(1, hv)
            s0s = [jnp.zeros((shapes[0][0], n_head, dk, dv), F32), state_gla[j]]
            for gi in range(2):
                bsz, t_len, _ = shapes[gi]
                n_seq = min(bsz, GLA_SEQS_PROMPT if gi == 0 else GLA_SEQS_SAMPLE)
                if w_ffn_in is not None and t_len <= GLA_ROWS:
                    og, s_fin = _gla_proj_mix(xs[gi].reshape(bsz, t_len, d), g_mix, wt, w_g2, b_g,
                                              s0s[gi], norm_g, n_seq)
                else:
                    if w_ffn_in is None:
                        proj, glog, w_ffn_in, w_ffn_down = _gla_proj(xs[gi], g_mix, wt, w_g2, b_g,
                                                                     row_tiles[gi], (ffn_w_in, ffn_w_down))
                    else:
                        proj, glog = _gla_proj(xs[gi], g_mix, wt, w_g2, b_g, row_tiles[gi])
                    og, s_fin = _gla_mix(proj.reshape(bsz, t_len, n_main), glog.reshape(bsz, t_len, hk),
                                         s0s[gi], norm_g, min(t_len, GLA_ROWS), n_seq)
                gla_states[gi].append(s_fin)
                xs[gi] = _post(xs[gi], og.reshape(-1, hv), w_out, g_ffn, w_ffn_in, w_ffn_down, i,
                               norm_final.reshape(1, d), row_tiles[gi], last)
        else:
            n_head = fox_b_f.shape[1]
            hw = fox_w_out.shape[1]
            hd = hw // n_head
            wt = fox_w_in[j].T
            bf_col = fox_b_f[j].reshape(n_head, 1)
            w_out = fox_w_out[j].astype(BF16)
            g_fin = norm_final.reshape(1, d)
            bsz, t_len, _ = shapes[0]
            qaug, kaug, kt, vt, vtb, logf_p = _fox_proj_prompt(xs[0].reshape(bsz, t_len, d), g_mix, wt, bf_col,
                                                               min(ROW_TILE, t_len), ATTN_TQ, n_head)
            o_p = _fox_attn_prompt(qaug, kaug, vtb, ATTN_PAIRS, n_head)
            fox_k[0].append(kt.reshape(bsz, n_head, hd, t_len).transpose(0, 3, 1, 2))
            fox_v[0].append(vt.reshape(bsz, n_head, hd, t_len).transpose(0, 3, 1, 2))
            fox_f[0].append(jnp.transpose(logf_p, (0, 2, 1)))
            bsz, t_len, _ = shapes[1]
            cache_len = cache_fox_logf.shape[2]
            q, k, v, logf_s, c_col, cext_new, cext_cache, k_rows, v_rows = _fox_proj_sample(
                xs[1].reshape(bsz, t_len, d), g_mix, wt, bf_col,
                jnp.transpose(cache_fox_logf[j], (0, 2, 1)), n_head, min(PROJ_SAMPLE_SEQS, bsz))
            kt_cache = jnp.transpose(cache_fox_k[j], (0, 2, 3, 1)).reshape(bsz, hw, cache_len)
            vt_cache = jnp.transpose(cache_fox_v[j], (0, 2, 3, 1)).reshape(bsz, hw, cache_len)
            attn_args = (q, c_col, kt_cache, vt_cache, cext_cache, k, v, cext_new)
            steps = bsz * (hw // (FUSED_ATTN_PAIRS * LANES))
            if xs[0].shape[0] % steps == 0 and (xs[0].shape[0] // steps) % LANES == 0:
                xs[0], o_s = _post_attn(xs[0], o_p.reshape(-1, hw), w_out, g_ffn, w_ffn_in, w_ffn_down, i,
                                        g_fin, last, *attn_args, FUSED_ATTN_PAIRS, n_head)
            else:
                xs[0] = _post(xs[0], o_p.reshape(-1, hw), w_out, g_ffn, w_ffn_in, w_ffn_down, i,
                              g_fin, row_tiles[0], last)
                o_s = _fox_attn_sample(*attn_args, hw // LANES, n_head)
            fox_k[1].append(k_rows.reshape(bsz, t_len, n_head, hd))
            fox_v[1].append(v_rows.reshape(bsz, t_len, n_head, hd))
            fox_f[1].append(jnp.transpose(logf_s, (0, 2, 1)))
            xs[1] = _post(xs[1], o_s.reshape(-1, hw), w_out, g_ffn, w_ffn_in, w_ffn_down, i,
                          g_fin, row_tiles[1], last)

    y_prompt = xs[0].reshape(shapes[0])
    y_sample = xs[1].reshape(shapes[1])
    st = lambda parts: jnp.stack(parts, axis=0)
    return (y_prompt, y_sample, st(gla_states[0]), st(fox_k[0]), st(fox_v[0]), st(fox_f[0]),
            st(gla_states[1]), st(fox_k[1]), st(fox_v[1]), st(fox_f[1]))
```
